```python
import jax, jax.numpy as jnp
from jax import lax
import numpy as np

D_MODEL = 1024
BATCH = 1
SEQ = 16384
DEPTH = 1
DEC_BATCH = 8
DEC_SEQ = 32
PAST_LEN = 4096

CHUNK = 64
EPS = 1e-6
GLA_HEADS = 4
GLA_DK = 64
GLA_DV = 128
GLA_RANK = 16
GLA_TAU = 16.0
SWA_HEADS = 8
SWA_KV = 2
SWA_HD = 64
SWA_G = SWA_HEADS // SWA_KV
WINDOW = 128
WIN_CHUNKS = WINDOW // CHUNK
SWA_ROWS = WINDOW
N_EXPERTS = 32
TOP_K = 4
D_FF = 1024
SWIGLU_ALPHA = 1.702
SWIGLU_LIMIT = 7.0
MOE_BLOCK = 128
D_SWA_Q = SWA_HEADS * SWA_HD
D_SWA_KV = SWA_KV * SWA_HD
D_GLA_K = GLA_HEADS * GLA_DK
D_GLA_V = GLA_HEADS * GLA_DV
SPLIT_SIZES = (D_SWA_Q, D_SWA_KV, D_SWA_KV, D_GLA_K, D_GLA_K, D_GLA_V, D_GLA_V, GLA_RANK)
D_IN = D_SWA_Q + 2 * D_SWA_KV + 2 * D_GLA_K + 2 * D_GLA_V + GLA_RANK
D_MIX = D_SWA_Q + D_GLA_V

kernel_name = "hymba_gla_swa_sink_moe_stream_step"


def _rmsnorm(x, g):
    xf = x.astype(jnp.float32)
    y = xf * lax.rsqrt(jnp.mean(xf * xf, axis=-1, keepdims=True) + EPS)
    return (y * g.astype(jnp.float32)).astype(x.dtype)


def _mixer_inputs(x, norm_g, w_in, w_a2, b_a, q_g, k_g):
    B, T, _ = x.shape
    u = _rmsnorm(x, norm_g) @ w_in
    cuts = np.cumsum(SPLIT_SIZES)[:-1].tolist()
    sq, sk, sv, gq, gk, gv, gr, ga = jnp.split(u, cuts, axis=-1)
    sq = _rmsnorm(sq.reshape(B, T, SWA_HEADS, SWA_HD), q_g)
    sk = _rmsnorm(sk.reshape(B, T, SWA_KV, SWA_HD), k_g)
    sv = sv.reshape(B, T, SWA_KV, SWA_HD)
    gq = gq.reshape(B, T, GLA_HEADS, GLA_DK).astype(jnp.float32) * (GLA_DK ** -0.5)
    gk = gk.reshape(B, T, GLA_HEADS, GLA_DK).astype(jnp.float32)
    gv = gv.reshape(B, T, GLA_HEADS, GLA_DV).astype(jnp.float32)
    gg = (jax.nn.log_sigmoid((ga @ w_a2 + b_a).astype(jnp.float32)) / GLA_TAU).reshape(B, T, GLA_HEADS, GLA_DK)
    return sq, sk, sv, gq, gk, gv, gg, gr


def _sink_softmax(s, sinks):
    sk = jnp.broadcast_to(sinks.astype(jnp.float32).reshape(SWA_KV, SWA_G, 1, 1), s.shape[:-1] + (1,))
    p = jax.nn.softmax(jnp.concatenate([s, sk], axis=-1), axis=-1)
    return p[..., :-1]


def _swa_prompt(q, k, v, sinks):
    B, T = q.shape[:2]
    nc = T // CHUNK
    qb = q.reshape(B, nc, CHUNK, SWA_KV, SWA_G, SWA_HD)
    pad = ((0, 0), (WINDOW, 0), (0, 0), (0, 0))
    kp = jnp.pad(k, pad).reshape(B, nc + WIN_CHUNKS, CHUNK, SWA_KV, SWA_HD)
    vp = jnp.pad(v, pad).reshape(B, nc + WIN_CHUNKS, CHUNK, SWA_KV, SWA_HD)
    kb = jnp.concatenate([kp[:, j:j + nc] for j in range(WIN_CHUNKS + 1)], axis=2)
    vb = jnp.concatenate([vp[:, j:j + nc] for j in range(WIN_CHUNKS + 1)], axis=2)
    s = jnp.einsum('bcqkgd,bcskd->bckgqs', qb, kb).astype(jnp.float32) * (SWA_HD ** -0.5)
    key_chunk = jnp.arange(nc)[:, None] - WIN_CHUNKS + jnp.arange(WIN_CHUNKS + 1)[None, :]
    valid = jnp.repeat(key_chunk >= 0, CHUNK, axis=1)
    s = jnp.where(valid[None, :, None, None, None, :], s, -jnp.inf)
    p = _sink_softmax(s, sinks)
    o = jnp.einsum('bckgqs,bcskd->bcqkgd', p.astype(v.dtype), vb)
    return o.reshape(B, T, D_SWA_Q)


def _swa_sample(q, k_all, v_all, sinks):
    B, T = q.shape[:2]
    qb = q.reshape(B, T, SWA_KV, SWA_G, SWA_HD)
    s = jnp.einsum('bqkgd,bskd->bkgqs', qb, k_all).astype(jnp.float32) * (SWA_HD ** -0.5)
    p = _sink_softmax(s, sinks)
    o = jnp.einsum('bkgqs,bskd->bqkgd', p.astype(v_all.dtype), v_all)
    return o.reshape(B, T, D_SWA_Q)


def _gla_block(S, qkvg):
    q, k, v, g = qkvg
    L = q.shape[2]
    b = jnp.cumsum(g, axis=2)
    causal = jnp.tril(jnp.ones((L, L), dtype=bool))
    diff = jnp.where(causal[None, None, :, :, None], b[:, :, :, None, :] - b[:, :, None, :, :], -jnp.inf)
    a = jnp.einsum('bhtd,bhsd,bhtsd->bhts', q, k, jnp.exp(diff))
    o = jnp.einsum('bhtd,bhde->bhte', q * jnp.exp(b), S) + jnp.einsum('bhts,bhse->bhte', a, v)
    b_last = b[:, :, -1:, :]
    S_new = jnp.exp(b_last[:, :, 0, :])[..., None] * S + jnp.einsum('bhsd,bhse->bhde', k * jnp.exp(b_last - b), v)
    return S_new, o


def _gla(q, k, v, g, S0, block):
    B, T = q.shape[:2]
    nc = T // block

    def to_blocks(t):
        return jnp.transpose(t.reshape(B, nc, block, GLA_HEADS, t.shape[-1]), (1, 0, 3, 2, 4))

    S, o = lax.scan(_gla_block, S0, (to_blocks(q), to_blocks(k), to_blocks(v), to_blocks(g)))
    o = jnp.transpose(o, (1, 0, 3, 2, 4)).reshape(B, T, GLA_HEADS, GLA_DV)
    return o, S


def _merge(o_swa, o_gla, gr, gla_norm_g, w_out):
    B, T = o_swa.shape[:2]
    og = _rmsnorm(o_gla, gla_norm_g).reshape(B, T, D_GLA_V) * jax.nn.silu(gr.astype(jnp.float32))
    return jnp.concatenate([o_swa, og.astype(o_swa.dtype)], axis=-1) @ w_out


def _moe(x, w_router, b_router, w_gate, b_gate, w_up, b_up, w_down, b_down):
    B, T, D = x.shape
    xf = x.reshape(B * T, D)
    N = B * T
    logits = (xf @ w_router).astype(jnp.float32) + b_router.astype(jnp.float32)
    top_v, top_i = lax.top_k(logits, TOP_K)
    top_w = jax.nn.softmax(top_v, axis=-1)
    A = N * TOP_K
    flat_e = top_i.reshape(A)
    flat_tok = jnp.repeat(jnp.arange(N, dtype=jnp.int32), TOP_K)
    flat_w = top_w.reshape(A)
    order = jnp.argsort(flat_e)
    se, stok, sw = flat_e[order], flat_tok[order], flat_w[order]
    counts = jnp.bincount(flat_e, length=N_EXPERTS)
    padded = (counts + MOE_BLOCK - 1) // MOE_BLOCK * MOE_BLOCK
    start_sorted = jnp.cumsum(counts) - counts
    end_padded = jnp.cumsum(padded)
    start_padded = end_padded - padded
    dest = start_padded[se] + jnp.arange(A, dtype=jnp.int32) - start_sorted[se]
    n_blocks = -(-A // MOE_BLOCK) + N_EXPERTS
    xs = jnp.zeros((n_blocks * MOE_BLOCK, D), xf.dtype).at[dest].set(xf[stok])
    block_e = jnp.minimum(jnp.searchsorted(end_padded, jnp.arange(n_blocks) * MOE_BLOCK, side='right'), N_EXPERTS - 1)

    def expert_block(args):
        e, xb = args
        gate = jnp.minimum(xb @ w_gate[e] + b_gate[e], SWIGLU_LIMIT)
        up = jnp.clip(xb @ w_up[e] + b_up[e], -SWIGLU_LIMIT, SWIGLU_LIMIT)
        hdn = (up + 1.0) * gate * jax.nn.sigmoid(gate * SWIGLU_ALPHA)
        return hdn @ w_down[e] + b_down[e]

    ys = lax.map(expert_block, (block_e, xs.reshape(n_blocks, MOE_BLOCK, D))).reshape(n_blocks * MOE_BLOCK, D)
    out = jnp.zeros((N, D), jnp.float32).at[stok].add(ys[dest].astype(jnp.float32) * sw[:, None])
    return out.astype(x.dtype).reshape(B, T, D)


def setup_inputs(seed: int = 0) -> dict:
    key = jax.random.key(seed)
    ks = jax.random.split(key, 24)
    f32 = jnp.float32

    def nrm(k, shape, s):
        return jax.random.normal(k, shape, f32) * s

    L = DEPTH
    return {
        "x_prompt": nrm(ks[0], (BATCH, SEQ, D_MODEL), 1.0),
        "x_sample": nrm(ks[1], (DEC_BATCH, DEC_SEQ, D_MODEL), 1.0),
        "state_gla": nrm(ks[2], (L, DEC_BATCH, GLA_HEADS, GLA_DK, GLA_DV), 1.0),
        "cache_swa_k": nrm(ks[3], (L, DEC_BATCH, SWA_ROWS, SWA_KV, SWA_HD), 1.0),
        "cache_swa_v": nrm(ks[4], (L, DEC_BATCH, SWA_ROWS, SWA_KV, SWA_HD), 1.0),
        "norm_mix_g": 1.0 + nrm(ks[5], (L, D_MODEL), 0.02),
        "w_in": nrm(ks[6], (L, D_MODEL, D_IN), D_MODEL ** -0.5),
        "w_gla_a2": nrm(ks[7], (L, GLA_RANK, D_GLA_K), GLA_RANK ** -0.5),
        "b_gla_a": nrm(ks[8], (L, D_GLA_K), 0.1),
        "q_norm_g": 1.0 + nrm(ks[9], (L, SWA_HD), 0.02),
        "k_norm_g": 1.0 + nrm(ks[10], (L, SWA_HD), 0.02),
        "swa_sinks": nrm(ks[11], (L, SWA_HEADS), 0.5),
        "gla_norm_g": 1.0 + nrm(ks[12], (L, GLA_DV), 0.02),
        "w_out": nrm(ks[13], (L, D_MIX, D_MODEL), D_MIX ** -0.5),
        "norm_ffn_g": 1.0 + nrm(ks[14], (L, D_MODEL), 0.02),
        "w_router": nrm(ks[15], (L, D_MODEL, N_EXPERTS), D_MODEL ** -0.5),
        "b_router": nrm(ks[16], (L, N_EXPERTS), 0.01),
        "w_gate": nrm(ks[17], (L, N_EXPERTS, D_MODEL, D_FF), D_MODEL ** -0.5),
        "b_gate": nrm(ks[18], (L, N_EXPERTS, D_FF), 0.01),
        "w_up": nrm(ks[19], (L, N_EXPERTS, D_MODEL, D_FF), D_MODEL ** -0.5),
        "b_up": nrm(ks[20], (L, N_EXPERTS, D_FF), 0.01),
        "w_down": nrm(ks[21], (L, N_EXPERTS, D_FF, D_MODEL), D_FF ** -0.5),
        "b_down": nrm(ks[22], (L, N_EXPERTS, D_MODEL), 0.01),
    }


def reference(x_prompt, x_sample, state_gla, cache_swa_k, cache_swa_v,
              norm_mix_g, w_in, w_gla_a2, b_gla_a, q_norm_g, k_norm_g, swa_sinks,
              gla_norm_g, w_out, norm_ffn_g, w_router, b_router,
              w_gate, b_gate, w_up, b_up, w_down, b_down):
    hp, hs = x_prompt, x_sample
    gla_p, kc_p, vc_p, gla_s, kc_s, vc_s = [], [], [], [], [], []
    for l in range(DEPTH):
        sq, sk, sv, gq, gk, gv, gg, gr = _mixer_inputs(hp, norm_mix_g[l], w_in[l], w_gla_a2[l], b_gla_a[l],
                                                       q_norm_g[l], k_norm_g[l])
        o_swa = _swa_prompt(sq, sk, sv, swa_sinks[l])
        s0 = jnp.zeros((hp.shape[0], GLA_HEADS, GLA_DK, GLA_DV), jnp.float32)
        o_gla, s_fin = _gla(gq, gk, gv, gg, s0, CHUNK)
        hp = hp + _merge(o_swa, o_gla, gr, gla_norm_g[l], w_out[l])
        hp = hp + _moe(_rmsnorm(hp, norm_ffn_g[l]), w_router[l], b_router[l], w_gate[l], b_gate[l],
                       w_up[l], b_up[l], w_down[l], b_down[l])
        gla_p.append(s_fin)
        kc_p.append(sk[:, -SWA_ROWS:])
        vc_p.append(sv[:, -SWA_ROWS:])
        sq, sk, sv, gq, gk, gv, gg, gr = _mixer_inputs(hs, norm_mix_g[l], w_in[l], w_gla_a2[l], b_gla_a[l],
                                                       q_norm_g[l], k_norm_g[l])
        k_all = jnp.concatenate([cache_swa_k[l].astype(sk.dtype), sk], axis=1)
        v_all = jnp.concatenate([cache_swa_v[l].astype(sv.dtype), sv], axis=1)
        o_swa = _swa_sample(sq, k_all, v_all, swa_sinks[l])
        o_gla, s_new = _gla(gq, gk, gv, gg, state_gla[l].astype(jnp.float32), hs.shape[1])
        hs = hs + _merge(o_swa, o_gla, gr, gla_norm_g[l], w_out[l])
        hs = hs + _moe(_rmsnorm(hs, norm_ffn_g[l]), w_router[l], b_router[l], w_gate[l], b_gate[l],
                       w_up[l], b_up[l], w_down[l], b_down[l])
        gla_s.append(s_new)
        kc_s.append(k_all[:, -SWA_ROWS:])
        vc_s.append(v_all[:, -SWA_ROWS:])
    return (hp, hs, jnp.stack(gla_p), jnp.stack(kc_p), jnp.stack(vc_p),
            jnp.stack(gla_s), jnp.stack(kc_s), jnp.stack(vc_s))
```

```python
import functools

import numpy as np
import jax
import jax.numpy as jnp
from jax import lax
from jax.experimental import pallas as pl
from jax.experimental.pallas import tpu as pltpu

F32 = jnp.float32
BF16 = jnp.bfloat16
I32 = jnp.int32
U32 = jnp.uint32

D_MODEL = 1024
N_PROMPT = 16384
N_STREAMS = 8
T_SAMPLE = 32
N_SAMPLE = N_STREAMS * T_SAMPLE
N_ROWS = N_PROMPT + N_SAMPLE
EPS = 1e-6

CHUNK = 64
SUB = 16
N_HEADS_SWA = 8
HD = 64
WINDOW = 128
N_HEADS_GLA = 4
DK = 64
DV = 128
GLA_TAU = 16.0
N_EXPERTS = 32
TOP_K = 4
SWIGLU_ALPHA = 1.702
SWIGLU_LIMIT = 7.0

TM = 256
N_TILES = N_ROWS // TM
N_PROMPT_TILES = N_PROMPT // TM
N_ASSIGN = N_ROWS * TOP_K
N_EXPERT_TILES = N_ASSIGN // TM + N_EXPERTS
N_SORTED_ROWS = N_EXPERT_TILES * TM
LANES = 128
NEG_BIG = -1e30

W_SQ, W_SK, W_SV, W_GQ, W_GK, W_GV, W_GR, W_GA = 512, 128, 128, 256, 256, 512, 512, 16
OFF_SQ = 0
OFF_SK = OFF_SQ + W_SQ
OFF_SV = OFF_SK + W_SK
OFF_GQ = OFF_SV + W_SV
OFF_GK = OFF_GQ + W_GQ
OFF_GV = OFF_GK + W_GK
OFF_GR = OFF_GV + W_GV
OFF_GA = OFF_GR + W_GR
W_MAIN = OFF_GA


def _cparams(semantics, vmem_mib):
    return pltpu.CompilerParams(dimension_semantics=semantics, vmem_limit_bytes=vmem_mib * 2 ** 20)


def _dot(a, b):
    return jnp.dot(a, b, preferred_element_type=F32)


def _dot_nt(a, b):
    return lax.dot_general(a, b, (((1,), (1,)), ((), ())), preferred_element_type=F32)


def _dot_tn(a, b):
    return lax.dot_general(a, b, (((0,), (0,)), ((), ())), preferred_element_type=F32)


def _split_bf16(x):
    hi = x.astype(BF16)
    lo = (x - hi.astype(F32)).astype(BF16)
    return hi, lo


def _rms(x):
    return x * lax.rsqrt(jnp.mean(x * x, axis=-1, keepdims=True) + EPS)


def _proj_kernel(xp_ref, xs_ref, g_ref, w_ref, wga_ref, wa2_ref, ba_ref, qg_ref, kg_ref, bdq_ref, bdk_ref,
                 sq_ref, sk_ref, sv_ref, gq_ref, gk_ref, gv_ref, gg_ref, gr_ref):
    i = pl.program_id(0)
    x = jnp.where(i < N_PROMPT_TILES, xp_ref[...], xs_ref[...])
    xb = (_rms(x) * g_ref[...]).astype(BF16)

    def seg(off, width):
        return _dot(xb, w_ref[:, off:off + width])

    def head_norm(u, bd_ref):
        hi, lo = _split_bf16(u * u)
        ss = _dot(hi, bd_ref[...]) + _dot(lo, bd_ref[...])
        return u * lax.rsqrt(ss * (1.0 / HD) + EPS)

    sq_ref[...] = (head_norm(seg(OFF_SQ, W_SQ), bdq_ref) * qg_ref[...]).astype(BF16)
    sk_ref[...] = head_norm(seg(OFF_SK, W_SK), bdk_ref) * kg_ref[...]
    sv_ref[...] = seg(OFF_SV, W_SV)
    gq_ref[...] = (seg(OFF_GQ, W_GQ) * (DK ** -0.5)).astype(BF16)
    gk_ref[...] = seg(OFF_GK, W_GK).astype(BF16)
    gv_ref[...] = seg(OFF_GV, W_GV).astype(BF16)
    gr_ref[...] = seg(OFF_GR, W_GR).astype(BF16)
    ga = _dot(xb, wga_ref[...]).astype(BF16)
    z = _dot(ga, wa2_ref[...]) + ba_ref[...]
    log_sig = jnp.minimum(z, 0.0) - jnp.log(1.0 + jnp.exp(-jnp.abs(z)))
    gg_ref[...] = log_sig * (1.0 / GLA_TAU)


def _proj(xp, xs, norm_g, w_main, w_ga, w_a2, b_a, qg, kg, bdq, bdk):
    def rows(width):
        return pl.BlockSpec((TM, width), lambda i: (i, 0))

    def full(a):
        return pl.BlockSpec(a.shape, lambda i: (0,) * a.ndim)

    consts = (norm_g, w_main, w_ga, w_a2, b_a, qg, kg, bdq, bdk)
    out_widths = ((W_SQ, BF16), (W_SK, F32), (W_SV, F32), (W_GQ, BF16), (W_GK, BF16), (W_GV, BF16),
                  (W_GQ, F32), (W_GR, BF16))
    return pl.pallas_call(
        _proj_kernel,
        grid=(N_TILES,),
        in_specs=[pl.BlockSpec((TM, D_MODEL), lambda i: (jnp.minimum(i, N_PROMPT_TILES - 1), 0)),
                  pl.BlockSpec((TM, D_MODEL), lambda i: (0, 0))] + [full(a) for a in consts],
        out_specs=[rows(w) for w, _ in out_widths],
        out_shape=[jax.ShapeDtypeStruct((N_ROWS, w), dt) for w, dt in out_widths],
        compiler_params=_cparams(("arbitrary",), 40),
        name="proj",
    )(xp, xs, *consts)


def _dup_kv_heads(x):
    r = pltpu.roll(x, HD, axis=1)
    lo = lax.broadcasted_iota(I32, x.shape, 1) < HD
    out = []
    for a in (jnp.where(lo, x, r), jnp.where(lo, r, x)):
        out.append(jnp.concatenate([a, a], axis=1).astype(BF16))
    return out


def _swa_chunk(q_c, k_dup, v_dup, head_mask, sink_col, valid):
    tq = q_c.shape[0]
    lane_head = lax.broadcasted_iota(I32, (tq, 4 * HD), 1) // HD
    outs = []
    for j in range(2):
        qg = q_c[:, 4 * HD * j:4 * HD * (j + 1)]
        qm = jnp.concatenate([qg] * 4, axis=0) * head_mask
        s = _dot_nt(qm, k_dup[j])
        if valid is not None:
            s = jnp.where(valid, s, -jnp.inf)
        m = jnp.maximum(jnp.max(s, axis=1, keepdims=True), sink_col[j])
        p = jnp.exp(s - m)
        den = jnp.sum(p, axis=1, keepdims=True) + jnp.exp(sink_col[j] - m)
        o_full = _dot((p / den).astype(BF16), v_dup[j])
        o = jnp.zeros((tq, 4 * HD), F32)
        for a in range(4):
            o = o + jnp.where(lane_head == a, o_full[a * tq:(a + 1) * tq], 0.0)
        outs.append(o)
    return jnp.concatenate(outs, axis=1).astype(BF16)


def _swa_prompt_kernel(q_ref, kp_ref, kc_ref, vp_ref, vc_ref, hm_ref, sink_ref, o_ref):
    i = pl.program_id(0)
    k_dup = _dup_kv_heads(jnp.concatenate([kp_ref[...], kc_ref[...]], axis=0))
    v_dup = _dup_kv_heads(jnp.concatenate([vp_ref[...], vc_ref[...]], axis=0))
    sink_col = [sink_ref[j][:, 0:1] for j in range(2)]
    span = WINDOW + CHUNK
    col = lax.broadcasted_iota(I32, (4 * CHUNK, span), 1)
    for c in range(TM // CHUNK):
        lo = CHUNK * c
        valid = (i * TM - WINDOW + lo + col) >= 0
        o_ref[lo:lo + CHUNK, :] = _swa_chunk(
            q_ref[lo:lo + CHUNK, :], [k[lo:lo + span] for k in k_dup], [v[lo:lo + span] for v in v_dup],
            hm_ref[...], sink_col, valid)


def _swa_prompt(sq, sk, sv, head_mask, sink_b):
    half = TM // 2
    prev = pl.BlockSpec((half, 2 * HD), lambda i: (jnp.maximum(2 * i - 1, 0), 0))
    cur = pl.BlockSpec((TM, 2 * HD), lambda i: (i, 0))
    return pl.pallas_call(
        _swa_prompt_kernel,
        grid=(N_PROMPT_TILES,),
        in_specs=[pl.BlockSpec((TM, W_SQ), lambda i: (i, 0)), prev, cur, prev, cur,
                  pl.BlockSpec(head_mask.shape, lambda i: (0, 0)),
                  pl.BlockSpec(sink_b.shape, lambda i: (0, 0, 0))],
        out_specs=pl.BlockSpec((TM, W_SQ), lambda i: (i, 0)),
        out_shape=jax.ShapeDtypeStruct((N_PROMPT, W_SQ), BF16),
        compiler_params=_cparams(("arbitrary",), 32),
        name="swa_prompt",
    )(sq, sk, sk, sv, sv, head_mask, sink_b)


def _swa_sample_kernel(q_ref, kc_ref, kn_ref, vc_ref, vn_ref, hm_ref, sink_ref, o_ref):
    k_dup = _dup_kv_heads(jnp.concatenate([kc_ref[...], kn_ref[...]], axis=0))
    v_dup = _dup_kv_heads(jnp.concatenate([vc_ref[...], vn_ref[...]], axis=0))
    sink_col = [sink_ref[j][:, 0:1] for j in range(2)]
    o_ref[...] = _swa_chunk(q_ref[...], k_dup, v_dup, hm_ref[...], sink_col, None)


def _swa_sample(sq, sk, sv, cache_k, cache_v, head_mask, sink_b):
    first = N_PROMPT // T_SAMPLE
    new = lambda width: pl.BlockSpec((T_SAMPLE, width), lambda b: (first + b, 0))
    cache = pl.BlockSpec((None, WINDOW, 2 * HD), lambda b: (b, 0, 0))
    return pl.pallas_call(
        _swa_sample_kernel,
        grid=(N_STREAMS,),
        in_specs=[new(W_SQ), cache, new(2 * HD), cache, new(2 * HD),
                  pl.BlockSpec(head_mask.shape, lambda b: (0, 0)),
                  pl.BlockSpec(sink_b.shape, lambda b: (0, 0, 0))],
        out_specs=pl.BlockSpec((T_SAMPLE, W_SQ), lambda b: (b, 0)),
        out_shape=jax.ShapeDtypeStruct((N_SAMPLE, W_SQ), BF16),
        compiler_params=_cparams(("arbitrary",), 32),
        name="swa_sample",
    )(sq, cache_k, sk, cache_v, sv, head_mask, sink_b)


def _gla_kernel(q_ref, k_ref, v_ref, g_ref, gr_ref, s0_ref, tri_ref, m_ref, bd_ref, gn_ref,
                og_ref, sfin_ref, st_ref, *, t):
    c = pl.program_id(1)
    nb = t // SUB
    lanes = N_HEADS_GLA * DK

    @pl.when(c == 0)
    def _():
        st_ref[...] = s0_ref[...]

    q = q_ref[...].astype(F32)
    k = k_ref[...].astype(F32)
    v = v_ref[...]
    g_hi, g_lo = _split_bf16(g_ref[...])
    b = _dot(tri_ref[...], g_hi) + _dot(tri_ref[...], g_lo)
    b_ref_rows = jnp.broadcast_to(b.reshape(nb, SUB, lanes)[:, 0:1, :], (nb, SUB, lanes)).reshape(t, lanes)
    qd = (q * jnp.exp(b - b_ref_rows)).astype(BF16)
    row = lax.broadcasted_iota(I32, (t, lanes), 0)
    k_parts = []
    for blk in range(nb):
        arg = jnp.where(row < SUB * (blk + 1), b[SUB * blk:SUB * blk + 1, :] - b, NEG_BIG)
        k_parts.append((k * jnp.exp(arg)).astype(BF16))
    k_cat = jnp.concatenate(k_parts, axis=1)
    lhs = jnp.concatenate([jnp.concatenate([qd] * nb, axis=1)] * N_HEADS_GLA, axis=0) * m_ref[...]
    a = _dot_nt(lhs, k_cat)
    row_a = lax.broadcasted_iota(I32, a.shape, 0) & (t - 1)
    col_a = lax.broadcasted_iota(I32, a.shape, 1)
    a = jnp.where(row_a >= col_a, a, 0.0).astype(BF16)
    o_full = _dot(a, v)
    o_intra = jnp.concatenate(
        [o_full[h * t:(h + 1) * t, h * DV:(h + 1) * DV] for h in range(N_HEADS_GLA)], axis=1)
    st = st_ref[...]
    o = o_intra + _dot_nt((q * jnp.exp(b)).astype(BF16), st.astype(BF16))

    gr = gr_ref[...].astype(F32)
    gate = gr / (1.0 + jnp.exp(-gr))
    outs = []
    for h in range(N_HEADS_GLA):
        sl = slice(h * DV, (h + 1) * DV)
        outs.append(_rms(o[:, sl]) * gn_ref[...] * gate[:, sl])
    og_ref[...] = jnp.concatenate(outs, axis=1).astype(BF16)

    b_last = b[t - 1:t, :]
    k_last = (k * jnp.exp(b_last - b)).astype(BF16)
    st_new = st * jnp.exp(b_last) + _dot_tn(v, k_last) * bd_ref[...]
    st_ref[...] = st_new

    @pl.when(c == pl.num_programs(1) - 1)
    def _():
        sfin_ref[...] = st_new


def _gla(gq, gk, gv, gg, gr, s0, tri, mask, bd, gn, *, t, n_batch, n_chunks, first_block, name):
    def rows(width):
        return pl.BlockSpec((t, width), lambda b, c: (first_block + b * n_chunks + c, 0))

    def full(a):
        return pl.BlockSpec(a.shape, lambda b, c: (0,) * a.ndim)

    state = pl.BlockSpec((None,) + s0.shape[1:], lambda b, c: (b, 0, 0))
    return pl.pallas_call(
        functools.partial(_gla_kernel, t=t),
        grid=(n_batch, n_chunks),
        in_specs=[rows(W_GQ), rows(W_GK), rows(W_GV), rows(W_GQ), rows(W_GR), state,
                  full(tri), full(mask), full(bd), full(gn)],
        out_specs=[pl.BlockSpec((t, W_GV), lambda b, c: (b * n_chunks + c, 0)), state],
        out_shape=[jax.ShapeDtypeStruct((n_batch * n_chunks * t, W_GV), BF16),
                   jax.ShapeDtypeStruct(s0.shape, F32)],
        scratch_shapes=[pltpu.VMEM(s0.shape[1:], F32)],
        compiler_params=_cparams(("arbitrary", "arbitrary"), 32),
        name=name,
    )(gq, gk, gv, gg, gr, s0, tri, mask, bd, gn)


def _merge_kernel(oswp_ref, osws_ref, ogp_ref, ogs_ref, xp_ref, xs_ref, wo1_ref, wo2_ref, gf_ref, wrh_ref, wrl_ref,
                  br_ref, tri_ref, h_ref, xpk_ref, ti_ref, tw_ref, rk_ref, cnt_ref, base_ref):
    i = pl.program_id(0)

    @pl.when(i == 0)
    def _():
        base_ref[...] = jnp.zeros_like(base_ref)

    is_prompt = i < N_PROMPT_TILES
    x = jnp.where(is_prompt, xp_ref[...], xs_ref[...])
    o_swa = jnp.where(is_prompt, oswp_ref[...], osws_ref[...])
    og = jnp.where(is_prompt, ogp_ref[...], ogs_ref[...])
    h = x + (_dot(o_swa, wo1_ref[...]) + _dot(og, wo2_ref[...]))
    h_ref[...] = h
    xn = _rms(h) * gf_ref[...]
    x_hi, x_lo = _split_bf16(xn)

    bits = lax.bitcast_convert_type(x_hi.astype(F32), U32)
    half = D_MODEL // 2
    xpk_ref[...] = bits[:, half:] | (bits[:, :half] >> 16)

    logits = _dot(x_hi, wrh_ref[...]) + _dot(x_lo, wrh_ref[...]) + _dot(x_hi, wrl_ref[...]) + br_ref[...]
    lane = lax.broadcasted_iota(I32, logits.shape, 1)
    vals, hots = [], []
    ti = jnp.zeros(logits.shape, I32)
    for kk in range(TOP_K):
        m = jnp.max(logits, axis=1, keepdims=True)
        idx = jnp.min(jnp.where(logits == m, lane, LANES), axis=1, keepdims=True)
        hot = lane == idx
        logits = jnp.where(hot, NEG_BIG * 2, logits)
        vals.append(m)
        hots.append(hot)
        ti = jnp.where(lane == kk, idx, ti)
    ti_ref[...] = ti
    exps = [jnp.exp(v - vals[0]) for v in vals]
    den = exps[0] + exps[1] + exps[2] + exps[3]
    tw = jnp.zeros(logits.shape, F32)
    for kk in range(TOP_K):
        tw = jnp.where(lane == kk, exps[kk] / den, tw)
    tw_ref[...] = tw

    onehot = jnp.zeros(logits.shape, F32)
    for hot in hots:
        onehot = onehot + jnp.where(hot, 1.0, 0.0)
    before = _dot(tri_ref[...], onehot.astype(BF16)) + base_ref[0:1, :]
    rk = jnp.zeros(logits.shape, I32)
    for kk in range(TOP_K):
        r = jnp.sum(jnp.where(hots[kk], before, 0.0), axis=1, keepdims=True).astype(I32)
        rk = jnp.where(lane == kk, r, rk)
    rk_ref[...] = rk
    total = base_ref[...] + jnp.sum(onehot, axis=0, keepdims=True)
    base_ref[...] = total
    cnt_ref[...] = total.astype(I32)


def _merge(o_swa_p, o_swa_s, og_p, og_s, xp, xs, wo1, wo2, gf, wrh, wrl, br, tri):
    def rows(width):
        return pl.BlockSpec((TM, width), lambda i: (i, 0))

    def prompt_rows(width):
        return pl.BlockSpec((TM, width), lambda i: (jnp.minimum(i, N_PROMPT_TILES - 1), 0))

    def sample_rows(width):
        return pl.BlockSpec((TM, width), lambda i: (0, 0))

    def full(a):
        return pl.BlockSpec(a.shape, lambda i: (0,) * a.ndim)

    consts = (wo1, wo2, gf, wrh, wrl, br, tri)
    return pl.pallas_call(
        _merge_kernel,
        grid=(N_TILES,),
        in_specs=[prompt_rows(W_SQ), sample_rows(W_SQ), prompt_rows(W_GV), sample_rows(W_GV),
                  prompt_rows(D_MODEL), sample_rows(D_MODEL)] + [full(a) for a in consts],
        out_specs=[rows(D_MODEL), rows(D_MODEL // 2), rows(LANES), rows(LANES), rows(LANES),
                   pl.BlockSpec((8, LANES), lambda i: (0, 0))],
        out_shape=[jax.ShapeDtypeStruct((N_ROWS, D_MODEL), F32),
                   jax.ShapeDtypeStruct((N_ROWS, D_MODEL // 2), U32),
                   jax.ShapeDtypeStruct((N_ROWS, LANES), I32),
                   jax.ShapeDtypeStruct((N_ROWS, LANES), F32),
                   jax.ShapeDtypeStruct((N_ROWS, LANES), I32),
                   jax.ShapeDtypeStruct((8, LANES), I32)],
        scratch_shapes=[pltpu.VMEM((8, LANES), F32)],
        compiler_params=_cparams(("arbitrary",), 32),
        name="merge_router",
    )(o_swa_p, o_swa_s, og_p, og_s, xp, xs, *consts)


def _row_copy(src_ref, src_row, dst_ref, dst_row, sem):
    return pltpu.make_async_copy(src_ref.at[pl.ds(src_row, 1)], dst_ref.at[pl.ds(dst_row, 1)], sem)


def _dispatch_kernel(dest_ref, x_ref, xs_in_ref, xs_ref, sem):
    del xs_in_ref
    base = pl.program_id(0) * (TM * TOP_K)

    def issue(n, carry):
        for kk in range(TOP_K):
            _row_copy(x_ref, n, xs_ref, dest_ref[base + n * TOP_K + kk], sem).start()
        return carry

    lax.fori_loop(0, TM, issue, 0)

    def drain(n, carry):
        for kk in range(TOP_K):
            _row_copy(x_ref, 0, xs_ref, 0, sem).wait()
        return carry

    lax.fori_loop(0, TM, drain, 0)


def _dispatch(dest, x_packed, xs_init):
    return pl.pallas_call(
        _dispatch_kernel,
        grid_spec=pltpu.PrefetchScalarGridSpec(
            num_scalar_prefetch=1,
            grid=(N_TILES,),
            in_specs=[pl.BlockSpec((TM, D_MODEL // 2), lambda i, d: (i, 0)),
                      pl.BlockSpec(memory_space=pl.ANY)],
            out_specs=pl.BlockSpec(memory_space=pl.ANY),
            scratch_shapes=[pltpu.SemaphoreType.DMA],
        ),
        out_shape=jax.ShapeDtypeStruct(xs_init.shape, U32),
        input_output_aliases={2: 0},
        compiler_params=_cparams(("arbitrary",), 32),
        name="dispatch",
    )(dest, x_packed, xs_init)


CAST_ROWS = 128


def _moe_kernel(te_ref, nu_ref, xs_ref, wg_ref, bg_ref, wu_ref, bu_ref, wd_ref, bd_ref, ys_ref, wgb, wub, wdb):
    t = pl.program_id(0)
    e = te_ref[t]
    e_prev = te_ref[jnp.maximum(t - 1, 0)]

    @pl.when((t == 0) | (e != e_prev))
    def _():
        def cast(r, carry):
            sl = pl.ds(pl.multiple_of(r * CAST_ROWS, CAST_ROWS), CAST_ROWS)
            wgb[sl, :] = wg_ref[sl, :].astype(BF16)
            wub[sl, :] = wu_ref[sl, :].astype(BF16)
            wdb[sl, :] = wd_ref[sl, :].astype(BF16)
            return carry

        lax.fori_loop(0, D_MODEL // CAST_ROWS, cast, 0)

    @pl.when(t < nu_ref[0])
    def _():
        bits = xs_ref[...]
        lo = lax.bitcast_convert_type(bits << 16, F32).astype(BF16)
        hi = lax.bitcast_convert_type(bits & jnp.uint32(0xFFFF0000), F32).astype(BF16)
        x = jnp.concatenate([lo, hi], axis=1)
        gate = jnp.minimum(_dot(x, wgb[...]) + bg_ref[...], SWIGLU_LIMIT)
        up = jnp.clip(_dot(x, wub[...]) + bu_ref[...], -SWIGLU_LIMIT, SWIGLU_LIMIT)
        hdn = (up + 1.0) * gate * (1.0 / (1.0 + jnp.exp(-SWIGLU_ALPHA * gate)))
        ys_ref[...] = _dot(hdn.astype(BF16), wdb[...]) + bd_ref[...]

    @pl.when(t >= nu_ref[0])
    def _():
        ys_ref[...] = jnp.zeros_like(ys_ref)


def _moe(tile_expert, n_used, xs, w_gate, b_gate, w_up, b_up, w_down, b_down):
    weight = pl.BlockSpec((None, D_MODEL, D_MODEL), lambda t, te, nu: (te[t], 0, 0))
    bias = pl.BlockSpec((None, 1, D_MODEL), lambda t, te, nu: (te[t], 0, 0))
    return pl.pallas_call(
        _moe_kernel,
        grid_spec=pltpu.PrefetchScalarGridSpec(
            num_scalar_prefetch=2,
            grid=(N_EXPERT_TILES,),
            in_specs=[pl.BlockSpec((TM, D_MODEL // 2), lambda t, te, nu: (t, 0)),
                      weight, bias, weight, bias, weight, bias],
            out_specs=pl.BlockSpec((TM, D_MODEL), lambda t, te, nu: (t, 0)),
            scratch_shapes=[pltpu.VMEM((D_MODEL, D_MODEL), BF16)] * 3,
        ),
        out_shape=jax.ShapeDtypeStruct((N_SORTED_ROWS, D_MODEL), F32),
        compiler_params=_cparams(("arbitrary",), 52),
        name="experts",
    )(tile_expert, n_used, xs, w_gate, b_gate, w_up, b_up, w_down, b_down)


def _combine_kernel(dest_ref, h_ref, tw_ref, ys_ref, yp_ref, ysm_ref, g_ref, sem):
    i = pl.program_id(0)
    base = i * (TM * TOP_K)

    def issue(n, carry):
        for kk in range(TOP_K):
            _row_copy(ys_ref, dest_ref[base + n * TOP_K + kk], g_ref, kk * TM + n, sem).start()
        return carry

    lax.fori_loop(0, TM, issue, 0)

    def drain(n, carry):
        for kk in range(TOP_K):
            _row_copy(ys_ref, 0, g_ref, 0, sem).wait()
        return carry

    lax.fori_loop(0, TM, drain, 0)

    tw = tw_ref[...]
    acc = tw[:, 0:1] * g_ref[0:TM, :]
    for kk in range(1, TOP_K):
        acc = acc + tw[:, kk:kk + 1] * g_ref[kk * TM:(kk + 1) * TM, :]
    out = h_ref[...] + acc

    @pl.when(i < N_PROMPT_TILES)
    def _():
        yp_ref[...] = out

    @pl.when(i >= N_PROMPT_TILES)
    def _():
        ysm_ref[...] = out


def _combine(dest, h, tw, ys):
    return pl.pallas_call(
        _combine_kernel,
        grid_spec=pltpu.PrefetchScalarGridSpec(
            num_scalar_prefetch=1,
            grid=(N_TILES,),
            in_specs=[pl.BlockSpec((TM, D_MODEL), lambda i, d: (i, 0)),
                      pl.BlockSpec((TM, LANES), lambda i, d: (i, 0)),
                      pl.BlockSpec(memory_space=pl.ANY)],
            out_specs=[pl.BlockSpec((TM, D_MODEL), lambda i, d: (jnp.minimum(i, N_PROMPT_TILES - 1), 0)),
                       pl.BlockSpec((TM, D_MODEL), lambda i, d: (0, 0))],
            scratch_shapes=[pltpu.VMEM((TOP_K * TM, D_MODEL), F32), pltpu.SemaphoreType.DMA],
        ),
        out_shape=[jax.ShapeDtypeStruct((N_PROMPT, D_MODEL), F32),
                   jax.ShapeDtypeStruct((N_SAMPLE, D_MODEL), F32)],
        compiler_params=_cparams(("arbitrary",), 32),
        name="combine",
    )(dest, h, tw, ys)


def _block_diag_ones(n, blk):
    idx = np.arange(n) // blk
    return (idx[:, None] == idx[None, :]).astype(np.float32)


def _swa_head_mask(tq):
    row_head = np.arange(4 * tq)[:, None] // tq
    lane_head = np.arange(4 * HD)[None, :] // HD
    return jnp.asarray((row_head == lane_head).astype(np.float32), BF16)


def _gla_masks(t):
    nb = t // SUB
    lanes = N_HEADS_GLA * DK
    tri = jnp.asarray(np.tril(np.ones((t, t), np.float32)), BF16)
    row = np.arange(N_HEADS_GLA * t)
    col = np.arange(nb * lanes)
    same_head = (row[:, None] // t) == ((col[None, :] % lanes) // DK)
    same_blk = ((row[:, None] % t) // SUB) == (col[None, :] // lanes)
    mask = jnp.asarray((same_head & same_blk).astype(np.float32), BF16)
    return tri, mask


def _sink_rows(sinks, tq):
    s = jnp.repeat(sinks.astype(F32).reshape(2, 4), tq, axis=1)
    return jnp.broadcast_to(s[:, :, None], (2, 4 * tq, LANES))


def kernel(x_prompt, x_sample, state_gla, cache_swa_k, cache_swa_v, norm_mix_g, w_in, w_gla_a2, b_gla_a, q_norm_g,
           k_norm_g, swa_sinks, gla_norm_g, w_out, norm_ffn_g, w_router, b_router, w_gate, b_gate, w_up, b_up,
           w_down, b_down):
    xp = x_prompt.reshape(N_PROMPT, D_MODEL)
    xs = x_sample.reshape(N_SAMPLE, D_MODEL)

    w_in0 = w_in[0]
    w_main = w_in0[:, :W_MAIN].astype(BF16)
    w_ga = jnp.pad(w_in0[:, OFF_GA:], ((0, 0), (0, LANES - W_GA))).astype(BF16)
    w_a2 = jnp.pad(w_gla_a2[0], ((0, LANES - W_GA), (0, 0))).astype(BF16)
    b_a = b_gla_a[0].reshape(1, -1)
    qg = (jnp.tile(q_norm_g[0], N_HEADS_SWA) * (HD ** -0.5)).reshape(1, -1)
    kg = jnp.tile(k_norm_g[0], 2).reshape(1, -1)
    bdq = jnp.asarray(_block_diag_ones(W_SQ, HD), BF16)
    bdk = jnp.asarray(_block_diag_ones(W_SK, HD), BF16)

    sq, sk, sv, gq, gk, gv, gg, gr = _proj(xp, xs, norm_mix_g[0].reshape(1, -1), w_main, w_ga, w_a2, b_a,
                                           qg, kg, bdq, bdk)

    o_swa_p = _swa_prompt(sq, sk, sv, _swa_head_mask(CHUNK), _sink_rows(swa_sinks[0], CHUNK))
    cache_k = cache_swa_k[0].reshape(N_STREAMS, WINDOW, 2 * HD)
    cache_v = cache_swa_v[0].reshape(N_STREAMS, WINDOW, 2 * HD)
    o_swa_s = _swa_sample(sq, sk, sv, cache_k, cache_v, _swa_head_mask(T_SAMPLE),
                          _sink_rows(swa_sinks[0], T_SAMPLE))

    bd_state = jnp.asarray(_block_diag_ones(N_HEADS_GLA, 1).repeat(DV, axis=0).repeat(DK, axis=1), F32)
    gn = gla_norm_g[0].reshape(1, -1)
    tri_p, mask_p = _gla_masks(CHUNK)
    tri_s, mask_s = _gla_masks(T_SAMPLE)
    s0_p = jnp.zeros((1, N_HEADS_GLA * DV, N_HEADS_GLA * DK), F32)
    og_p, sfin_p = _gla(gq, gk, gv, gg, gr, s0_p, tri_p, mask_p, bd_state, gn, t=CHUNK, n_batch=1,
                        n_chunks=N_PROMPT // CHUNK, first_block=0, name="gla_prompt")
    eye = jnp.eye(N_HEADS_GLA, dtype=F32)
    s0_s = jnp.einsum('bhde,hg->bhegd', state_gla[0].astype(F32), eye).reshape(
        N_STREAMS, N_HEADS_GLA * DV, N_HEADS_GLA * DK)
    og_s, sfin_s = _gla(gq, gk, gv, gg, gr, s0_s, tri_s, mask_s, bd_state, gn, t=T_SAMPLE, n_batch=N_STREAMS,
                        n_chunks=1, first_block=N_PROMPT // T_SAMPLE, name="gla_sample")

    def unpack_state(sfin):
        s = sfin.reshape(-1, N_HEADS_GLA, DV, N_HEADS_GLA, DK)
        s = jnp.stack([s[:, h, :, h, :] for h in range(N_HEADS_GLA)], axis=1)
        return jnp.transpose(s, (0, 1, 3, 2))[None]

    w_out0 = w_out[0].astype(BF16)
    wr = jnp.pad(w_router[0], ((0, 0), (0, LANES - N_EXPERTS)))
    wr_hi = wr.astype(BF16)
    wr_lo = (wr - wr_hi.astype(F32)).astype(BF16)
    br = jnp.pad(b_router[0], (0, LANES - N_EXPERTS), constant_values=NEG_BIG).reshape(1, -1)
    tri_strict = jnp.asarray(np.tril(np.ones((TM, TM), np.float32), -1), BF16)
    h1, x_packed, top_i, top_w, rank, counts = _merge(
        o_swa_p, o_swa_s, og_p, og_s, xp, xs, w_out0[:W_SQ], w_out0[W_SQ:], norm_ffn_g[0].reshape(1, -1), wr_hi, wr_lo, br,
        tri_strict)

    counts = counts[0, :N_EXPERTS]
    padded = (counts + TM - 1) // TM * TM
    end = jnp.cumsum(padded)
    start = end - padded
    dest = (start[top_i[:, :TOP_K]] + rank[:, :TOP_K]).reshape(-1).astype(I32)
    n_used = (end[-1] // TM).astype(I32)
    tiles = jnp.arange(N_EXPERT_TILES, dtype=I32)
    tile_expert = jnp.searchsorted(end, jnp.minimum(tiles, n_used - 1) * TM, side='right').astype(I32)
    tile_expert = jnp.minimum(tile_expert, N_EXPERTS - 1)

    xs_sorted = _dispatch(dest, x_packed, jnp.zeros((N_SORTED_ROWS, D_MODEL // 2), U32))
    ys = _moe(tile_expert, n_used.reshape(1), xs_sorted, w_gate[0], b_gate[0].reshape(N_EXPERTS, 1, -1),
              w_up[0], b_up[0].reshape(N_EXPERTS, 1, -1), w_down[0], b_down[0].reshape(N_EXPERTS, 1, -1))
    y_p, y_s = _combine(dest, h1, top_w, ys)

    sk_s = sk[N_PROMPT:].reshape(N_STREAMS, T_SAMPLE, 2 * HD)
    sv_s = sv[N_PROMPT:].reshape(N_STREAMS, T_SAMPLE, 2 * HD)
    kc_s = jnp.concatenate([cache_k[:, T_SAMPLE:], sk_s], axis=1).reshape(1, N_STREAMS, WINDOW, 2, HD)
    vc_s = jnp.concatenate([cache_v[:, T_SAMPLE:], sv_s], axis=1).reshape(1, N_STREAMS, WINDOW, 2, HD)
    kc_p = sk[N_PROMPT - WINDOW:N_PROMPT].reshape(1, 1, WINDOW, 2, HD)
    vc_p = sv[N_PROMPT - WINDOW:N_PROMPT].reshape(1, 1, WINDOW, 2, HD)
    return (y_p.reshape(1, N_PROMPT, D_MODEL), y_s.reshape(N_STREAMS, T_SAMPLE, D_MODEL),
            unpack_state(sfin_p), kc_p, vc_p, unpack_state(sfin_s), kc_s, vc_s)
```

```python
import functools

import numpy as np
import jax
import jax.numpy as jnp
from jax import lax
from jax.experimental import pallas as pl
from jax.experimental.pallas import tpu as pltpu

F32 = jnp.float32
BF16 = jnp.bfloat16
I32 = jnp.int32
U32 = jnp.uint32

D_MODEL = 1024
N_PROMPT = 16384
N_STREAMS = 8
T_SAMPLE = 32
N_SAMPLE = N_STREAMS * T_SAMPLE
N_ROWS = N_PROMPT + N_SAMPLE
EPS = 1e-6

CHUNK = 64
SUB = 16
N_HEADS_SWA = 8
HD = 64
WINDOW = 128
N_HEADS_GLA = 4
DK = 64
DV = 128
GLA_TAU = 16.0
N_EXPERTS = 32
TOP_K = 4
SWIGLU_ALPHA = 1.702
SWIGLU_LIMIT = 7.0

TM = 256
N_TILES = N_ROWS // TM
N_PROMPT_TILES = N_PROMPT // TM
N_ASSIGN = N_ROWS * TOP_K
N_EXPERT_TILES = N_ASSIGN // TM + N_EXPERTS
N_SORTED_ROWS = N_EXPERT_TILES * TM
LANES = 128
SLOT_ROWS = 16
NEG_BIG = -1e30

W_SQ, W_SK, W_SV, W_GQ, W_GK, W_GV, W_GR, W_GA = 512, 128, 128, 256, 256, 512, 512, 16
OFF_SQ = 0
OFF_SK = OFF_SQ + W_SQ
OFF_SV = OFF_SK + W_SK
OFF_GQ = OFF_SV + W_SV
OFF_GK = OFF_GQ + W_GQ
OFF_GV = OFF_GK + W_GK
OFF_GR = OFF_GV + W_GV
OFF_GA = OFF_GR + W_GR
W_MAIN = OFF_GA


def _cparams(semantics, vmem_mib):
    return pltpu.CompilerParams(dimension_semantics=semantics, vmem_limit_bytes=vmem_mib * 2 ** 20)


def _dot(a, b):
    return jnp.dot(a, b, preferred_element_type=F32)


def _dot_nt(a, b):
    return lax.dot_general(a, b, (((1,), (1,)), ((), ())), preferred_element_type=F32)


def _dot_tn(a, b):
    return lax.dot_general(a, b, (((0,), (0,)), ((), ())), preferred_element_type=F32)


def _split_bf16(x):
    hi = x.astype(BF16)
    lo = (x - hi.astype(F32)).astype(BF16)
    return hi, lo


def _rms(x):
    return x * lax.rsqrt(jnp.mean(x * x, axis=-1, keepdims=True) + EPS)


ROW_SUB = 8
ROW_TILE = (ROW_SUB, LANES)


def _store_tile_rows(ref, x):
    for a in range(ROW_SUB):
        ref[:, a, :] = x[:, a * LANES:(a + 1) * LANES]


def _load_tile_rows(ref, lo, hi):
    return [ref[lo:hi, a, :] for a in range(ROW_SUB)]


def _proj_kernel(xp_ref, xs_ref, g_ref, w_ref, wga_ref, wa2_ref, ba_ref, qg_ref, kg_ref, bdq_ref, bdk_ref,
                 sq_ref, sk_ref, sv_ref, gq_ref, gk_ref, gv_ref, gg_ref, gr_ref):
    i = pl.program_id(0)
    x = jnp.where(i < N_PROMPT_TILES, xp_ref[...], xs_ref[...])
    xb = (_rms(x) * g_ref[...]).astype(BF16)

    def seg(off, width):
        return _dot(xb, w_ref[:, off:off + width])

    def head_norm(u, bd_ref):
        hi, lo = _split_bf16(u * u)
        ss = _dot(hi, bd_ref[...]) + _dot(lo, bd_ref[...])
        return u * lax.rsqrt(ss * (1.0 / HD) + EPS)

    sq_ref[...] = (head_norm(seg(OFF_SQ, W_SQ), bdq_ref) * qg_ref[...]).astype(BF16)
    sk_ref[...] = head_norm(seg(OFF_SK, W_SK), bdk_ref) * kg_ref[...]
    sv_ref[...] = seg(OFF_SV, W_SV)
    gq_ref[...] = (seg(OFF_GQ, W_GQ) * (DK ** -0.5)).astype(BF16)
    gk_ref[...] = seg(OFF_GK, W_GK).astype(BF16)
    gv_ref[...] = seg(OFF_GV, W_GV).astype(BF16)
    gr_ref[...] = seg(OFF_GR, W_GR).astype(BF16)
    ga = _dot(xb, wga_ref[...]).astype(BF16)
    z = _dot(ga, wa2_ref[...]) + ba_ref[...]
    log_sig = jnp.minimum(z, 0.0) - jnp.log(1.0 + jnp.exp(-jnp.abs(z)))
    gg_ref[...] = log_sig * (1.0 / GLA_TAU)


def _proj(xp, xs, norm_g, w_main, w_ga, w_a2, b_a, qg, kg, bdq, bdk):
    def rows(width):
        return pl.BlockSpec((TM, width), lambda i: (i, 0))

    def full(a):
        return pl.BlockSpec(a.shape, lambda i: (0,) * a.ndim)

    consts = (norm_g, w_main, w_ga, w_a2, b_a, qg, kg, bdq, bdk)
    out_widths = ((W_SQ, BF16), (W_SK, F32), (W_SV, F32), (W_GQ, BF16), (W_GK, BF16), (W_GV, BF16),
                  (W_GQ, F32), (W_GR, BF16))
    return pl.pallas_call(
        _proj_kernel,
        grid=(N_TILES,),
        in_specs=[pl.BlockSpec((TM, D_MODEL), lambda i: (jnp.minimum(i, N_PROMPT_TILES - 1), 0)),
                  pl.BlockSpec((TM, D_MODEL), lambda i: (0, 0))] + [full(a) for a in consts],
        out_specs=[rows(w) for w, _ in out_widths],
        out_shape=[jax.ShapeDtypeStruct((N_ROWS, w), dt) for w, dt in out_widths],
        compiler_params=_cparams(("arbitrary",), 40),
        name="proj",
    )(xp, xs, *consts)


def _dup_kv_heads(x):
    r = pltpu.roll(x, HD, axis=1)
    lo = lax.broadcasted_iota(I32, x.shape, 1) < HD
    out = []
    for a in (jnp.where(lo, x, r), jnp.where(lo, r, x)):
        out.append(jnp.concatenate([a, a], axis=1).astype(BF16))
    return out


def _swa_blocks(q_blocks, k_blocks, v_blocks, sinks, valids, head_mask):
    tq = q_blocks[0].shape[0]
    n_q = 4 * tq
    scores = []
    for q, k, valid in zip(q_blocks, k_blocks, valids):
        s_t = _dot_nt(k, jnp.concatenate([q] * 4, axis=0) * head_mask)
        scores.append(s_t if valid is None else jnp.where(valid, s_t, -jnp.inf))
    eye = (lax.broadcasted_iota(I32, (n_q, n_q), 0) == lax.broadcasted_iota(I32, (n_q, n_q), 1)
           ).astype(F32).astype(BF16)
    probs = []
    for s_t, sink in zip(scores, sinks):
        m = jnp.maximum(jnp.max(s_t, axis=0, keepdims=True), sink)
        p_t = jnp.exp(s_t - m)
        den = jnp.sum(p_t, axis=0, keepdims=True) + jnp.exp(sink - m)
        probs.append(_dot_nt(eye, (p_t / den).astype(BF16)).astype(BF16))
    lane_head = lax.broadcasted_iota(I32, (tq, 4 * HD), 1) // HD
    outs = []
    for p, v in zip(probs, v_blocks):
        o_full = _dot(p, v)
        o = jnp.zeros((tq, 4 * HD), F32)
        for a in range(4):
            o = o + jnp.where(lane_head == a, o_full[a * tq:(a + 1) * tq], 0.0)
        outs.append(o)
    return outs


def _swa_prompt_kernel(q_ref, kp_ref, kc_ref, vp_ref, vc_ref, hm_ref, sink_ref, o_ref):
    i = pl.program_id(0)
    k_dup = _dup_kv_heads(jnp.concatenate([kp_ref[...], kc_ref[...]], axis=0))
    v_dup = _dup_kv_heads(jnp.concatenate([vp_ref[...], vc_ref[...]], axis=0))
    sink_row = [sink_ref[j][0:1, :] for j in range(2)]
    span = WINDOW + CHUNK
    key = lax.broadcasted_iota(I32, (span, 4 * CHUNK), 0)
    qs, ks, vs, sinks, valids, where = [], [], [], [], [], []
    for c in range(TM // CHUNK):
        lo = CHUNK * c
        valid = (i * TM - WINDOW + lo + key) >= 0
        for j in range(2):
            qs.append(q_ref[lo:lo + CHUNK, 4 * HD * j:4 * HD * (j + 1)])
            ks.append(k_dup[j][lo:lo + span])
            vs.append(v_dup[j][lo:lo + span])
            sinks.append(sink_row[j])
            valids.append(valid)
            where.append((lo, j))
    outs = _swa_blocks(qs, ks, vs, sinks, valids, hm_ref[...])
    for (lo, j), o in zip(where, outs):
        o_ref[lo:lo + CHUNK, 4 * HD * j:4 * HD * (j + 1)] = o.astype(BF16)


def _swa_prompt(sq, sk, sv, head_mask, sink_b):
    half = TM // 2
    prev = pl.BlockSpec((half, 2 * HD), lambda i: (jnp.maximum(2 * i - 1, 0), 0))
    cur = pl.BlockSpec((TM, 2 * HD), lambda i: (i, 0))
    return pl.pallas_call(
        _swa_prompt_kernel,
        grid=(N_PROMPT_TILES,),
        in_specs=[pl.BlockSpec((TM, W_SQ), lambda i: (i, 0)), prev, cur, prev, cur,
                  pl.BlockSpec(head_mask.shape, lambda i: (0, 0)),
                  pl.BlockSpec(sink_b.shape, lambda i: (0, 0, 0))],
        out_specs=pl.BlockSpec((TM, W_SQ), lambda i: (i, 0)),
        out_shape=jax.ShapeDtypeStruct((N_PROMPT, W_SQ), BF16),
        compiler_params=_cparams(("arbitrary",), 32),
        name="swa_prompt",
    )(sq, sk, sk, sv, sv, head_mask, sink_b)


def _swa_sample_kernel(q_ref, kc_ref, kn_ref, vc_ref, vn_ref, hm_ref, sink_ref, o_ref):
    k_dup = _dup_kv_heads(jnp.concatenate([kc_ref[...], kn_ref[...]], axis=0))
    v_dup = _dup_kv_heads(jnp.concatenate([vc_ref[...], vn_ref[...]], axis=0))
    sink_row = [sink_ref[j][0:1, :] for j in range(2)]
    qs = [q_ref[:, 4 * HD * j:4 * HD * (j + 1)] for j in range(2)]
    outs = _swa_blocks(qs, k_dup, v_dup, sink_row, [None, None], hm_ref[...])
    o_ref[...] = jnp.concatenate(outs, axis=1).astype(BF16)


def _swa_sample(sq, sk, sv, cache_k, cache_v, head_mask, sink_b):
    first = N_PROMPT // T_SAMPLE
    new = lambda width: pl.BlockSpec((T_SAMPLE, width), lambda b: (first + b, 0))
    cache = pl.BlockSpec((None, WINDOW, 2 * HD), lambda b: (b, 0, 0))
    return pl.pallas_call(
        _swa_sample_kernel,
        grid=(N_STREAMS,),
        in_specs=[new(W_SQ), cache, new(2 * HD), cache, new(2 * HD),
                  pl.BlockSpec(head_mask.shape, lambda b: (0, 0)),
                  pl.BlockSpec(sink_b.shape, lambda b: (0, 0, 0))],
        out_specs=pl.BlockSpec((T_SAMPLE, W_SQ), lambda b: (b, 0)),
        out_shape=jax.ShapeDtypeStruct((N_SAMPLE, W_SQ), BF16),
        compiler_params=_cparams(("arbitrary",), 32),
        name="swa_sample",
    )(sq, cache_k, sk, cache_v, sv, head_mask, sink_b)


def _gla_kernel(q_ref, k_ref, v_ref, g_ref, gr_ref, s0_ref, tri_ref, m_ref, bd_ref, gn_ref,
                og_ref, sfin_ref, st_ref, *, t, n_sub):
    c = pl.program_id(1)
    nb = t // SUB
    lanes = N_HEADS_GLA * DK

    @pl.when(c == 0)
    def _():
        st_ref[...] = s0_ref[...]

    n_rows = t * n_sub

    def group_row(x, period, offset):
        g = x.reshape(n_rows // period, period, lanes)[:, offset:offset + 1, :]
        return jnp.broadcast_to(g, (n_rows // period, period, lanes)).reshape(n_rows, lanes)

    q = q_ref[...].astype(F32)
    k = k_ref[...].astype(F32)
    g_hi, g_lo = _split_bf16(g_ref[...])
    b = _dot(tri_ref[...], g_hi) + _dot(tri_ref[...], g_lo)
    qd = (q * jnp.exp(b - group_row(b, SUB, 0))).astype(BF16)
    pos = lax.broadcasted_iota(I32, (n_rows, lanes), 0) & (t - 1)
    k_parts = []
    for blk in range(nb):
        arg = jnp.where(pos < SUB * (blk + 1), group_row(b, t, SUB * blk) - b, NEG_BIG)
        k_parts.append((k * jnp.exp(arg)).astype(BF16))
    k_cat = jnp.concatenate(k_parts, axis=1)
    qd_cat = jnp.concatenate([qd] * nb, axis=1)
    q_dec = (q * jnp.exp(b)).astype(BF16)
    k_last = (k * jnp.exp(group_row(b, t, t - 1) - b)).astype(BF16)
    row_a = lax.broadcasted_iota(I32, (N_HEADS_GLA * t, t), 0) & (t - 1)
    col_a = lax.broadcasted_iota(I32, (N_HEADS_GLA * t, t), 1)

    o_intra, q_decayed, state_add, state_decay = [], [], [], []
    for u in range(n_sub):
        rows = slice(u * t, (u + 1) * t)
        v = v_ref[rows, :]
        lhs = jnp.concatenate([qd_cat[rows]] * N_HEADS_GLA, axis=0) * m_ref[...]
        a = _dot_nt(lhs, k_cat[rows])
        a = jnp.where(row_a >= col_a, a, 0.0).astype(BF16)
        o_full = _dot(a, v)
        o_intra.append(jnp.concatenate(
            [o_full[h * t:(h + 1) * t, h * DV:(h + 1) * DV] for h in range(N_HEADS_GLA)], axis=1))
        q_decayed.append(q_dec[rows])
        state_add.append(_dot_tn(v, k_last[rows]) * bd_ref[...])
        state_decay.append(jnp.exp(b[(u + 1) * t - 1:(u + 1) * t, :]))

    st = st_ref[...]
    o_parts = []
    for u in range(n_sub):
        o_parts.append(o_intra[u] + _dot_nt(q_decayed[u], st.astype(BF16)))
        st = st * state_decay[u] + state_add[u]
    st_ref[...] = st
    o = jnp.concatenate(o_parts, axis=0) if n_sub > 1 else o_parts[0]

    gr = gr_ref[...].astype(F32)
    gate = gr / (1.0 + jnp.exp(-gr))
    outs = []
    for h in range(N_HEADS_GLA):
        sl = slice(h * DV, (h + 1) * DV)
        outs.append(_rms(o[:, sl]) * gn_ref[...] * gate[:, sl])
    og_ref[...] = jnp.concatenate(outs, axis=1).astype(BF16)

    @pl.when(c == pl.num_programs(1) - 1)
    def _():
        sfin_ref[...] = st


def _gla(gq, gk, gv, gg, gr, s0, tri, mask, bd, gn, *, t, n_sub, n_batch, n_steps, first_block, name):
    rows_per_step = t * n_sub

    def rows(width):
        return pl.BlockSpec((rows_per_step, width), lambda b, c: (first_block + b * n_steps + c, 0))

    def full(a):
        return pl.BlockSpec(a.shape, lambda b, c: (0,) * a.ndim)

    state = pl.BlockSpec((None,) + s0.shape[1:], lambda b, c: (b, 0, 0))
    return pl.pallas_call(
        functools.partial(_gla_kernel, t=t, n_sub=n_sub),
        grid=(n_batch, n_steps),
        in_specs=[rows(W_GQ), rows(W_GK), rows(W_GV), rows(W_GQ), rows(W_GR), state,
                  full(tri), full(mask), full(bd), full(gn)],
        out_specs=[pl.BlockSpec((rows_per_step, W_GV), lambda b, c: (b * n_steps + c, 0)), state],
        out_shape=[jax.ShapeDtypeStruct((n_batch * n_steps * rows_per_step, W_GV), BF16),
                   jax.ShapeDtypeStruct(s0.shape, F32)],
        scratch_shapes=[pltpu.VMEM(s0.shape[1:], F32)],
        compiler_params=_cparams(("arbitrary", "arbitrary"), 32),
        name=name,
    )(gq, gk, gv, gg, gr, s0, tri, mask, bd, gn)


def _merge_kernel(oswp_ref, osws_ref, ogp_ref, ogs_ref, xp_ref, xs_ref, wo1_ref, wo2_ref, gf_ref, wrh_ref, wrl_ref,
                  br_ref, tri_ref, h_ref, xpk_ref, ti_ref, tw_ref, rk_ref, cnt_ref, base_ref):
    i = pl.program_id(0)

    @pl.when(i == 0)
    def _():
        base_ref[...] = jnp.zeros_like(base_ref)

    is_prompt = i < N_PROMPT_TILES
    x = jnp.where(is_prompt, xp_ref[...], xs_ref[...])
    o_swa = jnp.where(is_prompt, oswp_ref[...], osws_ref[...])
    og = jnp.where(is_prompt, ogp_ref[...], ogs_ref[...])
    h = x + (_dot(o_swa, wo1_ref[...]) + _dot(og, wo2_ref[...]))
    h_ref[...] = h
    xn = _rms(h) * gf_ref[...]
    x_hi, x_lo = _split_bf16(xn)
    _store_tile_rows(xpk_ref, xn)

    logits_t = (_dot_nt(wrh_ref[...], x_hi) + _dot_nt(wrh_ref[...], x_lo) + _dot_nt(wrl_ref[...], x_hi)
                + br_ref[:, 0:1])
    expert = lax.broadcasted_iota(I32, logits_t.shape, 0)
    slot = lax.broadcasted_iota(I32, (SLOT_ROWS, TM), 0)
    vals, hots = [], []
    ti = jnp.zeros((SLOT_ROWS, TM), I32)
    for kk in range(TOP_K):
        m = jnp.max(logits_t, axis=0, keepdims=True)
        idx = jnp.min(jnp.where(logits_t == m, expert, N_EXPERTS), axis=0, keepdims=True)
        hot = expert == idx
        logits_t = jnp.where(hot, NEG_BIG, logits_t)
        vals.append(m)
        hots.append(hot)
        ti = jnp.where(slot == kk, idx, ti)
    ti_ref[...] = ti
    exps = [jnp.exp(v - vals[0]) for v in vals]
    den = exps[0] + exps[1] + exps[2] + exps[3]
    tw_t = jnp.zeros((SLOT_ROWS, TM), F32)
    for kk in range(TOP_K):
        tw_t = jnp.where(slot == kk, exps[kk] / den, tw_t)
    eye = (lax.broadcasted_iota(I32, (SLOT_ROWS, LANES), 0)
           == lax.broadcasted_iota(I32, (SLOT_ROWS, LANES), 1)).astype(F32).astype(BF16)
    w_hi = tw_t.astype(BF16)
    w_mid, w_lo = _split_bf16(tw_t - w_hi.astype(F32))
    tw_ref[...] = _dot_tn(w_hi, eye) + _dot_tn(w_mid, eye) + _dot_tn(w_lo, eye)

    onehot_t = jnp.zeros(logits_t.shape, F32)
    for hot in hots:
        onehot_t = onehot_t + jnp.where(hot, 1.0, 0.0)
    before_t = _dot(onehot_t.astype(BF16), tri_ref[...]) + base_ref[:, 0:1]
    rk = jnp.zeros((SLOT_ROWS, TM), I32)
    for kk in range(TOP_K):
        r = jnp.sum(jnp.where(hots[kk], before_t, 0.0), axis=0, keepdims=True).astype(I32)
        rk = jnp.where(slot == kk, r, rk)
    rk_ref[...] = rk
    total = base_ref[...] + jnp.sum(onehot_t, axis=1, keepdims=True)
    base_ref[...] = total
    cnt_ref[...] = total.astype(I32)


def _merge(o_swa_p, o_swa_s, og_p, og_s, xp, xs, wo1, wo2, gf, wrh, wrl, br, tri):
    def rows(width):
        return pl.BlockSpec((TM, width), lambda i: (i, 0))

    def prompt_rows(width):
        return pl.BlockSpec((TM, width), lambda i: (jnp.minimum(i, N_PROMPT_TILES - 1), 0))

    def sample_rows(width):
        return pl.BlockSpec((TM, width), lambda i: (0, 0))

    def full(a):
        return pl.BlockSpec(a.shape, lambda i: (0,) * a.ndim)

    consts = (wo1, wo2, gf, wrh, wrl, br, tri)
    return pl.pallas_call(
        _merge_kernel,
        grid=(N_TILES,),
        in_specs=[prompt_rows(W_SQ), sample_rows(W_SQ), prompt_rows(W_GV), sample_rows(W_GV),
                  prompt_rows(D_MODEL), sample_rows(D_MODEL)] + [full(a) for a in consts],
        out_specs=[rows(D_MODEL), pl.BlockSpec((TM,) + ROW_TILE, lambda i: (i, 0, 0)),
                   pl.BlockSpec((SLOT_ROWS, TM), lambda i: (0, i)), rows(LANES),
                   pl.BlockSpec((SLOT_ROWS, TM), lambda i: (0, i)),
                   pl.BlockSpec((N_EXPERTS, LANES), lambda i: (0, 0))],
        out_shape=[jax.ShapeDtypeStruct((N_ROWS, D_MODEL), F32),
                   jax.ShapeDtypeStruct((N_ROWS,) + ROW_TILE, F32),
                   jax.ShapeDtypeStruct((SLOT_ROWS, N_ROWS), I32),
                   jax.ShapeDtypeStruct((N_ROWS, LANES), F32),
                   jax.ShapeDtypeStruct((SLOT_ROWS, N_ROWS), I32),
                   jax.ShapeDtypeStruct((N_EXPERTS, LANES), I32)],
        scratch_shapes=[pltpu.VMEM((N_EXPERTS, LANES), F32)],
        compiler_params=_cparams(("arbitrary",), 32),
        name="merge_router",
    )(o_swa_p, o_swa_s, og_p, og_s, xp, xs, *consts)


def _row_copy(src_ref, src_row, dst_ref, dst_row, sem):
    return pltpu.make_async_copy(src_ref.at[pl.ds(src_row, 1)], dst_ref.at[pl.ds(dst_row, 1)], sem)


def _dispatch_kernel(dest_ref, end_ref, x_ref, xs_ref, zero_ref, sem, zsem):
    i = pl.program_id(0)
    base = i * TM

    @pl.when(i == 0)
    def _():
        zero_ref[...] = jnp.zeros_like(zero_ref)

        def tail_copy(e):
            last = jnp.maximum(end_ref[e] - TM, 0)
            return pltpu.make_async_copy(zero_ref, xs_ref.at[pl.ds(pl.multiple_of(last, TM), TM)], zsem)

        def fill(e, carry):
            tail_copy(e).start()
            return carry

        def fill_wait(e, carry):
            tail_copy(e).wait()
            return carry

        lax.fori_loop(0, N_EXPERTS, fill, 0)
        lax.fori_loop(0, N_EXPERTS, fill_wait, 0)

        def unused_copy(t):
            return pltpu.make_async_copy(zero_ref, xs_ref.at[pl.ds(pl.multiple_of(t * TM, TM), TM)], zsem)

        def fill_unused(t, carry):
            unused_copy(t).start()
            unused_copy(t).wait()
            return carry

        lax.fori_loop(end_ref[N_EXPERTS - 1] // TM, N_EXPERT_TILES, fill_unused, 0)

    def issue(n, carry):
        for kk in range(TOP_K):
            _row_copy(x_ref, n, xs_ref, dest_ref[kk * N_ROWS + base + n], sem).start(priority=kk % 2)
        return carry

    lax.fori_loop(0, TM, issue, 0)

    def drain(n, carry):
        for kk in range(TOP_K):
            _row_copy(x_ref, 0, xs_ref, 0, sem).wait()
        return carry

    lax.fori_loop(0, TM, drain, 0)


def _dispatch(dest, end, x_packed):
    return pl.pallas_call(
        _dispatch_kernel,
        grid_spec=pltpu.PrefetchScalarGridSpec(
            num_scalar_prefetch=2,
            grid=(N_TILES,),
            in_specs=[pl.BlockSpec((TM,) + ROW_TILE, lambda i, d, e: (i, 0, 0))],
            out_specs=pl.BlockSpec(memory_space=pl.ANY),
            scratch_shapes=[pltpu.VMEM((TM,) + ROW_TILE, F32), pltpu.SemaphoreType.DMA,
                            pltpu.SemaphoreType.DMA],
        ),
        out_shape=jax.ShapeDtypeStruct((N_SORTED_ROWS,) + ROW_TILE, F32),
        compiler_params=_cparams(("arbitrary",), 32),
        name="dispatch",
    )(dest, end, x_packed)


CAST_ROWS = 128


def _moe_kernel(te_ref, nu_ref, xs_ref, wg_ref, bg_ref, wu_ref, bu_ref, wd_ref, bd_ref, ys_ref, wgb, wub, wdb):
    t = pl.program_id(0)
    e = te_ref[t]
    e_prev = te_ref[jnp.maximum(t - 1, 0)]

    @pl.when((t == 0) | (e != e_prev))
    def _():
        def cast(r, carry):
            sl = pl.ds(pl.multiple_of(r * CAST_ROWS, CAST_ROWS), CAST_ROWS)
            wgb[sl, :] = wg_ref[sl, :].astype(BF16)
            wub[sl, :] = wu_ref[sl, :].astype(BF16)
            wdb[sl, :] = wd_ref[sl, :].astype(BF16)
            return carry

        lax.fori_loop(0, D_MODEL // CAST_ROWS, cast, 0)

    @pl.when(t < nu_ref[0])
    def _():
        x = jnp.concatenate(_load_tile_rows(xs_ref, 0, TM), axis=1).astype(BF16)
        gate = jnp.minimum(_dot(x, wgb[...]) + bg_ref[...], SWIGLU_LIMIT)
        up = jnp.clip(_dot(x, wub[...]) + bu_ref[...], -SWIGLU_LIMIT, SWIGLU_LIMIT)
        hdn = (up + 1.0) * gate * (1.0 / (1.0 + jnp.exp(-SWIGLU_ALPHA * gate)))
        _store_tile_rows(ys_ref, _dot(hdn.astype(BF16), wdb[...]) + bd_ref[...])

    @pl.when(t >= nu_ref[0])
    def _():
        ys_ref[...] = jnp.zeros_like(ys_ref)


def _moe(tile_expert, n_used, xs, w_gate, b_gate, w_up, b_up, w_down, b_down):
    weight = pl.BlockSpec((None, D_MODEL, D_MODEL), lambda t, te, nu: (te[t], 0, 0))
    bias = pl.BlockSpec((None, 1, D_MODEL), lambda t, te, nu: (te[t], 0, 0))
    return pl.pallas_call(
        _moe_kernel,
        grid_spec=pltpu.PrefetchScalarGridSpec(
            num_scalar_prefetch=2,
            grid=(N_EXPERT_TILES,),
            in_specs=[pl.BlockSpec((TM,) + ROW_TILE, lambda t, te, nu: (jnp.minimum(t, nu[0] - 1), 0, 0)),
                      weight, bias, weight, bias, weight, bias],
            out_specs=pl.BlockSpec((TM,) + ROW_TILE, lambda t, te, nu: (t, 0, 0)),
            scratch_shapes=[pltpu.VMEM((D_MODEL, D_MODEL), BF16)] * 3,
        ),
        out_shape=jax.ShapeDtypeStruct((N_SORTED_ROWS,) + ROW_TILE, F32),
        compiler_params=_cparams(("arbitrary",), 52),
        name="experts",
    )(tile_expert, n_used, xs, w_gate, b_gate, w_up, b_up, w_down, b_down)


def _combine_kernel(dest_ref, h_ref, tw_ref, ys_ref, yp_ref, ysm_ref, g_ref, sem):
    i = pl.program_id(0)
    base = i * TM

    def issue(n, carry):
        for kk in range(TOP_K):
            _row_copy(ys_ref, dest_ref[kk * N_ROWS + base + n], g_ref, kk * TM + n, sem).start(priority=kk % 2)
        return carry

    lax.fori_loop(0, TM, issue, 0)

    def drain(n, carry):
        for kk in range(TOP_K):
            _row_copy(ys_ref, 0, g_ref, 0, sem).wait()
        return carry

    lax.fori_loop(0, TM, drain, 0)

    tw = tw_ref[...]
    acc = None
    for kk in range(TOP_K):
        w_col = tw[:, kk:kk + 1]
        part = [w_col * blk for blk in _load_tile_rows(g_ref, kk * TM, (kk + 1) * TM)]
        acc = part if acc is None else [a + p for a, p in zip(acc, part)]
    out = h_ref[...] + jnp.concatenate(acc, axis=1)

    @pl.when(i < N_PROMPT_TILES)
    def _():
        yp_ref[...] = out

    @pl.when(i >= N_PROMPT_TILES)
    def _():
        ysm_ref[...] = out


def _combine(dest, h, tw, ys):
    return pl.pallas_call(
        _combine_kernel,
        grid_spec=pltpu.PrefetchScalarGridSpec(
            num_scalar_prefetch=1,
            grid=(N_TILES,),
            in_specs=[pl.BlockSpec((TM, D_MODEL), lambda i, d: (i, 0)),
                      pl.BlockSpec((TM, LANES), lambda i, d: (i, 0)),
                      pl.BlockSpec(memory_space=pl.ANY)],
            out_specs=[pl.BlockSpec((TM, D_MODEL), lambda i, d: (jnp.minimum(i, N_PROMPT_TILES - 1), 0)),
                       pl.BlockSpec((TM, D_MODEL), lambda i, d: (0, 0))],
            scratch_shapes=[pltpu.VMEM((TOP_K * TM,) + ROW_TILE, F32), pltpu.SemaphoreType.DMA],
        ),
        out_shape=[jax.ShapeDtypeStruct((N_PROMPT, D_MODEL), F32),
                   jax.ShapeDtypeStruct((N_SAMPLE, D_MODEL), F32)],
        compiler_params=_cparams(("arbitrary",), 32),
        name="combine",
    )(dest, h, tw, ys)


def _block_diag_ones(n, blk):
    idx = np.arange(n) // blk
    return (idx[:, None] == idx[None, :]).astype(np.float32)


def _swa_head_mask(tq):
    row_head = np.arange(4 * tq)[:, None] // tq
    lane_head = np.arange(4 * HD)[None, :] // HD
    return jnp.asarray((row_head == lane_head).astype(np.float32), BF16)


def _gla_masks(t, n_sub):
    nb = t // SUB
    lanes = N_HEADS_GLA * DK
    tri = jnp.asarray(np.kron(np.eye(n_sub, dtype=np.float32), np.tril(np.ones((t, t), np.float32))), BF16)
    row = np.arange(N_HEADS_GLA * t)
    col = np.arange(nb * lanes)
    same_head = (row[:, None] // t) == ((col[None, :] % lanes) // DK)
    same_blk = ((row[:, None] % t) // SUB) == (col[None, :] // lanes)
    mask = jnp.asarray((same_head & same_blk).astype(np.float32), BF16)
    return tri, mask


def _sink_rows(sinks, tq):
    s = jnp.repeat(sinks.astype(F32).reshape(2, 4), tq, axis=1)
    return jnp.broadcast_to(s[:, None, :], (2, 8, 4 * tq))


def kernel(x_prompt, x_sample, state_gla, cache_swa_k, cache_swa_v, norm_mix_g, w_in, w_gla_a2, b_gla_a, q_norm_g,
           k_norm_g, swa_sinks, gla_norm_g, w_out, norm_ffn_g, w_router, b_router, w_gate, b_gate, w_up, b_up,
           w_down, b_down):
    xp = x_prompt.reshape(N_PROMPT, D_MODEL)
    xs = x_sample.reshape(N_SAMPLE, D_MODEL)

    w_in0 = w_in[0]
    w_main = w_in0[:, :W_MAIN].astype(BF16)
    w_ga = jnp.pad(w_in0[:, OFF_GA:], ((0, 0), (0, LANES - W_GA))).astype(BF16)
    w_a2 = jnp.pad(w_gla_a2[0], ((0, LANES - W_GA), (0, 0))).astype(BF16)
    b_a = b_gla_a[0].reshape(1, -1)
    qg = (jnp.tile(q_norm_g[0], N_HEADS_SWA) * (HD ** -0.5)).reshape(1, -1)
    kg = jnp.tile(k_norm_g[0], 2).reshape(1, -1)
    bdq = jnp.asarray(_block_diag_ones(W_SQ, HD), BF16)
    bdk = jnp.asarray(_block_diag_ones(W_SK, HD), BF16)

    sq, sk, sv, gq, gk, gv, gg, gr = _proj(xp, xs, norm_mix_g[0].reshape(1, -1), w_main, w_ga, w_a2, b_a,
                                           qg, kg, bdq, bdk)

    o_swa_p = _swa_prompt(sq, sk, sv, _swa_head_mask(CHUNK), _sink_rows(swa_sinks[0], CHUNK))
    cache_k = cache_swa_k[0].reshape(N_STREAMS, WINDOW, 2 * HD)
    cache_v = cache_swa_v[0].reshape(N_STREAMS, WINDOW, 2 * HD)
    o_swa_s = _swa_sample(sq, sk, sv, cache_k, cache_v, _swa_head_mask(T_SAMPLE),
                          _sink_rows(swa_sinks[0], T_SAMPLE))

    bd_state = jnp.asarray(_block_diag_ones(N_HEADS_GLA, 1).repeat(DV, axis=0).repeat(DK, axis=1), F32)
    gn = gla_norm_g[0].reshape(1, -1)
    tri_p, mask_p = _gla_masks(CHUNK, TM // CHUNK)
    tri_s, mask_s = _gla_masks(T_SAMPLE, 1)
    s0_p = jnp.zeros((1, N_HEADS_GLA * DV, N_HEADS_GLA * DK), F32)
    og_p, sfin_p = _gla(gq, gk, gv, gg, gr, s0_p, tri_p, mask_p, bd_state, gn, t=CHUNK, n_sub=TM // CHUNK,
                        n_batch=1, n_steps=N_PROMPT // TM, first_block=0, name="gla_prompt")
    eye = jnp.eye(N_HEADS_GLA, dtype=F32)
    s0_s = jnp.einsum('bhde,hg->bhegd', state_gla[0].astype(F32), eye).reshape(
        N_STREAMS, N_HEADS_GLA * DV, N_HEADS_GLA * DK)
    og_s, sfin_s = _gla(gq, gk, gv, gg, gr, s0_s, tri_s, mask_s, bd_state, gn, t=T_SAMPLE, n_sub=1,
                        n_batch=N_STREAMS, n_steps=1, first_block=N_PROMPT // T_SAMPLE, name="gla_sample")

    def unpack_state(sfin):
        s = sfin.reshape(-1, N_HEADS_GLA, DV, N_HEADS_GLA, DK)
        s = jnp.stack([s[:, h, :, h, :] for h in range(N_HEADS_GLA)], axis=1)
        return jnp.transpose(s, (0, 1, 3, 2))[None]

    w_out0 = w_out[0].astype(BF16)
    wr = w_router[0].T
    wr_hi = wr.astype(BF16)
    wr_lo = (wr - wr_hi.astype(F32)).astype(BF16)
    br = jnp.broadcast_to(b_router[0].astype(F32)[:, None], (N_EXPERTS, LANES))
    tri_strict = jnp.asarray(np.triu(np.ones((TM, TM), np.float32), 1), BF16)
    h1, x_packed, top_i, top_w, rank, counts = _merge(
        o_swa_p, o_swa_s, og_p, og_s, xp, xs, w_out0[:W_SQ], w_out0[W_SQ:], norm_ffn_g[0].reshape(1, -1), wr_hi, wr_lo, br,
        tri_strict)

    counts = counts[:, 0]
    padded = (counts + TM - 1) // TM * TM
    end = jnp.cumsum(padded)
    start = end - padded
    experts = jnp.arange(N_EXPERTS, dtype=I32)
    is_e = top_i[:TOP_K, :, None] == experts
    dest = (rank[:TOP_K] + jnp.sum(jnp.where(is_e, start, 0), axis=-1)).reshape(-1).astype(I32)
    n_used = (end[-1] // TM).astype(I32)
    tiles = jnp.minimum(jnp.arange(N_EXPERT_TILES, dtype=I32), n_used - 1)
    tile_expert = jnp.sum((tiles[:, None] * TM >= end[None, :]).astype(I32), axis=1)

    xs_sorted = _dispatch(dest, end.astype(I32), x_packed)
    ys = _moe(tile_expert, n_used.reshape(1), xs_sorted, w_gate[0], b_gate[0].reshape(N_EXPERTS, 1, -1),
              w_up[0], b_up[0].reshape(N_EXPERTS, 1, -1), w_down[0], b_down[0].reshape(N_EXPERTS, 1, -1))
    y_p, y_s = _combine(dest, h1, top_w, ys)

    sk_s = sk[N_PROMPT:].reshape(N_STREAMS, T_SAMPLE, 2 * HD)
    sv_s = sv[N_PROMPT:].reshape(N_STREAMS, T_SAMPLE, 2 * HD)
    kc_s = jnp.concatenate([cache_k[:, T_SAMPLE:], sk_s], axis=1).reshape(1, N_STREAMS, WINDOW, 2, HD)
    vc_s = jnp.concatenate([cache_v[:, T_SAMPLE:], sv_s], axis=1).reshape(1, N_STREAMS, WINDOW, 2, HD)
    kc_p = sk[N_PROMPT - WINDOW:N_PROMPT].reshape(1, 1, WINDOW, 2, HD)
    vc_p = sv[N_PROMPT - WINDOW:N_PROMPT].reshape(1, 1, WINDOW, 2, HD)
    return (y_p.reshape(1, N_PROMPT, D_MODEL), y_s.reshape(N_STREAMS, T_SAMPLE, D_MODEL),
            unpack_state(sfin_p), kc_p, vc_p, unpack_state(sfin_s), kc_s, vc_s)
```

```python
import functools

import numpy as np
import jax
import jax.numpy as jnp
from jax import lax
from jax.experimental import pallas as pl
from jax.experimental.pallas import tpu as pltpu

F32 = jnp.float32
BF16 = jnp.bfloat16
I32 = jnp.int32
U32 = jnp.uint32

D_MODEL = 1024
N_PROMPT = 16384
N_STREAMS = 8
T_SAMPLE = 32
N_SAMPLE = N_STREAMS * T_SAMPLE
N_ROWS = N_PROMPT + N_SAMPLE
EPS = 1e-6

CHUNK = 64
SUB = 16
N_HEADS_SWA = 8
HD = 64
WINDOW = 128
N_HEADS_GLA = 4
DK = 64
DV = 128
GLA_TAU = 16.0
N_EXPERTS = 32
TOP_K = 4
SWIGLU_ALPHA = 1.702
SWIGLU_LIMIT = 7.0

TM = 256
N_TILES = N_ROWS // TM
N_PROMPT_TILES = N_PROMPT // TM
N_ASSIGN = N_ROWS * TOP_K
N_EXPERT_TILES = N_ASSIGN // TM + N_EXPERTS
N_SORTED_ROWS = N_EXPERT_TILES * TM
LANES = 128
SLOT_ROWS = 16
NEG_BIG = -1e30

W_SQ, W_SK, W_SV, W_GQ, W_GK, W_GV, W_GR, W_GA = 512, 128, 128, 256, 256, 512, 512, 16
OFF_SQ = 0
OFF_SK = OFF_SQ + W_SQ
OFF_SV = OFF_SK + W_SK
OFF_GQ = OFF_SV + W_SV
OFF_GK = OFF_GQ + W_GQ
OFF_GV = OFF_GK + W_GK
OFF_GR = OFF_GV + W_GV
OFF_GA = OFF_GR + W_GR
W_MAIN = OFF_GA


def _cparams(semantics, vmem_mib):
    return pltpu.CompilerParams(dimension_semantics=semantics, vmem_limit_bytes=vmem_mib * 2 ** 20)


def _dot(a, b):
    return jnp.dot(a, b, preferred_element_type=F32)


def _dot_nt(a, b):
    return lax.dot_general(a, b, (((1,), (1,)), ((), ())), preferred_element_type=F32)


def _dot_tn(a, b):
    return lax.dot_general(a, b, (((0,), (0,)), ((), ())), preferred_element_type=F32)


def _split_bf16(x):
    hi = x.astype(BF16)
    lo = (x - hi.astype(F32)).astype(BF16)
    return hi, lo


def _rms(x):
    return x * lax.rsqrt(jnp.mean(x * x, axis=-1, keepdims=True) + EPS)


ROW_SUB = 8
ROW_TILE = (ROW_SUB, LANES)


def _load_tile_rows(ref, lo, hi):
    return [ref[lo:hi, a, :] for a in range(ROW_SUB)]


def _proj_kernel(xp_ref, xs_ref, g_ref, w_ref, wga_ref, wa2_ref, ba_ref, qg_ref, kg_ref, bdq_ref, bdk_ref,
                 sq_ref, sk_ref, sv_ref, gq_ref, gk_ref, gv_ref, gg_ref, gr_ref):
    i = pl.program_id(0)
    x = jnp.where(i < N_PROMPT_TILES, xp_ref[...], xs_ref[...])
    xb = (_rms(x) * g_ref[...]).astype(BF16)

    def seg(off, width):
        return _dot(xb, w_ref[:, off:off + width])

    def head_norm(u, bd_ref):
        hi, lo = _split_bf16(u * u)
        ss = _dot(hi, bd_ref[...]) + _dot(lo, bd_ref[...])
        return u * lax.rsqrt(ss * (1.0 / HD) + EPS)

    sq_ref[...] = (head_norm(seg(OFF_SQ, W_SQ), bdq_ref) * qg_ref[...]).astype(BF16)
    sk_ref[...] = head_norm(seg(OFF_SK, W_SK), bdk_ref) * kg_ref[...]
    sv_ref[...] = seg(OFF_SV, W_SV)
    gq_ref[...] = (seg(OFF_GQ, W_GQ) * (DK ** -0.5)).astype(BF16)
    gk_ref[...] = seg(OFF_GK, W_GK).astype(BF16)
    gv_ref[...] = seg(OFF_GV, W_GV).astype(BF16)
    gr_ref[...] = seg(OFF_GR, W_GR).astype(BF16)
    ga = _dot(xb, wga_ref[...]).astype(BF16)
    z = _dot(ga, wa2_ref[...]) + ba_ref[...]
    log_sig = jnp.minimum(z, 0.0) - jnp.log(1.0 + jnp.exp(-jnp.abs(z)))
    gg_ref[...] = log_sig * (1.0 / GLA_TAU)


def _proj(xp, xs, norm_g, w_main, w_ga, w_a2, b_a, qg, kg, bdq, bdk):
    def rows(width):
        return pl.BlockSpec((TM, width), lambda i: (i, 0))

    def full(a):
        return pl.BlockSpec(a.shape, lambda i: (0,) * a.ndim)

    consts = (norm_g, w_main, w_ga, w_a2, b_a, qg, kg, bdq, bdk)
    out_widths = ((W_SQ, BF16), (W_SK, F32), (W_SV, F32), (W_GQ, BF16), (W_GK, BF16), (W_GV, BF16),
                  (W_GQ, F32), (W_GR, BF16))
    return pl.pallas_call(
        _proj_kernel,
        grid=(N_TILES,),
        in_specs=[pl.BlockSpec((TM, D_MODEL), lambda i: (jnp.minimum(i, N_PROMPT_TILES - 1), 0)),
                  pl.BlockSpec((TM, D_MODEL), lambda i: (0, 0))] + [full(a) for a in consts],
        out_specs=[rows(w) for w, _ in out_widths],
        out_shape=[jax.ShapeDtypeStruct((N_ROWS, w), dt) for w, dt in out_widths],
        compiler_params=_cparams(("arbitrary",), 40),
        name="proj",
    )(xp, xs, *consts)


def _dup_kv_heads(x):
    r = pltpu.roll(x, HD, axis=1)
    lo = lax.broadcasted_iota(I32, x.shape, 1) < HD
    out = []
    for a in (jnp.where(lo, x, r), jnp.where(lo, r, x)):
        out.append(jnp.concatenate([a, a], axis=1).astype(BF16))
    return out


def _swa_blocks(q_blocks, k_blocks, v_blocks, sinks, valids, head_mask):
    tq = q_blocks[0].shape[0]
    n_q = 4 * tq
    scores = []
    for q, k, valid in zip(q_blocks, k_blocks, valids):
        s_t = _dot_nt(k, jnp.concatenate([q] * 4, axis=0) * head_mask)
        scores.append(s_t if valid is None else jnp.where(valid, s_t, -jnp.inf))
    eye = (lax.broadcasted_iota(I32, (n_q, n_q), 0) == lax.broadcasted_iota(I32, (n_q, n_q), 1)
           ).astype(F32).astype(BF16)
    probs = []
    for s_t, sink in zip(scores, sinks):
        m = jnp.maximum(jnp.max(s_t, axis=0, keepdims=True), sink)
        p_t = jnp.exp(s_t - m)
        den = jnp.sum(p_t, axis=0, keepdims=True) + jnp.exp(sink - m)
        probs.append(_dot_nt(eye, (p_t / den).astype(BF16)).astype(BF16))
    lane_head = lax.broadcasted_iota(I32, (tq, 4 * HD), 1) // HD
    outs = []
    for p, v in zip(probs, v_blocks):
        o_full = _dot(p, v)
        o = jnp.zeros((tq, 4 * HD), F32)
        for a in range(4):
            o = o + jnp.where(lane_head == a, o_full[a * tq:(a + 1) * tq], 0.0)
        outs.append(o)
    return outs


def _swa_prompt_kernel(q_ref, kp_ref, kc_ref, vp_ref, vc_ref, hm_ref, sink_ref, o_ref):
    i = pl.program_id(0)
    k_dup = _dup_kv_heads(jnp.concatenate([kp_ref[...], kc_ref[...]], axis=0))
    v_dup = _dup_kv_heads(jnp.concatenate([vp_ref[...], vc_ref[...]], axis=0))
    sink_row = [sink_ref[j][0:1, :] for j in range(2)]
    span = WINDOW + CHUNK
    key = lax.broadcasted_iota(I32, (span, 4 * CHUNK), 0)
    qs, ks, vs, sinks, valids, where = [], [], [], [], [], []
    for c in range(TM // CHUNK):
        lo = CHUNK * c
        valid = (i * TM - WINDOW + lo + key) >= 0
        for j in range(2):
            qs.append(q_ref[lo:lo + CHUNK, 4 * HD * j:4 * HD * (j + 1)])
            ks.append(k_dup[j][lo:lo + span])
            vs.append(v_dup[j][lo:lo + span])
            sinks.append(sink_row[j])
            valids.append(valid)
            where.append((lo, j))
    outs = _swa_blocks(qs, ks, vs, sinks, valids, hm_ref[...])
    for (lo, j), o in zip(where, outs):
        o_ref[lo:lo + CHUNK, 4 * HD * j:4 * HD * (j + 1)] = o.astype(BF16)


def _swa_prompt(sq, sk, sv, head_mask, sink_b):
    half = TM // 2
    prev = pl.BlockSpec((half, 2 * HD), lambda i: (jnp.maximum(2 * i - 1, 0), 0))
    cur = pl.BlockSpec((TM, 2 * HD), lambda i: (i, 0))
    return pl.pallas_call(
        _swa_prompt_kernel,
        grid=(N_PROMPT_TILES,),
        in_specs=[pl.BlockSpec((TM, W_SQ), lambda i: (i, 0)), prev, cur, prev, cur,
                  pl.BlockSpec(head_mask.shape, lambda i: (0, 0)),
                  pl.BlockSpec(sink_b.shape, lambda i: (0, 0, 0))],
        out_specs=pl.BlockSpec((TM, W_SQ), lambda i: (i, 0)),
        out_shape=jax.ShapeDtypeStruct((N_PROMPT, W_SQ), BF16),
        compiler_params=_cparams(("arbitrary",), 32),
        name="swa_prompt",
    )(sq, sk, sk, sv, sv, head_mask, sink_b)


def _swa_sample_kernel(q_ref, kc_ref, kn_ref, vc_ref, vn_ref, hm_ref, sink_ref, o_ref):
    k_dup = _dup_kv_heads(jnp.concatenate([kc_ref[...], kn_ref[...]], axis=0))
    v_dup = _dup_kv_heads(jnp.concatenate([vc_ref[...], vn_ref[...]], axis=0))
    sink_row = [sink_ref[j][0:1, :] for j in range(2)]
    qs = [q_ref[:, 4 * HD * j:4 * HD * (j + 1)] for j in range(2)]
    outs = _swa_blocks(qs, k_dup, v_dup, sink_row, [None, None], hm_ref[...])
    o_ref[...] = jnp.concatenate(outs, axis=1).astype(BF16)


def _swa_sample(sq, sk, sv, cache_k, cache_v, head_mask, sink_b):
    first = N_PROMPT // T_SAMPLE
    new = lambda width: pl.BlockSpec((T_SAMPLE, width), lambda b: (first + b, 0))
    cache = pl.BlockSpec((None, WINDOW, 2 * HD), lambda b: (b, 0, 0))
    return pl.pallas_call(
        _swa_sample_kernel,
        grid=(N_STREAMS,),
        in_specs=[new(W_SQ), cache, new(2 * HD), cache, new(2 * HD),
                  pl.BlockSpec(head_mask.shape, lambda b: (0, 0)),
                  pl.BlockSpec(sink_b.shape, lambda b: (0, 0, 0))],
        out_specs=pl.BlockSpec((T_SAMPLE, W_SQ), lambda b: (b, 0)),
        out_shape=jax.ShapeDtypeStruct((N_SAMPLE, W_SQ), BF16),
        compiler_params=_cparams(("arbitrary",), 32),
        name="swa_sample",
    )(sq, cache_k, sk, cache_v, sv, head_mask, sink_b)


def _gla_kernel(q_ref, k_ref, v_ref, g_ref, gr_ref, s0_ref, tri_ref, m_ref, bd_ref, gn_ref,
                og_ref, sfin_ref, st_ref, *, t, n_sub):
    c = pl.program_id(1)
    nb = t // SUB
    lanes = N_HEADS_GLA * DK

    @pl.when(c == 0)
    def _():
        st_ref[...] = s0_ref[...]

    n_rows = t * n_sub

    def group_row(x, period, offset):
        g = x.reshape(n_rows // period, period, lanes)[:, offset:offset + 1, :]
        return jnp.broadcast_to(g, (n_rows // period, period, lanes)).reshape(n_rows, lanes)

    q = q_ref[...].astype(F32)
    k = k_ref[...].astype(F32)
    g_hi, g_lo = _split_bf16(g_ref[...])
    b = _dot(tri_ref[...], g_hi) + _dot(tri_ref[...], g_lo)
    qd = (q * jnp.exp(b - group_row(b, SUB, 0))).astype(BF16)
    pos = lax.broadcasted_iota(I32, (n_rows, lanes), 0) & (t - 1)
    k_parts = []
    for blk in range(nb):
        arg = jnp.where(pos < SUB * (blk + 1), group_row(b, t, SUB * blk) - b, NEG_BIG)
        k_parts.append((k * jnp.exp(arg)).astype(BF16))
    k_cat = jnp.concatenate(k_parts, axis=1)
    qd_cat = jnp.concatenate([qd] * nb, axis=1)
    q_dec = (q * jnp.exp(b)).astype(BF16)
    k_last = (k * jnp.exp(group_row(b, t, t - 1) - b)).astype(BF16)
    row_a = lax.broadcasted_iota(I32, (N_HEADS_GLA * t, t), 0) & (t - 1)
    col_a = lax.broadcasted_iota(I32, (N_HEADS_GLA * t, t), 1)

    o_intra, q_decayed, state_add, state_decay = [], [], [], []
    for u in range(n_sub):
        rows = slice(u * t, (u + 1) * t)
        v = v_ref[rows, :]
        lhs = jnp.concatenate([qd_cat[rows]] * N_HEADS_GLA, axis=0) * m_ref[...]
        a = _dot_nt(lhs, k_cat[rows])
        a = jnp.where(row_a >= col_a, a, 0.0).astype(BF16)
        o_full = _dot(a, v)
        o_intra.append(jnp.concatenate(
            [o_full[h * t:(h + 1) * t, h * DV:(h + 1) * DV] for h in range(N_HEADS_GLA)], axis=1))
        q_decayed.append(q_dec[rows])
        state_add.append(_dot_tn(v, k_last[rows]) * bd_ref[...])
        state_decay.append(jnp.exp(b[(u + 1) * t - 1:(u + 1) * t, :]))

    st = st_ref[...]
    o_parts = []
    for u in range(n_sub):
        o_parts.append(o_intra[u] + _dot_nt(q_decayed[u], st.astype(BF16)))
        st = st * state_decay[u] + state_add[u]
    st_ref[...] = st
    o = jnp.concatenate(o_parts, axis=0) if n_sub > 1 else o_parts[0]

    gr = gr_ref[...].astype(F32)
    gate = gr / (1.0 + jnp.exp(-gr))
    outs = []
    for h in range(N_HEADS_GLA):
        sl = slice(h * DV, (h + 1) * DV)
        outs.append(_rms(o[:, sl]) * gn_ref[...] * gate[:, sl])
    og_ref[...] = jnp.concatenate(outs, axis=1).astype(BF16)

    @pl.when(c == pl.num_programs(1) - 1)
    def _():
        sfin_ref[...] = st


def _gla(gq, gk, gv, gg, gr, s0, tri, mask, bd, gn, *, t, n_sub, n_batch, n_steps, first_block, name):
    rows_per_step = t * n_sub

    def rows(width):
        return pl.BlockSpec((rows_per_step, width), lambda b, c: (first_block + b * n_steps + c, 0))

    def full(a):
        return pl.BlockSpec(a.shape, lambda b, c: (0,) * a.ndim)

    state = pl.BlockSpec((None,) + s0.shape[1:], lambda b, c: (b, 0, 0))
    return pl.pallas_call(
        functools.partial(_gla_kernel, t=t, n_sub=n_sub),
        grid=(n_batch, n_steps),
        in_specs=[rows(W_GQ), rows(W_GK), rows(W_GV), rows(W_GQ), rows(W_GR), state,
                  full(tri), full(mask), full(bd), full(gn)],
        out_specs=[pl.BlockSpec((rows_per_step, W_GV), lambda b, c: (b * n_steps + c, 0)), state],
        out_shape=[jax.ShapeDtypeStruct((n_batch * n_steps * rows_per_step, W_GV), BF16),
                   jax.ShapeDtypeStruct(s0.shape, F32)],
        scratch_shapes=[pltpu.VMEM(s0.shape[1:], F32)],
        compiler_params=_cparams(("arbitrary", "arbitrary"), 32),
        name=name,
    )(gq, gk, gv, gg, gr, s0, tri, mask, bd, gn)


def _merge_kernel(oswp_ref, osws_ref, ogp_ref, ogs_ref, xp_ref, xs_ref, wo1_ref, wo2_ref, gf_ref, wrh_ref, wrl_ref,
                  br_ref, tri_ref, h_ref, xpk_ref, ti_ref, tw_ref, rk_ref, cnt_ref, base_ref, xn_buf, xn_sem):
    i = pl.program_id(0)

    @pl.when(i == 0)
    def _():
        base_ref[...] = jnp.zeros_like(base_ref)

    is_prompt = i < N_PROMPT_TILES
    x = jnp.where(is_prompt, xp_ref[...], xs_ref[...])
    o_swa = jnp.where(is_prompt, oswp_ref[...], osws_ref[...])
    og = jnp.where(is_prompt, ogp_ref[...], ogs_ref[...])
    h = x + (_dot(o_swa, wo1_ref[...]) + _dot(og, wo2_ref[...]))
    h_ref[...] = h
    xn = _rms(h) * gf_ref[...]
    x_hi, x_lo = _split_bf16(xn)

    slot = lax.rem(i, 2)

    def store(tile, s):
        return _tile_row_copies(xpk_ref, tile, xn_buf.at[s], xn_sem.at[s], to_hbm=True)

    @pl.when(i >= 2)
    def _():
        for c in store(i - 2, slot):
            c.wait()

    xn_buf[slot] = xn
    for c in store(i, slot):
        c.start()

    @pl.when(i == N_TILES - 1)
    def _():
        for c in store(i - 1, 1 - slot) + store(i, slot):
            c.wait()

    logits_t = (_dot_nt(wrh_ref[...], x_hi) + _dot_nt(wrh_ref[...], x_lo) + _dot_nt(wrl_ref[...], x_hi)
                + br_ref[:, 0:1])
    expert = lax.broadcasted_iota(I32, logits_t.shape, 0)
    slot = lax.broadcasted_iota(I32, (SLOT_ROWS, TM), 0)
    vals, hots = [], []
    ti = jnp.zeros((SLOT_ROWS, TM), I32)
    for kk in range(TOP_K):
        m = jnp.max(logits_t, axis=0, keepdims=True)
        idx = jnp.min(jnp.where(logits_t == m, expert, N_EXPERTS), axis=0, keepdims=True)
        hot = expert == idx
        logits_t = jnp.where(hot, NEG_BIG, logits_t)
        vals.append(m)
        hots.append(hot)
        ti = jnp.where(slot == kk, idx, ti)
    ti_ref[...] = ti
    exps = [jnp.exp(v - vals[0]) for v in vals]
    den = exps[0] + exps[1] + exps[2] + exps[3]
    tw_t = jnp.zeros((SLOT_ROWS, TM), F32)
    for kk in range(TOP_K):
        tw_t = jnp.where(slot == kk, exps[kk] / den, tw_t)
    eye = (lax.broadcasted_iota(I32, (SLOT_ROWS, LANES), 0)
           == lax.broadcasted_iota(I32, (SLOT_ROWS, LANES), 1)).astype(F32).astype(BF16)
    w_hi = tw_t.astype(BF16)
    w_mid, w_lo = _split_bf16(tw_t - w_hi.astype(F32))
    tw_ref[...] = _dot_tn(w_hi, eye) + _dot_tn(w_mid, eye) + _dot_tn(w_lo, eye)

    onehot_t = jnp.zeros(logits_t.shape, F32)
    for hot in hots:
        onehot_t = onehot_t + jnp.where(hot, 1.0, 0.0)
    before_t = _dot(onehot_t.astype(BF16), tri_ref[...]) + base_ref[:, 0:1]
    rk = jnp.zeros((SLOT_ROWS, TM), I32)
    for kk in range(TOP_K):
        r = jnp.sum(jnp.where(hots[kk], before_t, 0.0), axis=0, keepdims=True).astype(I32)
        rk = jnp.where(slot == kk, r, rk)
    rk_ref[...] = rk
    total = base_ref[...] + jnp.sum(onehot_t, axis=1, keepdims=True)
    base_ref[...] = total
    cnt_ref[...] = total.astype(I32)


def _merge(o_swa_p, o_swa_s, og_p, og_s, xp, xs, wo1, wo2, gf, wrh, wrl, br, tri):
    def rows(width):
        return pl.BlockSpec((TM, width), lambda i: (i, 0))

    def prompt_rows(width):
        return pl.BlockSpec((TM, width), lambda i: (jnp.minimum(i, N_PROMPT_TILES - 1), 0))

    def sample_rows(width):
        return pl.BlockSpec((TM, width), lambda i: (0, 0))

    def full(a):
        return pl.BlockSpec(a.shape, lambda i: (0,) * a.ndim)

    consts = (wo1, wo2, gf, wrh, wrl, br, tri)
    return pl.pallas_call(
        _merge_kernel,
        grid=(N_TILES,),
        in_specs=[prompt_rows(W_SQ), sample_rows(W_SQ), prompt_rows(W_GV), sample_rows(W_GV),
                  prompt_rows(D_MODEL), sample_rows(D_MODEL)] + [full(a) for a in consts],
        out_specs=[rows(D_MODEL), pl.BlockSpec(memory_space=pl.ANY),
                   pl.BlockSpec((SLOT_ROWS, TM), lambda i: (0, i)), rows(LANES),
                   pl.BlockSpec((SLOT_ROWS, TM), lambda i: (0, i)),
                   pl.BlockSpec((N_EXPERTS, LANES), lambda i: (0, 0))],
        out_shape=[jax.ShapeDtypeStruct((N_ROWS, D_MODEL), F32),
                   jax.ShapeDtypeStruct((N_ROWS,) + ROW_TILE, F32),
                   jax.ShapeDtypeStruct((SLOT_ROWS, N_ROWS), I32),
                   jax.ShapeDtypeStruct((N_ROWS, LANES), F32),
                   jax.ShapeDtypeStruct((SLOT_ROWS, N_ROWS), I32),
                   jax.ShapeDtypeStruct((N_EXPERTS, LANES), I32)],
        scratch_shapes=[pltpu.VMEM((N_EXPERTS, LANES), F32), pltpu.VMEM((2, TM, D_MODEL), F32),
                        pltpu.SemaphoreType.DMA((2,))],
        compiler_params=_cparams(("arbitrary",), 32),
        name="merge_router",
    )(o_swa_p, o_swa_s, og_p, og_s, xp, xs, *consts)


def _row_copy(src_ref, src_row, dst_ref, dst_row, sem):
    return pltpu.make_async_copy(src_ref.at[pl.ds(src_row, 1)], dst_ref.at[pl.ds(dst_row, 1)], sem)


def _dispatch_kernel(dest_ref, end_ref, x_ref, xs_ref, zero_ref, sem, zsem):
    i = pl.program_id(0)
    base = i * TM

    @pl.when(i == 0)
    def _():
        zero_ref[...] = jnp.zeros_like(zero_ref)

        def tail_copy(e):
            last = jnp.maximum(end_ref[e] - TM, 0)
            return pltpu.make_async_copy(zero_ref, xs_ref.at[pl.ds(pl.multiple_of(last, TM), TM)], zsem)

        def fill(e, carry):
            tail_copy(e).start()
            return carry

        def fill_wait(e, carry):
            tail_copy(e).wait()
            return carry

        lax.fori_loop(0, N_EXPERTS, fill, 0)
        lax.fori_loop(0, N_EXPERTS, fill_wait, 0)

        def unused_copy(t):
            return pltpu.make_async_copy(zero_ref, xs_ref.at[pl.ds(pl.multiple_of(t * TM, TM), TM)], zsem)

        def fill_unused(t, carry):
            unused_copy(t).start()
            unused_copy(t).wait()
            return carry

        lax.fori_loop(end_ref[N_EXPERTS - 1] // TM, N_EXPERT_TILES, fill_unused, 0)

    def issue(n, carry):
        for kk in range(TOP_K):
            _row_copy(x_ref, n, xs_ref, dest_ref[kk * N_ROWS + base + n], sem).start(priority=kk % 2)
        return carry

    lax.fori_loop(0, TM, issue, 0)

    for kk in range(TOP_K):
        pltpu.make_async_copy(x_ref, xs_ref.at[pl.ds(0, TM)], sem).wait()


def _dispatch(dest, end, x_packed):
    return pl.pallas_call(
        _dispatch_kernel,
        grid_spec=pltpu.PrefetchScalarGridSpec(
            num_scalar_prefetch=2,
            grid=(N_TILES,),
            in_specs=[pl.BlockSpec((TM,) + ROW_TILE, lambda i, d, e: (i, 0, 0))],
            out_specs=pl.BlockSpec(memory_space=pl.ANY),
            scratch_shapes=[pltpu.VMEM((TM,) + ROW_TILE, F32), pltpu.SemaphoreType.DMA,
                            pltpu.SemaphoreType.DMA],
        ),
        out_shape=jax.ShapeDtypeStruct((N_SORTED_ROWS,) + ROW_TILE, F32),
        compiler_params=_cparams(("arbitrary",), 32),
        name="dispatch",
    )(dest, end, x_packed)


CAST_ROWS = 128


def _tile_row_copies(hbm_ref, tile, vmem_ref, sem, to_hbm):
    copies = []
    for a in range(ROW_SUB):
        h = hbm_ref.at[pl.ds(tile * TM, TM), a, :]
        v = vmem_ref.at[:, pl.ds(a * LANES, LANES)]
        copies.append(pltpu.make_async_copy(v, h, sem) if to_hbm else pltpu.make_async_copy(h, v, sem))
    return copies


def _moe_kernel(te_ref, nu_ref, xs_ref, wg_ref, bg_ref, wu_ref, bu_ref, wd_ref, bd_ref, ys_ref, wgb, wub, wdb,
                x_buf, y_buf, zero_buf, in_sem, out_sem, zero_sem):
    t = pl.program_id(0)
    n_used = nu_ref[0]
    slot = lax.rem(t, 2)
    e = te_ref[t]
    e_prev = te_ref[jnp.maximum(t - 1, 0)]

    def load(tile, s):
        return _tile_row_copies(xs_ref, tile, x_buf.at[s], in_sem.at[s], to_hbm=False)

    def store(tile, s):
        return _tile_row_copies(ys_ref, tile, y_buf.at[s], out_sem.at[s], to_hbm=True)

    @pl.when(t == 0)
    def _():
        for c in load(0, 0):
            c.start()

    @pl.when(t + 1 < n_used)
    def _():
        for c in load(t + 1, 1 - slot):
            c.start()

    @pl.when((t == 0) | (e != e_prev))
    def _():
        def cast(r, carry):
            sl = pl.ds(pl.multiple_of(r * CAST_ROWS, CAST_ROWS), CAST_ROWS)
            wgb[sl, :] = wg_ref[sl, :].astype(BF16)
            wub[sl, :] = wu_ref[sl, :].astype(BF16)
            wdb[sl, :] = wd_ref[sl, :].astype(BF16)
            return carry

        lax.fori_loop(0, D_MODEL // CAST_ROWS, cast, 0)

    @pl.when(t < n_used)
    def _():
        for c in load(t, slot):
            c.wait()

        @pl.when(t >= 2)
        def _():
            for c in store(t - 2, slot):
                c.wait()

        x = x_buf[slot].astype(BF16)
        gate = jnp.minimum(_dot(x, wgb[...]) + bg_ref[...], SWIGLU_LIMIT)
        up = jnp.clip(_dot(x, wub[...]) + bu_ref[...], -SWIGLU_LIMIT, SWIGLU_LIMIT)
        hdn = (up + 1.0) * gate * (1.0 / (1.0 + jnp.exp(-SWIGLU_ALPHA * gate)))
        y_buf[slot] = _dot(hdn.astype(BF16), wdb[...]) + bd_ref[...]
        for c in store(t, slot):
            c.start()

    @pl.when(t >= n_used)
    def _():
        zero_buf[...] = jnp.zeros_like(zero_buf)
        fill = pltpu.make_async_copy(zero_buf, ys_ref.at[pl.ds(t * TM, TM)], zero_sem)
        fill.start()
        fill.wait()

    @pl.when(t == N_EXPERT_TILES - 1)
    def _():
        @pl.when(n_used >= 2)
        def _():
            for c in store(n_used - 2, lax.rem(n_used, 2)):
                c.wait()

        for c in store(n_used - 1, lax.rem(n_used - 1, 2)):
            c.wait()


def _moe(tile_expert, n_used, xs, w_gate, b_gate, w_up, b_up, w_down, b_down):
    weight = pl.BlockSpec((None, D_MODEL, D_MODEL), lambda t, te, nu: (te[t], 0, 0))
    bias = pl.BlockSpec((None, 1, D_MODEL), lambda t, te, nu: (te[t], 0, 0))
    return pl.pallas_call(
        _moe_kernel,
        grid_spec=pltpu.PrefetchScalarGridSpec(
            num_scalar_prefetch=2,
            grid=(N_EXPERT_TILES,),
            in_specs=[pl.BlockSpec(memory_space=pl.ANY), weight, bias, weight, bias, weight, bias],
            out_specs=pl.BlockSpec(memory_space=pl.ANY),
            scratch_shapes=[pltpu.VMEM((D_MODEL, D_MODEL), BF16)] * 3 + [
                pltpu.VMEM((2, TM, D_MODEL), F32), pltpu.VMEM((2, TM, D_MODEL), F32),
                pltpu.VMEM((TM,) + ROW_TILE, F32), pltpu.SemaphoreType.DMA((2,)),
                pltpu.SemaphoreType.DMA((2,)), pltpu.SemaphoreType.DMA],
        ),
        out_shape=jax.ShapeDtypeStruct((N_SORTED_ROWS,) + ROW_TILE, F32),
        compiler_params=_cparams(("arbitrary",), 52),
        name="experts",
    )(tile_expert, n_used, xs, w_gate, b_gate, w_up, b_up, w_down, b_down)


def _combine_kernel(dest_ref, h_ref, tw_ref, ys_ref, yp_ref, ysm_ref, g_ref, sem):
    i = pl.program_id(0)
    base = i * TM

    def issue(n, carry):
        for kk in range(TOP_K):
            _row_copy(ys_ref, dest_ref[kk * N_ROWS + base + n], g_ref, kk * TM + n, sem).start(priority=kk % 2)
        return carry

    lax.fori_loop(0, TM, issue, 0)

    for kk in range(TOP_K):
        pltpu.make_async_copy(ys_ref.at[pl.ds(0, TM)], g_ref.at[pl.ds(kk * TM, TM)], sem).wait()

    tw = tw_ref[...]
    acc = None
    for kk in range(TOP_K):
        w_col = tw[:, kk:kk + 1]
        part = [w_col * blk for blk in _load_tile_rows(g_ref, kk * TM, (kk + 1) * TM)]
        acc = part if acc is None else [a + p for a, p in zip(acc, part)]
    out = h_ref[...] + jnp.concatenate(acc, axis=1)

    @pl.when(i < N_PROMPT_TILES)
    def _():
        yp_ref[...] = out

    @pl.when(i >= N_PROMPT_TILES)
    def _():
        ysm_ref[...] = out


def _combine(dest, h, tw, ys):
    return pl.pallas_call(
        _combine_kernel,
        grid_spec=pltpu.PrefetchScalarGridSpec(
            num_scalar_prefetch=1,
            grid=(N_TILES,),
            in_specs=[pl.BlockSpec((TM, D_MODEL), lambda i, d: (i, 0)),
                      pl.BlockSpec((TM, LANES), lambda i, d: (i, 0)),
                      pl.BlockSpec(memory_space=pl.ANY)],
            out_specs=[pl.BlockSpec((TM, D_MODEL), lambda i, d: (jnp.minimum(i, N_PROMPT_TILES - 1), 0)),
                       pl.BlockSpec((TM, D_MODEL), lambda i, d: (0, 0))],
            scratch_shapes=[pltpu.VMEM((TOP_K * TM,) + ROW_TILE, F32), pltpu.SemaphoreType.DMA],
        ),
        out_shape=[jax.ShapeDtypeStruct((N_PROMPT, D_MODEL), F32),
                   jax.ShapeDtypeStruct((N_SAMPLE, D_MODEL), F32)],
        compiler_params=_cparams(("arbitrary",), 32),
        name="combine",
    )(dest, h, tw, ys)


def _block_diag_ones(n, blk):
    idx = np.arange(n) // blk
    return (idx[:, None] == idx[None, :]).astype(np.float32)


def _swa_head_mask(tq):
    row_head = np.arange(4 * tq)[:, None] // tq
    lane_head = np.arange(4 * HD)[None, :] // HD
    return jnp.asarray((row_head == lane_head).astype(np.float32), BF16)


def _gla_masks(t, n_sub):
    nb = t // SUB
    lanes = N_HEADS_GLA * DK
    tri = jnp.asarray(np.kron(np.eye(n_sub, dtype=np.float32), np.tril(np.ones((t, t), np.float32))), BF16)
    row = np.arange(N_HEADS_GLA * t)
    col = np.arange(nb * lanes)
    same_head = (row[:, None] // t) == ((col[None, :] % lanes) // DK)
    same_blk = ((row[:, None] % t) // SUB) == (col[None, :] // lanes)
    mask = jnp.asarray((same_head & same_blk).astype(np.float32), BF16)
    return tri, mask


def _sink_rows(sinks, tq):
    s = jnp.repeat(sinks.astype(F32).reshape(2, 4), tq, axis=1)
    return jnp.broadcast_to(s[:, None, :], (2, 8, 4 * tq))


def kernel(x_prompt, x_sample, state_gla, cache_swa_k, cache_swa_v, norm_mix_g, w_in, w_gla_a2, b_gla_a, q_norm_g,
           k_norm_g, swa_sinks, gla_norm_g, w_out, norm_ffn_g, w_router, b_router, w_gate, b_gate, w_up, b_up,
           w_down, b_down):
    xp = x_prompt.reshape(N_PROMPT, D_MODEL)
    xs = x_sample.reshape(N_SAMPLE, D_MODEL)

    w_in0 = w_in[0]
    w_main = w_in0[:, :W_MAIN].astype(BF16)
    w_ga = jnp.pad(w_in0[:, OFF_GA:], ((0, 0), (0, LANES - W_GA))).astype(BF16)
    w_a2 = jnp.pad(w_gla_a2[0], ((0, LANES - W_GA), (0, 0))).astype(BF16)
    b_a = b_gla_a[0].reshape(1, -1)
    qg = (jnp.tile(q_norm_g[0], N_HEADS_SWA) * (HD ** -0.5)).reshape(1, -1)
    kg = jnp.tile(k_norm_g[0], 2).reshape(1, -1)
    bdq = jnp.asarray(_block_diag_ones(W_SQ, HD), BF16)
    bdk = jnp.asarray(_block_diag_ones(W_SK, HD), BF16)

    sq, sk, sv, gq, gk, gv, gg, gr = _proj(xp, xs, norm_mix_g[0].reshape(1, -1), w_main, w_ga, w_a2, b_a,
                                           qg, kg, bdq, bdk)

    o_swa_p = _swa_prompt(sq, sk, sv, _swa_head_mask(CHUNK), _sink_rows(swa_sinks[0], CHUNK))
    cache_k = cache_swa_k[0].reshape(N_STREAMS, WINDOW, 2 * HD)
    cache_v = cache_swa_v[0].reshape(N_STREAMS, WINDOW, 2 * HD)
    o_swa_s = _swa_sample(sq, sk, sv, cache_k, cache_v, _swa_head_mask(T_SAMPLE),
                          _sink_rows(swa_sinks[0], T_SAMPLE))

    bd_state = jnp.asarray(_block_diag_ones(N_HEADS_GLA, 1).repeat(DV, axis=0).repeat(DK, axis=1), F32)
    gn = gla_norm_g[0].reshape(1, -1)
    tri_p, mask_p = _gla_masks(CHUNK, TM // CHUNK)
    tri_s, mask_s = _gla_masks(T_SAMPLE, 1)
    s0_p = jnp.zeros((1, N_HEADS_GLA * DV, N_HEADS_GLA * DK), F32)
    og_p, sfin_p = _gla(gq, gk, gv, gg, gr, s0_p, tri_p, mask_p, bd_state, gn, t=CHUNK, n_sub=TM // CHUNK,
                        n_batch=1, n_steps=N_PROMPT // TM, first_block=0, name="gla_prompt")
    eye = jnp.eye(N_HEADS_GLA, dtype=F32)
    s0_s = jnp.einsum('bhde,hg->bhegd', state_gla[0].astype(F32), eye).reshape(
        N_STREAMS, N_HEADS_GLA * DV, N_HEADS_GLA * DK)
    og_s, sfin_s = _gla(gq, gk, gv, gg, gr, s0_s, tri_s, mask_s, bd_state, gn, t=T_SAMPLE, n_sub=1,
                        n_batch=N_STREAMS, n_steps=1, first_block=N_PROMPT // T_SAMPLE, name="gla_sample")

    def unpack_state(sfin):
        s = sfin.reshape(-1, N_HEADS_GLA, DV, N_HEADS_GLA, DK)
        s = jnp.stack([s[:, h, :, h, :] for h in range(N_HEADS_GLA)], axis=1)
        return jnp.transpose(s, (0, 1, 3, 2))[None]

    w_out0 = w_out[0].astype(BF16)
    wr = w_router[0].T
    wr_hi = wr.astype(BF16)
    wr_lo = (wr - wr_hi.astype(F32)).astype(BF16)
    br = jnp.broadcast_to(b_router[0].astype(F32)[:, None], (N_EXPERTS, LANES))
    tri_strict = jnp.asarray(np.triu(np.ones((TM, TM), np.float32), 1), BF16)
    h1, x_packed, top_i, top_w, rank, counts = _merge(
        o_swa_p, o_swa_s, og_p, og_s, xp, xs, w_out0[:W_SQ], w_out0[W_SQ:], norm_ffn_g[0].reshape(1, -1), wr_hi, wr_lo, br,
        tri_strict)

    counts = counts[:, 0]
    padded = (counts + TM - 1) // TM * TM
    end = jnp.cumsum(padded)
    start = end - padded
    experts = jnp.arange(N_EXPERTS, dtype=I32)
    is_e = top_i[:TOP_K, :, None] == experts
    dest = (rank[:TOP_K] + jnp.sum(jnp.where(is_e, start, 0), axis=-1)).reshape(-1).astype(I32)
    n_used = (end[-1] // TM).astype(I32)
    tiles = jnp.minimum(jnp.arange(N_EXPERT_TILES, dtype=I32), n_used - 1)
    tile_expert = jnp.sum((tiles[:, None] * TM >= end[None, :]).astype(I32), axis=1)

    xs_sorted = _dispatch(dest, end.astype(I32), x_packed)
    ys = _moe(tile_expert, n_used.reshape(1), xs_sorted, w_gate[0], b_gate[0].reshape(N_EXPERTS, 1, -1),
              w_up[0], b_up[0].reshape(N_EXPERTS, 1, -1), w_down[0], b_down[0].reshape(N_EXPERTS, 1, -1))
    y_p, y_s = _combine(dest, h1, top_w, ys)

    sk_s = sk[N_PROMPT:].reshape(N_STREAMS, T_SAMPLE, 2 * HD)
    sv_s = sv[N_PROMPT:].reshape(N_STREAMS, T_SAMPLE, 2 * HD)
    kc_s = jnp.concatenate([cache_k[:, T_SAMPLE:], sk_s], axis=1).reshape(1, N_STREAMS, WINDOW, 2, HD)
    vc_s = jnp.concatenate([cache_v[:, T_SAMPLE:], sv_s], axis=1).reshape(1, N_STREAMS, WINDOW, 2, HD)
    kc_p = sk[N_PROMPT - WINDOW:N_PROMPT].reshape(1, 1, WINDOW, 2, HD)
    vc_p = sv[N_PROMPT - WINDOW:N_PROMPT].reshape(1, 1, WINDOW, 2, HD)
    return (y_p.reshape(1, N_PROMPT, D_MODEL), y_s.reshape(N_STREAMS, T_SAMPLE, D_MODEL),
            unpack_state(sfin_p), kc_p, vc_p, unpack_state(sfin_s), kc_s, vc_s)
```

```python
import functools

import numpy as np
import jax
import jax.numpy as jnp
from jax import lax
from jax.experimental import pallas as pl
from jax.experimental.pallas import tpu as pltpu

F32 = jnp.float32
BF16 = jnp.bfloat16
I32 = jnp.int32
U32 = jnp.uint32

D_MODEL = 1024
N_PROMPT = 16384
N_STREAMS = 8
T_SAMPLE = 32
N_SAMPLE = N_STREAMS * T_SAMPLE
N_ROWS = N_PROMPT + N_SAMPLE
EPS = 1e-6

CHUNK = 64
SUB = 16
N_HEADS_SWA = 8
HD = 64
WINDOW = 128
N_HEADS_GLA = 4
DK = 64
DV = 128
GLA_TAU = 16.0
N_EXPERTS = 32
TOP_K = 4
SWIGLU_ALPHA = 1.702
SWIGLU_LIMIT = 7.0

TM = 256
N_TILES = N_ROWS // TM
N_PROMPT_TILES = N_PROMPT // TM
N_ASSIGN = N_ROWS * TOP_K
N_EXPERT_TILES = N_ASSIGN // TM + N_EXPERTS
N_SORTED_ROWS = N_EXPERT_TILES * TM
LANES = 128
SLOT_ROWS = 16
NEG_BIG = -1e30

W_SQ, W_SK, W_SV, W_GQ, W_GK, W_GV, W_GR, W_GA = 512, 128, 128, 256, 256, 512, 512, 16
OFF_SQ = 0
OFF_SK = OFF_SQ + W_SQ
OFF_SV = OFF_SK + W_SK
OFF_GQ = OFF_SV + W_SV
OFF_GK = OFF_GQ + W_GQ
OFF_GV = OFF_GK + W_GK
OFF_GR = OFF_GV + W_GV
OFF_GA = OFF_GR + W_GR
W_MAIN = OFF_GA


def _cparams(semantics, vmem_mib):
    return pltpu.CompilerParams(dimension_semantics=semantics, vmem_limit_bytes=vmem_mib * 2 ** 20)


def _dot(a, b):
    return jnp.dot(a, b, preferred_element_type=F32)


def _dot_nt(a, b):
    return lax.dot_general(a, b, (((1,), (1,)), ((), ())), preferred_element_type=F32)


def _dot_tn(a, b):
    return lax.dot_general(a, b, (((0,), (0,)), ((), ())), preferred_element_type=F32)


def _split_bf16(x):
    hi = x.astype(BF16)
    lo = (x - hi.astype(F32)).astype(BF16)
    return hi, lo


def _rms(x):
    return x * lax.rsqrt(jnp.mean(x * x, axis=-1, keepdims=True) + EPS)


ROW_SUB = 8
ROW_TILE = (ROW_SUB, LANES)


def _proj_kernel(xp_ref, xs_ref, g_ref, w_ref, wga_ref, wa2_ref, ba_ref, qg_ref, kg_ref, bdq_ref, bdk_ref,
                 sq_ref, sk_ref, sv_ref, gq_ref, gk_ref, gv_ref, gg_ref, gr_ref):
    i = pl.program_id(0)
    x = jnp.where(i < N_PROMPT_TILES, xp_ref[...], xs_ref[...])
    xb = (_rms(x) * g_ref[...]).astype(BF16)

    def seg(off, width):
        return _dot(xb, w_ref[:, off:off + width])

    def head_norm(u, bd_ref):
        hi, lo = _split_bf16(u * u)
        ss = _dot(hi, bd_ref[...]) + _dot(lo, bd_ref[...])
        return u * lax.rsqrt(ss * (1.0 / HD) + EPS)

    sq_ref[...] = (head_norm(seg(OFF_SQ, W_SQ), bdq_ref) * qg_ref[...]).astype(BF16)
    sk_ref[...] = head_norm(seg(OFF_SK, W_SK), bdk_ref) * kg_ref[...]
    sv_ref[...] = seg(OFF_SV, W_SV)
    gq_ref[...] = (seg(OFF_GQ, W_GQ) * (DK ** -0.5)).astype(BF16)
    gk_ref[...] = seg(OFF_GK, W_GK).astype(BF16)
    gv_ref[...] = seg(OFF_GV, W_GV).astype(BF16)
    gr_ref[...] = seg(OFF_GR, W_GR).astype(BF16)
    ga = _dot(xb, wga_ref[...]).astype(BF16)
    z = _dot(ga, wa2_ref[...]) + ba_ref[...]
    log_sig = jnp.minimum(z, 0.0) - jnp.log(1.0 + jnp.exp(-jnp.abs(z)))
    gg_ref[...] = log_sig * (1.0 / GLA_TAU)


def _proj(xp, xs, norm_g, w_main, w_ga, w_a2, b_a, qg, kg, bdq, bdk):
    def rows(width):
        return pl.BlockSpec((TM, width), lambda i: (i, 0))

    def full(a):
        return pl.BlockSpec(a.shape, lambda i: (0,) * a.ndim)

    consts = (norm_g, w_main, w_ga, w_a2, b_a, qg, kg, bdq, bdk)
    out_widths = ((W_SQ, BF16), (W_SK, F32), (W_SV, F32), (W_GQ, BF16), (W_GK, BF16), (W_GV, BF16),
                  (W_GQ, F32), (W_GR, BF16))
    return pl.pallas_call(
        _proj_kernel,
        grid=(N_TILES,),
        in_specs=[pl.BlockSpec((TM, D_MODEL), lambda i: (jnp.minimum(i, N_PROMPT_TILES - 1), 0)),
                  pl.BlockSpec((TM, D_MODEL), lambda i: (0, 0))] + [full(a) for a in consts],
        out_specs=[rows(w) for w, _ in out_widths],
        out_shape=[jax.ShapeDtypeStruct((N_ROWS, w), dt) for w, dt in out_widths],
        compiler_params=_cparams(("arbitrary",), 40),
        name="proj",
    )(xp, xs, *consts)


def _dup_kv_heads(x):
    r = pltpu.roll(x, HD, axis=1)
    lo = lax.broadcasted_iota(I32, x.shape, 1) < HD
    out = []
    for a in (jnp.where(lo, x, r), jnp.where(lo, r, x)):
        out.append(jnp.concatenate([a, a], axis=1).astype(BF16))
    return out


def _swa_blocks(q_blocks, k_blocks, v_blocks, sinks, valids, head_mask):
    tq = q_blocks[0].shape[0]
    n_q = 4 * tq
    scores = []
    for q, k, valid in zip(q_blocks, k_blocks, valids):
        s_t = _dot_nt(k, jnp.concatenate([q] * 4, axis=0) * head_mask)
        scores.append(s_t if valid is None else jnp.where(valid, s_t, -jnp.inf))
    eye = (lax.broadcasted_iota(I32, (n_q, n_q), 0) == lax.broadcasted_iota(I32, (n_q, n_q), 1)
           ).astype(F32).astype(BF16)
    probs = []
    for s_t, sink in zip(scores, sinks):
        m = jnp.maximum(jnp.max(s_t, axis=0, keepdims=True), sink)
        p_t = jnp.exp(s_t - m)
        den = jnp.sum(p_t, axis=0, keepdims=True) + jnp.exp(sink - m)
        probs.append(_dot_nt(eye, (p_t / den).astype(BF16)).astype(BF16))
    lane_head = lax.broadcasted_iota(I32, (tq, 4 * HD), 1) // HD
    outs = []
    for p, v in zip(probs, v_blocks):
        o_full = _dot(p, v)
        o = jnp.zeros((tq, 4 * HD), F32)
        for a in range(4):
            o = o + jnp.where(lane_head == a, o_full[a * tq:(a + 1) * tq], 0.0)
        outs.append(o)
    return outs


def _swa_prompt_kernel(q_ref, kp_ref, kc_ref, vp_ref, vc_ref, hm_ref, sink_ref, o_ref):
    i = pl.program_id(0)
    k_dup = _dup_kv_heads(jnp.concatenate([kp_ref[...], kc_ref[...]], axis=0))
    v_dup = _dup_kv_heads(jnp.concatenate([vp_ref[...], vc_ref[...]], axis=0))
    sink_row = [sink_ref[j][0:1, :] for j in range(2)]
    span = WINDOW + CHUNK
    key = lax.broadcasted_iota(I32, (span, 4 * CHUNK), 0)
    qs, ks, vs, sinks, valids, where = [], [], [], [], [], []
    for c in range(TM // CHUNK):
        lo = CHUNK * c
        valid = (i * TM - WINDOW + lo + key) >= 0
        for j in range(2):
            qs.append(q_ref[lo:lo + CHUNK, 4 * HD * j:4 * HD * (j + 1)])
            ks.append(k_dup[j][lo:lo + span])
            vs.append(v_dup[j][lo:lo + span])
            sinks.append(sink_row[j])
            valids.append(valid)
            where.append((lo, j))
    outs = _swa_blocks(qs, ks, vs, sinks, valids, hm_ref[...])
    for (lo, j), o in zip(where, outs):
        o_ref[lo:lo + CHUNK, 4 * HD * j:4 * HD * (j + 1)] = o.astype(BF16)


def _swa_prompt(sq, sk, sv, head_mask, sink_b):
    half = TM // 2
    prev = pl.BlockSpec((half, 2 * HD), lambda i: (jnp.maximum(2 * i - 1, 0), 0))
    cur = pl.BlockSpec((TM, 2 * HD), lambda i: (i, 0))
    return pl.pallas_call(
        _swa_prompt_kernel,
        grid=(N_PROMPT_TILES,),
        in_specs=[pl.BlockSpec((TM, W_SQ), lambda i: (i, 0)), prev, cur, prev, cur,
                  pl.BlockSpec(head_mask.shape, lambda i: (0, 0)),
                  pl.BlockSpec(sink_b.shape, lambda i: (0, 0, 0))],
        out_specs=pl.BlockSpec((TM, W_SQ), lambda i: (i, 0)),
        out_shape=jax.ShapeDtypeStruct((N_PROMPT, W_SQ), BF16),
        compiler_params=_cparams(("arbitrary",), 32),
        name="swa_prompt",
    )(sq, sk, sk, sv, sv, head_mask, sink_b)


def _swa_sample_kernel(q_ref, kc_ref, kn_ref, vc_ref, vn_ref, hm_ref, sink_ref, o_ref):
    k_dup = _dup_kv_heads(jnp.concatenate([kc_ref[...], kn_ref[...]], axis=0))
    v_dup = _dup_kv_heads(jnp.concatenate([vc_ref[...], vn_ref[...]], axis=0))
    sink_row = [sink_ref[j][0:1, :] for j in range(2)]
    qs = [q_ref[:, 4 * HD * j:4 * HD * (j + 1)] for j in range(2)]
    outs = _swa_blocks(qs, k_dup, v_dup, sink_row, [None, None], hm_ref[...])
    o_ref[...] = jnp.concatenate(outs, axis=1).astype(BF16)


def _swa_sample(sq, sk, sv, cache_k, cache_v, head_mask, sink_b):
    first = N_PROMPT // T_SAMPLE
    new = lambda width: pl.BlockSpec((T_SAMPLE, width), lambda b: (first + b, 0))
    cache = pl.BlockSpec((None, WINDOW, 2 * HD), lambda b: (b, 0, 0))
    return pl.pallas_call(
        _swa_sample_kernel,
        grid=(N_STREAMS,),
        in_specs=[new(W_SQ), cache, new(2 * HD), cache, new(2 * HD),
                  pl.BlockSpec(head_mask.shape, lambda b: (0, 0)),
                  pl.BlockSpec(sink_b.shape, lambda b: (0, 0, 0))],
        out_specs=pl.BlockSpec((T_SAMPLE, W_SQ), lambda b: (b, 0)),
        out_shape=jax.ShapeDtypeStruct((N_SAMPLE, W_SQ), BF16),
        compiler_params=_cparams(("arbitrary",), 32),
        name="swa_sample",
    )(sq, cache_k, sk, cache_v, sv, head_mask, sink_b)


def _gla_kernel(q_ref, k_ref, v_ref, g_ref, gr_ref, s0_ref, tri_ref, m_ref, bd_ref, gn_ref,
                og_ref, sfin_ref, st_ref, *, t, n_sub):
    c = pl.program_id(1)
    nb = t // SUB
    lanes = N_HEADS_GLA * DK

    @pl.when(c == 0)
    def _():
        st_ref[...] = s0_ref[...]

    n_rows = t * n_sub

    def group_row(x, period, offset):
        g = x.reshape(n_rows // period, period, lanes)[:, offset:offset + 1, :]
        return jnp.broadcast_to(g, (n_rows // period, period, lanes)).reshape(n_rows, lanes)

    q = q_ref[...].astype(F32)
    k = k_ref[...].astype(F32)
    g_hi, g_lo = _split_bf16(g_ref[...])
    b = _dot(tri_ref[...], g_hi) + _dot(tri_ref[...], g_lo)
    qd = (q * jnp.exp(b - group_row(b, SUB, 0))).astype(BF16)
    pos = lax.broadcasted_iota(I32, (n_rows, lanes), 0) & (t - 1)
    k_parts = []
    for blk in range(nb):
        arg = jnp.where(pos < SUB * (blk + 1), group_row(b, t, SUB * blk) - b, NEG_BIG)
        k_parts.append((k * jnp.exp(arg)).astype(BF16))
    k_cat = jnp.concatenate(k_parts, axis=1)
    qd_cat = jnp.concatenate([qd] * nb, axis=1)
    q_dec = (q * jnp.exp(b)).astype(BF16)
    k_last = (k * jnp.exp(group_row(b, t, t - 1) - b)).astype(BF16)
    row_a = lax.broadcasted_iota(I32, (N_HEADS_GLA * t, t), 0) & (t - 1)
    col_a = lax.broadcasted_iota(I32, (N_HEADS_GLA * t, t), 1)

    o_intra, q_decayed, state_add, state_decay = [], [], [], []
    for u in range(n_sub):
        rows = slice(u * t, (u + 1) * t)
        v = v_ref[rows, :]
        lhs = jnp.concatenate([qd_cat[rows]] * N_HEADS_GLA, axis=0) * m_ref[...]
        a = _dot_nt(lhs, k_cat[rows])
        a = jnp.where(row_a >= col_a, a, 0.0).astype(BF16)
        o_full = _dot(a, v)
        o_intra.append(jnp.concatenate(
            [o_full[h * t:(h + 1) * t, h * DV:(h + 1) * DV] for h in range(N_HEADS_GLA)], axis=1))
        q_decayed.append(q_dec[rows])
        state_add.append(_dot_tn(v, k_last[rows]) * bd_ref[...])
        state_decay.append(jnp.exp(b[(u + 1) * t - 1:(u + 1) * t, :]))

    st = st_ref[...]
    o_parts = []
    for u in range(n_sub):
        o_parts.append(o_intra[u] + _dot_nt(q_decayed[u], st.astype(BF16)))
        st = st * state_decay[u] + state_add[u]
    st_ref[...] = st
    o = jnp.concatenate(o_parts, axis=0) if n_sub > 1 else o_parts[0]

    gr = gr_ref[...].astype(F32)
    gate = gr / (1.0 + jnp.exp(-gr))
    outs = []
    for h in range(N_HEADS_GLA):
        sl = slice(h * DV, (h + 1) * DV)
        outs.append(_rms(o[:, sl]) * gn_ref[...] * gate[:, sl])
    og_ref[...] = jnp.concatenate(outs, axis=1).astype(BF16)

    @pl.when(c == pl.num_programs(1) - 1)
    def _():
        sfin_ref[...] = st


def _gla(gq, gk, gv, gg, gr, s0, tri, mask, bd, gn, *, t, n_sub, n_batch, n_steps, first_block, name):
    rows_per_step = t * n_sub

    def rows(width):
        return pl.BlockSpec((rows_per_step, width), lambda b, c: (first_block + b * n_steps + c, 0))

    def full(a):
        return pl.BlockSpec(a.shape, lambda b, c: (0,) * a.ndim)

    state = pl.BlockSpec((None,) + s0.shape[1:], lambda b, c: (b, 0, 0))
    return pl.pallas_call(
        functools.partial(_gla_kernel, t=t, n_sub=n_sub),
        grid=(n_batch, n_steps),
        in_specs=[rows(W_GQ), rows(W_GK), rows(W_GV), rows(W_GQ), rows(W_GR), state,
                  full(tri), full(mask), full(bd), full(gn)],
        out_specs=[pl.BlockSpec((rows_per_step, W_GV), lambda b, c: (b * n_steps + c, 0)), state],
        out_shape=[jax.ShapeDtypeStruct((n_batch * n_steps * rows_per_step, W_GV), BF16),
                   jax.ShapeDtypeStruct(s0.shape, F32)],
        scratch_shapes=[pltpu.VMEM(s0.shape[1:], F32)],
        compiler_params=_cparams(("arbitrary", "arbitrary"), 32),
        name=name,
    )(gq, gk, gv, gg, gr, s0, tri, mask, bd, gn)


def _merge_kernel(oswp_ref, osws_ref, ogp_ref, ogs_ref, xp_ref, xs_ref, wo1_ref, wo2_ref, gf_ref, wrh_ref, wrl_ref,
                  br_ref, tri_ref, xrow_ref, hrow_ref, wrow_ref, ti_ref, rk_ref, cnt_ref, base_ref, row_buf, row_sem):
    i = pl.program_id(0)

    @pl.when(i == 0)
    def _():
        base_ref[...] = jnp.zeros_like(base_ref)

    is_prompt = i < N_PROMPT_TILES
    x = jnp.where(is_prompt, xp_ref[...], xs_ref[...])
    o_swa = jnp.where(is_prompt, oswp_ref[...], osws_ref[...])
    og = jnp.where(is_prompt, ogp_ref[...], ogs_ref[...])
    h = x + (_dot(o_swa, wo1_ref[...]) + _dot(og, wo2_ref[...]))
    xn = _rms(h) * gf_ref[...]
    x_hi, x_lo = _split_bf16(xn)

    logits_t = (_dot_nt(wrh_ref[...], x_hi) + _dot_nt(wrh_ref[...], x_lo) + _dot_nt(wrl_ref[...], x_hi)
                + br_ref[:, 0:1])
    expert = lax.broadcasted_iota(I32, logits_t.shape, 0)
    slot = lax.broadcasted_iota(I32, (SLOT_ROWS, TM), 0)
    vals, hots = [], []
    ti = jnp.zeros((SLOT_ROWS, TM), I32)
    for kk in range(TOP_K):
        m = jnp.max(logits_t, axis=0, keepdims=True)
        idx = jnp.min(jnp.where(logits_t == m, expert, N_EXPERTS), axis=0, keepdims=True)
        hot = expert == idx
        logits_t = jnp.where(hot, NEG_BIG, logits_t)
        vals.append(m)
        hots.append(hot)
        ti = jnp.where(slot == kk, idx, ti)
    ti_ref[...] = ti
    exps = [jnp.exp(v - vals[0]) for v in vals]
    den = exps[0] + exps[1] + exps[2] + exps[3]
    tw_t = jnp.zeros((SLOT_ROWS, TM), F32)
    for kk in range(TOP_K):
        tw_t = jnp.where(slot == kk, exps[kk] / den, tw_t)
    eye = (lax.broadcasted_iota(I32, (SLOT_ROWS, LANES), 0)
           == lax.broadcasted_iota(I32, (SLOT_ROWS, LANES), 1)).astype(F32).astype(BF16)
    w_hi = tw_t.astype(BF16)
    w_mid, w_lo = _split_bf16(tw_t - w_hi.astype(F32))
    tw_col = _dot_tn(w_hi, eye) + _dot_tn(w_mid, eye) + _dot_tn(w_lo, eye)

    onehot_t = jnp.zeros(logits_t.shape, F32)
    for hot in hots:
        onehot_t = onehot_t + jnp.where(hot, 1.0, 0.0)
    before_t = _dot(onehot_t.astype(BF16), tri_ref[...]) + base_ref[:, 0:1]
    rk = jnp.zeros((SLOT_ROWS, TM), I32)
    for kk in range(TOP_K):
        r = jnp.sum(jnp.where(hots[kk], before_t, 0.0), axis=0, keepdims=True).astype(I32)
        rk = jnp.where(slot == kk, r, rk)
    rk_ref[...] = rk
    total = base_ref[...] + jnp.sum(onehot_t, axis=1, keepdims=True)
    base_ref[...] = total
    cnt_ref[...] = total.astype(I32)

    w_lanes = [jnp.broadcast_to(tw_col[:, kk:kk + 1], (TM, LANES)) for kk in range(TOP_K)]
    w_rows = jnp.concatenate(w_lanes + [jnp.zeros((TM, D_MODEL - TOP_K * LANES), F32)], axis=1)
    outputs = ((xrow_ref, xn), (hrow_ref, h), (wrow_ref, w_rows))
    buf_slot = lax.rem(i, 2)

    def store(j, tile, s):
        return _tile_row_copies(outputs[j][0], tile, row_buf.at[j, s], row_sem.at[j, s], to_hbm=True)

    for j, (_, value) in enumerate(outputs):
        @pl.when(i >= 2)
        def _():
            for c in store(j, i - 2, buf_slot):
                c.wait()

        row_buf[j, buf_slot] = value
        for c in store(j, i, buf_slot):
            c.start()

        @pl.when(i == N_TILES - 1)
        def _():
            for c in store(j, i - 1, 1 - buf_slot) + store(j, i, buf_slot):
                c.wait()


def _merge(o_swa_p, o_swa_s, og_p, og_s, xp, xs, wo1, wo2, gf, wrh, wrl, br, tri):
    def prompt_rows(width):
        return pl.BlockSpec((TM, width), lambda i: (jnp.minimum(i, N_PROMPT_TILES - 1), 0))

    def sample_rows(width):
        return pl.BlockSpec((TM, width), lambda i: (0, 0))

    def full(a):
        return pl.BlockSpec(a.shape, lambda i: (0,) * a.ndim)

    consts = (wo1, wo2, gf, wrh, wrl, br, tri)
    return pl.pallas_call(
        _merge_kernel,
        grid=(N_TILES,),
        in_specs=[prompt_rows(W_SQ), sample_rows(W_SQ), prompt_rows(W_GV), sample_rows(W_GV),
                  prompt_rows(D_MODEL), sample_rows(D_MODEL)] + [full(a) for a in consts],
        out_specs=[pl.BlockSpec(memory_space=pl.ANY)] * 3 + [
                   pl.BlockSpec((SLOT_ROWS, TM), lambda i: (0, i)),
                   pl.BlockSpec((SLOT_ROWS, TM), lambda i: (0, i)),
                   pl.BlockSpec((N_EXPERTS, LANES), lambda i: (0, 0))],
        out_shape=[jax.ShapeDtypeStruct((N_ROWS,) + ROW_TILE, F32)] * 3 + [
                   jax.ShapeDtypeStruct((SLOT_ROWS, N_ROWS), I32),
                   jax.ShapeDtypeStruct((SLOT_ROWS, N_ROWS), I32),
                   jax.ShapeDtypeStruct((N_EXPERTS, LANES), I32)],
        scratch_shapes=[pltpu.VMEM((N_EXPERTS, LANES), F32), pltpu.VMEM((3, 2, TM, D_MODEL), F32),
                        pltpu.SemaphoreType.DMA((3, 2))],
        compiler_params=_cparams(("arbitrary",), 32),
        name="merge_router",
    )(o_swa_p, o_swa_s, og_p, og_s, xp, xs, *consts)


def _row_copy(src_ref, src_row, dst_ref, dst_row, sem):
    return pltpu.make_async_copy(src_ref.at[pl.ds(src_row, 1)], dst_ref.at[pl.ds(dst_row, 1)], sem)


def _dispatch_kernel(dest_ref, end_ref, x_ref, xs_ref, zero_ref, sem, zsem):
    i = pl.program_id(0)
    base = i * TM

    @pl.when(i == 0)
    def _():
        zero_ref[...] = jnp.zeros_like(zero_ref)

        def tail_copy(e):
            last = jnp.maximum(end_ref[e] - TM, 0)
            return pltpu.make_async_copy(zero_ref, xs_ref.at[pl.ds(pl.multiple_of(last, TM), TM)], zsem)

        def fill(e, carry):
            tail_copy(e).start()
            return carry

        def fill_wait(e, carry):
            tail_copy(e).wait()
            return carry

        lax.fori_loop(0, N_EXPERTS, fill, 0)
        lax.fori_loop(0, N_EXPERTS, fill_wait, 0)

        def unused_copy(t):
            return pltpu.make_async_copy(zero_ref, xs_ref.at[pl.ds(pl.multiple_of(t * TM, TM), TM)], zsem)

        def fill_unused(t, carry):
            unused_copy(t).start()
            unused_copy(t).wait()
            return carry

        lax.fori_loop(end_ref[N_EXPERTS - 1] // TM, N_EXPERT_TILES, fill_unused, 0)

    def issue(n, carry):
        for kk in range(TOP_K):
            _row_copy(x_ref, n, xs_ref, dest_ref[kk * N_ROWS + base + n], sem).start(priority=kk % 2)
        return carry

    lax.fori_loop(0, TM, issue, 0)

    for kk in range(TOP_K):
        pltpu.make_async_copy(x_ref, xs_ref.at[pl.ds(0, TM)], sem).wait()


def _dispatch(dest, end, x_packed):
    return pl.pallas_call(
        _dispatch_kernel,
        grid_spec=pltpu.PrefetchScalarGridSpec(
            num_scalar_prefetch=2,
            grid=(N_TILES,),
            in_specs=[pl.BlockSpec((TM,) + ROW_TILE, lambda i, d, e: (i, 0, 0))],
            out_specs=pl.BlockSpec(memory_space=pl.ANY),
            scratch_shapes=[pltpu.VMEM((TM,) + ROW_TILE, F32), pltpu.SemaphoreType.DMA,
                            pltpu.SemaphoreType.DMA],
        ),
        out_shape=jax.ShapeDtypeStruct((N_SORTED_ROWS,) + ROW_TILE, F32),
        compiler_params=_cparams(("arbitrary",), 32),
        name="dispatch",
    )(dest, end, x_packed)


CAST_ROWS = 128


def _tile_row_copies(hbm_ref, tile, vmem_ref, sem, to_hbm):
    copies = []
    for a in range(ROW_SUB):
        h = hbm_ref.at[pl.ds(tile * TM, TM), a, :]
        v = vmem_ref.at[:, pl.ds(a * LANES, LANES)]
        copies.append(pltpu.make_async_copy(v, h, sem) if to_hbm else pltpu.make_async_copy(h, v, sem))
    return copies


def _moe_kernel(te_ref, nu_ref, xs_ref, wg_ref, bg_ref, wu_ref, bu_ref, wd_ref, bd_ref, ys_ref, wgb, wub, wdb,
                x_buf, in_sem):
    t = pl.program_id(0)
    n_used = nu_ref[0]
    slot = lax.rem(t, 2)
    e = te_ref[t]
    e_prev = te_ref[jnp.maximum(t - 1, 0)]

    def load(tile, s):
        return _tile_row_copies(xs_ref, tile, x_buf.at[s], in_sem.at[s], to_hbm=False)

    @pl.when(t == 0)
    def _():
        for c in load(0, 0):
            c.start()

    @pl.when(t + 1 < n_used)
    def _():
        for c in load(t + 1, 1 - slot):
            c.start()

    @pl.when((t == 0) | (e != e_prev))
    def _():
        def cast(r, carry):
            sl = pl.ds(pl.multiple_of(r * CAST_ROWS, CAST_ROWS), CAST_ROWS)
            wgb[sl, :] = wg_ref[sl, :].astype(BF16)
            wub[sl, :] = wu_ref[sl, :].astype(BF16)
            wdb[sl, :] = wd_ref[sl, :].astype(BF16)
            return carry

        lax.fori_loop(0, D_MODEL // CAST_ROWS, cast, 0)

    @pl.when(t < n_used)
    def _():
        for c in load(t, slot):
            c.wait()

        x = x_buf[slot].astype(BF16)
        gate = jnp.minimum(_dot(x, wgb[...]) + bg_ref[...], SWIGLU_LIMIT)
        up = jnp.clip(_dot(x, wub[...]) + bu_ref[...], -SWIGLU_LIMIT, SWIGLU_LIMIT)
        hdn = (up + 1.0) * gate * (1.0 / (1.0 + jnp.exp(-SWIGLU_ALPHA * gate)))
        y = _dot(hdn.astype(BF16), wdb[...]) + bd_ref[...]
        for a in range(ROW_SUB):
            ys_ref[:, a, :] = y[:, a * LANES:(a + 1) * LANES]

    @pl.when(t >= n_used)
    def _():
        ys_ref[...] = jnp.zeros_like(ys_ref)


def _moe(tile_expert, n_used, xs, w_gate, b_gate, w_up, b_up, w_down, b_down):
    weight = pl.BlockSpec((None, D_MODEL, D_MODEL), lambda t, te, nu: (te[t], 0, 0))
    bias = pl.BlockSpec((None, 1, D_MODEL), lambda t, te, nu: (te[t], 0, 0))
    return pl.pallas_call(
        _moe_kernel,
        grid_spec=pltpu.PrefetchScalarGridSpec(
            num_scalar_prefetch=2,
            grid=(N_EXPERT_TILES,),
            in_specs=[pl.BlockSpec(memory_space=pl.ANY), weight, bias, weight, bias, weight, bias],
            out_specs=pl.BlockSpec((TM,) + ROW_TILE, lambda t, te, nu: (t, 0, 0)),
            scratch_shapes=[pltpu.VMEM((D_MODEL, D_MODEL), BF16)] * 3 + [
                pltpu.VMEM((2, TM, D_MODEL), F32), pltpu.SemaphoreType.DMA((2,))],
        ),
        out_shape=jax.ShapeDtypeStruct((N_SORTED_ROWS,) + ROW_TILE, F32),
        compiler_params=_cparams(("arbitrary",), 52),
        name="experts",
    )(tile_expert, n_used, xs, w_gate, b_gate, w_up, b_up, w_down, b_down)


def _combine_kernel(dest_ref, h_ref, w_ref, ys_ref, yp_ref, ysm_ref, g_ref, sem):
    i = pl.program_id(0)
    slot = lax.rem(i, 2)

    def issue_tile(tile, s):
        base = tile * TM

        def issue(n, carry):
            for kk in range(TOP_K):
                _row_copy(ys_ref, dest_ref[kk * N_ROWS + base + n], g_ref.at[s], kk * TM + n,
                          sem.at[s]).start(priority=kk % 2)
            return carry

        lax.fori_loop(0, TM, issue, 0)

    @pl.when(i == 0)
    def _():
        issue_tile(0, 0)

    @pl.when(i + 1 < N_TILES)
    def _():
        issue_tile(i + 1, 1 - slot)

    for kk in range(TOP_K):
        pltpu.make_async_copy(ys_ref.at[pl.ds(0, TM)], g_ref.at[slot, pl.ds(kk * TM, TM)], sem.at[slot]).wait()

    w = w_ref[...]
    acc = None
    for kk in range(TOP_K):
        part = jnp.broadcast_to(w[:, kk:kk + 1, :], (TM,) + ROW_TILE) * g_ref[slot, kk * TM:(kk + 1) * TM]
        acc = part if acc is None else acc + part
    out = h_ref[...] + acc

    @pl.when(i < N_PROMPT_TILES)
    def _():
        yp_ref[...] = out

    @pl.when(i >= N_PROMPT_TILES)
    def _():
        ysm_ref[...] = out


def _combine(dest, h_rows, w_rows, ys):
    tile = lambda index: pl.BlockSpec((TM,) + ROW_TILE, index)
    return pl.pallas_call(
        _combine_kernel,
        grid_spec=pltpu.PrefetchScalarGridSpec(
            num_scalar_prefetch=1,
            grid=(N_TILES,),
            in_specs=[tile(lambda i, d: (i, 0, 0)), tile(lambda i, d: (i, 0, 0)),
                      pl.BlockSpec(memory_space=pl.ANY)],
            out_specs=[tile(lambda i, d: (jnp.minimum(i, N_PROMPT_TILES - 1), 0, 0)),
                       tile(lambda i, d: (0, 0, 0))],
            scratch_shapes=[pltpu.VMEM((2, TOP_K * TM) + ROW_TILE, F32), pltpu.SemaphoreType.DMA((2,))],
        ),
        out_shape=[jax.ShapeDtypeStruct((N_PROMPT,) + ROW_TILE, F32),
                   jax.ShapeDtypeStruct((N_SAMPLE,) + ROW_TILE, F32)],
        compiler_params=_cparams(("arbitrary",), 40),
        name="combine",
    )(dest, h_rows, w_rows, ys)


def _block_diag_ones(n, blk):
    idx = np.arange(n) // blk
    return (idx[:, None] == idx[None, :]).astype(np.float32)


def _swa_head_mask(tq):
    row_head = np.arange(4 * tq)[:, None] // tq
    lane_head = np.arange(4 * HD)[None, :] // HD
    return jnp.asarray((row_head == lane_head).astype(np.float32), BF16)


def _gla_masks(t, n_sub):
    nb = t // SUB
    lanes = N_HEADS_GLA * DK
    tri = jnp.asarray(np.kron(np.eye(n_sub, dtype=np.float32), np.tril(np.ones((t, t), np.float32))), BF16)
    row = np.arange(N_HEADS_GLA * t)
    col = np.arange(nb * lanes)
    same_head = (row[:, None] // t) == ((col[None, :] % lanes) // DK)
    same_blk = ((row[:, None] % t) // SUB) == (col[None, :] // lanes)
    mask = jnp.asarray((same_head & same_blk).astype(np.float32), BF16)
    return tri, mask


def _sink_rows(sinks, tq):
    s = jnp.repeat(sinks.astype(F32).reshape(2, 4), tq, axis=1)
    return jnp.broadcast_to(s[:, None, :], (2, 8, 4 * tq))


def kernel(x_prompt, x_sample, state_gla, cache_swa_k, cache_swa_v, norm_mix_g, w_in, w_gla_a2, b_gla_a, q_norm_g,
           k_norm_g, swa_sinks, gla_norm_g, w_out, norm_ffn_g, w_router, b_router, w_gate, b_gate, w_up, b_up,
           w_down, b_down):
    xp = x_prompt.reshape(N_PROMPT, D_MODEL)
    xs = x_sample.reshape(N_SAMPLE, D_MODEL)

    w_in0 = w_in[0]
    w_main = w_in0[:, :W_MAIN].astype(BF16)
    w_ga = jnp.pad(w_in0[:, OFF_GA:], ((0, 0), (0, LANES - W_GA))).astype(BF16)
    w_a2 = jnp.pad(w_gla_a2[0], ((0, LANES - W_GA), (0, 0))).astype(BF16)
    b_a = b_gla_a[0].reshape(1, -1)
    qg = (jnp.tile(q_norm_g[0], N_HEADS_SWA) * (HD ** -0.5)).reshape(1, -1)
    kg = jnp.tile(k_norm_g[0], 2).reshape(1, -1)
    bdq = jnp.asarray(_block_diag_ones(W_SQ, HD), BF16)
    bdk = jnp.asarray(_block_diag_ones(W_SK, HD), BF16)

    sq, sk, sv, gq, gk, gv, gg, gr = _proj(xp, xs, norm_mix_g[0].reshape(1, -1), w_main, w_ga, w_a2, b_a,
                                           qg, kg, bdq, bdk)

    o_swa_p = _swa_prompt(sq, sk, sv, _swa_head_mask(CHUNK), _sink_rows(swa_sinks[0], CHUNK))
    cache_k = cache_swa_k[0].reshape(N_STREAMS, WINDOW, 2 * HD)
    cache_v = cache_swa_v[0].reshape(N_STREAMS, WINDOW, 2 * HD)
    o_swa_s = _swa_sample(sq, sk, sv, cache_k, cache_v, _swa_head_mask(T_SAMPLE),
                          _sink_rows(swa_sinks[0], T_SAMPLE))

    bd_state = jnp.asarray(_block_diag_ones(N_HEADS_GLA, 1).repeat(DV, axis=0).repeat(DK, axis=1), F32)
    gn = gla_norm_g[0].reshape(1, -1)
    tri_p, mask_p = _gla_masks(CHUNK, TM // CHUNK)
    tri_s, mask_s = _gla_masks(T_SAMPLE, 1)
    s0_p = jnp.zeros((1, N_HEADS_GLA * DV, N_HEADS_GLA * DK), F32)
    og_p, sfin_p = _gla(gq, gk, gv, gg, gr, s0_p, tri_p, mask_p, bd_state, gn, t=CHUNK, n_sub=TM // CHUNK,
                        n_batch=1, n_steps=N_PROMPT // TM, first_block=0, name="gla_prompt")
    eye = jnp.eye(N_HEADS_GLA, dtype=F32)
    s0_s = jnp.einsum('bhde,hg->bhegd', state_gla[0].astype(F32), eye).reshape(
        N_STREAMS, N_HEADS_GLA * DV, N_HEADS_GLA * DK)
    og_s, sfin_s = _gla(gq, gk, gv, gg, gr, s0_s, tri_s, mask_s, bd_state, gn, t=T_SAMPLE, n_sub=1,
                        n_batch=N_STREAMS, n_steps=1, first_block=N_PROMPT // T_SAMPLE, name="gla_sample")

    def unpack_state(sfin):
        s = sfin.reshape(-1, N_HEADS_GLA, DV, N_HEADS_GLA, DK)
        s = jnp.stack([s[:, h, :, h, :] for h in range(N_HEADS_GLA)], axis=1)
        return jnp.transpose(s, (0, 1, 3, 2))[None]

    w_out0 = w_out[0].astype(BF16)
    wr = w_router[0].T
    wr_hi = wr.astype(BF16)
    wr_lo = (wr - wr_hi.astype(F32)).astype(BF16)
    br = jnp.broadcast_to(b_router[0].astype(F32)[:, None], (N_EXPERTS, LANES))
    tri_strict = jnp.asarray(np.triu(np.ones((TM, TM), np.float32), 1), BF16)
    x_rows, h_rows, w_rows, top_i, rank, counts = _merge(
        o_swa_p, o_swa_s, og_p, og_s, xp, xs, w_out0[:W_SQ], w_out0[W_SQ:], norm_ffn_g[0].reshape(1, -1), wr_hi, wr_lo, br,
        tri_strict)

    counts = counts[:, 0]
    padded = (counts + TM - 1) // TM * TM
    end = jnp.cumsum(padded)
    start = end - padded
    experts = jnp.arange(N_EXPERTS, dtype=I32)
    is_e = top_i[:TOP_K, :, None] == experts
    dest = (rank[:TOP_K] + jnp.sum(jnp.where(is_e, start, 0), axis=-1)).reshape(-1).astype(I32)
    n_used = (end[-1] // TM).astype(I32)
    tiles = jnp.minimum(jnp.arange(N_EXPERT_TILES, dtype=I32), n_used - 1)
    tile_expert = jnp.sum((tiles[:, None] * TM >= end[None, :]).astype(I32), axis=1)

    xs_sorted = _dispatch(dest, end.astype(I32), x_rows)
    ys = _moe(tile_expert, n_used.reshape(1), xs_sorted, w_gate[0], b_gate[0].reshape(N_EXPERTS, 1, -1),
              w_up[0], b_up[0].reshape(N_EXPERTS, 1, -1), w_down[0], b_down[0].reshape(N_EXPERTS, 1, -1))
    y_p, y_s = _combine(dest, h_rows, w_rows, ys)

    sk_s = sk[N_PROMPT:].reshape(N_STREAMS, T_SAMPLE, 2 * HD)
    sv_s = sv[N_PROMPT:].reshape(N_STREAMS, T_SAMPLE, 2 * HD)
    kc_s = jnp.concatenate([cache_k[:, T_SAMPLE:], sk_s], axis=1).reshape(1, N_STREAMS, WINDOW, 2, HD)
    vc_s = jnp.concatenate([cache_v[:, T_SAMPLE:], sv_s], axis=1).reshape(1, N_STREAMS, WINDOW, 2, HD)
    kc_p = sk[N_PROMPT - WINDOW:N_PROMPT].reshape(1, 1, WINDOW, 2, HD)
    vc_p = sv[N_PROMPT - WINDOW:N_PROMPT].reshape(1, 1, WINDOW, 2, HD)
    return (y_p.reshape(1, N_PROMPT, D_MODEL), y_s.reshape(N_STREAMS, T_SAMPLE, D_MODEL),
            unpack_state(sfin_p), kc_p, vc_p, unpack_state(sfin_s), kc_s, vc_s)
```

```python
import functools

import numpy as np
import jax
import jax.numpy as jnp
from jax import lax
from jax.experimental import pallas as pl
from jax.experimental.pallas import tpu as pltpu

F32 = jnp.float32
BF16 = jnp.bfloat16
I32 = jnp.int32
U32 = jnp.uint32

D_MODEL = 1024
N_PROMPT = 16384
N_STREAMS = 8
T_SAMPLE = 32
N_SAMPLE = N_STREAMS * T_SAMPLE
N_ROWS = N_PROMPT + N_SAMPLE
EPS = 1e-6

CHUNK = 64
SUB = 16
N_HEADS_SWA = 8
HD = 64
WINDOW = 128
N_HEADS_GLA = 4
DK = 64
DV = 128
GLA_TAU = 16.0
N_EXPERTS = 32
TOP_K = 4
SWIGLU_ALPHA = 1.702
SWIGLU_LIMIT = 7.0

TM = 256
N_TILES = N_ROWS // TM
N_PROMPT_TILES = N_PROMPT // TM
N_ASSIGN = N_ROWS * TOP_K
N_EXPERT_TILES = N_ASSIGN // TM + N_EXPERTS
N_SORTED_ROWS = N_EXPERT_TILES * TM
LANES = 128
SLOT_ROWS = 16
NEG_BIG = -1e30

W_SQ, W_SK, W_SV, W_GQ, W_GK, W_GV, W_GR, W_GA = 512, 128, 128, 256, 256, 512, 512, 16
OFF_SQ = 0
OFF_SK = OFF_SQ + W_SQ
OFF_SV = OFF_SK + W_SK
OFF_GQ = OFF_SV + W_SV
OFF_GK = OFF_GQ + W_GQ
OFF_GV = OFF_GK + W_GK
OFF_GR = OFF_GV + W_GV
OFF_GA = OFF_GR + W_GR
W_MAIN = OFF_GA


def _cparams(semantics, vmem_mib):
    return pltpu.CompilerParams(dimension_semantics=semantics, vmem_limit_bytes=vmem_mib * 2 ** 20)


def _dot(a, b):
    return jnp.dot(a, b, preferred_element_type=F32)


def _dot_nt(a, b):
    return lax.dot_general(a, b, (((1,), (1,)), ((), ())), preferred_element_type=F32)


def _dot_tn(a, b):
    return lax.dot_general(a, b, (((0,), (0,)), ((), ())), preferred_element_type=F32)


def _split_bf16(x):
    hi = x.astype(BF16)
    lo = (x - hi.astype(F32)).astype(BF16)
    return hi, lo


def _rms(x):
    return x * lax.rsqrt(jnp.mean(x * x, axis=-1, keepdims=True) + EPS)


ROW_SUB = 8
ROW_TILE = (ROW_SUB, LANES)


def _proj_kernel(xp_ref, xs_ref, g_ref, w_ref, wga_ref, wa2_ref, ba_ref, qg_ref, kg_ref, bdq_ref, bdk_ref,
                 sq_ref, sk_ref, sv_ref, gq_ref, gk_ref, gv_ref, gg_ref, gr_ref):
    i = pl.program_id(0)
    x = jnp.where(i < N_PROMPT_TILES, xp_ref[...], xs_ref[...])
    xb = (_rms(x) * g_ref[...]).astype(BF16)

    def seg(off, width):
        return _dot(xb, w_ref[:, off:off + width])

    def head_norm(u, bd_ref):
        hi, lo = _split_bf16(u * u)
        ss = _dot(hi, bd_ref[...]) + _dot(lo, bd_ref[...])
        return u * lax.rsqrt(ss * (1.0 / HD) + EPS)

    sq_ref[...] = (head_norm(seg(OFF_SQ, W_SQ), bdq_ref) * qg_ref[...]).astype(BF16)
    sk_ref[...] = head_norm(seg(OFF_SK, W_SK), bdk_ref) * kg_ref[...]
    sv_ref[...] = seg(OFF_SV, W_SV)
    gq_ref[...] = (seg(OFF_GQ, W_GQ) * (DK ** -0.5)).astype(BF16)
    gk_ref[...] = seg(OFF_GK, W_GK).astype(BF16)
    gv_ref[...] = seg(OFF_GV, W_GV).astype(BF16)
    gr_ref[...] = seg(OFF_GR, W_GR).astype(BF16)
    ga = _dot(xb, wga_ref[...]).astype(BF16)
    z = _dot(ga, wa2_ref[...]) + ba_ref[...]
    log_sig = jnp.minimum(z, 0.0) - jnp.log(1.0 + jnp.exp(-jnp.abs(z)))
    gg_ref[...] = log_sig * (1.0 / GLA_TAU)


def _proj(xp, xs, norm_g, w_main, w_ga, w_a2, b_a, qg, kg, bdq, bdk):
    def rows(width):
        return pl.BlockSpec((TM, width), lambda i: (i, 0))

    def full(a):
        return pl.BlockSpec(a.shape, lambda i: (0,) * a.ndim)

    consts = (norm_g, w_main, w_ga, w_a2, b_a, qg, kg, bdq, bdk)
    out_widths = ((W_SQ, BF16), (W_SK, F32), (W_SV, F32), (W_GQ, BF16), (W_GK, BF16), (W_GV, BF16),
                  (W_GQ, F32), (W_GR, BF16))
    return pl.pallas_call(
        _proj_kernel,
        grid=(N_TILES,),
        in_specs=[pl.BlockSpec((TM, D_MODEL), lambda i: (jnp.minimum(i, N_PROMPT_TILES - 1), 0)),
                  pl.BlockSpec((TM, D_MODEL), lambda i: (0, 0))] + [full(a) for a in consts],
        out_specs=[rows(w) for w, _ in out_widths],
        out_shape=[jax.ShapeDtypeStruct((N_ROWS, w), dt) for w, dt in out_widths],
        compiler_params=_cparams(("arbitrary",), 40),
        name="proj",
    )(xp, xs, *consts)


def _dup_kv_heads(x):
    r = pltpu.roll(x, HD, axis=1)
    lo = lax.broadcasted_iota(I32, x.shape, 1) < HD
    out = []
    for a in (jnp.where(lo, x, r), jnp.where(lo, r, x)):
        out.append(jnp.concatenate([a, a], axis=1).astype(BF16))
    return out


def _swa_blocks(q_blocks, k_blocks, v_blocks, sinks, valids, head_mask):
    tq = q_blocks[0].shape[0]
    n_q = 4 * tq
    scores = []
    for q, k, valid in zip(q_blocks, k_blocks, valids):
        s_t = _dot_nt(k, jnp.concatenate([q] * 4, axis=0) * head_mask)
        scores.append(s_t if valid is None else jnp.where(valid, s_t, -jnp.inf))
    eye = (lax.broadcasted_iota(I32, (n_q, n_q), 0) == lax.broadcasted_iota(I32, (n_q, n_q), 1)
           ).astype(F32).astype(BF16)
    probs = []
    for s_t, sink in zip(scores, sinks):
        m = jnp.maximum(jnp.max(s_t, axis=0, keepdims=True), sink)
        p_t = jnp.exp(s_t - m)
        den = jnp.sum(p_t, axis=0, keepdims=True) + jnp.exp(sink - m)
        probs.append(_dot_nt(eye, (p_t / den).astype(BF16)).astype(BF16))
    lane_head = lax.broadcasted_iota(I32, (tq, 4 * HD), 1) // HD
    outs = []
    for p, v in zip(probs, v_blocks):
        o_full = _dot(p, v)
        o = jnp.zeros((tq, 4 * HD), F32)
        for a in range(4):
            o = o + jnp.where(lane_head == a, o_full[a * tq:(a + 1) * tq], 0.0)
        outs.append(o)
    return outs


def _swa_prompt_kernel(q_ref, kp_ref, kc_ref, vp_ref, vc_ref, hm_ref, sink_ref, o_ref):
    i = pl.program_id(0)
    k_dup = _dup_kv_heads(jnp.concatenate([kp_ref[...], kc_ref[...]], axis=0))
    v_dup = _dup_kv_heads(jnp.concatenate([vp_ref[...], vc_ref[...]], axis=0))
    sink_row = [sink_ref[j][0:1, :] for j in range(2)]
    span = WINDOW + CHUNK
    key = lax.broadcasted_iota(I32, (span, 4 * CHUNK), 0)
    qs, ks, vs, sinks, valids, where = [], [], [], [], [], []
    for c in range(TM // CHUNK):
        lo = CHUNK * c
        valid = (i * TM - WINDOW + lo + key) >= 0
        for j in range(2):
            qs.append(q_ref[lo:lo + CHUNK, 4 * HD * j:4 * HD * (j + 1)])
            ks.append(k_dup[j][lo:lo + span])
            vs.append(v_dup[j][lo:lo + span])
            sinks.append(sink_row[j])
            valids.append(valid)
            where.append((lo, j))
    outs = _swa_blocks(qs, ks, vs, sinks, valids, hm_ref[...])
    for (lo, j), o in zip(where, outs):
        o_ref[lo:lo + CHUNK, 4 * HD * j:4 * HD * (j + 1)] = o.astype(BF16)


def _swa_prompt(sq, sk, sv, head_mask, sink_b):
    half = TM // 2
    prev = pl.BlockSpec((half, 2 * HD), lambda i: (jnp.maximum(2 * i - 1, 0), 0))
    cur = pl.BlockSpec((TM, 2 * HD), lambda i: (i, 0))
    return pl.pallas_call(
        _swa_prompt_kernel,
        grid=(N_PROMPT_TILES,),
        in_specs=[pl.BlockSpec((TM, W_SQ), lambda i: (i, 0)), prev, cur, prev, cur,
                  pl.BlockSpec(head_mask.shape, lambda i: (0, 0)),
                  pl.BlockSpec(sink_b.shape, lambda i: (0, 0, 0))],
        out_specs=pl.BlockSpec((TM, W_SQ), lambda i: (i, 0)),
        out_shape=jax.ShapeDtypeStruct((N_PROMPT, W_SQ), BF16),
        compiler_params=_cparams(("arbitrary",), 32),
        name="swa_prompt",
    )(sq, sk, sk, sv, sv, head_mask, sink_b)


def _swa_sample_kernel(q_ref, kc_ref, kn_ref, vc_ref, vn_ref, hm_ref, sink_ref, o_ref):
    k_dup = _dup_kv_heads(jnp.concatenate([kc_ref[...], kn_ref[...]], axis=0))
    v_dup = _dup_kv_heads(jnp.concatenate([vc_ref[...], vn_ref[...]], axis=0))
    sink_row = [sink_ref[j][0:1, :] for j in range(2)]
    qs = [q_ref[:, 4 * HD * j:4 * HD * (j + 1)] for j in range(2)]
    outs = _swa_blocks(qs, k_dup, v_dup, sink_row, [None, None], hm_ref[...])
    o_ref[...] = jnp.concatenate(outs, axis=1).astype(BF16)


def _swa_sample(sq, sk, sv, cache_k, cache_v, head_mask, sink_b):
    first = N_PROMPT // T_SAMPLE
    new = lambda width: pl.BlockSpec((T_SAMPLE, width), lambda b: (first + b, 0))
    cache = pl.BlockSpec((None, WINDOW, 2 * HD), lambda b: (b, 0, 0))
    return pl.pallas_call(
        _swa_sample_kernel,
        grid=(N_STREAMS,),
        in_specs=[new(W_SQ), cache, new(2 * HD), cache, new(2 * HD),
                  pl.BlockSpec(head_mask.shape, lambda b: (0, 0)),
                  pl.BlockSpec(sink_b.shape, lambda b: (0, 0, 0))],
        out_specs=pl.BlockSpec((T_SAMPLE, W_SQ), lambda b: (b, 0)),
        out_shape=jax.ShapeDtypeStruct((N_SAMPLE, W_SQ), BF16),
        compiler_params=_cparams(("arbitrary",), 32),
        name="swa_sample",
    )(sq, cache_k, sk, cache_v, sv, head_mask, sink_b)


def _gla_kernel(q_ref, k_ref, v_ref, g_ref, gr_ref, s0_ref, tri_ref, m_ref, bd_ref, gn_ref,
                og_ref, sfin_ref, st_ref, *, t, n_sub):
    c = pl.program_id(1)
    nb = t // SUB
    lanes = N_HEADS_GLA * DK

    @pl.when(c == 0)
    def _():
        st_ref[...] = s0_ref[...]

    n_rows = t * n_sub

    def group_row(x, period, offset):
        g = x.reshape(n_rows // period, period, lanes)[:, offset:offset + 1, :]
        return jnp.broadcast_to(g, (n_rows // period, period, lanes)).reshape(n_rows, lanes)

    q = q_ref[...].astype(F32)
    k = k_ref[...].astype(F32)
    g_hi, g_lo = _split_bf16(g_ref[...])
    b = _dot(tri_ref[...], g_hi) + _dot(tri_ref[...], g_lo)
    qd = (q * jnp.exp(b - group_row(b, SUB, 0))).astype(BF16)
    pos = lax.broadcasted_iota(I32, (n_rows, lanes), 0) & (t - 1)
    k_parts = []
    for blk in range(nb):
        arg = jnp.where(pos < SUB * (blk + 1), group_row(b, t, SUB * blk) - b, NEG_BIG)
        k_parts.append((k * jnp.exp(arg)).astype(BF16))
    k_cat = jnp.concatenate(k_parts, axis=1)
    qd_cat = jnp.concatenate([qd] * nb, axis=1)
    q_dec = (q * jnp.exp(b)).astype(BF16)
    k_last = (k * jnp.exp(group_row(b, t, t - 1) - b)).astype(BF16)
    row_a = lax.broadcasted_iota(I32, (N_HEADS_GLA * t, t), 0) & (t - 1)
    col_a = lax.broadcasted_iota(I32, (N_HEADS_GLA * t, t), 1)

    o_intra, q_decayed, state_add, state_decay = [], [], [], []
    for u in range(n_sub):
        rows = slice(u * t, (u + 1) * t)
        v = v_ref[rows, :]
        lhs = jnp.concatenate([qd_cat[rows]] * N_HEADS_GLA, axis=0) * m_ref[...]
        a = _dot_nt(lhs, k_cat[rows])
        a = jnp.where(row_a >= col_a, a, 0.0).astype(BF16)
        o_full = _dot(a, v)
        o_intra.append(jnp.concatenate(
            [o_full[h * t:(h + 1) * t, h * DV:(h + 1) * DV] for h in range(N_HEADS_GLA)], axis=1))
        q_decayed.append(q_dec[rows])
        state_add.append(_dot_tn(v, k_last[rows]) * bd_ref[...])
        state_decay.append(jnp.exp(b[(u + 1) * t - 1:(u + 1) * t, :]))

    st = st_ref[...]
    o_parts = []
    for u in range(n_sub):
        o_parts.append(o_intra[u] + _dot_nt(q_decayed[u], st.astype(BF16)))
        st = st * state_decay[u] + state_add[u]
    st_ref[...] = st
    o = jnp.concatenate(o_parts, axis=0) if n_sub > 1 else o_parts[0]

    gr = gr_ref[...].astype(F32)
    gate = gr / (1.0 + jnp.exp(-gr))
    outs = []
    for h in range(N_HEADS_GLA):
        sl = slice(h * DV, (h + 1) * DV)
        outs.append(_rms(o[:, sl]) * gn_ref[...] * gate[:, sl])
    og_ref[...] = jnp.concatenate(outs, axis=1).astype(BF16)

    @pl.when(c == pl.num_programs(1) - 1)
    def _():
        sfin_ref[...] = st


def _gla(gq, gk, gv, gg, gr, s0, tri, mask, bd, gn, *, t, n_sub, n_batch, n_steps, first_block, name):
    rows_per_step = t * n_sub

    def rows(width):
        return pl.BlockSpec((rows_per_step, width), lambda b, c: (first_block + b * n_steps + c, 0))

    def full(a):
        return pl.BlockSpec(a.shape, lambda b, c: (0,) * a.ndim)

    state = pl.BlockSpec((None,) + s0.shape[1:], lambda b, c: (b, 0, 0))
    return pl.pallas_call(
        functools.partial(_gla_kernel, t=t, n_sub=n_sub),
        grid=(n_batch, n_steps),
        in_specs=[rows(W_GQ), rows(W_GK), rows(W_GV), rows(W_GQ), rows(W_GR), state,
                  full(tri), full(mask), full(bd), full(gn)],
        out_specs=[pl.BlockSpec((rows_per_step, W_GV), lambda b, c: (b * n_steps + c, 0)), state],
        out_shape=[jax.ShapeDtypeStruct((n_batch * n_steps * rows_per_step, W_GV), BF16),
                   jax.ShapeDtypeStruct(s0.shape, F32)],
        scratch_shapes=[pltpu.VMEM(s0.shape[1:], F32)],
        compiler_params=_cparams(("arbitrary", "arbitrary"), 32),
        name=name,
    )(gq, gk, gv, gg, gr, s0, tri, mask, bd, gn)


def _merge_kernel(oswp_ref, osws_ref, ogp_ref, ogs_ref, xp_ref, xs_ref, wo1_ref, wo2_ref, gf_ref, wrh_ref, wrl_ref,
                  br_ref, tri_ref, xrow_ref, hrow_ref, wrow_ref, ti_ref, rk_ref, cnt_ref, base_ref, row_buf, row_sem):
    i = pl.program_id(0)

    @pl.when(i == 0)
    def _():
        base_ref[...] = jnp.zeros_like(base_ref)

    is_prompt = i < N_PROMPT_TILES
    x = jnp.where(is_prompt, xp_ref[...], xs_ref[...])
    o_swa = jnp.where(is_prompt, oswp_ref[...], osws_ref[...])
    og = jnp.where(is_prompt, ogp_ref[...], ogs_ref[...])
    h = x + (_dot(o_swa, wo1_ref[...]) + _dot(og, wo2_ref[...]))
    xn = _rms(h) * gf_ref[...]
    x_hi, x_lo = _split_bf16(xn)

    logits_t = (_dot_nt(wrh_ref[...], x_hi) + _dot_nt(wrh_ref[...], x_lo) + _dot_nt(wrl_ref[...], x_hi)
                + br_ref[:, 0:1])
    expert = lax.broadcasted_iota(I32, logits_t.shape, 0)
    slot = lax.broadcasted_iota(I32, (SLOT_ROWS, TM), 0)
    vals, hots = [], []
    ti = jnp.zeros((SLOT_ROWS, TM), I32)
    for kk in range(TOP_K):
        m = jnp.max(logits_t, axis=0, keepdims=True)
        idx = jnp.min(jnp.where(logits_t == m, expert, N_EXPERTS), axis=0, keepdims=True)
        hot = expert == idx
        logits_t = jnp.where(hot, NEG_BIG, logits_t)
        vals.append(m)
        hots.append(hot)
        ti = jnp.where(slot == kk, idx, ti)
    ti_ref[...] = ti
    exps = [jnp.exp(v - vals[0]) for v in vals]
    den = exps[0] + exps[1] + exps[2] + exps[3]
    tw_t = jnp.zeros((SLOT_ROWS, TM), F32)
    for kk in range(TOP_K):
        tw_t = jnp.where(slot == kk, exps[kk] / den, tw_t)
    eye = (lax.broadcasted_iota(I32, (SLOT_ROWS, LANES), 0)
           == lax.broadcasted_iota(I32, (SLOT_ROWS, LANES), 1)).astype(F32).astype(BF16)
    w_hi = tw_t.astype(BF16)
    w_mid, w_lo = _split_bf16(tw_t - w_hi.astype(F32))
    tw_col = _dot_tn(w_hi, eye) + _dot_tn(w_mid, eye) + _dot_tn(w_lo, eye)

    onehot_t = jnp.zeros(logits_t.shape, F32)
    for hot in hots:
        onehot_t = onehot_t + jnp.where(hot, 1.0, 0.0)
    before_t = _dot(onehot_t.astype(BF16), tri_ref[...]) + base_ref[:, 0:1]
    rk = jnp.zeros((SLOT_ROWS, TM), I32)
    for kk in range(TOP_K):
        r = jnp.sum(jnp.where(hots[kk], before_t, 0.0), axis=0, keepdims=True).astype(I32)
        rk = jnp.where(slot == kk, r, rk)
    rk_ref[...] = rk
    total = base_ref[...] + jnp.sum(onehot_t, axis=1, keepdims=True)
    base_ref[...] = total
    cnt_ref[...] = total.astype(I32)

    w_lanes = [jnp.broadcast_to(tw_col[:, kk:kk + 1], (TM, LANES)) for kk in range(TOP_K)]
    w_rows = jnp.concatenate(w_lanes + [jnp.zeros((TM, D_MODEL - TOP_K * LANES), F32)], axis=1)
    outputs = ((xrow_ref, xn), (hrow_ref, h), (wrow_ref, w_rows))
    buf_slot = lax.rem(i, 2)

    def store(j, tile, s):
        return _tile_row_copies(outputs[j][0], tile, row_buf.at[j, s], row_sem.at[j, s], to_hbm=True)

    for j, (_, value) in enumerate(outputs):
        @pl.when(i >= 2)
        def _():
            for c in store(j, i - 2, buf_slot):
                c.wait()

        row_buf[j, buf_slot] = value
        for c in store(j, i, buf_slot):
            c.start()

        @pl.when(i == N_TILES - 1)
        def _():
            for c in store(j, i - 1, 1 - buf_slot) + store(j, i, buf_slot):
                c.wait()


def _merge(o_swa_p, o_swa_s, og_p, og_s, xp, xs, wo1, wo2, gf, wrh, wrl, br, tri):
    def prompt_rows(width):
        return pl.BlockSpec((TM, width), lambda i: (jnp.minimum(i, N_PROMPT_TILES - 1), 0))

    def sample_rows(width):
        return pl.BlockSpec((TM, width), lambda i: (0, 0))

    def full(a):
        return pl.BlockSpec(a.shape, lambda i: (0,) * a.ndim)

    consts = (wo1, wo2, gf, wrh, wrl, br, tri)
    return pl.pallas_call(
        _merge_kernel,
        grid=(N_TILES,),
        in_specs=[prompt_rows(W_SQ), sample_rows(W_SQ), prompt_rows(W_GV), sample_rows(W_GV),
                  prompt_rows(D_MODEL), sample_rows(D_MODEL)] + [full(a) for a in consts],
        out_specs=[pl.BlockSpec(memory_space=pl.ANY)] * 3 + [
                   pl.BlockSpec((SLOT_ROWS, TM), lambda i: (0, i)),
                   pl.BlockSpec((SLOT_ROWS, TM), lambda i: (0, i)),
                   pl.BlockSpec((N_EXPERTS, LANES), lambda i: (0, 0))],
        out_shape=[jax.ShapeDtypeStruct((N_ROWS,) + ROW_TILE, F32)] * 3 + [
                   jax.ShapeDtypeStruct((SLOT_ROWS, N_ROWS), I32),
                   jax.ShapeDtypeStruct((SLOT_ROWS, N_ROWS), I32),
                   jax.ShapeDtypeStruct((N_EXPERTS, LANES), I32)],
        scratch_shapes=[pltpu.VMEM((N_EXPERTS, LANES), F32), pltpu.VMEM((3, 2, TM, D_MODEL), F32),
                        pltpu.SemaphoreType.DMA((3, 2))],
        compiler_params=_cparams(("arbitrary",), 32),
        name="merge_router",
    )(o_swa_p, o_swa_s, og_p, og_s, xp, xs, *consts)


def _row_copy(src_ref, src_row, dst_ref, dst_row, sem):
    return pltpu.make_async_copy(src_ref.at[pl.ds(src_row, 1)], dst_ref.at[pl.ds(dst_row, 1)], sem)


def _dispatch_kernel(dest_ref, end_ref, x_ref, xs_ref, zero_ref, sem, zsem):
    i = pl.program_id(0)
    base = i * TM

    @pl.when(i == 0)
    def _():
        zero_ref[...] = jnp.zeros_like(zero_ref)

        def tail_copy(e):
            last = jnp.maximum(end_ref[e] - TM, 0)
            return pltpu.make_async_copy(zero_ref, xs_ref.at[pl.ds(pl.multiple_of(last, TM), TM)], zsem)

        def fill(e, carry):
            tail_copy(e).start()
            return carry

        def fill_wait(e, carry):
            tail_copy(e).wait()
            return carry

        lax.fori_loop(0, N_EXPERTS, fill, 0)
        lax.fori_loop(0, N_EXPERTS, fill_wait, 0)

        def unused_copy(t):
            return pltpu.make_async_copy(zero_ref, xs_ref.at[pl.ds(pl.multiple_of(t * TM, TM), TM)], zsem)

        def fill_unused(t, carry):
            unused_copy(t).start()
            unused_copy(t).wait()
            return carry

        lax.fori_loop(end_ref[N_EXPERTS - 1] // TM, N_EXPERT_TILES, fill_unused, 0)

    def issue(n, carry):
        for kk in range(TOP_K):
            _row_copy(x_ref, n, xs_ref, dest_ref[kk * N_ROWS + base + n], sem).start(priority=kk % 2)
        return carry

    lax.fori_loop(0, TM, issue, 0)

    for kk in range(TOP_K):
        pltpu.make_async_copy(x_ref, xs_ref.at[pl.ds(0, TM)], sem).wait()


def _dispatch(dest, end, x_packed):
    return pl.pallas_call(
        _dispatch_kernel,
        grid_spec=pltpu.PrefetchScalarGridSpec(
            num_scalar_prefetch=2,
            grid=(N_TILES,),
            in_specs=[pl.BlockSpec((TM,) + ROW_TILE, lambda i, d, e: (i, 0, 0))],
            out_specs=pl.BlockSpec(memory_space=pl.ANY),
            scratch_shapes=[pltpu.VMEM((TM,) + ROW_TILE, F32), pltpu.SemaphoreType.DMA,
                            pltpu.SemaphoreType.DMA],
        ),
        out_shape=jax.ShapeDtypeStruct((N_SORTED_ROWS,) + ROW_TILE, F32),
        compiler_params=_cparams(("arbitrary",), 32),
        name="dispatch",
    )(dest, end, x_packed)


CAST_ROWS = 128


def _tile_row_copies(hbm_ref, tile, vmem_ref, sem, to_hbm):
    copies = []
    for a in range(ROW_SUB):
        h = hbm_ref.at[pl.ds(tile * TM, TM), a, :]
        v = vmem_ref.at[:, pl.ds(a * LANES, LANES)]
        copies.append(pltpu.make_async_copy(v, h, sem) if to_hbm else pltpu.make_async_copy(h, v, sem))
    return copies


def _moe_kernel(te_ref, nu_ref, nx_ref, xs_ref, wg_ref, bg_ref, wu_ref, bu_ref, wd_ref, bd_ref, ys_ref,
                w_stage, w_bf, x_buf, y_buf, zero_buf, w_sem, in_sem, out_sem, zero_sem):
    t = pl.program_id(0)
    n_used = nu_ref[0]
    slot = lax.rem(t, 2)
    e = te_ref[t]
    e_prev = te_ref[jnp.maximum(t - 1, 0)]

    def load(tile, s):
        return _tile_row_copies(xs_ref, tile, x_buf.at[s], in_sem.at[s], to_hbm=False)

    def store(tile, s):
        return _tile_row_copies(ys_ref, tile, y_buf.at[s], out_sem.at[s], to_hbm=True)

    def weight_copies(expert):
        return [pltpu.make_async_copy(w.at[expert], w_stage.at[j], w_sem.at[j])
                for j, w in enumerate((wg_ref, wu_ref, wd_ref))]

    @pl.when(t == 0)
    def _():
        for c in weight_copies(e) + load(0, 0):
            c.start()

    @pl.when(t + 1 < n_used)
    def _():
        for c in load(t + 1, 1 - slot):
            c.start()

    @pl.when((t == 0) | (e != e_prev))
    def _():
        for c in weight_copies(e):
            c.wait()

        def cast(r, carry):
            sl = pl.ds(pl.multiple_of(r * CAST_ROWS, CAST_ROWS), CAST_ROWS)
            for j in range(3):
                w_bf[j, sl, :] = w_stage[j, sl, :].astype(BF16)
            return carry

        lax.fori_loop(0, D_MODEL // CAST_ROWS, cast, 0)
        e_next = nx_ref[e]

        @pl.when(e_next >= 0)
        def _():
            for c in weight_copies(e_next):
                c.start()

    @pl.when(t < n_used)
    def _():
        for c in load(t, slot):
            c.wait()

        @pl.when(t >= 2)
        def _():
            for c in store(t - 2, slot):
                c.wait()

        x = x_buf[slot].astype(BF16)
        gate = jnp.minimum(_dot(x, w_bf[0]) + bg_ref[...], SWIGLU_LIMIT)
        up = jnp.clip(_dot(x, w_bf[1]) + bu_ref[...], -SWIGLU_LIMIT, SWIGLU_LIMIT)
        hdn = (up + 1.0) * gate * (1.0 / (1.0 + jnp.exp(-SWIGLU_ALPHA * gate)))
        y_buf[slot] = _dot(hdn.astype(BF16), w_bf[2]) + bd_ref[...]
        for c in store(t, slot):
            c.start()

    @pl.when(t >= n_used)
    def _():
        zero_buf[...] = jnp.zeros_like(zero_buf)
        fill = pltpu.make_async_copy(zero_buf, ys_ref.at[pl.ds(t * TM, TM)], zero_sem)
        fill.start()
        fill.wait()

    @pl.when(t == N_EXPERT_TILES - 1)
    def _():
        @pl.when(n_used >= 2)
        def _():
            for c in store(n_used - 2, lax.rem(n_used, 2)):
                c.wait()

        for c in store(n_used - 1, lax.rem(n_used - 1, 2)):
            c.wait()


def _moe(tile_expert, n_used, next_expert, xs, w_gate, b_gate, w_up, b_up, w_down, b_down):
    hbm = pl.BlockSpec(memory_space=pl.ANY)
    bias = pl.BlockSpec((None, 1, D_MODEL), lambda t, te, nu, nx: (te[t], 0, 0))
    return pl.pallas_call(
        _moe_kernel,
        grid_spec=pltpu.PrefetchScalarGridSpec(
            num_scalar_prefetch=3,
            grid=(N_EXPERT_TILES,),
            in_specs=[hbm, hbm, bias, hbm, bias, hbm, bias],
            out_specs=hbm,
            scratch_shapes=[pltpu.VMEM((3, D_MODEL, D_MODEL), F32), pltpu.VMEM((3, D_MODEL, D_MODEL), BF16),
                            pltpu.VMEM((2, TM, D_MODEL), F32), pltpu.VMEM((2, TM, D_MODEL), F32),
                            pltpu.VMEM((TM,) + ROW_TILE, F32), pltpu.SemaphoreType.DMA((3,)),
                            pltpu.SemaphoreType.DMA((2,)), pltpu.SemaphoreType.DMA((2,)),
                            pltpu.SemaphoreType.DMA],
        ),
        out_shape=jax.ShapeDtypeStruct((N_SORTED_ROWS,) + ROW_TILE, F32),
        compiler_params=_cparams(("arbitrary",), 52),
        name="experts",
    )(tile_expert, n_used, next_expert, xs, w_gate, b_gate, w_up, b_up, w_down, b_down)


def _combine_kernel(dest_ref, h_ref, w_ref, ys_ref, yp_ref, ysm_ref, g_ref, out_ref, sem):
    i = pl.program_id(0)
    slot = lax.rem(i, 2)

    def issue_tile(tile, s):
        base = tile * TM

        def issue(n, carry):
            for kk in range(TOP_K):
                _row_copy(ys_ref, dest_ref[kk * N_ROWS + base + n], g_ref.at[s], kk * TM + n,
                          sem.at[s]).start(priority=kk % 2)
            return carry

        lax.fori_loop(0, TM, issue, 0)

    @pl.when(i == 0)
    def _():
        issue_tile(0, 0)

    @pl.when(i + 1 < N_TILES)
    def _():
        issue_tile(i + 1, 1 - slot)

    for kk in range(TOP_K):
        pltpu.make_async_copy(ys_ref.at[pl.ds(0, TM)], g_ref.at[slot, pl.ds(kk * TM, TM)], sem.at[slot]).wait()

    w = w_ref[...]
    acc = None
    for kk in range(TOP_K):
        part = jnp.broadcast_to(w[:, kk:kk + 1, :], (TM,) + ROW_TILE) * g_ref[slot, kk * TM:(kk + 1) * TM]
        acc = part if acc is None else acc + part
    out_ref[...] = h_ref[...] + acc

    def write(y_ref):
        for a in range(ROW_SUB):
            y_ref[:, a * LANES:(a + 1) * LANES] = out_ref[:, a, :]

    @pl.when(i < N_PROMPT_TILES)
    def _():
        write(yp_ref)

    @pl.when(i >= N_PROMPT_TILES)
    def _():
        write(ysm_ref)


def _combine(dest, h_rows, w_rows, ys):
    tile = lambda index: pl.BlockSpec((TM,) + ROW_TILE, index)
    return pl.pallas_call(
        _combine_kernel,
        grid_spec=pltpu.PrefetchScalarGridSpec(
            num_scalar_prefetch=1,
            grid=(N_TILES,),
            in_specs=[tile(lambda i, d: (i, 0, 0)), tile(lambda i, d: (i, 0, 0)),
                      pl.BlockSpec(memory_space=pl.ANY)],
            out_specs=[pl.BlockSpec((TM, D_MODEL), lambda i, d: (jnp.minimum(i, N_PROMPT_TILES - 1), 0)),
                       pl.BlockSpec((TM, D_MODEL), lambda i, d: (0, 0))],
            scratch_shapes=[pltpu.VMEM((2, TOP_K * TM) + ROW_TILE, F32), pltpu.VMEM((TM,) + ROW_TILE, F32),
                            pltpu.SemaphoreType.DMA((2,))],
        ),
        out_shape=[jax.ShapeDtypeStruct((N_PROMPT, D_MODEL), F32),
                   jax.ShapeDtypeStruct((N_SAMPLE, D_MODEL), F32)],
        compiler_params=_cparams(("arbitrary",), 40),
        name="combine",
    )(dest, h_rows, w_rows, ys)


def _block_diag_ones(n, blk):
    idx = np.arange(n) // blk
    return (idx[:, None] == idx[None, :]).astype(np.float32)


def _swa_head_mask(tq):
    row_head = np.arange(4 * tq)[:, None] // tq
    lane_head = np.arange(4 * HD)[None, :] // HD
    return jnp.asarray((row_head == lane_head).astype(np.float32), BF16)


def _gla_masks(t, n_sub):
    nb = t // SUB
    lanes = N_HEADS_GLA * DK
    tri = jnp.asarray(np.kron(np.eye(n_sub, dtype=np.float32), np.tril(np.ones((t, t), np.float32))), BF16)
    row = np.arange(N_HEADS_GLA * t)
    col = np.arange(nb * lanes)
    same_head = (row[:, None] // t) == ((col[None, :] % lanes) // DK)
    same_blk = ((row[:, None] % t) // SUB) == (col[None, :] // lanes)
    mask = jnp.asarray((same_head & same_blk).astype(np.float32), BF16)
    return tri, mask


def _sink_rows(sinks, tq):
    s = jnp.repeat(sinks.astype(F32).reshape(2, 4), tq, axis=1)
    return jnp.broadcast_to(s[:, None, :], (2, 8, 4 * tq))


def kernel(x_prompt, x_sample, state_gla, cache_swa_k, cache_swa_v, norm_mix_g, w_in, w_gla_a2, b_gla_a, q_norm_g,
           k_norm_g, swa_sinks, gla_norm_g, w_out, norm_ffn_g, w_router, b_router, w_gate, b_gate, w_up, b_up,
           w_down, b_down):
    xp = x_prompt.reshape(N_PROMPT, D_MODEL)
    xs = x_sample.reshape(N_SAMPLE, D_MODEL)

    w_in0 = w_in[0]
    w_main = w_in0[:, :W_MAIN].astype(BF16)
    w_ga = jnp.pad(w_in0[:, OFF_GA:], ((0, 0), (0, LANES - W_GA))).astype(BF16)
    w_a2 = jnp.pad(w_gla_a2[0], ((0, LANES - W_GA), (0, 0))).astype(BF16)
    b_a = b_gla_a[0].reshape(1, -1)
    qg = (jnp.tile(q_norm_g[0], N_HEADS_SWA) * (HD ** -0.5)).reshape(1, -1)
    kg = jnp.tile(k_norm_g[0], 2).reshape(1, -1)
    bdq = jnp.asarray(_block_diag_ones(W_SQ, HD), BF16)
    bdk = jnp.asarray(_block_diag_ones(W_SK, HD), BF16)

    sq, sk, sv, gq, gk, gv, gg, gr = _proj(xp, xs, norm_mix_g[0].reshape(1, -1), w_main, w_ga, w_a2, b_a,
                                           qg, kg, bdq, bdk)

    o_swa_p = _swa_prompt(sq, sk, sv, _swa_head_mask(CHUNK), _sink_rows(swa_sinks[0], CHUNK))
    cache_k = cache_swa_k[0].reshape(N_STREAMS, WINDOW, 2 * HD)
    cache_v = cache_swa_v[0].reshape(N_STREAMS, WINDOW, 2 * HD)
    o_swa_s = _swa_sample(sq, sk, sv, cache_k, cache_v, _swa_head_mask(T_SAMPLE),
                          _sink_rows(swa_sinks[0], T_SAMPLE))

    bd_state = jnp.asarray(_block_diag_ones(N_HEADS_GLA, 1).repeat(DV, axis=0).repeat(DK, axis=1), F32)
    gn = gla_norm_g[0].reshape(1, -1)
    tri_p, mask_p = _gla_masks(CHUNK, TM // CHUNK)
    tri_s, mask_s = _gla_masks(T_SAMPLE, 1)
    s0_p = jnp.zeros((1, N_HEADS_GLA * DV, N_HEADS_GLA * DK), F32)
    og_p, sfin_p = _gla(gq, gk, gv, gg, gr, s0_p, tri_p, mask_p, bd_state, gn, t=CHUNK, n_sub=TM // CHUNK,
                        n_batch=1, n_steps=N_PROMPT // TM, first_block=0, name="gla_prompt")
    eye = jnp.eye(N_HEADS_GLA, dtype=F32)
    s0_s = jnp.einsum('bhde,hg->bhegd', state_gla[0].astype(F32), eye).reshape(
        N_STREAMS, N_HEADS_GLA * DV, N_HEADS_GLA * DK)
    og_s, sfin_s = _gla(gq, gk, gv, gg, gr, s0_s, tri_s, mask_s, bd_state, gn, t=T_SAMPLE, n_sub=1,
                        n_batch=N_STREAMS, n_steps=1, first_block=N_PROMPT // T_SAMPLE, name="gla_sample")

    def unpack_state(sfin):
        s = sfin.reshape(-1, N_HEADS_GLA, DV, N_HEADS_GLA, DK)
        s = jnp.stack([s[:, h, :, h, :] for h in range(N_HEADS_GLA)], axis=1)
        return jnp.transpose(s, (0, 1, 3, 2))[None]

    w_out0 = w_out[0].astype(BF16)
    wr = w_router[0].T
    wr_hi = wr.astype(BF16)
    wr_lo = (wr - wr_hi.astype(F32)).astype(BF16)
    br = jnp.broadcast_to(b_router[0].astype(F32)[:, None], (N_EXPERTS, LANES))
    tri_strict = jnp.asarray(np.triu(np.ones((TM, TM), np.float32), 1), BF16)
    x_rows, h_rows, w_rows, top_i, rank, counts = _merge(
        o_swa_p, o_swa_s, og_p, og_s, xp, xs, w_out0[:W_SQ], w_out0[W_SQ:], norm_ffn_g[0].reshape(1, -1), wr_hi, wr_lo, br,
        tri_strict)

    counts = counts[:, 0]
    padded = (counts + TM - 1) // TM * TM
    end = jnp.cumsum(padded)
    start = end - padded
    experts = jnp.arange(N_EXPERTS, dtype=I32)
    is_e = top_i[:TOP_K, :, None] == experts
    dest = (rank[:TOP_K] + jnp.sum(jnp.where(is_e, start, 0), axis=-1)).reshape(-1).astype(I32)
    n_used = (end[-1] // TM).astype(I32)
    tiles = jnp.minimum(jnp.arange(N_EXPERT_TILES, dtype=I32), n_used - 1)
    tile_expert = jnp.sum((tiles[:, None] * TM >= end[None, :]).astype(I32), axis=1)
    later_nonempty = (experts[None, :] > experts[:, None]) & (padded[None, :] > 0)
    next_expert = jnp.min(jnp.where(later_nonempty, experts[None, :], N_EXPERTS), axis=1)
    next_expert = jnp.where(next_expert < N_EXPERTS, next_expert, -1).astype(I32)

    xs_sorted = _dispatch(dest, end.astype(I32), x_rows)
    ys = _moe(tile_expert, n_used.reshape(1), next_expert, xs_sorted, w_gate[0], b_gate[0].reshape(N_EXPERTS, 1, -1),
              w_up[0], b_up[0].reshape(N_EXPERTS, 1, -1), w_down[0], b_down[0].reshape(N_EXPERTS, 1, -1))
    y_p, y_s = _combine(dest, h_rows, w_rows, ys)

    sk_s = sk[N_PROMPT:].reshape(N_STREAMS, T_SAMPLE, 2 * HD)
    sv_s = sv[N_PROMPT:].reshape(N_STREAMS, T_SAMPLE, 2 * HD)
    kc_s = jnp.concatenate([cache_k[:, T_SAMPLE:], sk_s], axis=1).reshape(1, N_STREAMS, WINDOW, 2, HD)
    vc_s = jnp.concatenate([cache_v[:, T_SAMPLE:], sv_s], axis=1).reshape(1, N_STREAMS, WINDOW, 2, HD)
    kc_p = sk[N_PROMPT - WINDOW:N_PROMPT].reshape(1, 1, WINDOW, 2, HD)
    vc_p = sv[N_PROMPT - WINDOW:N_PROMPT].reshape(1, 1, WINDOW, 2, HD)
    return (y_p.reshape(1, N_PROMPT, D_MODEL), y_s.reshape(N_STREAMS, T_SAMPLE, D_MODEL),
            unpack_state(sfin_p), kc_p, vc_p, unpack_state(sfin_s), kc_s, vc_s)
```

```python
import functools

import numpy as np
import jax
import jax.numpy as jnp
from jax import lax
from jax.experimental import pallas as pl
from jax.experimental.pallas import tpu as pltpu

F32 = jnp.float32
BF16 = jnp.bfloat16
I32 = jnp.int32
U32 = jnp.uint32

D_MODEL = 1024
N_PROMPT = 16384
N_STREAMS = 8
T_SAMPLE = 32
N_SAMPLE = N_STREAMS * T_SAMPLE
N_ROWS = N_PROMPT + N_SAMPLE
EPS = 1e-6

CHUNK = 64
SUB = 16
N_HEADS_SWA = 8
HD = 64
WINDOW = 128
N_HEADS_GLA = 4
DK = 64
DV = 128
GLA_TAU = 16.0
N_EXPERTS = 32
TOP_K = 4
SWIGLU_ALPHA = 1.702
SWIGLU_LIMIT = 7.0

TM = 256
N_TILES = N_ROWS // TM
N_PROMPT_TILES = N_PROMPT // TM
N_ASSIGN = N_ROWS * TOP_K
N_EXPERT_TILES = N_ASSIGN // TM + N_EXPERTS
N_SORTED_ROWS = N_EXPERT_TILES * TM
LANES = 128
SLOT_ROWS = 16
NEG_BIG = -1e30

W_SQ, W_SK, W_SV, W_GQ, W_GK, W_GV, W_GR, W_GA = 512, 128, 128, 256, 256, 512, 512, 16
OFF_SQ = 0
OFF_SK = OFF_SQ + W_SQ
OFF_SV = OFF_SK + W_SK
OFF_GQ = OFF_SV + W_SV
OFF_GK = OFF_GQ + W_GQ
OFF_GV = OFF_GK + W_GK
OFF_GR = OFF_GV + W_GV
OFF_GA = OFF_GR + W_GR
W_MAIN = OFF_GA


def _cparams(semantics, vmem_mib):
    return pltpu.CompilerParams(dimension_semantics=semantics, vmem_limit_bytes=vmem_mib * 2 ** 20)


def _dot(a, b):
    return jnp.dot(a, b, preferred_element_type=F32)


def _dot_nt(a, b):
    return lax.dot_general(a, b, (((1,), (1,)), ((), ())), preferred_element_type=F32)


def _dot_tn(a, b):
    return lax.dot_general(a, b, (((0,), (0,)), ((), ())), preferred_element_type=F32)


def _split_bf16(x):
    hi = x.astype(BF16)
    lo = (x - hi.astype(F32)).astype(BF16)
    return hi, lo


def _rms(x):
    return x * lax.rsqrt(jnp.mean(x * x, axis=-1, keepdims=True) + EPS)


ROW_SUB = 8
ROW_TILE = (ROW_SUB, LANES)


def _proj_kernel(xp_ref, xs_ref, g_ref, w_ref, wga_ref, wa2_ref, ba_ref, qg_ref, kg_ref, bdq_ref, bdk_ref,
                 sq_ref, sk_ref, sv_ref, gq_ref, gk_ref, gv_ref, gg_ref, gr_ref):
    i = pl.program_id(0)
    x = jnp.where(i < N_PROMPT_TILES, xp_ref[...], xs_ref[...])
    xb = (_rms(x) * g_ref[...]).astype(BF16)

    def seg(off, width):
        return _dot(xb, w_ref[:, off:off + width])

    def head_norm(u, bd_ref):
        hi, lo = _split_bf16(u * u)
        ss = _dot(hi, bd_ref[...]) + _dot(lo, bd_ref[...])
        return u * lax.rsqrt(ss * (1.0 / HD) + EPS)

    sq_ref[...] = (head_norm(seg(OFF_SQ, W_SQ), bdq_ref) * qg_ref[...]).astype(BF16)
    sk_ref[...] = head_norm(seg(OFF_SK, W_SK), bdk_ref) * kg_ref[...]
    sv_ref[...] = seg(OFF_SV, W_SV)
    gq_ref[...] = (seg(OFF_GQ, W_GQ) * (DK ** -0.5)).astype(BF16)
    gk_ref[...] = seg(OFF_GK, W_GK).astype(BF16)
    gv_ref[...] = seg(OFF_GV, W_GV).astype(BF16)
    gr_ref[...] = seg(OFF_GR, W_GR).astype(BF16)
    ga = _dot(xb, wga_ref[...]).astype(BF16)
    z = _dot(ga, wa2_ref[...]) + ba_ref[...]
    log_sig = jnp.minimum(z, 0.0) - jnp.log(1.0 + jnp.exp(-jnp.abs(z)))
    gg_ref[...] = log_sig * (1.0 / GLA_TAU)


def _proj(xp, xs, norm_g, w_main, w_ga, w_a2, b_a, qg, kg, bdq, bdk):
    def rows(width):
        return pl.BlockSpec((TM, width), lambda i: (i, 0))

    def full(a):
        return pl.BlockSpec(a.shape, lambda i: (0,) * a.ndim)

    consts = (norm_g, w_main, w_ga, w_a2, b_a, qg, kg, bdq, bdk)
    out_widths = ((W_SQ, BF16), (W_SK, F32), (W_SV, F32), (W_GQ, BF16), (W_GK, BF16), (W_GV, BF16),
                  (W_GQ, F32), (W_GR, BF16))
    return pl.pallas_call(
        _proj_kernel,
        grid=(N_TILES,),
        in_specs=[pl.BlockSpec((TM, D_MODEL), lambda i: (jnp.minimum(i, N_PROMPT_TILES - 1), 0)),
                  pl.BlockSpec((TM, D_MODEL), lambda i: (0, 0))] + [full(a) for a in consts],
        out_specs=[rows(w) for w, _ in out_widths],
        out_shape=[jax.ShapeDtypeStruct((N_ROWS, w), dt) for w, dt in out_widths],
        compiler_params=_cparams(("arbitrary",), 40),
        name="proj",
    )(xp, xs, *consts)


def _dup_kv_heads(x):
    r = pltpu.roll(x, HD, axis=1)
    lo = lax.broadcasted_iota(I32, x.shape, 1) < HD
    out = []
    for a in (jnp.where(lo, x, r), jnp.where(lo, r, x)):
        out.append(jnp.concatenate([a, a], axis=1).astype(BF16))
    return out


def _swa_blocks(q_blocks, k_blocks, v_blocks, sinks, valids, head_mask):
    tq = q_blocks[0].shape[0]
    n_q = 4 * tq
    scores = []
    for q, k, valid in zip(q_blocks, k_blocks, valids):
        s_t = _dot_nt(k, jnp.concatenate([q] * 4, axis=0) * head_mask)
        scores.append(s_t if valid is None else jnp.where(valid, s_t, -jnp.inf))
    eye = (lax.broadcasted_iota(I32, (n_q, n_q), 0) == lax.broadcasted_iota(I32, (n_q, n_q), 1)
           ).astype(F32).astype(BF16)
    probs = []
    for s_t, sink in zip(scores, sinks):
        m = jnp.maximum(jnp.max(s_t, axis=0, keepdims=True), sink)
        p_t = jnp.exp(s_t - m)
        den = jnp.sum(p_t, axis=0, keepdims=True) + jnp.exp(sink - m)
        probs.append(_dot_nt(eye, (p_t / den).astype(BF16)).astype(BF16))
    lane_head = lax.broadcasted_iota(I32, (tq, 4 * HD), 1) // HD
    outs = []
    for p, v in zip(probs, v_blocks):
        o_full = _dot(p, v)
        o = jnp.zeros((tq, 4 * HD), F32)
        for a in range(4):
            o = o + jnp.where(lane_head == a, o_full[a * tq:(a + 1) * tq], 0.0)
        outs.append(o)
    return outs


def _swa_prompt_kernel(q_ref, kp_ref, kc_ref, vp_ref, vc_ref, hm_ref, sink_ref, o_ref):
    i = pl.program_id(0)
    k_dup = _dup_kv_heads(jnp.concatenate([kp_ref[...], kc_ref[...]], axis=0))
    v_dup = _dup_kv_heads(jnp.concatenate([vp_ref[...], vc_ref[...]], axis=0))
    sink_row = [sink_ref[j][0:1, :] for j in range(2)]
    span = WINDOW + CHUNK
    key = lax.broadcasted_iota(I32, (span, 4 * CHUNK), 0)
    qs, ks, vs, sinks, valids, where = [], [], [], [], [], []
    for c in range(TM // CHUNK):
        lo = CHUNK * c
        valid = (i * TM - WINDOW + lo + key) >= 0
        for j in range(2):
            qs.append(q_ref[lo:lo + CHUNK, 4 * HD * j:4 * HD * (j + 1)])
            ks.append(k_dup[j][lo:lo + span])
            vs.append(v_dup[j][lo:lo + span])
            sinks.append(sink_row[j])
            valids.append(valid)
            where.append((lo, j))
    outs = _swa_blocks(qs, ks, vs, sinks, valids, hm_ref[...])
    for (lo, j), o in zip(where, outs):
        o_ref[lo:lo + CHUNK, 4 * HD * j:4 * HD * (j + 1)] = o.astype(BF16)


def _swa_prompt(sq, sk, sv, head_mask, sink_b):
    half = TM // 2
    prev = pl.BlockSpec((half, 2 * HD), lambda i: (jnp.maximum(2 * i - 1, 0), 0))
    cur = pl.BlockSpec((TM, 2 * HD), lambda i: (i, 0))
    return pl.pallas_call(
        _swa_prompt_kernel,
        grid=(N_PROMPT_TILES,),
        in_specs=[pl.BlockSpec((TM, W_SQ), lambda i: (i, 0)), prev, cur, prev, cur,
                  pl.BlockSpec(head_mask.shape, lambda i: (0, 0)),
                  pl.BlockSpec(sink_b.shape, lambda i: (0, 0, 0))],
        out_specs=pl.BlockSpec((TM, W_SQ), lambda i: (i, 0)),
        out_shape=jax.ShapeDtypeStruct((N_PROMPT, W_SQ), BF16),
        compiler_params=_cparams(("arbitrary",), 32),
        name="swa_prompt",
    )(sq, sk, sk, sv, sv, head_mask, sink_b)


def _swa_sample_kernel(q_ref, kc_ref, kn_ref, vc_ref, vn_ref, hm_ref, sink_ref, o_ref):
    k_dup = _dup_kv_heads(jnp.concatenate([kc_ref[...], kn_ref[...]], axis=0))
    v_dup = _dup_kv_heads(jnp.concatenate([vc_ref[...], vn_ref[...]], axis=0))
    sink_row = [sink_ref[j][0:1, :] for j in range(2)]
    qs = [q_ref[:, 4 * HD * j:4 * HD * (j + 1)] for j in range(2)]
    outs = _swa_blocks(qs, k_dup, v_dup, sink_row, [None, None], hm_ref[...])
    o_ref[...] = jnp.concatenate(outs, axis=1).astype(BF16)


def _swa_sample(sq, sk, sv, cache_k, cache_v, head_mask, sink_b):
    first = N_PROMPT // T_SAMPLE
    new = lambda width: pl.BlockSpec((T_SAMPLE, width), lambda b: (first + b, 0))
    cache = pl.BlockSpec((None, WINDOW, 2 * HD), lambda b: (b, 0, 0))
    return pl.pallas_call(
        _swa_sample_kernel,
        grid=(N_STREAMS,),
        in_specs=[new(W_SQ), cache, new(2 * HD), cache, new(2 * HD),
                  pl.BlockSpec(head_mask.shape, lambda b: (0, 0)),
                  pl.BlockSpec(sink_b.shape, lambda b: (0, 0, 0))],
        out_specs=pl.BlockSpec((T_SAMPLE, W_SQ), lambda b: (b, 0)),
        out_shape=jax.ShapeDtypeStruct((N_SAMPLE, W_SQ), BF16),
        compiler_params=_cparams(("arbitrary",), 32),
        name="swa_sample",
    )(sq, cache_k, sk, cache_v, sv, head_mask, sink_b)


def _gla_kernel(q_ref, k_ref, v_ref, g_ref, gr_ref, s0_ref, tri_ref, m_ref, bd_ref, gn_ref,
                og_ref, sfin_ref, st_ref, *, t, n_sub):
    c = pl.program_id(1)
    nb = t // SUB
    lanes = N_HEADS_GLA * DK

    @pl.when(c == 0)
    def _():
        st_ref[...] = s0_ref[...]

    n_rows = t * n_sub

    def group_row(x, period, offset):
        g = x.reshape(n_rows // period, period, lanes)[:, offset:offset + 1, :]
        return jnp.broadcast_to(g, (n_rows // period, period, lanes)).reshape(n_rows, lanes)

    q = q_ref[...].astype(F32)
    k = k_ref[...].astype(F32)
    g_hi, g_lo = _split_bf16(g_ref[...])
    b = _dot(tri_ref[...], g_hi) + _dot(tri_ref[...], g_lo)
    qd = (q * jnp.exp(b - group_row(b, SUB, 0))).astype(BF16)
    pos = lax.broadcasted_iota(I32, (n_rows, lanes), 0) & (t - 1)
    k_parts = []
    for blk in range(nb):
        arg = jnp.where(pos < SUB * (blk + 1), group_row(b, t, SUB * blk) - b, NEG_BIG)
        k_parts.append((k * jnp.exp(arg)).astype(BF16))
    k_cat = jnp.concatenate(k_parts, axis=1)
    qd_cat = jnp.concatenate([qd] * nb, axis=1)
    q_dec = (q * jnp.exp(b)).astype(BF16)
    k_last = (k * jnp.exp(group_row(b, t, t - 1) - b)).astype(BF16)
    row_a = lax.broadcasted_iota(I32, (N_HEADS_GLA * t, t), 0) & (t - 1)
    col_a = lax.broadcasted_iota(I32, (N_HEADS_GLA * t, t), 1)

    o_intra, q_decayed, state_add, state_decay = [], [], [], []
    for u in range(n_sub):
        rows = slice(u * t, (u + 1) * t)
        v = v_ref[rows, :]
        lhs = jnp.concatenate([qd_cat[rows]] * N_HEADS_GLA, axis=0) * m_ref[...]
        a = _dot_nt(lhs, k_cat[rows])
        a = jnp.where(row_a >= col_a, a, 0.0).astype(BF16)
        o_full = _dot(a, v)
        o_intra.append(jnp.concatenate(
            [o_full[h * t:(h + 1) * t, h * DV:(h + 1) * DV] for h in range(N_HEADS_GLA)], axis=1))
        q_decayed.append(q_dec[rows])
        state_add.append(_dot_tn(v, k_last[rows]) * bd_ref[...])
        state_decay.append(jnp.exp(b[(u + 1) * t - 1:(u + 1) * t, :]))

    st = st_ref[...]
    o_parts = []
    for u in range(n_sub):
        o_parts.append(o_intra[u] + _dot_nt(q_decayed[u], st.astype(BF16)))
        st = st * state_decay[u] + state_add[u]
    st_ref[...] = st
    o = jnp.concatenate(o_parts, axis=0) if n_sub > 1 else o_parts[0]

    gr = gr_ref[...].astype(F32)
    gate = gr / (1.0 + jnp.exp(-gr))
    outs = []
    for h in range(N_HEADS_GLA):
        sl = slice(h * DV, (h + 1) * DV)
        outs.append(_rms(o[:, sl]) * gn_ref[...] * gate[:, sl])
    og_ref[...] = jnp.concatenate(outs, axis=1).astype(BF16)

    @pl.when(c == pl.num_programs(1) - 1)
    def _():
        sfin_ref[...] = st


def _gla(gq, gk, gv, gg, gr, s0, tri, mask, bd, gn, *, t, n_sub, n_batch, n_steps, first_block, name):
    rows_per_step = t * n_sub

    def rows(width):
        return pl.BlockSpec((rows_per_step, width), lambda b, c: (first_block + b * n_steps + c, 0))

    def full(a):
        return pl.BlockSpec(a.shape, lambda b, c: (0,) * a.ndim)

    state = pl.BlockSpec((None,) + s0.shape[1:], lambda b, c: (b, 0, 0))
    return pl.pallas_call(
        functools.partial(_gla_kernel, t=t, n_sub=n_sub),
        grid=(n_batch, n_steps),
        in_specs=[rows(W_GQ), rows(W_GK), rows(W_GV), rows(W_GQ), rows(W_GR), state,
                  full(tri), full(mask), full(bd), full(gn)],
        out_specs=[pl.BlockSpec((rows_per_step, W_GV), lambda b, c: (b * n_steps + c, 0)), state],
        out_shape=[jax.ShapeDtypeStruct((n_batch * n_steps * rows_per_step, W_GV), BF16),
                   jax.ShapeDtypeStruct(s0.shape, F32)],
        scratch_shapes=[pltpu.VMEM(s0.shape[1:], F32)],
        compiler_params=_cparams(("arbitrary", "arbitrary"), 32),
        name=name,
    )(gq, gk, gv, gg, gr, s0, tri, mask, bd, gn)


def _tile_row_copies(hbm_ref, tile, vmem_ref, sem, to_hbm, rows=TM):
    copies = []
    for a in range(ROW_SUB):
        h = hbm_ref.at[pl.ds(tile * rows, rows), a, :]
        v = vmem_ref.at[:, pl.ds(a * LANES, LANES)]
        copies.append(pltpu.make_async_copy(v, h, sem) if to_hbm else pltpu.make_async_copy(h, v, sem))
    return copies


def _store_tile_rows(i, n_steps, outputs, row_buf, row_sem, rows=TM):
    buf_slot = lax.rem(i, 2)

    def store(j, tile, s):
        return _tile_row_copies(outputs[j][0], tile, row_buf.at[j, s], row_sem.at[j, s], True, rows)

    for j, (_, value) in enumerate(outputs):
        @pl.when(i >= 2)
        def _():
            for c in store(j, i - 2, buf_slot):
                c.wait()

        row_buf[j, buf_slot] = value
        for c in store(j, i, buf_slot):
            c.start()

        @pl.when(i == n_steps - 1)
        def _():
            for c in store(j, i - 1, 1 - buf_slot) + store(j, i, buf_slot):
                c.wait()


def _merge_kernel(oswp_ref, osws_ref, ogp_ref, ogs_ref, xp_ref, xs_ref, wo1_ref, wo2_ref, gf_ref, wrh_ref, wrl_ref,
                  br_ref, xrow_ref, hrow_ref, lg_ref, row_buf, row_sem):
    i = pl.program_id(0)
    is_prompt = i < N_PROMPT_TILES
    x = jnp.where(is_prompt, xp_ref[...], xs_ref[...])
    o_swa = jnp.where(is_prompt, oswp_ref[...], osws_ref[...])
    og = jnp.where(is_prompt, ogp_ref[...], ogs_ref[...])
    h = x + (_dot(o_swa, wo1_ref[...]) + _dot(og, wo2_ref[...]))
    xn = _rms(h) * gf_ref[...]
    x_hi, x_lo = _split_bf16(xn)
    logits = _dot(x_hi, wrh_ref[...]) + _dot(x_lo, wrh_ref[...]) + _dot(x_hi, wrl_ref[...]) + br_ref[...]
    lg_ref[...] = logits.T[:N_EXPERTS]
    _store_tile_rows(i, N_TILES, ((xrow_ref, xn), (hrow_ref, h)), row_buf, row_sem)


def _merge(o_swa_p, o_swa_s, og_p, og_s, xp, xs, wo1, wo2, gf, wrh, wrl, br):
    def prompt_rows(width):
        return pl.BlockSpec((TM, width), lambda i: (jnp.minimum(i, N_PROMPT_TILES - 1), 0))

    def sample_rows(width):
        return pl.BlockSpec((TM, width), lambda i: (0, 0))

    def full(a):
        return pl.BlockSpec(a.shape, lambda i: (0,) * a.ndim)

    consts = (wo1, wo2, gf, wrh, wrl, br)
    return pl.pallas_call(
        _merge_kernel,
        grid=(N_TILES,),
        in_specs=[prompt_rows(W_SQ), sample_rows(W_SQ), prompt_rows(W_GV), sample_rows(W_GV),
                  prompt_rows(D_MODEL), sample_rows(D_MODEL)] + [full(a) for a in consts],
        out_specs=[pl.BlockSpec(memory_space=pl.ANY)] * 2 + [pl.BlockSpec((N_EXPERTS, TM), lambda i: (0, i))],
        out_shape=[jax.ShapeDtypeStruct((N_ROWS,) + ROW_TILE, F32)] * 2 + [
                   jax.ShapeDtypeStruct((N_EXPERTS, N_ROWS), F32)],
        scratch_shapes=[pltpu.VMEM((2, 2, TM, D_MODEL), F32), pltpu.SemaphoreType.DMA((2, 2))],
        compiler_params=_cparams(("arbitrary",), 32),
        name="merge",
    )(o_swa_p, o_swa_s, og_p, og_s, xp, xs, *consts)


RT = 1280
N_ROUTER_STEPS = N_ROWS // RT


def _router_kernel(lg_ref, tri_ref, wrow_ref, ti_ref, rk_ref, cnt_ref, base_ref, row_buf, row_sem):
    i = pl.program_id(0)

    @pl.when(i == 0)
    def _():
        base_ref[...] = jnp.zeros_like(base_ref)

    logits_t = lg_ref[...]
    expert = lax.broadcasted_iota(I32, logits_t.shape, 0)
    slot = lax.broadcasted_iota(I32, (SLOT_ROWS, RT), 0)
    vals, hots = [], []
    ti = jnp.zeros((SLOT_ROWS, RT), I32)
    for kk in range(TOP_K):
        m = jnp.max(logits_t, axis=0, keepdims=True)
        idx = jnp.min(jnp.where(logits_t == m, expert, N_EXPERTS), axis=0, keepdims=True)
        hot = expert == idx
        logits_t = jnp.where(hot, NEG_BIG, logits_t)
        vals.append(m)
        hots.append(hot)
        ti = jnp.where(slot == kk, idx, ti)
    ti_ref[...] = ti
    exps = [jnp.exp(v - vals[0]) for v in vals]
    den = exps[0] + exps[1] + exps[2] + exps[3]
    tw_t = jnp.zeros((SLOT_ROWS, RT), F32)
    for kk in range(TOP_K):
        tw_t = jnp.where(slot == kk, exps[kk] / den, tw_t)
    eye = (lax.broadcasted_iota(I32, (SLOT_ROWS, LANES), 0)
           == lax.broadcasted_iota(I32, (SLOT_ROWS, LANES), 1)).astype(F32).astype(BF16)
    w_hi = tw_t.astype(BF16)
    w_mid, w_lo = _split_bf16(tw_t - w_hi.astype(F32))
    tw_col = _dot_tn(w_hi, eye) + _dot_tn(w_mid, eye) + _dot_tn(w_lo, eye)

    onehot_t = jnp.zeros(logits_t.shape, F32)
    for hot in hots:
        onehot_t = onehot_t + jnp.where(hot, 1.0, 0.0)
    before_t = _dot(onehot_t.astype(BF16), tri_ref[...]) + base_ref[:, 0:1]
    rk = jnp.zeros((SLOT_ROWS, RT), I32)
    for kk in range(TOP_K):
        r = jnp.sum(jnp.where(hots[kk], before_t, 0.0), axis=0, keepdims=True).astype(I32)
        rk = jnp.where(slot == kk, r, rk)
    rk_ref[...] = rk
    total = base_ref[...] + jnp.sum(onehot_t, axis=1, keepdims=True)
    base_ref[...] = total
    cnt_ref[...] = total.astype(I32)

    w_lanes = [jnp.broadcast_to(tw_col[:, kk:kk + 1], (RT, LANES)) for kk in range(TOP_K)]
    w_rows = jnp.concatenate(w_lanes + [jnp.zeros((RT, D_MODEL - TOP_K * LANES), F32)], axis=1)
    _store_tile_rows(i, N_ROUTER_STEPS, ((wrow_ref, w_rows),), row_buf, row_sem, RT)


def _router(logits, tri):
    return pl.pallas_call(
        _router_kernel,
        grid=(N_ROUTER_STEPS,),
        in_specs=[pl.BlockSpec((N_EXPERTS, RT), lambda i: (0, i)), pl.BlockSpec(tri.shape, lambda i: (0, 0))],
        out_specs=[pl.BlockSpec(memory_space=pl.ANY),
                   pl.BlockSpec((SLOT_ROWS, RT), lambda i: (0, i)),
                   pl.BlockSpec((SLOT_ROWS, RT), lambda i: (0, i)),
                   pl.BlockSpec((N_EXPERTS, LANES), lambda i: (0, 0))],
        out_shape=[jax.ShapeDtypeStruct((N_ROWS,) + ROW_TILE, F32),
                   jax.ShapeDtypeStruct((SLOT_ROWS, N_ROWS), I32),
                   jax.ShapeDtypeStruct((SLOT_ROWS, N_ROWS), I32),
                   jax.ShapeDtypeStruct((N_EXPERTS, LANES), I32)],
        scratch_shapes=[pltpu.VMEM((N_EXPERTS, LANES), F32), pltpu.VMEM((1, 2, RT, D_MODEL), F32),
                        pltpu.SemaphoreType.DMA((1, 2))],
        compiler_params=_cparams(("arbitrary",), 48),
        name="router",
    )(logits, tri)


def _row_copy(src_ref, src_row, dst_ref, dst_row, sem):
    return pltpu.make_async_copy(src_ref.at[pl.ds(src_row, 1)], dst_ref.at[pl.ds(dst_row, 1)], sem)


def _dispatch_kernel(dest_ref, end_ref, x_ref, xs_ref, zero_ref, sem, zsem):
    i = pl.program_id(0)
    base = i * TM

    @pl.when(i == 0)
    def _():
        zero_ref[...] = jnp.zeros_like(zero_ref)

        def tail_copy(e):
            last = jnp.maximum(end_ref[e] - TM, 0)
            return pltpu.make_async_copy(zero_ref, xs_ref.at[pl.ds(pl.multiple_of(last, TM), TM)], zsem)

        def fill(e, carry):
            tail_copy(e).start()
            return carry

        def fill_wait(e, carry):
            tail_copy(e).wait()
            return carry

        lax.fori_loop(0, N_EXPERTS, fill, 0)
        lax.fori_loop(0, N_EXPERTS, fill_wait, 0)

        def unused_copy(t):
            return pltpu.make_async_copy(zero_ref, xs_ref.at[pl.ds(pl.multiple_of(t * TM, TM), TM)], zsem)

        def fill_unused(t, carry):
            unused_copy(t).start()
            unused_copy(t).wait()
            return carry

        lax.fori_loop(end_ref[N_EXPERTS - 1] // TM, N_EXPERT_TILES, fill_unused, 0)

    def issue(n, carry):
        for kk in range(TOP_K):
            _row_copy(x_ref, n, xs_ref, dest_ref[kk * N_ROWS + base + n], sem).start(priority=kk % 2)
        return carry

    lax.fori_loop(0, TM, issue, 0)

    for kk in range(TOP_K):
        pltpu.make_async_copy(x_ref, xs_ref.at[pl.ds(0, TM)], sem).wait()


def _dispatch(dest, end, x_packed):
    return pl.pallas_call(
        _dispatch_kernel,
        grid_spec=pltpu.PrefetchScalarGridSpec(
            num_scalar_prefetch=2,
            grid=(N_TILES,),
            in_specs=[pl.BlockSpec((TM,) + ROW_TILE, lambda i, d, e: (i, 0, 0))],
            out_specs=pl.BlockSpec(memory_space=pl.ANY),
            scratch_shapes=[pltpu.VMEM((TM,) + ROW_TILE, F32), pltpu.SemaphoreType.DMA,
                            pltpu.SemaphoreType.DMA],
        ),
        out_shape=jax.ShapeDtypeStruct((N_SORTED_ROWS,) + ROW_TILE, F32),
        compiler_params=_cparams(("arbitrary",), 32),
        name="dispatch",
    )(dest, end, x_packed)


CAST_ROWS = 128


def _moe_kernel(te_ref, nu_ref, nx_ref, xs_ref, wg_ref, bg_ref, wu_ref, bu_ref, wd_ref, bd_ref, ys_ref,
                w_stage, w_bf, x_buf, y_buf, zero_buf, w_sem, in_sem, out_sem, zero_sem):
    t = pl.program_id(0)
    n_used = nu_ref[0]
    slot = lax.rem(t, 2)
    e = te_ref[t]
    e_prev = te_ref[jnp.maximum(t - 1, 0)]

    def load(tile, s):
        return _tile_row_copies(xs_ref, tile, x_buf.at[s], in_sem.at[s], to_hbm=False)

    def store(tile, s):
        return _tile_row_copies(ys_ref, tile, y_buf.at[s], out_sem.at[s], to_hbm=True)

    def weight_copies(expert):
        return [pltpu.make_async_copy(w.at[expert], w_stage.at[j], w_sem.at[j])
                for j, w in enumerate((wg_ref, wu_ref, wd_ref))]

    @pl.when(t == 0)
    def _():
        for c in weight_copies(e) + load(0, 0):
            c.start()

    @pl.when(t + 1 < n_used)
    def _():
        for c in load(t + 1, 1 - slot):
            c.start()

    @pl.when((t == 0) | (e != e_prev))
    def _():
        for c in weight_copies(e):
            c.wait()

        def cast(r, carry):
            sl = pl.ds(pl.multiple_of(r * CAST_ROWS, CAST_ROWS), CAST_ROWS)
            for j in range(3):
                w_bf[j, sl, :] = w_stage[j, sl, :].astype(BF16)
            return carry

        lax.fori_loop(0, D_MODEL // CAST_ROWS, cast, 0)
        e_next = nx_ref[e]

        @pl.when(e_next >= 0)
        def _():
            for c in weight_copies(e_next):
                c.start()

    @pl.when(t < n_used)
    def _():
        for c in load(t, slot):
            c.wait()

        @pl.when(t >= 2)
        def _():
            for c in store(t - 2, slot):
                c.wait()

        x = x_buf[slot].astype(BF16)
        gate = jnp.minimum(_dot(x, w_bf[0]) + bg_ref[...], SWIGLU_LIMIT)
        up = jnp.clip(_dot(x, w_bf[1]) + bu_ref[...], -SWIGLU_LIMIT, SWIGLU_LIMIT)
        hdn = (up + 1.0) * gate * (1.0 / (1.0 + jnp.exp(-SWIGLU_ALPHA * gate)))
        y_buf[slot] = _dot(hdn.astype(BF16), w_bf[2]) + bd_ref[...]
        for c in store(t, slot):
            c.start()

    @pl.when(t >= n_used)
    def _():
        zero_buf[...] = jnp.zeros_like(zero_buf)
        fill = pltpu.make_async_copy(zero_buf, ys_ref.at[pl.ds(t * TM, TM)], zero_sem)
        fill.start()
        fill.wait()

    @pl.when(t == N_EXPERT_TILES - 1)
    def _():
        @pl.when(n_used >= 2)
        def _():
            for c in store(n_used - 2, lax.rem(n_used, 2)):
                c.wait()

        for c in store(n_used - 1, lax.rem(n_used - 1, 2)):
            c.wait()


def _moe(tile_expert, n_used, next_expert, xs, w_gate, b_gate, w_up, b_up, w_down, b_down):
    hbm = pl.BlockSpec(memory_space=pl.ANY)
    bias = pl.BlockSpec((None, 1, D_MODEL), lambda t, te, nu, nx: (te[t], 0, 0))
    return pl.pallas_call(
        _moe_kernel,
        grid_spec=pltpu.PrefetchScalarGridSpec(
            num_scalar_prefetch=3,
            grid=(N_EXPERT_TILES,),
            in_specs=[hbm, hbm, bias, hbm, bias, hbm, bias],
            out_specs=hbm,
            scratch_shapes=[pltpu.VMEM((3, D_MODEL, D_MODEL), F32), pltpu.VMEM((3, D_MODEL, D_MODEL), BF16),
                            pltpu.VMEM((2, TM, D_MODEL), F32), pltpu.VMEM((2, TM, D_MODEL), F32),
                            pltpu.VMEM((TM,) + ROW_TILE, F32), pltpu.SemaphoreType.DMA((3,)),
                            pltpu.SemaphoreType.DMA((2,)), pltpu.SemaphoreType.DMA((2,)),
                            pltpu.SemaphoreType.DMA],
        ),
        out_shape=jax.ShapeDtypeStruct((N_SORTED_ROWS,) + ROW_TILE, F32),
        compiler_params=_cparams(("arbitrary",), 52),
        name="experts",
    )(tile_expert, n_used, next_expert, xs, w_gate, b_gate, w_up, b_up, w_down, b_down)


def _combine_kernel(dest_ref, h_ref, w_ref, ys_ref, yp_ref, ysm_ref, g_ref, out_ref, sem):
    i = pl.program_id(0)
    slot = lax.rem(i, 2)

    def issue_tile(tile, s):
        base = tile * TM

        def issue(n, carry):
            for kk in range(TOP_K):
                _row_copy(ys_ref, dest_ref[kk * N_ROWS + base + n], g_ref.at[s], kk * TM + n,
                          sem.at[s]).start(priority=kk % 2)
            return carry

        lax.fori_loop(0, TM, issue, 0)

    @pl.when(i == 0)
    def _():
        issue_tile(0, 0)

    @pl.when(i + 1 < N_TILES)
    def _():
        issue_tile(i + 1, 1 - slot)

    for kk in range(TOP_K):
        pltpu.make_async_copy(ys_ref.at[pl.ds(0, TM)], g_ref.at[slot, pl.ds(kk * TM, TM)], sem.at[slot]).wait()

    w = w_ref[...]
    acc = None
    for kk in range(TOP_K):
        part = jnp.broadcast_to(w[:, kk:kk + 1, :], (TM,) + ROW_TILE) * g_ref[slot, kk * TM:(kk + 1) * TM]
        acc = part if acc is None else acc + part
    out_ref[...] = h_ref[...] + acc

    def write(y_ref):
        for a in range(ROW_SUB):
            y_ref[:, a * LANES:(a + 1) * LANES] = out_ref[:, a, :]

    @pl.when(i < N_PROMPT_TILES)
    def _():
        write(yp_ref)

    @pl.when(i >= N_PROMPT_TILES)
    def _():
        write(ysm_ref)


def _combine(dest, h_rows, w_rows, ys):
    tile = lambda index: pl.BlockSpec((TM,) + ROW_TILE, index)
    return pl.pallas_call(
        _combine_kernel,
        grid_spec=pltpu.PrefetchScalarGridSpec(
            num_scalar_prefetch=1,
            grid=(N_TILES,),
            in_specs=[tile(lambda i, d: (i, 0, 0)), tile(lambda i, d: (i, 0, 0)),
                      pl.BlockSpec(memory_space=pl.ANY)],
            out_specs=[pl.BlockSpec((TM, D_MODEL), lambda i, d: (jnp.minimum(i, N_PROMPT_TILES - 1), 0)),
                       pl.BlockSpec((TM, D_MODEL), lambda i, d: (0, 0))],
            scratch_shapes=[pltpu.VMEM((2, TOP_K * TM) + ROW_TILE, F32), pltpu.VMEM((TM,) + ROW_TILE, F32),
                            pltpu.SemaphoreType.DMA((2,))],
        ),
        out_shape=[jax.ShapeDtypeStruct((N_PROMPT, D_MODEL), F32),
                   jax.ShapeDtypeStruct((N_SAMPLE, D_MODEL), F32)],
        compiler_params=_cparams(("arbitrary",), 40),
        name="combine",
    )(dest, h_rows, w_rows, ys)


def _block_diag_ones(n, blk):
    idx = np.arange(n) // blk
    return (idx[:, None] == idx[None, :]).astype(np.float32)


def _swa_head_mask(tq):
    row_head = np.arange(4 * tq)[:, None] // tq
    lane_head = np.arange(4 * HD)[None, :] // HD
    return jnp.asarray((row_head == lane_head).astype(np.float32), BF16)


def _gla_masks(t, n_sub):
    nb = t // SUB
    lanes = N_HEADS_GLA * DK
    tri = jnp.asarray(np.kron(np.eye(n_sub, dtype=np.float32), np.tril(np.ones((t, t), np.float32))), BF16)
    row = np.arange(N_HEADS_GLA * t)
    col = np.arange(nb * lanes)
    same_head = (row[:, None] // t) == ((col[None, :] % lanes) // DK)
    same_blk = ((row[:, None] % t) // SUB) == (col[None, :] // lanes)
    mask = jnp.asarray((same_head & same_blk).astype(np.float32), BF16)
    return tri, mask


def _sink_rows(sinks, tq):
    s = jnp.repeat(sinks.astype(F32).reshape(2, 4), tq, axis=1)
    return jnp.broadcast_to(s[:, None, :], (2, 8, 4 * tq))


def kernel(x_prompt, x_sample, state_gla, cache_swa_k, cache_swa_v, norm_mix_g, w_in, w_gla_a2, b_gla_a, q_norm_g,
           k_norm_g, swa_sinks, gla_norm_g, w_out, norm_ffn_g, w_router, b_router, w_gate, b_gate, w_up, b_up,
           w_down, b_down):
    xp = x_prompt.reshape(N_PROMPT, D_MODEL)
    xs = x_sample.reshape(N_SAMPLE, D_MODEL)

    w_in0 = w_in[0]
    w_main = w_in0[:, :W_MAIN].astype(BF16)
    w_ga = jnp.pad(w_in0[:, OFF_GA:], ((0, 0), (0, LANES - W_GA))).astype(BF16)
    w_a2 = jnp.pad(w_gla_a2[0], ((0, LANES - W_GA), (0, 0))).astype(BF16)
    b_a = b_gla_a[0].reshape(1, -1)
    qg = (jnp.tile(q_norm_g[0], N_HEADS_SWA) * (HD ** -0.5)).reshape(1, -1)
    kg = jnp.tile(k_norm_g[0], 2).reshape(1, -1)
    bdq = jnp.asarray(_block_diag_ones(W_SQ, HD), BF16)
    bdk = jnp.asarray(_block_diag_ones(W_SK, HD), BF16)

    sq, sk, sv, gq, gk, gv, gg, gr = _proj(xp, xs, norm_mix_g[0].reshape(1, -1), w_main, w_ga, w_a2, b_a,
                                           qg, kg, bdq, bdk)

    o_swa_p = _swa_prompt(sq, sk, sv, _swa_head_mask(CHUNK), _sink_rows(swa_sinks[0], CHUNK))
    cache_k = cache_swa_k[0].reshape(N_STREAMS, WINDOW, 2 * HD)
    cache_v = cache_swa_v[0].reshape(N_STREAMS, WINDOW, 2 * HD)
    o_swa_s = _swa_sample(sq, sk, sv, cache_k, cache_v, _swa_head_mask(T_SAMPLE),
                          _sink_rows(swa_sinks[0], T_SAMPLE))

    bd_state = jnp.asarray(_block_diag_ones(N_HEADS_GLA, 1).repeat(DV, axis=0).repeat(DK, axis=1), F32)
    gn = gla_norm_g[0].reshape(1, -1)
    tri_p, mask_p = _gla_masks(CHUNK, TM // CHUNK)
    tri_s, mask_s = _gla_masks(T_SAMPLE, 1)
    s0_p = jnp.zeros((1, N_HEADS_GLA * DV, N_HEADS_GLA * DK), F32)
    og_p, sfin_p = _gla(gq, gk, gv, gg, gr, s0_p, tri_p, mask_p, bd_state, gn, t=CHUNK, n_sub=TM // CHUNK,
                        n_batch=1, n_steps=N_PROMPT // TM, first_block=0, name="gla_prompt")
    eye = jnp.eye(N_HEADS_GLA, dtype=F32)
    s0_s = jnp.einsum('bhde,hg->bhegd', state_gla[0].astype(F32), eye).reshape(
        N_STREAMS, N_HEADS_GLA * DV, N_HEADS_GLA * DK)
    og_s, sfin_s = _gla(gq, gk, gv, gg, gr, s0_s, tri_s, mask_s, bd_state, gn, t=T_SAMPLE, n_sub=1,
                        n_batch=N_STREAMS, n_steps=1, first_block=N_PROMPT // T_SAMPLE, name="gla_sample")

    def unpack_state(sfin):
        s = sfin.reshape(-1, N_HEADS_GLA, DV, N_HEADS_GLA, DK)
        s = jnp.stack([s[:, h, :, h, :] for h in range(N_HEADS_GLA)], axis=1)
        return jnp.transpose(s, (0, 1, 3, 2))[None]

    w_out0 = w_out[0].astype(BF16)
    wr = jnp.pad(w_router[0], ((0, 0), (0, LANES - N_EXPERTS)))
    wr_hi = wr.astype(BF16)
    wr_lo = (wr - wr_hi.astype(F32)).astype(BF16)
    br = jnp.pad(b_router[0].astype(F32), (0, LANES - N_EXPERTS)).reshape(1, -1)
    x_rows, h_rows, logits = _merge(o_swa_p, o_swa_s, og_p, og_s, xp, xs, w_out0[:W_SQ], w_out0[W_SQ:],
                                    norm_ffn_g[0].reshape(1, -1), wr_hi, wr_lo, br)
    earlier = jnp.asarray(np.triu(np.ones((RT, RT), np.float32), 1), BF16)
    w_rows, top_i, rank, counts = _router(logits, earlier)

    counts = counts[:, 0]
    padded = (counts + TM - 1) // TM * TM
    end = jnp.cumsum(padded)
    start = end - padded
    experts = jnp.arange(N_EXPERTS, dtype=I32)
    is_e = top_i[:TOP_K, :, None] == experts
    dest = (rank[:TOP_K] + jnp.sum(jnp.where(is_e, start, 0), axis=-1)).reshape(-1).astype(I32)
    n_used = (end[-1] // TM).astype(I32)
    tiles = jnp.minimum(jnp.arange(N_EXPERT_TILES, dtype=I32), n_used - 1)
    tile_expert = jnp.sum((tiles[:, None] * TM >= end[None, :]).astype(I32), axis=1)
    later_nonempty = (experts[None, :] > experts[:, None]) & (padded[None, :] > 0)
    next_expert = jnp.min(jnp.where(later_nonempty, experts[None, :], N_EXPERTS), axis=1)
    next_expert = jnp.where(next_expert < N_EXPERTS, next_expert, -1).astype(I32)

    xs_sorted = _dispatch(dest, end.astype(I32), x_rows)
    ys = _moe(tile_expert, n_used.reshape(1), next_expert, xs_sorted, w_gate[0], b_gate[0].reshape(N_EXPERTS, 1, -1),
              w_up[0], b_up[0].reshape(N_EXPERTS, 1, -1), w_down[0], b_down[0].reshape(N_EXPERTS, 1, -1))
    y_p, y_s = _combine(dest, h_rows, w_rows, ys)

    sk_s = sk[N_PROMPT:].reshape(N_STREAMS, T_SAMPLE, 2 * HD)
    sv_s = sv[N_PROMPT:].reshape(N_STREAMS, T_SAMPLE, 2 * HD)
    kc_s = jnp.concatenate([cache_k[:, T_SAMPLE:], sk_s], axis=1).reshape(1, N_STREAMS, WINDOW, 2, HD)
    vc_s = jnp.concatenate([cache_v[:, T_SAMPLE:], sv_s], axis=1).reshape(1, N_STREAMS, WINDOW, 2, HD)
    kc_p = sk[N_PROMPT - WINDOW:N_PROMPT].reshape(1, 1, WINDOW, 2, HD)
    vc_p = sv[N_PROMPT - WINDOW:N_PROMPT].reshape(1, 1, WINDOW, 2, HD)
    return (y_p.reshape(1, N_PROMPT, D_MODEL), y_s.reshape(N_STREAMS, T_SAMPLE, D_MODEL),
            unpack_state(sfin_p), kc_p, vc_p, unpack_state(sfin_s), kc_s, vc_s)
```

```python
import functools

import numpy as np
import jax
import jax.numpy as jnp
from jax import lax
from jax.experimental import pallas as pl
from jax.experimental.pallas import tpu as pltpu

F32 = jnp.float32
BF16 = jnp.bfloat16
I32 = jnp.int32
U32 = jnp.uint32

D_MODEL = 1024
N_PROMPT = 16384
N_STREAMS = 8
T_SAMPLE = 32
N_SAMPLE = N_STREAMS * T_SAMPLE
N_ROWS = N_PROMPT + N_SAMPLE
EPS = 1e-6

CHUNK = 64
SUB = 16
N_HEADS_SWA = 8
HD = 64
WINDOW = 128
N_HEADS_GLA = 4
DK = 64
DV = 128
GLA_TAU = 16.0
N_EXPERTS = 32
TOP_K = 4
SWIGLU_ALPHA = 1.702
SWIGLU_LIMIT = 7.0

TM = 256
N_TILES = N_ROWS // TM
N_PROMPT_TILES = N_PROMPT // TM
N_ASSIGN = N_ROWS * TOP_K
N_EXPERT_TILES = N_ASSIGN // TM + N_EXPERTS
N_SORTED_ROWS = N_EXPERT_TILES * TM
LANES = 128
SLOT_ROWS = 16
NEG_BIG = -1e30

W_SQ, W_SK, W_SV, W_GQ, W_GK, W_GV, W_GR, W_GA = 512, 128, 128, 256, 256, 512, 512, 16
OFF_SQ = 0
OFF_SK = OFF_SQ + W_SQ
OFF_SV = OFF_SK + W_SK
OFF_GQ = OFF_SV + W_SV
OFF_GK = OFF_GQ + W_GQ
OFF_GV = OFF_GK + W_GK
OFF_GR = OFF_GV + W_GV
OFF_GA = OFF_GR + W_GR
W_MAIN = OFF_GA


def _cparams(semantics, vmem_mib):
    return pltpu.CompilerParams(dimension_semantics=semantics, vmem_limit_bytes=vmem_mib * 2 ** 20)


def _dot(a, b):
    return jnp.dot(a, b, preferred_element_type=F32)


def _dot_nt(a, b):
    return lax.dot_general(a, b, (((1,), (1,)), ((), ())), preferred_element_type=F32)


def _dot_tn(a, b):
    return lax.dot_general(a, b, (((0,), (0,)), ((), ())), preferred_element_type=F32)


def _split_bf16(x):
    hi = x.astype(BF16)
    lo = (x - hi.astype(F32)).astype(BF16)
    return hi, lo


def _rms(x):
    return x * lax.rsqrt(jnp.mean(x * x, axis=-1, keepdims=True) + EPS)


ROW_SUB = 8
ROW_TILE = (ROW_SUB, LANES)


def _proj_kernel(xp_ref, xs_ref, g_ref, w_ref, wga_ref, wa2_ref, ba_ref, qg_ref, kg_ref, bdq_ref, bdk_ref,
                 sq_ref, sk_ref, sv_ref, gq_ref, gk_ref, gv_ref, gg_ref, gr_ref):
    i = pl.program_id(0)
    x = jnp.where(i < N_PROMPT_TILES, xp_ref[...], xs_ref[...])
    xb = (_rms(x) * g_ref[...]).astype(BF16)

    def seg(off, width):
        return _dot(xb, w_ref[:, off:off + width])

    def head_norm(u, bd_ref):
        hi, lo = _split_bf16(u * u)
        ss = _dot(hi, bd_ref[...]) + _dot(lo, bd_ref[...])
        return u * lax.rsqrt(ss * (1.0 / HD) + EPS)

    sq_ref[...] = (head_norm(seg(OFF_SQ, W_SQ), bdq_ref) * qg_ref[...]).astype(BF16)
    sk_ref[...] = head_norm(seg(OFF_SK, W_SK), bdk_ref) * kg_ref[...]
    sv_ref[...] = seg(OFF_SV, W_SV)
    gq_ref[...] = (seg(OFF_GQ, W_GQ) * (DK ** -0.5)).astype(BF16)
    gk_ref[...] = seg(OFF_GK, W_GK).astype(BF16)
    gv_ref[...] = seg(OFF_GV, W_GV).astype(BF16)
    gr_ref[...] = seg(OFF_GR, W_GR).astype(BF16)
    ga = _dot(xb, wga_ref[...]).astype(BF16)
    z = _dot(ga, wa2_ref[...]) + ba_ref[...]
    log_sig = jnp.minimum(z, 0.0) - jnp.log(1.0 + jnp.exp(-jnp.abs(z)))
    gg_ref[...] = log_sig * (1.0 / GLA_TAU)


def _proj(xp, xs, norm_g, w_main, w_ga, w_a2, b_a, qg, kg, bdq, bdk):
    def rows(width):
        return pl.BlockSpec((TM, width), lambda i: (i, 0))

    def full(a):
        return pl.BlockSpec(a.shape, lambda i: (0,) * a.ndim)

    consts = (norm_g, w_main, w_ga, w_a2, b_a, qg, kg, bdq, bdk)
    out_widths = ((W_SQ, BF16), (W_SK, F32), (W_SV, F32), (W_GQ, BF16), (W_GK, BF16), (W_GV, BF16),
                  (W_GQ, F32), (W_GR, BF16))
    return pl.pallas_call(
        _proj_kernel,
        grid=(N_TILES,),
        in_specs=[pl.BlockSpec((TM, D_MODEL), lambda i: (jnp.minimum(i, N_PROMPT_TILES - 1), 0)),
                  pl.BlockSpec((TM, D_MODEL), lambda i: (0, 0))] + [full(a) for a in consts],
        out_specs=[rows(w) for w, _ in out_widths],
        out_shape=[jax.ShapeDtypeStruct((N_ROWS, w), dt) for w, dt in out_widths],
        compiler_params=_cparams(("arbitrary",), 40),
        name="proj",
    )(xp, xs, *consts)


def _dup_kv_heads(x):
    r = pltpu.roll(x, HD, axis=1)
    lo = lax.broadcasted_iota(I32, x.shape, 1) < HD
    out = []
    for a in (jnp.where(lo, x, r), jnp.where(lo, r, x)):
        out.append(jnp.concatenate([a, a], axis=1).astype(BF16))
    return out


def _swa_blocks(q_blocks, k_blocks, v_blocks, sinks, valids, head_mask):
    tq = q_blocks[0].shape[0]
    n_q = 4 * tq
    scores = []
    for q, k, valid in zip(q_blocks, k_blocks, valids):
        s_t = _dot_nt(k, jnp.concatenate([q] * 4, axis=0) * head_mask)
        scores.append(s_t if valid is None else jnp.where(valid, s_t, -jnp.inf))
    eye = (lax.broadcasted_iota(I32, (n_q, n_q), 0) == lax.broadcasted_iota(I32, (n_q, n_q), 1)
           ).astype(F32).astype(BF16)
    probs = []
    for s_t, sink in zip(scores, sinks):
        m = jnp.maximum(jnp.max(s_t, axis=0, keepdims=True), sink)
        p_t = jnp.exp(s_t - m)
        den = jnp.sum(p_t, axis=0, keepdims=True) + jnp.exp(sink - m)
        probs.append(_dot_nt(eye, (p_t / den).astype(BF16)).astype(BF16))
    lane_head = lax.broadcasted_iota(I32, (tq, 4 * HD), 1) // HD
    outs = []
    for p, v in zip(probs, v_blocks):
        o_full = _dot(p, v)
        o = jnp.zeros((tq, 4 * HD), F32)
        for a in range(4):
            o = o + jnp.where(lane_head == a, o_full[a * tq:(a + 1) * tq], 0.0)
        outs.append(o)
    return outs


def _swa_prompt_kernel(q_ref, kp_ref, kc_ref, vp_ref, vc_ref, hm_ref, sink_ref, o_ref):
    i = pl.program_id(0)
    k_dup = _dup_kv_heads(jnp.concatenate([kp_ref[...], kc_ref[...]], axis=0))
    v_dup = _dup_kv_heads(jnp.concatenate([vp_ref[...], vc_ref[...]], axis=0))
    sink_row = [sink_ref[j][0:1, :] for j in range(2)]
    span = WINDOW + CHUNK
    key = lax.broadcasted_iota(I32, (span, 4 * CHUNK), 0)
    qs, ks, vs, sinks, valids, where = [], [], [], [], [], []
    for c in range(TM // CHUNK):
        lo = CHUNK * c
        valid = (i * TM - WINDOW + lo + key) >= 0
        for j in range(2):
            qs.append(q_ref[lo:lo + CHUNK, 4 * HD * j:4 * HD * (j + 1)])
            ks.append(k_dup[j][lo:lo + span])
            vs.append(v_dup[j][lo:lo + span])
            sinks.append(sink_row[j])
            valids.append(valid)
            where.append((lo, j))
    outs = _swa_blocks(qs, ks, vs, sinks, valids, hm_ref[...])
    for (lo, j), o in zip(where, outs):
        o_ref[lo:lo + CHUNK, 4 * HD * j:4 * HD * (j + 1)] = o.astype(BF16)


def _swa_prompt(sq, sk, sv, head_mask, sink_b):
    half = TM // 2
    prev = pl.BlockSpec((half, 2 * HD), lambda i: (jnp.maximum(2 * i - 1, 0), 0))
    cur = pl.BlockSpec((TM, 2 * HD), lambda i: (i, 0))
    return pl.pallas_call(
        _swa_prompt_kernel,
        grid=(N_PROMPT_TILES,),
        in_specs=[pl.BlockSpec((TM, W_SQ), lambda i: (i, 0)), prev, cur, prev, cur,
                  pl.BlockSpec(head_mask.shape, lambda i: (0, 0)),
                  pl.BlockSpec(sink_b.shape, lambda i: (0, 0, 0))],
        out_specs=pl.BlockSpec((TM, W_SQ), lambda i: (i, 0)),
        out_shape=jax.ShapeDtypeStruct((N_PROMPT, W_SQ), BF16),
        compiler_params=_cparams(("arbitrary",), 32),
        name="swa_prompt",
    )(sq, sk, sk, sv, sv, head_mask, sink_b)


def _swa_sample_kernel(q_ref, kc_ref, kn_ref, vc_ref, vn_ref, hm_ref, sink_ref, o_ref):
    k_dup = _dup_kv_heads(jnp.concatenate([kc_ref[...], kn_ref[...]], axis=0))
    v_dup = _dup_kv_heads(jnp.concatenate([vc_ref[...], vn_ref[...]], axis=0))
    sink_row = [sink_ref[j][0:1, :] for j in range(2)]
    qs = [q_ref[:, 4 * HD * j:4 * HD * (j + 1)] for j in range(2)]
    outs = _swa_blocks(qs, k_dup, v_dup, sink_row, [None, None], hm_ref[...])
    o_ref[...] = jnp.concatenate(outs, axis=1).astype(BF16)


def _swa_sample(sq, sk, sv, cache_k, cache_v, head_mask, sink_b):
    first = N_PROMPT // T_SAMPLE
    new = lambda width: pl.BlockSpec((T_SAMPLE, width), lambda b: (first + b, 0))
    cache = pl.BlockSpec((None, WINDOW, 2 * HD), lambda b: (b, 0, 0))
    return pl.pallas_call(
        _swa_sample_kernel,
        grid=(N_STREAMS,),
        in_specs=[new(W_SQ), cache, new(2 * HD), cache, new(2 * HD),
                  pl.BlockSpec(head_mask.shape, lambda b: (0, 0)),
                  pl.BlockSpec(sink_b.shape, lambda b: (0, 0, 0))],
        out_specs=pl.BlockSpec((T_SAMPLE, W_SQ), lambda b: (b, 0)),
        out_shape=jax.ShapeDtypeStruct((N_SAMPLE, W_SQ), BF16),
        compiler_params=_cparams(("arbitrary",), 32),
        name="swa_sample",
    )(sq, cache_k, sk, cache_v, sv, head_mask, sink_b)


GLA_SAFE_EXP = 80.0
GLA_SLOW_ROWS = 16


def _gla_kernel(q_ref, k_ref, v_ref, g_ref, gr_ref, s0_ref, tri_ref, m_ref, bd_ref, gn_ref,
                og_ref, sfin_ref, st_ref, o_ref, *, t, n_sub):
    c = pl.program_id(1)
    lanes = N_HEADS_GLA * DK

    @pl.when(c == 0)
    def _():
        st_ref[...] = s0_ref[...]

    n_rows = t * n_sub

    def group_row(x, period, offset):
        g = x.reshape(n_rows // period, period, lanes)[:, offset:offset + 1, :]
        return jnp.broadcast_to(g, (n_rows // period, period, lanes)).reshape(n_rows, lanes)

    g_hi, g_lo = _split_bf16(g_ref[...])
    b = _dot(tri_ref[...], g_hi) + _dot(tri_ref[...], g_lo)
    since_block_start = group_row(b, SUB, 0) - b
    blocked_is_safe = jnp.max(since_block_start) <= GLA_SAFE_EXP

    def emit(o):
        gr = gr_ref[...].astype(F32)
        gate = gr / (1.0 + jnp.exp(-gr))
        outs = []
        for h in range(N_HEADS_GLA):
            sl = slice(h * DV, (h + 1) * DV)
            outs.append(_rms(o[:, sl]) * gn_ref[...] * gate[:, sl])
        og_ref[...] = jnp.concatenate(outs, axis=1).astype(BF16)

    @pl.when(blocked_is_safe)
    def _():
        emit(_gla_blocked(q_ref, k_ref, v_ref, m_ref, bd_ref, st_ref, b, since_block_start, group_row,
                          t=t, n_sub=n_sub))

    @pl.when(jnp.logical_not(blocked_is_safe))
    def _():
        _gla_tokenwise(q_ref, k_ref, v_ref, g_ref, bd_ref, st_ref, o_ref, n_rows=n_rows)
        emit(o_ref[...])

    @pl.when(c == pl.num_programs(1) - 1)
    def _():
        sfin_ref[...] = st_ref[...]


def _gla_tokenwise(q_ref, k_ref, v_ref, g_ref, bd_ref, st_ref, o_ref, *, n_rows):
    row = lax.broadcasted_iota(I32, (GLA_SLOW_ROWS, 1), 0)

    def group(gi, carry):
        rows = pl.ds(pl.multiple_of(gi * GLA_SLOW_ROWS, GLA_SLOW_ROWS), GLA_SLOW_ROWS)
        q = q_ref[rows, :].astype(F32)
        k = k_ref[rows, :].astype(F32)
        v = v_ref[rows, :].astype(F32)
        decay = jnp.exp(g_ref[rows, :])
        o = jnp.zeros((GLA_SLOW_ROWS, N_HEADS_GLA * DV), F32)
        for j in range(GLA_SLOW_ROWS):
            only_j = row == j
            k_j = jnp.where(only_j, k, 0.0).astype(BF16)
            v_j = jnp.where(only_j, v, 0.0).astype(BF16)
            q_j = jnp.where(only_j, q, 0.0).astype(BF16)
            st = st_ref[...] * decay[j:j + 1, :] + _dot_tn(v_j, k_j) * bd_ref[...]
            st_ref[...] = st
            o = o + _dot_nt(q_j, st.astype(BF16))
        o_ref[rows, :] = o
        return carry

    lax.fori_loop(0, n_rows // GLA_SLOW_ROWS, group, 0)


def _gla_blocked(q_ref, k_ref, v_ref, m_ref, bd_ref, st_ref, b, since_block_start, group_row, *, t, n_sub):
    nb = t // SUB
    n_rows = t * n_sub
    lanes = N_HEADS_GLA * DK
    q = q_ref[...].astype(F32)
    k = k_ref[...].astype(F32)
    qd = (q * jnp.exp(-since_block_start)).astype(BF16)
    pos = lax.broadcasted_iota(I32, (n_rows, lanes), 0) & (t - 1)
    k_parts = []
    for blk in range(nb):
        arg = jnp.where(pos < SUB * (blk + 1), group_row(b, t, SUB * blk) - b, NEG_BIG)
        k_parts.append((k * jnp.exp(arg)).astype(BF16))
    k_cat = jnp.concatenate(k_parts, axis=1)
    qd_cat = jnp.concatenate([qd] * nb, axis=1)
    q_dec = (q * jnp.exp(b)).astype(BF16)
    k_last = (k * jnp.exp(group_row(b, t, t - 1) - b)).astype(BF16)
    row_a = lax.broadcasted_iota(I32, (N_HEADS_GLA * t, t), 0) & (t - 1)
    col_a = lax.broadcasted_iota(I32, (N_HEADS_GLA * t, t), 1)

    o_intra, q_decayed, state_add, state_decay = [], [], [], []
    for u in range(n_sub):
        rows = slice(u * t, (u + 1) * t)
        v = v_ref[rows, :]
        lhs = jnp.concatenate([qd_cat[rows]] * N_HEADS_GLA, axis=0) * m_ref[...]
        a = _dot_nt(lhs, k_cat[rows])
        a = jnp.where(row_a >= col_a, a, 0.0).astype(BF16)
        o_full = _dot(a, v)
        o_intra.append(jnp.concatenate(
            [o_full[h * t:(h + 1) * t, h * DV:(h + 1) * DV] for h in range(N_HEADS_GLA)], axis=1))
        q_decayed.append(q_dec[rows])
        state_add.append(_dot_tn(v, k_last[rows]) * bd_ref[...])
        state_decay.append(jnp.exp(b[(u + 1) * t - 1:(u + 1) * t, :]))

    st = st_ref[...]
    o_parts = []
    for u in range(n_sub):
        o_parts.append(o_intra[u] + _dot_nt(q_decayed[u], st.astype(BF16)))
        st = st * state_decay[u] + state_add[u]
    st_ref[...] = st
    return jnp.concatenate(o_parts, axis=0) if n_sub > 1 else o_parts[0]


def _gla(gq, gk, gv, gg, gr, s0, tri, mask, bd, gn, *, t, n_sub, n_batch, n_steps, first_block, name):
    rows_per_step = t * n_sub

    def rows(width):
        return pl.BlockSpec((rows_per_step, width), lambda b, c: (first_block + b * n_steps + c, 0))

    def full(a):
        return pl.BlockSpec(a.shape, lambda b, c: (0,) * a.ndim)

    state = pl.BlockSpec((None,) + s0.shape[1:], lambda b, c: (b, 0, 0))
    return pl.pallas_call(
        functools.partial(_gla_kernel, t=t, n_sub=n_sub),
        grid=(n_batch, n_steps),
        in_specs=[rows(W_GQ), rows(W_GK), rows(W_GV), rows(W_GQ), rows(W_GR), state,
                  full(tri), full(mask), full(bd), full(gn)],
        out_specs=[pl.BlockSpec((rows_per_step, W_GV), lambda b, c: (b * n_steps + c, 0)), state],
        out_shape=[jax.ShapeDtypeStruct((n_batch * n_steps * rows_per_step, W_GV), BF16),
                   jax.ShapeDtypeStruct(s0.shape, F32)],
        scratch_shapes=[pltpu.VMEM(s0.shape[1:], F32), pltpu.VMEM((rows_per_step, W_GV), F32)],
        compiler_params=_cparams(("arbitrary", "arbitrary"), 32),
        name=name,
    )(gq, gk, gv, gg, gr, s0, tri, mask, bd, gn)


def _tile_row_copies(hbm_ref, tile, vmem_ref, sem, to_hbm, rows=TM):
    copies = []
    for a in range(ROW_SUB):
        h = hbm_ref.at[pl.ds(tile * rows, rows), a, :]
        v = vmem_ref.at[:, pl.ds(a * LANES, LANES)]
        copies.append(pltpu.make_async_copy(v, h, sem) if to_hbm else pltpu.make_async_copy(h, v, sem))
    return copies


def _store_tile_rows(i, n_steps, outputs, row_buf, row_sem, rows=TM):
    buf_slot = lax.rem(i, 2)

    def store(j, tile, s):
        return _tile_row_copies(outputs[j][0], tile, row_buf.at[j, s], row_sem.at[j, s], True, rows)

    for j, (_, value) in enumerate(outputs):
        @pl.when(i >= 2)
        def _():
            for c in store(j, i - 2, buf_slot):
                c.wait()

        row_buf[j, buf_slot] = value
        for c in store(j, i, buf_slot):
            c.start()

        @pl.when(i == n_steps - 1)
        def _():
            for c in store(j, i - 1, 1 - buf_slot) + store(j, i, buf_slot):
                c.wait()


def _merge_kernel(oswp_ref, osws_ref, ogp_ref, ogs_ref, xp_ref, xs_ref, wo1_ref, wo2_ref, gf_ref, wrh_ref, wrl_ref,
                  br_ref, xrow_ref, hrow_ref, lg_ref, row_buf, row_sem):
    i = pl.program_id(0)
    is_prompt = i < N_PROMPT_TILES
    x = jnp.where(is_prompt, xp_ref[...], xs_ref[...])
    o_swa = jnp.where(is_prompt, oswp_ref[...], osws_ref[...])
    og = jnp.where(is_prompt, ogp_ref[...], ogs_ref[...])
    h = x + (_dot(o_swa, wo1_ref[...]) + _dot(og, wo2_ref[...]))
    xn = _rms(h) * gf_ref[...]
    x_hi, x_lo = _split_bf16(xn)
    logits = _dot(x_hi, wrh_ref[...]) + _dot(x_lo, wrh_ref[...]) + _dot(x_hi, wrl_ref[...]) + br_ref[...]
    lg_ref[...] = logits.T[:N_EXPERTS]
    _store_tile_rows(i, N_TILES, ((xrow_ref, xn), (hrow_ref, h)), row_buf, row_sem)


def _merge(o_swa_p, o_swa_s, og_p, og_s, xp, xs, wo1, wo2, gf, wrh, wrl, br):
    def prompt_rows(width):
        return pl.BlockSpec((TM, width), lambda i: (jnp.minimum(i, N_PROMPT_TILES - 1), 0))

    def sample_rows(width):
        return pl.BlockSpec((TM, width), lambda i: (0, 0))

    def full(a):
        return pl.BlockSpec(a.shape, lambda i: (0,) * a.ndim)

    consts = (wo1, wo2, gf, wrh, wrl, br)
    return pl.pallas_call(
        _merge_kernel,
        grid=(N_TILES,),
        in_specs=[prompt_rows(W_SQ), sample_rows(W_SQ), prompt_rows(W_GV), sample_rows(W_GV),
                  prompt_rows(D_MODEL), sample_rows(D_MODEL)] + [full(a) for a in consts],
        out_specs=[pl.BlockSpec(memory_space=pl.ANY)] * 2 + [pl.BlockSpec((N_EXPERTS, TM), lambda i: (0, i))],
        out_shape=[jax.ShapeDtypeStruct((N_ROWS,) + ROW_TILE, F32)] * 2 + [
                   jax.ShapeDtypeStruct((N_EXPERTS, N_ROWS), F32)],
        scratch_shapes=[pltpu.VMEM((2, 2, TM, D_MODEL), F32), pltpu.SemaphoreType.DMA((2, 2))],
        compiler_params=_cparams(("arbitrary",), 32),
        name="merge",
    )(o_swa_p, o_swa_s, og_p, og_s, xp, xs, *consts)


RT = 1280
N_ROUTER_STEPS = N_ROWS // RT


def _router_kernel(lg_ref, tri_ref, wrow_ref, ti_ref, rk_ref, cnt_ref, base_ref, row_buf, row_sem):
    i = pl.program_id(0)

    @pl.when(i == 0)
    def _():
        base_ref[...] = jnp.zeros_like(base_ref)

    logits_t = lg_ref[...]
    expert = lax.broadcasted_iota(I32, logits_t.shape, 0)
    slot = lax.broadcasted_iota(I32, (SLOT_ROWS, RT), 0)
    vals, hots = [], []
    ti = jnp.zeros((SLOT_ROWS, RT), I32)
    for kk in range(TOP_K):
        m = jnp.max(logits_t, axis=0, keepdims=True)
        idx = jnp.min(jnp.where(logits_t == m, expert, N_EXPERTS), axis=0, keepdims=True)
        hot = expert == idx
        logits_t = jnp.where(hot, NEG_BIG, logits_t)
        vals.append(m)
        hots.append(hot)
        ti = jnp.where(slot == kk, idx, ti)
    ti_ref[...] = ti
    exps = [jnp.exp(v - vals[0]) for v in vals]
    den = exps[0] + exps[1] + exps[2] + exps[3]
    tw_t = jnp.zeros((SLOT_ROWS, RT), F32)
    for kk in range(TOP_K):
        tw_t = jnp.where(slot == kk, exps[kk] / den, tw_t)
    eye = (lax.broadcasted_iota(I32, (SLOT_ROWS, LANES), 0)
           == lax.broadcasted_iota(I32, (SLOT_ROWS, LANES), 1)).astype(F32).astype(BF16)
    w_hi = tw_t.astype(BF16)
    w_mid, w_lo = _split_bf16(tw_t - w_hi.astype(F32))
    tw_col = _dot_tn(w_hi, eye) + _dot_tn(w_mid, eye) + _dot_tn(w_lo, eye)

    onehot_t = jnp.zeros(logits_t.shape, F32)
    for hot in hots:
        onehot_t = onehot_t + jnp.where(hot, 1.0, 0.0)
    before_t = _dot(onehot_t.astype(BF16), tri_ref[...]) + base_ref[:, 0:1]
    rk = jnp.zeros((SLOT_ROWS, RT), I32)
    for kk in range(TOP_K):
        r = jnp.sum(jnp.where(hots[kk], before_t, 0.0), axis=0, keepdims=True).astype(I32)
        rk = jnp.where(slot == kk, r, rk)
    rk_ref[...] = rk
    total = base_ref[...] + jnp.sum(onehot_t, axis=1, keepdims=True)
    base_ref[...] = total
    cnt_ref[...] = total.astype(I32)

    w_lanes = [jnp.broadcast_to(tw_col[:, kk:kk + 1], (RT, LANES)) for kk in range(TOP_K)]
    w_rows = jnp.concatenate(w_lanes + [jnp.zeros((RT, D_MODEL - TOP_K * LANES), F32)], axis=1)
    _store_tile_rows(i, N_ROUTER_STEPS, ((wrow_ref, w_rows),), row_buf, row_sem, RT)


def _router(logits, tri):
    return pl.pallas_call(
        _router_kernel,
        grid=(N_ROUTER_STEPS,),
        in_specs=[pl.BlockSpec((N_EXPERTS, RT), lambda i: (0, i)), pl.BlockSpec(tri.shape, lambda i: (0, 0))],
        out_specs=[pl.BlockSpec(memory_space=pl.ANY),
                   pl.BlockSpec((SLOT_ROWS, RT), lambda i: (0, i)),
                   pl.BlockSpec((SLOT_ROWS, RT), lambda i: (0, i)),
                   pl.BlockSpec((N_EXPERTS, LANES), lambda i: (0, 0))],
        out_shape=[jax.ShapeDtypeStruct((N_ROWS,) + ROW_TILE, F32),
                   jax.ShapeDtypeStruct((SLOT_ROWS, N_ROWS), I32),
                   jax.ShapeDtypeStruct((SLOT_ROWS, N_ROWS), I32),
                   jax.ShapeDtypeStruct((N_EXPERTS, LANES), I32)],
        scratch_shapes=[pltpu.VMEM((N_EXPERTS, LANES), F32), pltpu.VMEM((1, 2, RT, D_MODEL), F32),
                        pltpu.SemaphoreType.DMA((1, 2))],
        compiler_params=_cparams(("arbitrary",), 48),
        name="router",
    )(logits, tri)


def _row_copy(src_ref, src_row, dst_ref, dst_row, sem):
    return pltpu.make_async_copy(src_ref.at[pl.ds(src_row, 1)], dst_ref.at[pl.ds(dst_row, 1)], sem)


def _dispatch_kernel(dest_ref, end_ref, x_ref, xs_ref, zero_ref, sem, zsem):
    i = pl.program_id(0)
    base = i * TM

    @pl.when(i == 0)
    def _():
        zero_ref[...] = jnp.zeros_like(zero_ref)

        def tail_copy(e):
            last = jnp.maximum(end_ref[e] - TM, 0)
            return pltpu.make_async_copy(zero_ref, xs_ref.at[pl.ds(pl.multiple_of(last, TM), TM)], zsem)

        def fill(e, carry):
            tail_copy(e).start()
            return carry

        def fill_wait(e, carry):
            tail_copy(e).wait()
            return carry

        lax.fori_loop(0, N_EXPERTS, fill, 0)
        lax.fori_loop(0, N_EXPERTS, fill_wait, 0)

        def unused_copy(t):
            return pltpu.make_async_copy(zero_ref, xs_ref.at[pl.ds(pl.multiple_of(t * TM, TM), TM)], zsem)

        def fill_unused(t, carry):
            unused_copy(t).start()
            unused_copy(t).wait()
            return carry

        lax.fori_loop(end_ref[N_EXPERTS - 1] // TM, N_EXPERT_TILES, fill_unused, 0)

    def issue(n, carry):
        for kk in range(TOP_K):
            _row_copy(x_ref, n, xs_ref, dest_ref[kk * N_ROWS + base + n], sem).start(priority=kk % 2)
        return carry

    lax.fori_loop(0, TM, issue, 0)

    for kk in range(TOP_K):
        pltpu.make_async_copy(x_ref, xs_ref.at[pl.ds(0, TM)], sem).wait()


def _dispatch(dest, end, x_packed):
    return pl.pallas_call(
        _dispatch_kernel,
        grid_spec=pltpu.PrefetchScalarGridSpec(
            num_scalar_prefetch=2,
            grid=(N_TILES,),
            in_specs=[pl.BlockSpec((TM,) + ROW_TILE, lambda i, d, e: (i, 0, 0))],
            out_specs=pl.BlockSpec(memory_space=pl.ANY),
            scratch_shapes=[pltpu.VMEM((TM,) + ROW_TILE, F32), pltpu.SemaphoreType.DMA,
                            pltpu.SemaphoreType.DMA],
        ),
        out_shape=jax.ShapeDtypeStruct((N_SORTED_ROWS,) + ROW_TILE, F32),
        compiler_params=_cparams(("arbitrary",), 32),
        name="dispatch",
    )(dest, end, x_packed)


CAST_ROWS = 128


def _moe_kernel(te_ref, nu_ref, nx_ref, xs_ref, wg_ref, bg_ref, wu_ref, bu_ref, wd_ref, bd_ref, ys_ref,
                w_stage, w_bf, x_buf, y_buf, zero_buf, w_sem, in_sem, out_sem, zero_sem):
    t = pl.program_id(0)
    n_used = nu_ref[0]
    slot = lax.rem(t, 2)
    e = te_ref[t]
    e_prev = te_ref[jnp.maximum(t - 1, 0)]

    def load(tile, s):
        return _tile_row_copies(xs_ref, tile, x_buf.at[s], in_sem.at[s], to_hbm=False)

    def store(tile, s):
        return _tile_row_copies(ys_ref, tile, y_buf.at[s], out_sem.at[s], to_hbm=True)

    def weight_copies(expert):
        return [pltpu.make_async_copy(w.at[expert], w_stage.at[j], w_sem.at[j])
                for j, w in enumerate((wg_ref, wu_ref, wd_ref))]

    @pl.when(t == 0)
    def _():
        for c in weight_copies(e) + load(0, 0):
            c.start()

    @pl.when(t + 1 < n_used)
    def _():
        for c in load(t + 1, 1 - slot):
            c.start()

    @pl.when((t == 0) | (e != e_prev))
    def _():
        for c in weight_copies(e):
            c.wait()

        def cast(r, carry):
            sl = pl.ds(pl.multiple_of(r * CAST_ROWS, CAST_ROWS), CAST_ROWS)
            for j in range(3):
                w_bf[j, sl, :] = w_stage[j, sl, :].astype(BF16)
            return carry

        lax.fori_loop(0, D_MODEL // CAST_ROWS, cast, 0)
        e_next = nx_ref[e]

        @pl.when(e_next >= 0)
        def _():
            for c in weight_copies(e_next):
                c.start()

    @pl.when(t < n_used)
    def _():
        for c in load(t, slot):
            c.wait()

        @pl.when(t >= 2)
        def _():
            for c in store(t - 2, slot):
                c.wait()

        x = x_buf[slot].astype(BF16)
        gate = jnp.minimum(_dot(x, w_bf[0]) + bg_ref[...], SWIGLU_LIMIT)
        up = jnp.clip(_dot(x, w_bf[1]) + bu_ref[...], -SWIGLU_LIMIT, SWIGLU_LIMIT)
        hdn = (up + 1.0) * gate * (1.0 / (1.0 + jnp.exp(-SWIGLU_ALPHA * gate)))
        y_buf[slot] = _dot(hdn.astype(BF16), w_bf[2]) + bd_ref[...]
        for c in store(t, slot):
            c.start()

    @pl.when(t >= n_used)
    def _():
        zero_buf[...] = jnp.zeros_like(zero_buf)
        fill = pltpu.make_async_copy(zero_buf, ys_ref.at[pl.ds(t * TM, TM)], zero_sem)
        fill.start()
        fill.wait()

    @pl.when(t == N_EXPERT_TILES - 1)
    def _():
        @pl.when(n_used >= 2)
        def _():
            for c in store(n_used - 2, lax.rem(n_used, 2)):
                c.wait()

        for c in store(n_used - 1, lax.rem(n_used - 1, 2)):
            c.wait()


def _moe(tile_expert, n_used, next_expert, xs, w_gate, b_gate, w_up, b_up, w_down, b_down):
    hbm = pl.BlockSpec(memory_space=pl.ANY)
    bias = pl.BlockSpec((None, 1, D_MODEL), lambda t, te, nu, nx: (te[t], 0, 0))
    return pl.pallas_call(
        _moe_kernel,
        grid_spec=pltpu.PrefetchScalarGridSpec(
            num_scalar_prefetch=3,
            grid=(N_EXPERT_TILES,),
            in_specs=[hbm, hbm, bias, hbm, bias, hbm, bias],
            out_specs=hbm,
            scratch_shapes=[pltpu.VMEM((3, D_MODEL, D_MODEL), F32), pltpu.VMEM((3, D_MODEL, D_MODEL), BF16),
                            pltpu.VMEM((2, TM, D_MODEL), F32), pltpu.VMEM((2, TM, D_MODEL), F32),
                            pltpu.VMEM((TM,) + ROW_TILE, F32), pltpu.SemaphoreType.DMA((3,)),
                            pltpu.SemaphoreType.DMA((2,)), pltpu.SemaphoreType.DMA((2,)),
                            pltpu.SemaphoreType.DMA],
        ),
        out_shape=jax.ShapeDtypeStruct((N_SORTED_ROWS,) + ROW_TILE, F32),
        compiler_params=_cparams(("arbitrary",), 52),
        name="experts",
    )(tile_expert, n_used, next_expert, xs, w_gate, b_gate, w_up, b_up, w_down, b_down)


def _combine_kernel(dest_ref, h_ref, w_ref, ys_ref, yp_ref, ysm_ref, g_ref, out_ref, sem):
    i = pl.program_id(0)
    slot = lax.rem(i, 2)

    def issue_tile(tile, s):
        base = tile * TM

        def issue(n, carry):
            for kk in range(TOP_K):
                _row_copy(ys_ref, dest_ref[kk * N_ROWS + base + n], g_ref.at[s], kk * TM + n,
                          sem.at[s]).start(priority=kk % 2)
            return carry

        lax.fori_loop(0, TM, issue, 0)

    @pl.when(i == 0)
    def _():
        issue_tile(0, 0)

    @pl.when(i + 1 < N_TILES)
    def _():
        issue_tile(i + 1, 1 - slot)

    for kk in range(TOP_K):
        pltpu.make_async_copy(ys_ref.at[pl.ds(0, TM)], g_ref.at[slot, pl.ds(kk * TM, TM)], sem.at[slot]).wait()

    w = w_ref[...]
    acc = None
    for kk in range(TOP_K):
        part = jnp.broadcast_to(w[:, kk:kk + 1, :], (TM,) + ROW_TILE) * g_ref[slot, kk * TM:(kk + 1) * TM]
        acc = part if acc is None else acc + part
    out_ref[...] = h_ref[...] + acc

    def write(y_ref):
        for a in range(ROW_SUB):
            y_ref[:, a * LANES:(a + 1) * LANES] = out_ref[:, a, :]

    @pl.when(i < N_PROMPT_TILES)
    def _():
        write(yp_ref)

    @pl.when(i >= N_PROMPT_TILES)
    def _():
        write(ysm_ref)


def _combine(dest, h_rows, w_rows, ys):
    tile = lambda index: pl.BlockSpec((TM,) + ROW_TILE, index)
    return pl.pallas_call(
        _combine_kernel,
        grid_spec=pltpu.PrefetchScalarGridSpec(
            num_scalar_prefetch=1,
            grid=(N_TILES,),
            in_specs=[tile(lambda i, d: (i, 0, 0)), tile(lambda i, d: (i, 0, 0)),
                      pl.BlockSpec(memory_space=pl.ANY)],
            out_specs=[pl.BlockSpec((TM, D_MODEL), lambda i, d: (jnp.minimum(i, N_PROMPT_TILES - 1), 0)),
                       pl.BlockSpec((TM, D_MODEL), lambda i, d: (0, 0))],
            scratch_shapes=[pltpu.VMEM((2, TOP_K * TM) + ROW_TILE, F32), pltpu.VMEM((TM,) + ROW_TILE, F32),
                            pltpu.SemaphoreType.DMA((2,))],
        ),
        out_shape=[jax.ShapeDtypeStruct((N_PROMPT, D_MODEL), F32),
                   jax.ShapeDtypeStruct((N_SAMPLE, D_MODEL), F32)],
        compiler_params=_cparams(("arbitrary",), 40),
        name="combine",
    )(dest, h_rows, w_rows, ys)


def _block_diag_ones(n, blk):
    idx = np.arange(n) // blk
    return (idx[:, None] == idx[None, :]).astype(np.float32)


def _swa_head_mask(tq):
    row_head = np.arange(4 * tq)[:, None] // tq
    lane_head = np.arange(4 * HD)[None, :] // HD
    return jnp.asarray((row_head == lane_head).astype(np.float32), BF16)


def _gla_masks(t, n_sub):
    nb = t // SUB
    lanes = N_HEADS_GLA * DK
    tri = jnp.asarray(np.kron(np.eye(n_sub, dtype=np.float32), np.tril(np.ones((t, t), np.float32))), BF16)
    row = np.arange(N_HEADS_GLA * t)
    col = np.arange(nb * lanes)
    same_head = (row[:, None] // t) == ((col[None, :] % lanes) // DK)
    same_blk = ((row[:, None] % t) // SUB) == (col[None, :] // lanes)
    mask = jnp.asarray((same_head & same_blk).astype(np.float32), BF16)
    return tri, mask


def _sink_rows(sinks, tq):
    s = jnp.repeat(sinks.astype(F32).reshape(2, 4), tq, axis=1)
    return jnp.broadcast_to(s[:, None, :], (2, 8, 4 * tq))


def kernel(x_prompt, x_sample, state_gla, cache_swa_k, cache_swa_v, norm_mix_g, w_in, w_gla_a2, b_gla_a, q_norm_g,
           k_norm_g, swa_sinks, gla_norm_g, w_out, norm_ffn_g, w_router, b_router, w_gate, b_gate, w_up, b_up,
           w_down, b_down):
    xp = x_prompt.reshape(N_PROMPT, D_MODEL)
    xs = x_sample.reshape(N_SAMPLE, D_MODEL)

    w_in0 = w_in[0]
    w_main = w_in0[:, :W_MAIN].astype(BF16)
    w_ga = jnp.pad(w_in0[:, OFF_GA:], ((0, 0), (0, LANES - W_GA))).astype(BF16)
    w_a2 = jnp.pad(w_gla_a2[0], ((0, LANES - W_GA), (0, 0))).astype(BF16)
    b_a = b_gla_a[0].reshape(1, -1)
    qg = (jnp.tile(q_norm_g[0], N_HEADS_SWA) * (HD ** -0.5)).reshape(1, -1)
    kg = jnp.tile(k_norm_g[0], 2).reshape(1, -1)
    bdq = jnp.asarray(_block_diag_ones(W_SQ, HD), BF16)
    bdk = jnp.asarray(_block_diag_ones(W_SK, HD), BF16)

    sq, sk, sv, gq, gk, gv, gg, gr = _proj(xp, xs, norm_mix_g[0].reshape(1, -1), w_main, w_ga, w_a2, b_a,
                                           qg, kg, bdq, bdk)

    o_swa_p = _swa_prompt(sq, sk, sv, _swa_head_mask(CHUNK), _sink_rows(swa_sinks[0], CHUNK))
    cache_k = cache_swa_k[0].reshape(N_STREAMS, WINDOW, 2 * HD)
    cache_v = cache_swa_v[0].reshape(N_STREAMS, WINDOW, 2 * HD)
    o_swa_s = _swa_sample(sq, sk, sv, cache_k, cache_v, _swa_head_mask(T_SAMPLE),
                          _sink_rows(swa_sinks[0], T_SAMPLE))

    bd_state = jnp.asarray(_block_diag_ones(N_HEADS_GLA, 1).repeat(DV, axis=0).repeat(DK, axis=1), F32)
    gn = gla_norm_g[0].reshape(1, -1)
    tri_p, mask_p = _gla_masks(CHUNK, TM // CHUNK)
    tri_s, mask_s = _gla_masks(T_SAMPLE, 1)
    s0_p = jnp.zeros((1, N_HEADS_GLA * DV, N_HEADS_GLA * DK), F32)
    og_p, sfin_p = _gla(gq, gk, gv, gg, gr, s0_p, tri_p, mask_p, bd_state, gn, t=CHUNK, n_sub=TM // CHUNK,
                        n_batch=1, n_steps=N_PROMPT // TM, first_block=0, name="gla_prompt")
    eye = jnp.eye(N_HEADS_GLA, dtype=F32)
    s0_s = jnp.einsum('bhde,hg->bhegd', state_gla[0].astype(F32), eye).reshape(
        N_STREAMS, N_HEADS_GLA * DV, N_HEADS_GLA * DK)
    og_s, sfin_s = _gla(gq, gk, gv, gg, gr, s0_s, tri_s, mask_s, bd_state, gn, t=T_SAMPLE, n_sub=1,
                        n_batch=N_STREAMS, n_steps=1, first_block=N_PROMPT // T_SAMPLE, name="gla_sample")

    def unpack_state(sfin):
        s = sfin.reshape(-1, N_HEADS_GLA, DV, N_HEADS_GLA, DK)
        s = jnp.stack([s[:, h, :, h, :] for h in range(N_HEADS_GLA)], axis=1)
        return jnp.transpose(s, (0, 1, 3, 2))[None]

    w_out0 = w_out[0].astype(BF16)
    wr = jnp.pad(w_router[0], ((0, 0), (0, LANES - N_EXPERTS)))
    wr_hi = wr.astype(BF16)
    wr_lo = (wr - wr_hi.astype(F32)).astype(BF16)
    br = jnp.pad(b_router[0].astype(F32), (0, LANES - N_EXPERTS)).reshape(1, -1)
    x_rows, h_rows, logits = _merge(o_swa_p, o_swa_s, og_p, og_s, xp, xs, w_out0[:W_SQ], w_out0[W_SQ:],
                                    norm_ffn_g[0].reshape(1, -1), wr_hi, wr_lo, br)
    earlier = jnp.asarray(np.triu(np.ones((RT, RT), np.float32), 1), BF16)
    w_rows, top_i, rank, counts = _router(logits, earlier)

    counts = counts[:, 0]
    padded = (counts + TM - 1) // TM * TM
    end = jnp.cumsum(padded)
    start = end - padded
    experts = jnp.arange(N_EXPERTS, dtype=I32)
    is_e = top_i[:TOP_K, :, None] == experts
    dest = (rank[:TOP_K] + jnp.sum(jnp.where(is_e, start, 0), axis=-1)).reshape(-1).astype(I32)
    n_used = (end[-1] // TM).astype(I32)
    tiles = jnp.minimum(jnp.arange(N_EXPERT_TILES, dtype=I32), n_used - 1)
    tile_expert = jnp.sum((tiles[:, None] * TM >= end[None, :]).astype(I32), axis=1)
    later_nonempty = (experts[None, :] > experts[:, None]) & (padded[None, :] > 0)
    next_expert = jnp.min(jnp.where(later_nonempty, experts[None, :], N_EXPERTS), axis=1)
    next_expert = jnp.where(next_expert < N_EXPERTS, next_expert, -1).astype(I32)

    xs_sorted = _dispatch(dest, end.astype(I32), x_rows)
    ys = _moe(tile_expert, n_used.reshape(1), next_expert, xs_sorted, w_gate[0], b_gate[0].reshape(N_EXPERTS, 1, -1),
              w_up[0], b_up[0].reshape(N_EXPERTS, 1, -1), w_down[0], b_down[0].reshape(N_EXPERTS, 1, -1))
    y_p, y_s = _combine(dest, h_rows, w_rows, ys)

    sk_s = sk[N_PROMPT:].reshape(N_STREAMS, T_SAMPLE, 2 * HD)
    sv_s = sv[N_PROMPT:].reshape(N_STREAMS, T_SAMPLE, 2 * HD)
    kc_s = jnp.concatenate([cache_k[:, T_SAMPLE:], sk_s], axis=1).reshape(1, N_STREAMS, WINDOW, 2, HD)
    vc_s = jnp.concatenate([cache_v[:, T_SAMPLE:], sv_s], axis=1).reshape(1, N_STREAMS, WINDOW, 2, HD)
    kc_p = sk[N_PROMPT - WINDOW:N_PROMPT].reshape(1, 1, WINDOW, 2, HD)
    vc_p = sv[N_PROMPT - WINDOW:N_PROMPT].reshape(1, 1, WINDOW, 2, HD)
    return (y_p.reshape(1, N_PROMPT, D_MODEL), y_s.reshape(N_STREAMS, T_SAMPLE, D_MODEL),
            unpack_state(sfin_p), kc_p, vc_p, unpack_state(sfin_s), kc_s, vc_s)
```

```python
import functools

import numpy as np
import jax
import jax.numpy as jnp
from jax import lax
from jax.experimental import pallas as pl
from jax.experimental.pallas import tpu as pltpu

F32 = jnp.float32
BF16 = jnp.bfloat16
I32 = jnp.int32
U32 = jnp.uint32

D_MODEL = 1024
N_PROMPT = 16384
N_STREAMS = 8
T_SAMPLE = 32
N_SAMPLE = N_STREAMS * T_SAMPLE
N_ROWS = N_PROMPT + N_SAMPLE
EPS = 1e-6

CHUNK = 64
SUB = 16
N_HEADS_SWA = 8
HD = 64
WINDOW = 128
N_HEADS_GLA = 4
DK = 64
DV = 128
GLA_TAU = 16.0
N_EXPERTS = 32
TOP_K = 4
SWIGLU_ALPHA = 1.702
SWIGLU_LIMIT = 7.0

TM = 256
N_TILES = N_ROWS // TM
N_PROMPT_TILES = N_PROMPT // TM
N_ASSIGN = N_ROWS * TOP_K
N_EXPERT_TILES = N_ASSIGN // TM + N_EXPERTS
N_SORTED_ROWS = N_EXPERT_TILES * TM
LANES = 128
SLOT_ROWS = 16
NEG_BIG = -1e30

W_SQ, W_SK, W_SV, W_GQ, W_GK, W_GV, W_GR, W_GA = 512, 128, 128, 256, 256, 512, 512, 16
OFF_SQ = 0
OFF_SK = OFF_SQ + W_SQ
OFF_SV = OFF_SK + W_SK
OFF_GQ = OFF_SV + W_SV
OFF_GK = OFF_GQ + W_GQ
OFF_GV = OFF_GK + W_GK
OFF_GR = OFF_GV + W_GV
OFF_GA = OFF_GR + W_GR
W_MAIN = OFF_GA


VMEM_MIB = dict(proj=40, mixer=32, merge=32, router=48, dispatch=32, experts=52, combine=40)


def _cparams(n_grid_axes, call):
    return pltpu.CompilerParams(dimension_semantics=("arbitrary",) * n_grid_axes,
                                vmem_limit_bytes=VMEM_MIB[call] * 2 ** 20)


def _dot(a, b):
    return jnp.dot(a, b, preferred_element_type=F32)


def _dot_nt(a, b):
    return lax.dot_general(a, b, (((1,), (1,)), ((), ())), preferred_element_type=F32)


def _dot_tn(a, b):
    return lax.dot_general(a, b, (((0,), (0,)), ((), ())), preferred_element_type=F32)


def _split_bf16(x):
    hi = x.astype(BF16)
    lo = (x - hi.astype(F32)).astype(BF16)
    return hi, lo


def _rms(x):
    return x * lax.rsqrt(jnp.mean(x * x, axis=-1, keepdims=True) + EPS)


ROW_SUB = 8
ROW_TILE = (ROW_SUB, LANES)


def _proj_kernel(xp_ref, xs_ref, g_ref, w_ref, wga_ref, wa2_ref, ba_ref, qg_ref, kg_ref, bdq_ref, bdk_ref,
                 sq_ref, sk_ref, sv_ref, gq_ref, gk_ref, gv_ref, gg_ref, gr_ref):
    i = pl.program_id(0)
    x = jnp.where(i < N_PROMPT_TILES, xp_ref[...], xs_ref[...])
    xb = (_rms(x) * g_ref[...]).astype(BF16)

    def seg(off, width):
        return _dot(xb, w_ref[:, off:off + width])

    def head_norm(u, bd_ref):
        hi, lo = _split_bf16(u * u)
        ss = _dot(hi, bd_ref[...]) + _dot(lo, bd_ref[...])
        return u * lax.rsqrt(ss * (1.0 / HD) + EPS)

    sq_ref[...] = (head_norm(seg(OFF_SQ, W_SQ), bdq_ref) * qg_ref[...]).astype(BF16)
    sk_ref[...] = head_norm(seg(OFF_SK, W_SK), bdk_ref) * kg_ref[...]
    sv_ref[...] = seg(OFF_SV, W_SV)
    gq_ref[...] = (seg(OFF_GQ, W_GQ) * (DK ** -0.5)).astype(BF16)
    gk_ref[...] = seg(OFF_GK, W_GK).astype(BF16)
    gv_ref[...] = seg(OFF_GV, W_GV).astype(BF16)
    gr_ref[...] = seg(OFF_GR, W_GR).astype(BF16)
    ga = _dot(xb, wga_ref[...]).astype(BF16)
    z = _dot(ga, wa2_ref[...]) + ba_ref[...]
    log_sig = jnp.minimum(z, 0.0) - jnp.log(1.0 + jnp.exp(-jnp.abs(z)))
    gg_ref[...] = log_sig * (1.0 / GLA_TAU)


def _proj(xp, xs, norm_g, w_main, w_ga, w_a2, b_a, qg, kg, bdq, bdk):
    def rows(width):
        return pl.BlockSpec((TM, width), lambda i: (i, 0))

    def full(a):
        return pl.BlockSpec(a.shape, lambda i: (0,) * a.ndim)

    consts = (norm_g, w_main, w_ga, w_a2, b_a, qg, kg, bdq, bdk)
    out_widths = ((W_SQ, BF16), (W_SK, F32), (W_SV, F32), (W_GQ, BF16), (W_GK, BF16), (W_GV, BF16),
                  (W_GQ, F32), (W_GR, BF16))
    return pl.pallas_call(
        _proj_kernel,
        grid=(N_TILES,),
        in_specs=[pl.BlockSpec((TM, D_MODEL), lambda i: (jnp.minimum(i, N_PROMPT_TILES - 1), 0)),
                  pl.BlockSpec((TM, D_MODEL), lambda i: (0, 0))] + [full(a) for a in consts],
        out_specs=[rows(w) for w, _ in out_widths],
        out_shape=[jax.ShapeDtypeStruct((N_ROWS, w), dt) for w, dt in out_widths],
        compiler_params=_cparams(1, "proj"),
        name="proj",
    )(xp, xs, *consts)


def _dup_kv_heads(x):
    r = pltpu.roll(x, HD, axis=1)
    lo = lax.broadcasted_iota(I32, x.shape, 1) < HD
    out = []
    for a in (jnp.where(lo, x, r), jnp.where(lo, r, x)):
        out.append(jnp.concatenate([a, a], axis=1).astype(BF16))
    return out


def _swa_blocks(q_blocks, k_blocks, v_blocks, sinks, valids, head_mask):
    tq = q_blocks[0].shape[0]
    n_q = 4 * tq
    scores = []
    for q, k, valid in zip(q_blocks, k_blocks, valids):
        s_t = _dot_nt(k, jnp.concatenate([q] * 4, axis=0) * head_mask)
        scores.append(s_t if valid is None else jnp.where(valid, s_t, -jnp.inf))
    eye = (lax.broadcasted_iota(I32, (n_q, n_q), 0) == lax.broadcasted_iota(I32, (n_q, n_q), 1)
           ).astype(F32).astype(BF16)
    probs = []
    for s_t, sink in zip(scores, sinks):
        m = jnp.maximum(jnp.max(s_t, axis=0, keepdims=True), sink)
        p_t = jnp.exp(s_t - m)
        den = jnp.sum(p_t, axis=0, keepdims=True) + jnp.exp(sink - m)
        probs.append(_dot_nt(eye, (p_t / den).astype(BF16)).astype(BF16))
    lane_head = lax.broadcasted_iota(I32, (tq, 4 * HD), 1) // HD
    outs = []
    for p, v in zip(probs, v_blocks):
        o_full = _dot(p, v)
        o = jnp.zeros((tq, 4 * HD), F32)
        for a in range(4):
            o = o + jnp.where(lane_head == a, o_full[a * tq:(a + 1) * tq], 0.0)
        outs.append(o)
    return outs


def _swa_prompt_kernel(q_ref, kp_ref, kc_ref, vp_ref, vc_ref, hm_ref, sink_ref, o_ref):
    i = pl.program_id(0)
    k_dup = _dup_kv_heads(jnp.concatenate([kp_ref[...], kc_ref[...]], axis=0))
    v_dup = _dup_kv_heads(jnp.concatenate([vp_ref[...], vc_ref[...]], axis=0))
    sink_row = [sink_ref[j][0:1, :] for j in range(2)]
    span = WINDOW + CHUNK
    key = lax.broadcasted_iota(I32, (span, 4 * CHUNK), 0)
    qs, ks, vs, sinks, valids, where = [], [], [], [], [], []
    for c in range(TM // CHUNK):
        lo = CHUNK * c
        valid = (i * TM - WINDOW + lo + key) >= 0
        for j in range(2):
            qs.append(q_ref[lo:lo + CHUNK, 4 * HD * j:4 * HD * (j + 1)])
            ks.append(k_dup[j][lo:lo + span])
            vs.append(v_dup[j][lo:lo + span])
            sinks.append(sink_row[j])
            valids.append(valid)
            where.append((lo, j))
    outs = _swa_blocks(qs, ks, vs, sinks, valids, hm_ref[...])
    for (lo, j), o in zip(where, outs):
        o_ref[lo:lo + CHUNK, 4 * HD * j:4 * HD * (j + 1)] = o.astype(BF16)


def _swa_prompt(sq, sk, sv, head_mask, sink_b):
    half = TM // 2
    prev = pl.BlockSpec((half, 2 * HD), lambda i: (jnp.maximum(2 * i - 1, 0), 0))
    cur = pl.BlockSpec((TM, 2 * HD), lambda i: (i, 0))
    return pl.pallas_call(
        _swa_prompt_kernel,
        grid=(N_PROMPT_TILES,),
        in_specs=[pl.BlockSpec((TM, W_SQ), lambda i: (i, 0)), prev, cur, prev, cur,
                  pl.BlockSpec(head_mask.shape, lambda i: (0, 0)),
                  pl.BlockSpec(sink_b.shape, lambda i: (0, 0, 0))],
        out_specs=pl.BlockSpec((TM, W_SQ), lambda i: (i, 0)),
        out_shape=jax.ShapeDtypeStruct((N_PROMPT, W_SQ), BF16),
        compiler_params=_cparams(1, "mixer"),
        name="swa_prompt",
    )(sq, sk, sk, sv, sv, head_mask, sink_b)


def _swa_sample_kernel(q_ref, kc_ref, kn_ref, vc_ref, vn_ref, hm_ref, sink_ref, o_ref):
    k_dup = _dup_kv_heads(jnp.concatenate([kc_ref[...], kn_ref[...]], axis=0))
    v_dup = _dup_kv_heads(jnp.concatenate([vc_ref[...], vn_ref[...]], axis=0))
    sink_row = [sink_ref[j][0:1, :] for j in range(2)]
    qs = [q_ref[:, 4 * HD * j:4 * HD * (j + 1)] for j in range(2)]
    outs = _swa_blocks(qs, k_dup, v_dup, sink_row, [None, None], hm_ref[...])
    o_ref[...] = jnp.concatenate(outs, axis=1).astype(BF16)


def _swa_sample(sq, sk, sv, cache_k, cache_v, head_mask, sink_b):
    first = N_PROMPT // T_SAMPLE
    new = lambda width: pl.BlockSpec((T_SAMPLE, width), lambda b: (first + b, 0))
    cache = pl.BlockSpec((None, WINDOW, 2 * HD), lambda b: (b, 0, 0))
    return pl.pallas_call(
        _swa_sample_kernel,
        grid=(N_STREAMS,),
        in_specs=[new(W_SQ), cache, new(2 * HD), cache, new(2 * HD),
                  pl.BlockSpec(head_mask.shape, lambda b: (0, 0)),
                  pl.BlockSpec(sink_b.shape, lambda b: (0, 0, 0))],
        out_specs=pl.BlockSpec((T_SAMPLE, W_SQ), lambda b: (b, 0)),
        out_shape=jax.ShapeDtypeStruct((N_SAMPLE, W_SQ), BF16),
        compiler_params=_cparams(1, "mixer"),
        name="swa_sample",
    )(sq, cache_k, sk, cache_v, sv, head_mask, sink_b)


GLA_SAFE_EXP = 80.0
GLA_SLOW_ROWS = 16


def _gla_kernel(q_ref, k_ref, v_ref, g_ref, gr_ref, s0_ref, tri_ref, m_ref, bd_ref, gn_ref,
                og_ref, sfin_ref, st_ref, o_ref, *, t, n_sub):
    c = pl.program_id(1)
    lanes = N_HEADS_GLA * DK

    @pl.when(c == 0)
    def _():
        st_ref[...] = s0_ref[...]

    n_rows = t * n_sub

    def group_row(x, period, offset):
        g = x.reshape(n_rows // period, period, lanes)[:, offset:offset + 1, :]
        return jnp.broadcast_to(g, (n_rows // period, period, lanes)).reshape(n_rows, lanes)

    g_hi, g_lo = _split_bf16(g_ref[...])
    b = _dot(tri_ref[...], g_hi) + _dot(tri_ref[...], g_lo)
    since_block_start = group_row(b, SUB, 0) - b
    blocked_is_safe = jnp.max(since_block_start) <= GLA_SAFE_EXP

    def emit(o):
        gr = gr_ref[...].astype(F32)
        gate = gr / (1.0 + jnp.exp(-gr))
        outs = []
        for h in range(N_HEADS_GLA):
            sl = slice(h * DV, (h + 1) * DV)
            outs.append(_rms(o[:, sl]) * gn_ref[...] * gate[:, sl])
        og_ref[...] = jnp.concatenate(outs, axis=1).astype(BF16)

    @pl.when(blocked_is_safe)
    def _():
        emit(_gla_blocked(q_ref, k_ref, v_ref, m_ref, bd_ref, st_ref, b, since_block_start, group_row,
                          t=t, n_sub=n_sub))

    @pl.when(jnp.logical_not(blocked_is_safe))
    def _():
        _gla_tokenwise(q_ref, k_ref, v_ref, g_ref, bd_ref, st_ref, o_ref, n_rows=n_rows)
        emit(o_ref[...])

    @pl.when(c == pl.num_programs(1) - 1)
    def _():
        sfin_ref[...] = st_ref[...]


def _gla_tokenwise(q_ref, k_ref, v_ref, g_ref, bd_ref, st_ref, o_ref, *, n_rows):
    row = lax.broadcasted_iota(I32, (GLA_SLOW_ROWS, 1), 0)

    def group(gi, carry):
        rows = pl.ds(pl.multiple_of(gi * GLA_SLOW_ROWS, GLA_SLOW_ROWS), GLA_SLOW_ROWS)
        q = q_ref[rows, :].astype(F32)
        k = k_ref[rows, :].astype(F32)
        v = v_ref[rows, :].astype(F32)
        decay = jnp.exp(g_ref[rows, :])
        o = jnp.zeros((GLA_SLOW_ROWS, N_HEADS_GLA * DV), F32)
        for j in range(GLA_SLOW_ROWS):
            only_j = row == j
            k_j = jnp.where(only_j, k, 0.0).astype(BF16)
            v_j = jnp.where(only_j, v, 0.0).astype(BF16)
            q_j = jnp.where(only_j, q, 0.0).astype(BF16)
            st = st_ref[...] * decay[j:j + 1, :] + _dot_tn(v_j, k_j) * bd_ref[...]
            st_ref[...] = st
            o = o + _dot_nt(q_j, st.astype(BF16))
        o_ref[rows, :] = o
        return carry

    lax.fori_loop(0, n_rows // GLA_SLOW_ROWS, group, 0)


def _gla_blocked(q_ref, k_ref, v_ref, m_ref, bd_ref, st_ref, b, since_block_start, group_row, *, t, n_sub):
    nb = t // SUB
    n_rows = t * n_sub
    lanes = N_HEADS_GLA * DK
    q = q_ref[...].astype(F32)
    k = k_ref[...].astype(F32)
    qd = (q * jnp.exp(-since_block_start)).astype(BF16)
    pos = lax.broadcasted_iota(I32, (n_rows, lanes), 0) & (t - 1)
    k_parts = []
    for blk in range(nb):
        arg = jnp.where(pos < SUB * (blk + 1), group_row(b, t, SUB * blk) - b, NEG_BIG)
        k_parts.append((k * jnp.exp(arg)).astype(BF16))
    k_cat = jnp.concatenate(k_parts, axis=1)
    qd_cat = jnp.concatenate([qd] * nb, axis=1)
    q_dec = (q * jnp.exp(b)).astype(BF16)
    k_last = (k * jnp.exp(group_row(b, t, t - 1) - b)).astype(BF16)
    row_a = lax.broadcasted_iota(I32, (N_HEADS_GLA * t, t), 0) & (t - 1)
    col_a = lax.broadcasted_iota(I32, (N_HEADS_GLA * t, t), 1)

    o_intra, q_decayed, state_add, state_decay = [], [], [], []
    for u in range(n_sub):
        rows = slice(u * t, (u + 1) * t)
        v = v_ref[rows, :]
        lhs = jnp.concatenate([qd_cat[rows]] * N_HEADS_GLA, axis=0) * m_ref[...]
        a = _dot_nt(lhs, k_cat[rows])
        a = jnp.where(row_a >= col_a, a, 0.0).astype(BF16)
        o_full = _dot(a, v)
        o_intra.append(jnp.concatenate(
            [o_full[h * t:(h + 1) * t, h * DV:(h + 1) * DV] for h in range(N_HEADS_GLA)], axis=1))
        q_decayed.append(q_dec[rows])
        state_add.append(_dot_tn(v, k_last[rows]) * bd_ref[...])
        state_decay.append(jnp.exp(b[(u + 1) * t - 1:(u + 1) * t, :]))

    st = st_ref[...]
    o_parts = []
    for u in range(n_sub):
        o_parts.append(o_intra[u] + _dot_nt(q_decayed[u], st.astype(BF16)))
        st = st * state_decay[u] + state_add[u]
    st_ref[...] = st
    return jnp.concatenate(o_parts, axis=0) if n_sub > 1 else o_parts[0]


def _gla(gq, gk, gv, gg, gr, s0, tri, mask, bd, gn, *, t, n_sub, n_batch, n_steps, first_block, name):
    rows_per_step = t * n_sub

    def rows(width):
        return pl.BlockSpec((rows_per_step, width), lambda b, c: (first_block + b * n_steps + c, 0))

    def full(a):
        return pl.BlockSpec(a.shape, lambda b, c: (0,) * a.ndim)

    state = pl.BlockSpec((None,) + s0.shape[1:], lambda b, c: (b, 0, 0))
    return pl.pallas_call(
        functools.partial(_gla_kernel, t=t, n_sub=n_sub),
        grid=(n_batch, n_steps),
        in_specs=[rows(W_GQ), rows(W_GK), rows(W_GV), rows(W_GQ), rows(W_GR), state,
                  full(tri), full(mask), full(bd), full(gn)],
        out_specs=[pl.BlockSpec((rows_per_step, W_GV), lambda b, c: (b * n_steps + c, 0)), state],
        out_shape=[jax.ShapeDtypeStruct((n_batch * n_steps * rows_per_step, W_GV), BF16),
                   jax.ShapeDtypeStruct(s0.shape, F32)],
        scratch_shapes=[pltpu.VMEM(s0.shape[1:], F32), pltpu.VMEM((rows_per_step, W_GV), F32)],
        compiler_params=_cparams(2, "mixer"),
        name=name,
    )(gq, gk, gv, gg, gr, s0, tri, mask, bd, gn)


def _tile_row_copies(hbm_ref, tile, vmem_ref, sem, to_hbm, rows=TM):
    copies = []
    for a in range(ROW_SUB):
        h = hbm_ref.at[pl.ds(tile * rows, rows), a, :]
        v = vmem_ref.at[:, pl.ds(a * LANES, LANES)]
        copies.append(pltpu.make_async_copy(v, h, sem) if to_hbm else pltpu.make_async_copy(h, v, sem))
    return copies


def _store_tile_rows(i, n_steps, outputs, row_buf, row_sem, rows=TM):
    buf_slot = lax.rem(i, 2)

    def store(j, tile, s):
        return _tile_row_copies(outputs[j][0], tile, row_buf.at[j, s], row_sem.at[j, s], True, rows)

    for j, (_, value) in enumerate(outputs):
        @pl.when(i >= 2)
        def _():
            for c in store(j, i - 2, buf_slot):
                c.wait()

        row_buf[j, buf_slot] = value
        for c in store(j, i, buf_slot):
            c.start()

        @pl.when(i == n_steps - 1)
        def _():
            for c in store(j, i - 1, 1 - buf_slot) + store(j, i, buf_slot):
                c.wait()


def _merge_kernel(oswp_ref, osws_ref, ogp_ref, ogs_ref, xp_ref, xs_ref, wo1_ref, wo2_ref, gf_ref, wrh_ref, wrl_ref,
                  br_ref, xrow_ref, hrow_ref, lg_ref, row_buf, row_sem):
    i = pl.program_id(0)
    is_prompt = i < N_PROMPT_TILES
    x = jnp.where(is_prompt, xp_ref[...], xs_ref[...])
    o_swa = jnp.where(is_prompt, oswp_ref[...], osws_ref[...])
    og = jnp.where(is_prompt, ogp_ref[...], ogs_ref[...])
    h = x + (_dot(o_swa, wo1_ref[...]) + _dot(og, wo2_ref[...]))
    xn = _rms(h) * gf_ref[...]
    x_hi, x_lo = _split_bf16(xn)
    logits = _dot(x_hi, wrh_ref[...]) + _dot(x_lo, wrh_ref[...]) + _dot(x_hi, wrl_ref[...]) + br_ref[...]
    lg_ref[...] = logits.T[:N_EXPERTS]
    _store_tile_rows(i, N_TILES, ((xrow_ref, xn), (hrow_ref, h)), row_buf, row_sem)


def _merge(o_swa_p, o_swa_s, og_p, og_s, xp, xs, wo1, wo2, gf, wrh, wrl, br):
    def prompt_rows(width):
        return pl.BlockSpec((TM, width), lambda i: (jnp.minimum(i, N_PROMPT_TILES - 1), 0))

    def sample_rows(width):
        return pl.BlockSpec((TM, width), lambda i: (0, 0))

    def full(a):
        return pl.BlockSpec(a.shape, lambda i: (0,) * a.ndim)

    consts = (wo1, wo2, gf, wrh, wrl, br)
    return pl.pallas_call(
        _merge_kernel,
        grid=(N_TILES,),
        in_specs=[prompt_rows(W_SQ), sample_rows(W_SQ), prompt_rows(W_GV), sample_rows(W_GV),
                  prompt_rows(D_MODEL), sample_rows(D_MODEL)] + [full(a) for a in consts],
        out_specs=[pl.BlockSpec(memory_space=pl.ANY)] * 2 + [pl.BlockSpec((N_EXPERTS, TM), lambda i: (0, i))],
        out_shape=[jax.ShapeDtypeStruct((N_ROWS,) + ROW_TILE, F32)] * 2 + [
                   jax.ShapeDtypeStruct((N_EXPERTS, N_ROWS), F32)],
        scratch_shapes=[pltpu.VMEM((2, 2, TM, D_MODEL), F32), pltpu.SemaphoreType.DMA((2, 2))],
        compiler_params=_cparams(1, "merge"),
        name="merge",
    )(o_swa_p, o_swa_s, og_p, og_s, xp, xs, *consts)


RT = 1280
N_ROUTER_STEPS = N_ROWS // RT


def _router_kernel(lg_ref, tri_ref, wrow_ref, ti_ref, rk_ref, cnt_ref, base_ref, row_buf, row_sem):
    i = pl.program_id(0)

    @pl.when(i == 0)
    def _():
        base_ref[...] = jnp.zeros_like(base_ref)

    logits_t = lg_ref[...]
    expert = lax.broadcasted_iota(I32, logits_t.shape, 0)
    slot = lax.broadcasted_iota(I32, (SLOT_ROWS, RT), 0)
    vals, hots = [], []
    ti = jnp.zeros((SLOT_ROWS, RT), I32)
    for kk in range(TOP_K):
        m = jnp.max(logits_t, axis=0, keepdims=True)
        idx = jnp.min(jnp.where(logits_t == m, expert, N_EXPERTS), axis=0, keepdims=True)
        hot = expert == idx
        logits_t = jnp.where(hot, NEG_BIG, logits_t)
        vals.append(m)
        hots.append(hot)
        ti = jnp.where(slot == kk, idx, ti)
    ti_ref[...] = ti
    exps = [jnp.exp(v - vals[0]) for v in vals]
    den = exps[0] + exps[1] + exps[2] + exps[3]
    tw_t = jnp.zeros((SLOT_ROWS, RT), F32)
    for kk in range(TOP_K):
        tw_t = jnp.where(slot == kk, exps[kk] / den, tw_t)
    eye = (lax.broadcasted_iota(I32, (SLOT_ROWS, LANES), 0)
           == lax.broadcasted_iota(I32, (SLOT_ROWS, LANES), 1)).astype(F32).astype(BF16)
    w_hi = tw_t.astype(BF16)
    w_mid, w_lo = _split_bf16(tw_t - w_hi.astype(F32))
    tw_col = _dot_tn(w_hi, eye) + _dot_tn(w_mid, eye) + _dot_tn(w_lo, eye)

    onehot_t = jnp.zeros(logits_t.shape, F32)
    for hot in hots:
        onehot_t = onehot_t + jnp.where(hot, 1.0, 0.0)
    before_t = _dot(onehot_t.astype(BF16), tri_ref[...]) + base_ref[:, 0:1]
    rk = jnp.zeros((SLOT_ROWS, RT), I32)
    for kk in range(TOP_K):
        r = jnp.sum(jnp.where(hots[kk], before_t, 0.0), axis=0, keepdims=True).astype(I32)
        rk = jnp.where(slot == kk, r, rk)
    rk_ref[...] = rk
    total = base_ref[...] + jnp.sum(onehot_t, axis=1, keepdims=True)
    base_ref[...] = total
    cnt_ref[...] = total.astype(I32)

    w_lanes = [jnp.broadcast_to(tw_col[:, kk:kk + 1], (RT, LANES)) for kk in range(TOP_K)]
    w_rows = jnp.concatenate(w_lanes + [jnp.zeros((RT, D_MODEL - TOP_K * LANES), F32)], axis=1)
    _store_tile_rows(i, N_ROUTER_STEPS, ((wrow_ref, w_rows),), row_buf, row_sem, RT)


def _router(logits, tri):
    return pl.pallas_call(
        _router_kernel,
        grid=(N_ROUTER_STEPS,),
        in_specs=[pl.BlockSpec((N_EXPERTS, RT), lambda i: (0, i)), pl.BlockSpec(tri.shape, lambda i: (0, 0))],
        out_specs=[pl.BlockSpec(memory_space=pl.ANY),
                   pl.BlockSpec((SLOT_ROWS, RT), lambda i: (0, i)),
                   pl.BlockSpec((SLOT_ROWS, RT), lambda i: (0, i)),
                   pl.BlockSpec((N_EXPERTS, LANES), lambda i: (0, 0))],
        out_shape=[jax.ShapeDtypeStruct((N_ROWS,) + ROW_TILE, F32),
                   jax.ShapeDtypeStruct((SLOT_ROWS, N_ROWS), I32),
                   jax.ShapeDtypeStruct((SLOT_ROWS, N_ROWS), I32),
                   jax.ShapeDtypeStruct((N_EXPERTS, LANES), I32)],
        scratch_shapes=[pltpu.VMEM((N_EXPERTS, LANES), F32), pltpu.VMEM((1, 2, RT, D_MODEL), F32),
                        pltpu.SemaphoreType.DMA((1, 2))],
        compiler_params=_cparams(1, "router"),
        name="router",
    )(logits, tri)


ISSUE_UNROLL = 4


def _row_copy(src_ref, src_row, dst_ref, dst_row, sem):
    return pltpu.make_async_copy(src_ref.at[pl.ds(src_row, 1)], dst_ref.at[pl.ds(dst_row, 1)], sem)


def _dispatch_kernel(dest_ref, end_ref, x_ref, xs_ref, zero_ref, sem, zsem):
    i = pl.program_id(0)
    base = i * TM

    @pl.when(i == 0)
    def _():
        zero_ref[...] = jnp.zeros_like(zero_ref)

        def tail_copy(e):
            last = jnp.maximum(end_ref[e] - TM, 0)
            return pltpu.make_async_copy(zero_ref, xs_ref.at[pl.ds(pl.multiple_of(last, TM), TM)], zsem)

        def fill(e, carry):
            tail_copy(e).start()
            return carry

        def fill_wait(e, carry):
            tail_copy(e).wait()
            return carry

        lax.fori_loop(0, N_EXPERTS, fill, 0)
        lax.fori_loop(0, N_EXPERTS, fill_wait, 0)

        def unused_copy(t):
            return pltpu.make_async_copy(zero_ref, xs_ref.at[pl.ds(pl.multiple_of(t * TM, TM), TM)], zsem)

        def fill_unused(t, carry):
            unused_copy(t).start()
            unused_copy(t).wait()
            return carry

        lax.fori_loop(end_ref[N_EXPERTS - 1] // TM, N_EXPERT_TILES, fill_unused, 0)

    def issue(n, carry):
        for kk in range(TOP_K):
            _row_copy(x_ref, n, xs_ref, dest_ref[kk * N_ROWS + base + n], sem).start(priority=kk % 2)
        return carry

    lax.fori_loop(0, TM, issue, 0, unroll=ISSUE_UNROLL)

    for kk in range(TOP_K):
        pltpu.make_async_copy(x_ref, xs_ref.at[pl.ds(0, TM)], sem).wait()


def _dispatch(dest, end, x_packed):
    return pl.pallas_call(
        _dispatch_kernel,
        grid_spec=pltpu.PrefetchScalarGridSpec(
            num_scalar_prefetch=2,
            grid=(N_TILES,),
            in_specs=[pl.BlockSpec((TM,) + ROW_TILE, lambda i, d, e: (i, 0, 0))],
            out_specs=pl.BlockSpec(memory_space=pl.ANY),
            scratch_shapes=[pltpu.VMEM((TM,) + ROW_TILE, F32), pltpu.SemaphoreType.DMA,
                            pltpu.SemaphoreType.DMA],
        ),
        out_shape=jax.ShapeDtypeStruct((N_SORTED_ROWS,) + ROW_TILE, F32),
        compiler_params=_cparams(1, "dispatch"),
        name="dispatch",
    )(dest, end, x_packed)


CAST_ROWS = 128


def _moe_kernel(te_ref, nu_ref, nx_ref, xs_ref, wg_ref, bg_ref, wu_ref, bu_ref, wd_ref, bd_ref, ys_ref,
                w_stage, w_bf, x_buf, y_buf, zero_buf, w_sem, in_sem, out_sem, zero_sem):
    t = pl.program_id(0)
    n_used = nu_ref[0]
    slot = lax.rem(t, 2)
    e = te_ref[t]
    e_prev = te_ref[jnp.maximum(t - 1, 0)]

    def load(tile, s):
        return _tile_row_copies(xs_ref, tile, x_buf.at[s], in_sem.at[s], to_hbm=False)

    def store(tile, s):
        return _tile_row_copies(ys_ref, tile, y_buf.at[s], out_sem.at[s], to_hbm=True)

    def weight_copies(expert):
        return [pltpu.make_async_copy(w.at[expert], w_stage.at[j], w_sem.at[j])
                for j, w in enumerate((wg_ref, wu_ref, wd_ref))]

    @pl.when(t == 0)
    def _():
        for c in weight_copies(e) + load(0, 0):
            c.start()

    @pl.when(t + 1 < n_used)
    def _():
        for c in load(t + 1, 1 - slot):
            c.start()

    @pl.when((t == 0) | (e != e_prev))
    def _():
        for c in weight_copies(e):
            c.wait()

        def cast(r, carry):
            sl = pl.ds(pl.multiple_of(r * CAST_ROWS, CAST_ROWS), CAST_ROWS)
            for j in range(3):
                w_bf[j, sl, :] = w_stage[j, sl, :].astype(BF16)
            return carry

        lax.fori_loop(0, D_MODEL // CAST_ROWS, cast, 0)
        e_next = nx_ref[e]

        @pl.when(e_next >= 0)
        def _():
            for c in weight_copies(e_next):
                c.start()

    @pl.when(t < n_used)
    def _():
        for c in load(t, slot):
            c.wait()

        @pl.when(t >= 2)
        def _():
            for c in store(t - 2, slot):
                c.wait()

        x = x_buf[slot].astype(BF16)
        gate = jnp.minimum(_dot(x, w_bf[0]) + bg_ref[...], SWIGLU_LIMIT)
        up = jnp.clip(_dot(x, w_bf[1]) + bu_ref[...], -SWIGLU_LIMIT, SWIGLU_LIMIT)
        hdn = (up + 1.0) * gate * (1.0 / (1.0 + jnp.exp(-SWIGLU_ALPHA * gate)))
        y_buf[slot] = _dot(hdn.astype(BF16), w_bf[2]) + bd_ref[...]
        for c in store(t, slot):
            c.start()

    @pl.when(t >= n_used)
    def _():
        zero_buf[...] = jnp.zeros_like(zero_buf)
        fill = pltpu.make_async_copy(zero_buf, ys_ref.at[pl.ds(t * TM, TM)], zero_sem)
        fill.start()
        fill.wait()

    @pl.when(t == N_EXPERT_TILES - 1)
    def _():
        @pl.when(n_used >= 2)
        def _():
            for c in store(n_used - 2, lax.rem(n_used, 2)):
                c.wait()

        for c in store(n_used - 1, lax.rem(n_used - 1, 2)):
            c.wait()


def _moe(tile_expert, n_used, next_expert, xs, w_gate, b_gate, w_up, b_up, w_down, b_down):
    hbm = pl.BlockSpec(memory_space=pl.ANY)
    bias = pl.BlockSpec((None, 1, D_MODEL), lambda t, te, nu, nx: (te[t], 0, 0))
    return pl.pallas_call(
        _moe_kernel,
        grid_spec=pltpu.PrefetchScalarGridSpec(
            num_scalar_prefetch=3,
            grid=(N_EXPERT_TILES,),
            in_specs=[hbm, hbm, bias, hbm, bias, hbm, bias],
            out_specs=hbm,
            scratch_shapes=[pltpu.VMEM((3, D_MODEL, D_MODEL), F32), pltpu.VMEM((3, D_MODEL, D_MODEL), BF16),
                            pltpu.VMEM((2, TM, D_MODEL), F32), pltpu.VMEM((2, TM, D_MODEL), F32),
                            pltpu.VMEM((TM,) + ROW_TILE, F32), pltpu.SemaphoreType.DMA((3,)),
                            pltpu.SemaphoreType.DMA((2,)), pltpu.SemaphoreType.DMA((2,)),
                            pltpu.SemaphoreType.DMA],
        ),
        out_shape=jax.ShapeDtypeStruct((N_SORTED_ROWS,) + ROW_TILE, F32),
        compiler_params=_cparams(1, "experts"),
        name="experts",
    )(tile_expert, n_used, next_expert, xs, w_gate, b_gate, w_up, b_up, w_down, b_down)


def _combine_kernel(dest_ref, h_ref, w_ref, ys_ref, yp_ref, ysm_ref, g_ref, out_ref, sem):
    i = pl.program_id(0)
    slot = lax.rem(i, 2)

    def issue_tile(tile, s):
        base = tile * TM

        def issue(n, carry):
            for kk in range(TOP_K):
                _row_copy(ys_ref, dest_ref[kk * N_ROWS + base + n], g_ref.at[s], kk * TM + n,
                          sem.at[s]).start(priority=kk % 2)
            return carry

        lax.fori_loop(0, TM, issue, 0, unroll=ISSUE_UNROLL)

    @pl.when(i == 0)
    def _():
        issue_tile(0, 0)

    @pl.when(i + 1 < N_TILES)
    def _():
        issue_tile(i + 1, 1 - slot)

    for kk in range(TOP_K):
        pltpu.make_async_copy(ys_ref.at[pl.ds(0, TM)], g_ref.at[slot, pl.ds(kk * TM, TM)], sem.at[slot]).wait()

    w = w_ref[...]
    acc = None
    for kk in range(TOP_K):
        part = jnp.broadcast_to(w[:, kk:kk + 1, :], (TM,) + ROW_TILE) * g_ref[slot, kk * TM:(kk + 1) * TM]
        acc = part if acc is None else acc + part
    out_ref[...] = h_ref[...] + acc

    def write(y_ref):
        for a in range(ROW_SUB):
            y_ref[:, a * LANES:(a + 1) * LANES] = out_ref[:, a, :]

    @pl.when(i < N_PROMPT_TILES)
    def _():
        write(yp_ref)

    @pl.when(i >= N_PROMPT_TILES)
    def _():
        write(ysm_ref)


def _combine(dest, h_rows, w_rows, ys):
    tile = lambda index: pl.BlockSpec((TM,) + ROW_TILE, index)
    return pl.pallas_call(
        _combine_kernel,
        grid_spec=pltpu.PrefetchScalarGridSpec(
            num_scalar_prefetch=1,
            grid=(N_TILES,),
            in_specs=[tile(lambda i, d: (i, 0, 0)), tile(lambda i, d: (i, 0, 0)),
                      pl.BlockSpec(memory_space=pl.ANY)],
            out_specs=[pl.BlockSpec((TM, D_MODEL), lambda i, d: (jnp.minimum(i, N_PROMPT_TILES - 1), 0)),
                       pl.BlockSpec((TM, D_MODEL), lambda i, d: (0, 0))],
            scratch_shapes=[pltpu.VMEM((2, TOP_K * TM) + ROW_TILE, F32), pltpu.VMEM((TM,) + ROW_TILE, F32),
                            pltpu.SemaphoreType.DMA((2,))],
        ),
        out_shape=[jax.ShapeDtypeStruct((N_PROMPT, D_MODEL), F32),
                   jax.ShapeDtypeStruct((N_SAMPLE, D_MODEL), F32)],
        compiler_params=_cparams(1, "combine"),
        name="combine",
    )(dest, h_rows, w_rows, ys)


def _block_diag_ones(n, blk):
    idx = np.arange(n) // blk
    return (idx[:, None] == idx[None, :]).astype(np.float32)


def _swa_head_mask(tq):
    row_head = np.arange(4 * tq)[:, None] // tq
    lane_head = np.arange(4 * HD)[None, :] // HD
    return jnp.asarray((row_head == lane_head).astype(np.float32), BF16)


def _gla_masks(t, n_sub):
    nb = t // SUB
    lanes = N_HEADS_GLA * DK
    tri = jnp.asarray(np.kron(np.eye(n_sub, dtype=np.float32), np.tril(np.ones((t, t), np.float32))), BF16)
    row = np.arange(N_HEADS_GLA * t)
    col = np.arange(nb * lanes)
    same_head = (row[:, None] // t) == ((col[None, :] % lanes) // DK)
    same_blk = ((row[:, None] % t) // SUB) == (col[None, :] // lanes)
    mask = jnp.asarray((same_head & same_blk).astype(np.float32), BF16)
    return tri, mask


def _sink_rows(sinks, tq):
    s = jnp.repeat(sinks.astype(F32).reshape(2, 4), tq, axis=1)
    return jnp.broadcast_to(s[:, None, :], (2, 8, 4 * tq))


def kernel(x_prompt, x_sample, state_gla, cache_swa_k, cache_swa_v, norm_mix_g, w_in, w_gla_a2, b_gla_a, q_norm_g,
           k_norm_g, swa_sinks, gla_norm_g, w_out, norm_ffn_g, w_router, b_router, w_gate, b_gate, w_up, b_up,
           w_down, b_down):
    xp = x_prompt.reshape(N_PROMPT, D_MODEL)
    xs = x_sample.reshape(N_SAMPLE, D_MODEL)

    w_in0 = w_in[0]
    w_main = w_in0[:, :W_MAIN].astype(BF16)
    w_ga = jnp.pad(w_in0[:, OFF_GA:], ((0, 0), (0, LANES - W_GA))).astype(BF16)
    w_a2 = jnp.pad(w_gla_a2[0], ((0, LANES - W_GA), (0, 0))).astype(BF16)
    b_a = b_gla_a[0].reshape(1, -1)
    qg = (jnp.tile(q_norm_g[0], N_HEADS_SWA) * (HD ** -0.5)).reshape(1, -1)
    kg = jnp.tile(k_norm_g[0], 2).reshape(1, -1)
    bdq = jnp.asarray(_block_diag_ones(W_SQ, HD), BF16)
    bdk = jnp.asarray(_block_diag_ones(W_SK, HD), BF16)

    sq, sk, sv, gq, gk, gv, gg, gr = _proj(xp, xs, norm_mix_g[0].reshape(1, -1), w_main, w_ga, w_a2, b_a,
                                           qg, kg, bdq, bdk)

    o_swa_p = _swa_prompt(sq, sk, sv, _swa_head_mask(CHUNK), _sink_rows(swa_sinks[0], CHUNK))
    cache_k = cache_swa_k[0].reshape(N_STREAMS, WINDOW, 2 * HD)
    cache_v = cache_swa_v[0].reshape(N_STREAMS, WINDOW, 2 * HD)
    o_swa_s = _swa_sample(sq, sk, sv, cache_k, cache_v, _swa_head_mask(T_SAMPLE),
                          _sink_rows(swa_sinks[0], T_SAMPLE))

    bd_state = jnp.asarray(_block_diag_ones(N_HEADS_GLA, 1).repeat(DV, axis=0).repeat(DK, axis=1), F32)
    gn = gla_norm_g[0].reshape(1, -1)
    tri_p, mask_p = _gla_masks(CHUNK, TM // CHUNK)
    tri_s, mask_s = _gla_masks(T_SAMPLE, 1)
    s0_p = jnp.zeros((1, N_HEADS_GLA * DV, N_HEADS_GLA * DK), F32)
    og_p, sfin_p = _gla(gq, gk, gv, gg, gr, s0_p, tri_p, mask_p, bd_state, gn, t=CHUNK, n_sub=TM // CHUNK,
                        n_batch=1, n_steps=N_PROMPT // TM, first_block=0, name="gla_prompt")
    eye = jnp.eye(N_HEADS_GLA, dtype=F32)
    s0_s = jnp.einsum('bhde,hg->bhegd', state_gla[0].astype(F32), eye).reshape(
        N_STREAMS, N_HEADS_GLA * DV, N_HEADS_GLA * DK)
    og_s, sfin_s = _gla(gq, gk, gv, gg, gr, s0_s, tri_s, mask_s, bd_state, gn, t=T_SAMPLE, n_sub=1,
                        n_batch=N_STREAMS, n_steps=1, first_block=N_PROMPT // T_SAMPLE, name="gla_sample")

    def unpack_state(sfin):
        s = sfin.reshape(-1, N_HEADS_GLA, DV, N_HEADS_GLA, DK)
        s = jnp.stack([s[:, h, :, h, :] for h in range(N_HEADS_GLA)], axis=1)
        return jnp.transpose(s, (0, 1, 3, 2))[None]

    w_out0 = w_out[0].astype(BF16)
    wr = jnp.pad(w_router[0], ((0, 0), (0, LANES - N_EXPERTS)))
    wr_hi = wr.astype(BF16)
    wr_lo = (wr - wr_hi.astype(F32)).astype(BF16)
    br = jnp.pad(b_router[0].astype(F32), (0, LANES - N_EXPERTS)).reshape(1, -1)
    x_rows, h_rows, logits = _merge(o_swa_p, o_swa_s, og_p, og_s, xp, xs, w_out0[:W_SQ], w_out0[W_SQ:],
                                    norm_ffn_g[0].reshape(1, -1), wr_hi, wr_lo, br)
    earlier = jnp.asarray(np.triu(np.ones((RT, RT), np.float32), 1), BF16)
    w_rows, top_i, rank, counts = _router(logits, earlier)

    counts = counts[:, 0]
    padded = (counts + TM - 1) // TM * TM
    end = jnp.cumsum(padded)
    start = end - padded
    experts = jnp.arange(N_EXPERTS, dtype=I32)
    is_e = top_i[:TOP_K, :, None] == experts
    dest = (rank[:TOP_K] + jnp.sum(jnp.where(is_e, start, 0), axis=-1)).reshape(-1).astype(I32)
    n_used = (end[-1] // TM).astype(I32)
    tiles = jnp.minimum(jnp.arange(N_EXPERT_TILES, dtype=I32), n_used - 1)
    tile_expert = jnp.sum((tiles[:, None] * TM >= end[None, :]).astype(I32), axis=1)
    later_nonempty = (experts[None, :] > experts[:, None]) & (padded[None, :] > 0)
    next_expert = jnp.min(jnp.where(later_nonempty, experts[None, :], N_EXPERTS), axis=1)
    next_expert = jnp.where(next_expert < N_EXPERTS, next_expert, -1).astype(I32)

    xs_sorted = _dispatch(dest, end.astype(I32), x_rows)
    ys = _moe(tile_expert, n_used.reshape(1), next_expert, xs_sorted, w_gate[0], b_gate[0].reshape(N_EXPERTS, 1, -1),
              w_up[0], b_up[0].reshape(N_EXPERTS, 1, -1), w_down[0], b_down[0].reshape(N_EXPERTS, 1, -1))
    y_p, y_s = _combine(dest, h_rows, w_rows, ys)

    sk_s = sk[N_PROMPT:].reshape(N_STREAMS, T_SAMPLE, 2 * HD)
    sv_s = sv[N_PROMPT:].reshape(N_STREAMS, T_SAMPLE, 2 * HD)
    kc_s = jnp.concatenate([cache_k[:, T_SAMPLE:], sk_s], axis=1).reshape(1, N_STREAMS, WINDOW, 2, HD)
    vc_s = jnp.concatenate([cache_v[:, T_SAMPLE:], sv_s], axis=1).reshape(1, N_STREAMS, WINDOW, 2, HD)
    kc_p = sk[N_PROMPT - WINDOW:N_PROMPT].reshape(1, 1, WINDOW, 2, HD)
    vc_p = sv[N_PROMPT - WINDOW:N_PROMPT].reshape(1, 1, WINDOW, 2, HD)
    return (y_p.reshape(1, N_PROMPT, D_MODEL), y_s.reshape(N_STREAMS, T_SAMPLE, D_MODEL),
            unpack_state(sfin_p), kc_p, vc_p, unpack_state(sfin_s), kc_s, vc_s)
```

```python
import functools

import numpy as np
import jax
import jax.numpy as jnp
from jax import lax
from jax.experimental import pallas as pl
from jax.experimental.pallas import tpu as pltpu

F32 = jnp.float32
BF16 = jnp.bfloat16
I32 = jnp.int32
U32 = jnp.uint32

D_MODEL = 1024
N_PROMPT = 16384
N_STREAMS = 8
T_SAMPLE = 32
N_SAMPLE = N_STREAMS * T_SAMPLE
N_ROWS = N_PROMPT + N_SAMPLE
EPS = 1e-6

CHUNK = 64
SUB = 16
N_HEADS_SWA = 8
HD = 64
WINDOW = 128
N_HEADS_GLA = 4
DK = 64
DV = 128
GLA_TAU = 16.0
N_EXPERTS = 32
TOP_K = 4
SWIGLU_ALPHA = 1.702
SWIGLU_LIMIT = 7.0

TM = 256
N_TILES = N_ROWS // TM
N_PROMPT_TILES = N_PROMPT // TM
N_ASSIGN = N_ROWS * TOP_K
N_EXPERT_TILES = N_ASSIGN // TM + N_EXPERTS
N_SORTED_ROWS = N_EXPERT_TILES * TM
LANES = 128
SLOT_ROWS = 16
NEG_BIG = -1e30

W_SQ, W_SK, W_SV, W_GQ, W_GK, W_GV, W_GR, W_GA = 512, 128, 128, 256, 256, 512, 512, 16
OFF_SQ = 0
OFF_SK = OFF_SQ + W_SQ
OFF_SV = OFF_SK + W_SK
OFF_GQ = OFF_SV + W_SV
OFF_GK = OFF_GQ + W_GQ
OFF_GV = OFF_GK + W_GK
OFF_GR = OFF_GV + W_GV
OFF_GA = OFF_GR + W_GR
W_MAIN = OFF_GA


VMEM_MIB = dict(proj=40, mixer=32, merge=32, router=48, dispatch=32, experts=52, combine=40)


def _cparams(n_grid_axes, call):
    return pltpu.CompilerParams(dimension_semantics=("arbitrary",) * n_grid_axes,
                                vmem_limit_bytes=VMEM_MIB[call] * 2 ** 20)


def _dot(a, b):
    return jnp.dot(a, b, preferred_element_type=F32)


def _dot_nt(a, b):
    return lax.dot_general(a, b, (((1,), (1,)), ((), ())), preferred_element_type=F32)


def _dot_tn(a, b):
    return lax.dot_general(a, b, (((0,), (0,)), ((), ())), preferred_element_type=F32)


def _split_bf16(x):
    hi = x.astype(BF16)
    lo = (x - hi.astype(F32)).astype(BF16)
    return hi, lo


def _rms(x):
    return x * lax.rsqrt(jnp.mean(x * x, axis=-1, keepdims=True) + EPS)


ROW_SUB = 8
ROW_TILE = (ROW_SUB, LANES)


def _proj_kernel(xp_ref, xs_ref, g_ref, w_ref, wga_ref, wa2_ref, ba_ref, qg_ref, kg_ref, bdq_ref, bdk_ref,
                 sq_ref, sk_ref, sv_ref, gq_ref, gk_ref, gv_ref, gg_ref, gr_ref):
    i = pl.program_id(0)
    x = jnp.where(i < N_PROMPT_TILES, xp_ref[...], xs_ref[...])
    xb = (_rms(x) * g_ref[...]).astype(BF16)

    def seg(off, width):
        return _dot(xb, w_ref[:, off:off + width])

    def head_norm(u, bd_ref):
        hi, lo = _split_bf16(u * u)
        ss = _dot(hi, bd_ref[...]) + _dot(lo, bd_ref[...])
        return u * lax.rsqrt(ss * (1.0 / HD) + EPS)

    sq_ref[...] = (head_norm(seg(OFF_SQ, W_SQ), bdq_ref) * qg_ref[...]).astype(BF16)
    sk_ref[...] = head_norm(seg(OFF_SK, W_SK), bdk_ref) * kg_ref[...]
    sv_ref[...] = seg(OFF_SV, W_SV)
    gq_ref[...] = (seg(OFF_GQ, W_GQ) * (DK ** -0.5)).astype(BF16)
    gk_ref[...] = seg(OFF_GK, W_GK).astype(BF16)
    gv_ref[...] = seg(OFF_GV, W_GV).astype(BF16)
    gr_ref[...] = seg(OFF_GR, W_GR).astype(BF16)
    ga = _dot(xb, wga_ref[...]).astype(BF16)
    z = _dot(ga, wa2_ref[...]) + ba_ref[...]
    log_sig = jnp.minimum(z, 0.0) - jnp.log(1.0 + jnp.exp(-jnp.abs(z)))
    gg_ref[...] = log_sig * (1.0 / GLA_TAU)


def _proj(xp, xs, norm_g, w_main, w_ga, w_a2, b_a, qg, kg, bdq, bdk):
    def rows(width):
        return pl.BlockSpec((TM, width), lambda i: (i, 0))

    def full(a):
        return pl.BlockSpec(a.shape, lambda i: (0,) * a.ndim)

    consts = (norm_g, w_main, w_ga, w_a2, b_a, qg, kg, bdq, bdk)
    out_widths = ((W_SQ, BF16), (W_SK, F32), (W_SV, F32), (W_GQ, BF16), (W_GK, BF16), (W_GV, BF16),
                  (W_GQ, F32), (W_GR, BF16))
    return pl.pallas_call(
        _proj_kernel,
        grid=(N_TILES,),
        in_specs=[pl.BlockSpec((TM, D_MODEL), lambda i: (jnp.minimum(i, N_PROMPT_TILES - 1), 0)),
                  pl.BlockSpec((TM, D_MODEL), lambda i: (0, 0))] + [full(a) for a in consts],
        out_specs=[rows(w) for w, _ in out_widths],
        out_shape=[jax.ShapeDtypeStruct((N_ROWS, w), dt) for w, dt in out_widths],
        compiler_params=_cparams(1, "proj"),
        name="proj",
    )(xp, xs, *consts)


def _dup_kv_heads(x):
    r = pltpu.roll(x, HD, axis=1)
    lo = lax.broadcasted_iota(I32, x.shape, 1) < HD
    out = []
    for a in (jnp.where(lo, x, r), jnp.where(lo, r, x)):
        out.append(jnp.concatenate([a, a], axis=1).astype(BF16))
    return out


def _swa_blocks(q_blocks, k_blocks, v_blocks, sinks, valids, head_mask):
    tq = q_blocks[0].shape[0]
    n_q = 4 * tq
    scores = []
    for q, k, valid in zip(q_blocks, k_blocks, valids):
        s_t = _dot_nt(k, jnp.concatenate([q] * 4, axis=0) * head_mask)
        scores.append(s_t if valid is None else jnp.where(valid, s_t, -jnp.inf))
    eye = (lax.broadcasted_iota(I32, (n_q, n_q), 0) == lax.broadcasted_iota(I32, (n_q, n_q), 1)
           ).astype(F32).astype(BF16)
    probs = []
    for s_t, sink in zip(scores, sinks):
        m = jnp.maximum(jnp.max(s_t, axis=0, keepdims=True), sink)
        p_t = jnp.exp(s_t - m)
        den = jnp.sum(p_t, axis=0, keepdims=True) + jnp.exp(sink - m)
        probs.append(_dot_nt(eye, (p_t / den).astype(BF16)).astype(BF16))
    lane_head = lax.broadcasted_iota(I32, (tq, 4 * HD), 1) // HD
    outs = []
    for p, v in zip(probs, v_blocks):
        o_full = _dot(p, v)
        o = jnp.zeros((tq, 4 * HD), F32)
        for a in range(4):
            o = o + jnp.where(lane_head == a, o_full[a * tq:(a + 1) * tq], 0.0)
        outs.append(o)
    return outs


def _swa_prompt_tile(i, q_ref, kp_ref, kc_ref, vp_ref, vc_ref, hm_ref, sink_ref, o_ref):
    k_dup = _dup_kv_heads(jnp.concatenate([kp_ref[...], kc_ref[...]], axis=0))
    v_dup = _dup_kv_heads(jnp.concatenate([vp_ref[...], vc_ref[...]], axis=0))
    sink_row = [sink_ref[j][0:1, :] for j in range(2)]
    span = WINDOW + CHUNK
    key = lax.broadcasted_iota(I32, (span, 4 * CHUNK), 0)
    qs, ks, vs, sinks, valids, where = [], [], [], [], [], []
    for c in range(TM // CHUNK):
        lo = CHUNK * c
        valid = (i * TM - WINDOW + lo + key) >= 0
        for j in range(2):
            qs.append(q_ref[lo:lo + CHUNK, 4 * HD * j:4 * HD * (j + 1)])
            ks.append(k_dup[j][lo:lo + span])
            vs.append(v_dup[j][lo:lo + span])
            sinks.append(sink_row[j])
            valids.append(valid)
            where.append((lo, j))
    outs = _swa_blocks(qs, ks, vs, sinks, valids, hm_ref[...])
    for (lo, j), o in zip(where, outs):
        o_ref[lo:lo + CHUNK, 4 * HD * j:4 * HD * (j + 1)] = o.astype(BF16)


def _swa_sample_kernel(q_ref, kc_ref, kn_ref, vc_ref, vn_ref, hm_ref, sink_ref, o_ref):
    k_dup = _dup_kv_heads(jnp.concatenate([kc_ref[...], kn_ref[...]], axis=0))
    v_dup = _dup_kv_heads(jnp.concatenate([vc_ref[...], vn_ref[...]], axis=0))
    sink_row = [sink_ref[j][0:1, :] for j in range(2)]
    qs = [q_ref[:, 4 * HD * j:4 * HD * (j + 1)] for j in range(2)]
    outs = _swa_blocks(qs, k_dup, v_dup, sink_row, [None, None], hm_ref[...])
    o_ref[...] = jnp.concatenate(outs, axis=1).astype(BF16)


def _swa_sample(sq, sk, sv, cache_k, cache_v, head_mask, sink_b):
    first = N_PROMPT // T_SAMPLE
    new = lambda width: pl.BlockSpec((T_SAMPLE, width), lambda b: (first + b, 0))
    cache = pl.BlockSpec((None, WINDOW, 2 * HD), lambda b: (b, 0, 0))
    return pl.pallas_call(
        _swa_sample_kernel,
        grid=(N_STREAMS,),
        in_specs=[new(W_SQ), cache, new(2 * HD), cache, new(2 * HD),
                  pl.BlockSpec(head_mask.shape, lambda b: (0, 0)),
                  pl.BlockSpec(sink_b.shape, lambda b: (0, 0, 0))],
        out_specs=pl.BlockSpec((T_SAMPLE, W_SQ), lambda b: (b, 0)),
        out_shape=jax.ShapeDtypeStruct((N_SAMPLE, W_SQ), BF16),
        compiler_params=_cparams(1, "mixer"),
        name="swa_sample",
    )(sq, cache_k, sk, cache_v, sv, head_mask, sink_b)


GLA_SAFE_EXP = 80.0
GLA_SLOW_ROWS = 16


def _gla_step(c, q_ref, k_ref, v_ref, g_ref, gr_ref, s0_ref, tri_ref, m_ref, bd_ref, gn_ref,
              og_ref, sfin_ref, st_ref, o_ref, *, t, n_sub, companion=None):
    lanes = N_HEADS_GLA * DK

    @pl.when(c == 0)
    def _():
        st_ref[...] = s0_ref[...]

    n_rows = t * n_sub

    def group_row(x, period, offset):
        g = x.reshape(n_rows // period, period, lanes)[:, offset:offset + 1, :]
        return jnp.broadcast_to(g, (n_rows // period, period, lanes)).reshape(n_rows, lanes)

    g_hi, g_lo = _split_bf16(g_ref[...])
    b = _dot(tri_ref[...], g_hi) + _dot(tri_ref[...], g_lo)
    since_block_start = group_row(b, SUB, 0) - b
    blocked_is_safe = jnp.max(since_block_start) <= GLA_SAFE_EXP

    def emit(o):
        gr = gr_ref[...].astype(F32)
        gate = gr / (1.0 + jnp.exp(-gr))
        outs = []
        for h in range(N_HEADS_GLA):
            sl = slice(h * DV, (h + 1) * DV)
            outs.append(_rms(o[:, sl]) * gn_ref[...] * gate[:, sl])
        og_ref[...] = jnp.concatenate(outs, axis=1).astype(BF16)

    @pl.when(blocked_is_safe)
    def _():
        if companion is not None:
            companion()
        emit(_gla_blocked(q_ref, k_ref, v_ref, m_ref, bd_ref, st_ref, b, since_block_start, group_row,
                          t=t, n_sub=n_sub))

    @pl.when(jnp.logical_not(blocked_is_safe))
    def _():
        if companion is not None:
            companion()
        _gla_tokenwise(q_ref, k_ref, v_ref, g_ref, bd_ref, st_ref, o_ref, n_rows=n_rows)
        emit(o_ref[...])

    @pl.when(c == pl.num_programs(1) - 1)
    def _():
        sfin_ref[...] = st_ref[...]


def _gla_kernel(*refs, t, n_sub):
    _gla_step(pl.program_id(1), *refs, t=t, n_sub=n_sub)


N_SWA_INPUTS = 7
N_GLA_INPUTS = 10


def _mixers_prompt_kernel(*refs, t, n_sub):
    swa_in = refs[:N_SWA_INPUTS]
    gla_in = refs[N_SWA_INPUTS:N_SWA_INPUTS + N_GLA_INPUTS]
    o_swa_ref, og_ref, sfin_ref, st_ref, o_ref = refs[N_SWA_INPUTS + N_GLA_INPUTS:]
    c = pl.program_id(1)
    _gla_step(c, *gla_in, og_ref, sfin_ref, st_ref, o_ref, t=t, n_sub=n_sub,
              companion=lambda: _swa_prompt_tile(c, *swa_in, o_swa_ref))


def _gla_tokenwise(q_ref, k_ref, v_ref, g_ref, bd_ref, st_ref, o_ref, *, n_rows):
    row = lax.broadcasted_iota(I32, (GLA_SLOW_ROWS, 1), 0)

    def group(gi, carry):
        rows = pl.ds(pl.multiple_of(gi * GLA_SLOW_ROWS, GLA_SLOW_ROWS), GLA_SLOW_ROWS)
        q = q_ref[rows, :].astype(F32)
        k = k_ref[rows, :].astype(F32)
        v = v_ref[rows, :].astype(F32)
        decay = jnp.exp(g_ref[rows, :])
        o = jnp.zeros((GLA_SLOW_ROWS, N_HEADS_GLA * DV), F32)
        for j in range(GLA_SLOW_ROWS):
            only_j = row == j
            k_j = jnp.where(only_j, k, 0.0).astype(BF16)
            v_j = jnp.where(only_j, v, 0.0).astype(BF16)
            q_j = jnp.where(only_j, q, 0.0).astype(BF16)
            st = st_ref[...] * decay[j:j + 1, :] + _dot_tn(v_j, k_j) * bd_ref[...]
            st_ref[...] = st
            o = o + _dot_nt(q_j, st.astype(BF16))
        o_ref[rows, :] = o
        return carry

    lax.fori_loop(0, n_rows // GLA_SLOW_ROWS, group, 0)


def _gla_blocked(q_ref, k_ref, v_ref, m_ref, bd_ref, st_ref, b, since_block_start, group_row, *, t, n_sub):
    nb = t // SUB
    n_rows = t * n_sub
    lanes = N_HEADS_GLA * DK
    q = q_ref[...].astype(F32)
    k = k_ref[...].astype(F32)
    qd = (q * jnp.exp(-since_block_start)).astype(BF16)
    pos = lax.broadcasted_iota(I32, (n_rows, lanes), 0) & (t - 1)
    k_parts = []
    for blk in range(nb):
        arg = jnp.where(pos < SUB * (blk + 1), group_row(b, t, SUB * blk) - b, NEG_BIG)
        k_parts.append((k * jnp.exp(arg)).astype(BF16))
    k_cat = jnp.concatenate(k_parts, axis=1)
    qd_cat = jnp.concatenate([qd] * nb, axis=1)
    q_dec = (q * jnp.exp(b)).astype(BF16)
    k_last = (k * jnp.exp(group_row(b, t, t - 1) - b)).astype(BF16)
    row_a = lax.broadcasted_iota(I32, (N_HEADS_GLA * t, t), 0) & (t - 1)
    col_a = lax.broadcasted_iota(I32, (N_HEADS_GLA * t, t), 1)

    o_intra, q_decayed, state_add, state_decay = [], [], [], []
    for u in range(n_sub):
        rows = slice(u * t, (u + 1) * t)
        v = v_ref[rows, :]
        lhs = jnp.concatenate([qd_cat[rows]] * N_HEADS_GLA, axis=0) * m_ref[...]
        a = _dot_nt(lhs, k_cat[rows])
        a = jnp.where(row_a >= col_a, a, 0.0).astype(BF16)
        o_full = _dot(a, v)
        o_intra.append(jnp.concatenate(
            [o_full[h * t:(h + 1) * t, h * DV:(h + 1) * DV] for h in range(N_HEADS_GLA)], axis=1))
        q_decayed.append(q_dec[rows])
        state_add.append(_dot_tn(v, k_last[rows]) * bd_ref[...])
        state_decay.append(jnp.exp(b[(u + 1) * t - 1:(u + 1) * t, :]))

    st = st_ref[...]
    o_parts = []
    for u in range(n_sub):
        o_parts.append(o_intra[u] + _dot_nt(q_decayed[u], st.astype(BF16)))
        st = st * state_decay[u] + state_add[u]
    st_ref[...] = st
    return jnp.concatenate(o_parts, axis=0) if n_sub > 1 else o_parts[0]


def _gla(gq, gk, gv, gg, gr, s0, tri, mask, bd, gn, *, t, n_sub, n_batch, n_steps, first_block, name):
    rows_per_step = t * n_sub

    def rows(width):
        return pl.BlockSpec((rows_per_step, width), lambda b, c: (first_block + b * n_steps + c, 0))

    def full(a):
        return pl.BlockSpec(a.shape, lambda b, c: (0,) * a.ndim)

    state = pl.BlockSpec((None,) + s0.shape[1:], lambda b, c: (b, 0, 0))
    return pl.pallas_call(
        functools.partial(_gla_kernel, t=t, n_sub=n_sub),
        grid=(n_batch, n_steps),
        in_specs=[rows(W_GQ), rows(W_GK), rows(W_GV), rows(W_GQ), rows(W_GR), state,
                  full(tri), full(mask), full(bd), full(gn)],
        out_specs=[pl.BlockSpec((rows_per_step, W_GV), lambda b, c: (b * n_steps + c, 0)), state],
        out_shape=[jax.ShapeDtypeStruct((n_batch * n_steps * rows_per_step, W_GV), BF16),
                   jax.ShapeDtypeStruct(s0.shape, F32)],
        scratch_shapes=[pltpu.VMEM(s0.shape[1:], F32), pltpu.VMEM((rows_per_step, W_GV), F32)],
        compiler_params=_cparams(2, "mixer"),
        name=name,
    )(gq, gk, gv, gg, gr, s0, tri, mask, bd, gn)


def _mixers_prompt(sq, sk, sv, head_mask, sink_b, gq, gk, gv, gg, gr, s0, tri, mask, bd, gn):
    half = TM // 2
    prev = pl.BlockSpec((half, 2 * HD), lambda b, c: (jnp.maximum(2 * c - 1, 0), 0))

    def rows(width):
        return pl.BlockSpec((TM, width), lambda b, c: (c, 0))

    def full(a):
        return pl.BlockSpec(a.shape, lambda b, c: (0,) * a.ndim)

    state = pl.BlockSpec((None,) + s0.shape[1:], lambda b, c: (b, 0, 0))
    return pl.pallas_call(
        functools.partial(_mixers_prompt_kernel, t=CHUNK, n_sub=TM // CHUNK),
        grid=(1, N_PROMPT_TILES),
        in_specs=[rows(W_SQ), prev, rows(2 * HD), prev, rows(2 * HD), full(head_mask), full(sink_b),
                  rows(W_GQ), rows(W_GK), rows(W_GV), rows(W_GQ), rows(W_GR), state,
                  full(tri), full(mask), full(bd), full(gn)],
        out_specs=[rows(W_SQ), rows(W_GV), state],
        out_shape=[jax.ShapeDtypeStruct((N_PROMPT, W_SQ), BF16), jax.ShapeDtypeStruct((N_PROMPT, W_GV), BF16),
                   jax.ShapeDtypeStruct(s0.shape, F32)],
        scratch_shapes=[pltpu.VMEM(s0.shape[1:], F32), pltpu.VMEM((TM, W_GV), F32)],
        compiler_params=_cparams(2, "mixer"),
        name="mixers_prompt",
    )(sq, sk, sk, sv, sv, head_mask, sink_b, gq, gk, gv, gg, gr, s0, tri, mask, bd, gn)


def _tile_row_copies(hbm_ref, tile, vmem_ref, sem, to_hbm, rows=TM):
    copies = []
    for a in range(ROW_SUB):
        h = hbm_ref.at[pl.ds(tile * rows, rows), a, :]
        v = vmem_ref.at[:, pl.ds(a * LANES, LANES)]
        copies.append(pltpu.make_async_copy(v, h, sem) if to_hbm else pltpu.make_async_copy(h, v, sem))
    return copies


def _store_tile_rows(i, n_steps, outputs, row_buf, row_sem, rows=TM):
    buf_slot = lax.rem(i, 2)

    def store(j, tile, s):
        return _tile_row_copies(outputs[j][0], tile, row_buf.at[j, s], row_sem.at[j, s], True, rows)

    for j, (_, value) in enumerate(outputs):
        @pl.when(i >= 2)
        def _():
            for c in store(j, i - 2, buf_slot):
                c.wait()

        row_buf[j, buf_slot] = value
        for c in store(j, i, buf_slot):
            c.start()

        @pl.when(i == n_steps - 1)
        def _():
            for c in store(j, i - 1, 1 - buf_slot) + store(j, i, buf_slot):
                c.wait()


def _merge_kernel(oswp_ref, osws_ref, ogp_ref, ogs_ref, xp_ref, xs_ref, wo1_ref, wo2_ref, gf_ref, wrh_ref, wrl_ref,
                  br_ref, xrow_ref, hrow_ref, lg_ref, row_buf, row_sem):
    i = pl.program_id(0)
    is_prompt = i < N_PROMPT_TILES
    x = jnp.where(is_prompt, xp_ref[...], xs_ref[...])
    o_swa = jnp.where(is_prompt, oswp_ref[...], osws_ref[...])
    og = jnp.where(is_prompt, ogp_ref[...], ogs_ref[...])
    h = x + (_dot(o_swa, wo1_ref[...]) + _dot(og, wo2_ref[...]))
    xn = _rms(h) * gf_ref[...]
    x_hi, x_lo = _split_bf16(xn)
    logits = _dot(x_hi, wrh_ref[...]) + _dot(x_lo, wrh_ref[...]) + _dot(x_hi, wrl_ref[...]) + br_ref[...]
    lg_ref[...] = logits.T[:N_EXPERTS]
    _store_tile_rows(i, N_TILES, ((xrow_ref, xn), (hrow_ref, h)), row_buf, row_sem)


def _merge(o_swa_p, o_swa_s, og_p, og_s, xp, xs, wo1, wo2, gf, wrh, wrl, br):
    def prompt_rows(width):
        return pl.BlockSpec((TM, width), lambda i: (jnp.minimum(i, N_PROMPT_TILES - 1), 0))

    def sample_rows(width):
        return pl.BlockSpec((TM, width), lambda i: (0, 0))

    def full(a):
        return pl.BlockSpec(a.shape, lambda i: (0,) * a.ndim)

    consts = (wo1, wo2, gf, wrh, wrl, br)
    return pl.pallas_call(
        _merge_kernel,
        grid=(N_TILES,),
        in_specs=[prompt_rows(W_SQ), sample_rows(W_SQ), prompt_rows(W_GV), sample_rows(W_GV),
                  prompt_rows(D_MODEL), sample_rows(D_MODEL)] + [full(a) for a in consts],
        out_specs=[pl.BlockSpec(memory_space=pl.ANY)] * 2 + [pl.BlockSpec((N_EXPERTS, TM), lambda i: (0, i))],
        out_shape=[jax.ShapeDtypeStruct((N_ROWS,) + ROW_TILE, F32)] * 2 + [
                   jax.ShapeDtypeStruct((N_EXPERTS, N_ROWS), F32)],
        scratch_shapes=[pltpu.VMEM((2, 2, TM, D_MODEL), F32), pltpu.SemaphoreType.DMA((2, 2))],
        compiler_params=_cparams(1, "merge"),
        name="merge",
    )(o_swa_p, o_swa_s, og_p, og_s, xp, xs, *consts)


RT = 1280
N_ROUTER_STEPS = N_ROWS // RT


def _router_kernel(lg_ref, tri_ref, wrow_ref, ti_ref, rk_ref, cnt_ref, base_ref, row_buf, row_sem):
    i = pl.program_id(0)

    @pl.when(i == 0)
    def _():
        base_ref[...] = jnp.zeros_like(base_ref)

    logits_t = lg_ref[...]
    expert = lax.broadcasted_iota(I32, logits_t.shape, 0)
    slot = lax.broadcasted_iota(I32, (SLOT_ROWS, RT), 0)
    vals, hots = [], []
    ti = jnp.zeros((SLOT_ROWS, RT), I32)
    for kk in range(TOP_K):
        m = jnp.max(logits_t, axis=0, keepdims=True)
        idx = jnp.min(jnp.where(logits_t == m, expert, N_EXPERTS), axis=0, keepdims=True)
        hot = expert == idx
        logits_t = jnp.where(hot, NEG_BIG, logits_t)
        vals.append(m)
        hots.append(hot)
        ti = jnp.where(slot == kk, idx, ti)
    ti_ref[...] = ti
    exps = [jnp.exp(v - vals[0]) for v in vals]
    den = exps[0] + exps[1] + exps[2] + exps[3]
    tw_t = jnp.zeros((SLOT_ROWS, RT), F32)
    for kk in range(TOP_K):
        tw_t = jnp.where(slot == kk, exps[kk] / den, tw_t)
    eye = (lax.broadcasted_iota(I32, (SLOT_ROWS, LANES), 0)
           == lax.broadcasted_iota(I32, (SLOT_ROWS, LANES), 1)).astype(F32).astype(BF16)
    w_hi = tw_t.astype(BF16)
    w_mid, w_lo = _split_bf16(tw_t - w_hi.astype(F32))
    tw_col = _dot_tn(w_hi, eye) + _dot_tn(w_mid, eye) + _dot_tn(w_lo, eye)

    onehot_t = jnp.zeros(logits_t.shape, F32)
    for hot in hots:
        onehot_t = onehot_t + jnp.where(hot, 1.0, 0.0)
    before_t = _dot(onehot_t.astype(BF16), tri_ref[...]) + base_ref[:, 0:1]
    rk = jnp.zeros((SLOT_ROWS, RT), I32)
    for kk in range(TOP_K):
        r = jnp.sum(jnp.where(hots[kk], before_t, 0.0), axis=0, keepdims=True).astype(I32)
        rk = jnp.where(slot == kk, r, rk)
    rk_ref[...] = rk
    total = base_ref[...] + jnp.sum(onehot_t, axis=1, keepdims=True)
    base_ref[...] = total
    cnt_ref[...] = total.astype(I32)

    w_lanes = [jnp.broadcast_to(tw_col[:, kk:kk + 1], (RT, LANES)) for kk in range(TOP_K)]
    w_rows = jnp.concatenate(w_lanes + [jnp.zeros((RT, D_MODEL - TOP_K * LANES), F32)], axis=1)
    _store_tile_rows(i, N_ROUTER_STEPS, ((wrow_ref, w_rows),), row_buf, row_sem, RT)


def _router(logits, tri):
    return pl.pallas_call(
        _router_kernel,
        grid=(N_ROUTER_STEPS,),
        in_specs=[pl.BlockSpec((N_EXPERTS, RT), lambda i: (0, i)), pl.BlockSpec(tri.shape, lambda i: (0, 0))],
        out_specs=[pl.BlockSpec(memory_space=pl.ANY),
                   pl.BlockSpec((SLOT_ROWS, RT), lambda i: (0, i)),
                   pl.BlockSpec((SLOT_ROWS, RT), lambda i: (0, i)),
                   pl.BlockSpec((N_EXPERTS, LANES), lambda i: (0, 0))],
        out_shape=[jax.ShapeDtypeStruct((N_ROWS,) + ROW_TILE, F32),
                   jax.ShapeDtypeStruct((SLOT_ROWS, N_ROWS), I32),
                   jax.ShapeDtypeStruct((SLOT_ROWS, N_ROWS), I32),
                   jax.ShapeDtypeStruct((N_EXPERTS, LANES), I32)],
        scratch_shapes=[pltpu.VMEM((N_EXPERTS, LANES), F32), pltpu.VMEM((1, 2, RT, D_MODEL), F32),
                        pltpu.SemaphoreType.DMA((1, 2))],
        compiler_params=_cparams(1, "router"),
        name="router",
    )(logits, tri)


ISSUE_UNROLL = 4


def _row_copy(src_ref, src_row, dst_ref, dst_row, sem):
    return pltpu.make_async_copy(src_ref.at[pl.ds(src_row, 1)], dst_ref.at[pl.ds(dst_row, 1)], sem)


def _dispatch_kernel(dest_ref, end_ref, x_ref, xs_ref, zero_ref, sem, zsem):
    i = pl.program_id(0)
    base = i * TM

    @pl.when(i == 0)
    def _():
        zero_ref[...] = jnp.zeros_like(zero_ref)

        def tail_copy(e):
            last = jnp.maximum(end_ref[e] - TM, 0)
            return pltpu.make_async_copy(zero_ref, xs_ref.at[pl.ds(pl.multiple_of(last, TM), TM)], zsem)

        def fill(e, carry):
            tail_copy(e).start()
            return carry

        def fill_wait(e, carry):
            tail_copy(e).wait()
            return carry

        lax.fori_loop(0, N_EXPERTS, fill, 0)
        lax.fori_loop(0, N_EXPERTS, fill_wait, 0)

        def unused_copy(t):
            return pltpu.make_async_copy(zero_ref, xs_ref.at[pl.ds(pl.multiple_of(t * TM, TM), TM)], zsem)

        def fill_unused(t, carry):
            unused_copy(t).start()
            unused_copy(t).wait()
            return carry

        lax.fori_loop(end_ref[N_EXPERTS - 1] // TM, N_EXPERT_TILES, fill_unused, 0)

    def issue(n, carry):
        for kk in range(TOP_K):
            _row_copy(x_ref, n, xs_ref, dest_ref[kk * N_ROWS + base + n], sem).start(priority=kk % 2)
        return carry

    lax.fori_loop(0, TM, issue, 0, unroll=ISSUE_UNROLL)

    for kk in range(TOP_K):
        pltpu.make_async_copy(x_ref, xs_ref.at[pl.ds(0, TM)], sem).wait()


def _dispatch(dest, end, x_packed):
    return pl.pallas_call(
        _dispatch_kernel,
        grid_spec=pltpu.PrefetchScalarGridSpec(
            num_scalar_prefetch=2,
            grid=(N_TILES,),
            in_specs=[pl.BlockSpec((TM,) + ROW_TILE, lambda i, d, e: (i, 0, 0))],
            out_specs=pl.BlockSpec(memory_space=pl.ANY),
            scratch_shapes=[pltpu.VMEM((TM,) + ROW_TILE, F32), pltpu.SemaphoreType.DMA,
                            pltpu.SemaphoreType.DMA],
        ),
        out_shape=jax.ShapeDtypeStruct((N_SORTED_ROWS,) + ROW_TILE, F32),
        compiler_params=_cparams(1, "dispatch"),
        name="dispatch",
    )(dest, end, x_packed)


CAST_ROWS = 128


def _moe_kernel(te_ref, nu_ref, nx_ref, xs_ref, wg_ref, bg_ref, wu_ref, bu_ref, wd_ref, bd_ref, ys_ref,
                w_stage, w_bf, x_buf, y_buf, zero_buf, w_sem, in_sem, out_sem, zero_sem):
    t = pl.program_id(0)
    n_used = nu_ref[0]
    slot = lax.rem(t, 2)
    e = te_ref[t]
    e_prev = te_ref[jnp.maximum(t - 1, 0)]

    def load(tile, s):
        return _tile_row_copies(xs_ref, tile, x_buf.at[s], in_sem.at[s], to_hbm=False)

    def store(tile, s):
        return _tile_row_copies(ys_ref, tile, y_buf.at[s], out_sem.at[s], to_hbm=True)

    def weight_copies(expert):
        return [pltpu.make_async_copy(w.at[expert], w_stage.at[j], w_sem.at[j])
                for j, w in enumerate((wg_ref, wu_ref, wd_ref))]

    @pl.when(t == 0)
    def _():
        for c in weight_copies(e) + load(0, 0):
            c.start()

    @pl.when(t + 1 < n_used)
    def _():
        for c in load(t + 1, 1 - slot):
            c.start()

    @pl.when((t == 0) | (e != e_prev))
    def _():
        for c in weight_copies(e):
            c.wait()

        def cast(r, carry):
            sl = pl.ds(pl.multiple_of(r * CAST_ROWS, CAST_ROWS), CAST_ROWS)
            for j in range(3):
                w_bf[j, sl, :] = w_stage[j, sl, :].astype(BF16)
            return carry

        lax.fori_loop(0, D_MODEL // CAST_ROWS, cast, 0)
        e_next = nx_ref[e]

        @pl.when(e_next >= 0)
        def _():
            for c in weight_copies(e_next):
                c.start()

    @pl.when(t < n_used)
    def _():
        for c in load(t, slot):
            c.wait()

        @pl.when(t >= 2)
        def _():
            for c in store(t - 2, slot):
                c.wait()

        x = x_buf[slot].astype(BF16)
        gate = jnp.minimum(_dot(x, w_bf[0]) + bg_ref[...], SWIGLU_LIMIT)
        up = jnp.clip(_dot(x, w_bf[1]) + bu_ref[...], -SWIGLU_LIMIT, SWIGLU_LIMIT)
        hdn = (up + 1.0) * gate * (1.0 / (1.0 + jnp.exp(-SWIGLU_ALPHA * gate)))
        y_buf[slot] = _dot(hdn.astype(BF16), w_bf[2]) + bd_ref[...]
        for c in store(t, slot):
            c.start()

    @pl.when(t >= n_used)
    def _():
        zero_buf[...] = jnp.zeros_like(zero_buf)
        fill = pltpu.make_async_copy(zero_buf, ys_ref.at[pl.ds(t * TM, TM)], zero_sem)
        fill.start()
        fill.wait()

    @pl.when(t == N_EXPERT_TILES - 1)
    def _():
        @pl.when(n_used >= 2)
        def _():
            for c in store(n_used - 2, lax.rem(n_used, 2)):
                c.wait()

        for c in store(n_used - 1, lax.rem(n_used - 1, 2)):
            c.wait()


def _moe(tile_expert, n_used, next_expert, xs, w_gate, b_gate, w_up, b_up, w_down, b_down):
    hbm = pl.BlockSpec(memory_space=pl.ANY)
    bias = pl.BlockSpec((None, 1, D_MODEL), lambda t, te, nu, nx: (te[t], 0, 0))
    return pl.pallas_call(
        _moe_kernel,
        grid_spec=pltpu.PrefetchScalarGridSpec(
            num_scalar_prefetch=3,
            grid=(N_EXPERT_TILES,),
            in_specs=[hbm, hbm, bias, hbm, bias, hbm, bias],
            out_specs=hbm,
            scratch_shapes=[pltpu.VMEM((3, D_MODEL, D_MODEL), F32), pltpu.VMEM((3, D_MODEL, D_MODEL), BF16),
                            pltpu.VMEM((2, TM, D_MODEL), F32), pltpu.VMEM((2, TM, D_MODEL), F32),
                            pltpu.VMEM((TM,) + ROW_TILE, F32), pltpu.SemaphoreType.DMA((3,)),
                            pltpu.SemaphoreType.DMA((2,)), pltpu.SemaphoreType.DMA((2,)),
                            pltpu.SemaphoreType.DMA],
        ),
        out_shape=jax.ShapeDtypeStruct((N_SORTED_ROWS,) + ROW_TILE, F32),
        compiler_params=_cparams(1, "experts"),
        name="experts",
    )(tile_expert, n_used, next_expert, xs, w_gate, b_gate, w_up, b_up, w_down, b_down)


def _combine_kernel(dest_ref, h_ref, w_ref, ys_ref, yp_ref, ysm_ref, g_ref, out_ref, sem):
    i = pl.program_id(0)
    slot = lax.rem(i, 2)

    def issue_tile(tile, s):
        base = tile * TM

        def issue(n, carry):
            for kk in range(TOP_K):
                _row_copy(ys_ref, dest_ref[kk * N_ROWS + base + n], g_ref.at[s], kk * TM + n,
                          sem.at[s]).start(priority=kk % 2)
            return carry

        lax.fori_loop(0, TM, issue, 0, unroll=ISSUE_UNROLL)

    @pl.when(i == 0)
    def _():
        issue_tile(0, 0)

    @pl.when(i + 1 < N_TILES)
    def _():
        issue_tile(i + 1, 1 - slot)

    for kk in range(TOP_K):
        pltpu.make_async_copy(ys_ref.at[pl.ds(0, TM)], g_ref.at[slot, pl.ds(kk * TM, TM)], sem.at[slot]).wait()

    w = w_ref[...]
    acc = None
    for kk in range(TOP_K):
        part = jnp.broadcast_to(w[:, kk:kk + 1, :], (TM,) + ROW_TILE) * g_ref[slot, kk * TM:(kk + 1) * TM]
        acc = part if acc is None else acc + part
    out_ref[...] = h_ref[...] + acc

    def write(y_ref):
        for a in range(ROW_SUB):
            y_ref[:, a * LANES:(a + 1) * LANES] = out_ref[:, a, :]

    @pl.when(i < N_PROMPT_TILES)
    def _():
        write(yp_ref)

    @pl.when(i >= N_PROMPT_TILES)
    def _():
        write(ysm_ref)


def _combine(dest, h_rows, w_rows, ys):
    tile = lambda index: pl.BlockSpec((TM,) + ROW_TILE, index)
    return pl.pallas_call(
        _combine_kernel,
        grid_spec=pltpu.PrefetchScalarGridSpec(
            num_scalar_prefetch=1,
            grid=(N_TILES,),
            in_specs=[tile(lambda i, d: (i, 0, 0)), tile(lambda i, d: (i, 0, 0)),
                      pl.BlockSpec(memory_space=pl.ANY)],
            out_specs=[pl.BlockSpec((TM, D_MODEL), lambda i, d: (jnp.minimum(i, N_PROMPT_TILES - 1), 0)),
                       pl.BlockSpec((TM, D_MODEL), lambda i, d: (0, 0))],
            scratch_shapes=[pltpu.VMEM((2, TOP_K * TM) + ROW_TILE, F32), pltpu.VMEM((TM,) + ROW_TILE, F32),
                            pltpu.SemaphoreType.DMA((2,))],
        ),
        out_shape=[jax.ShapeDtypeStruct((N_PROMPT, D_MODEL), F32),
                   jax.ShapeDtypeStruct((N_SAMPLE, D_MODEL), F32)],
        compiler_params=_cparams(1, "combine"),
        name="combine",
    )(dest, h_rows, w_rows, ys)


def _block_diag_ones(n, blk):
    idx = np.arange(n) // blk
    return (idx[:, None] == idx[None, :]).astype(np.float32)


def _swa_head_mask(tq):
    row_head = np.arange(4 * tq)[:, None] // tq
    lane_head = np.arange(4 * HD)[None, :] // HD
    return jnp.asarray((row_head == lane_head).astype(np.float32), BF16)


def _gla_masks(t, n_sub):
    nb = t // SUB
    lanes = N_HEADS_GLA * DK
    tri = jnp.asarray(np.kron(np.eye(n_sub, dtype=np.float32), np.tril(np.ones((t, t), np.float32))), BF16)
    row = np.arange(N_HEADS_GLA * t)
    col = np.arange(nb * lanes)
    same_head = (row[:, None] // t) == ((col[None, :] % lanes) // DK)
    same_blk = ((row[:, None] % t) // SUB) == (col[None, :] // lanes)
    mask = jnp.asarray((same_head & same_blk).astype(np.float32), BF16)
    return tri, mask


def _sink_rows(sinks, tq):
    s = jnp.repeat(sinks.astype(F32).reshape(2, 4), tq, axis=1)
    return jnp.broadcast_to(s[:, None, :], (2, 8, 4 * tq))


def kernel(x_prompt, x_sample, state_gla, cache_swa_k, cache_swa_v, norm_mix_g, w_in, w_gla_a2, b_gla_a, q_norm_g,
           k_norm_g, swa_sinks, gla_norm_g, w_out, norm_ffn_g, w_router, b_router, w_gate, b_gate, w_up, b_up,
           w_down, b_down):
    xp = x_prompt.reshape(N_PROMPT, D_MODEL)
    xs = x_sample.reshape(N_SAMPLE, D_MODEL)

    w_in0 = w_in[0]
    w_main = w_in0[:, :W_MAIN].astype(BF16)
    w_ga = jnp.pad(w_in0[:, OFF_GA:], ((0, 0), (0, LANES - W_GA))).astype(BF16)
    w_a2 = jnp.pad(w_gla_a2[0], ((0, LANES - W_GA), (0, 0))).astype(BF16)
    b_a = b_gla_a[0].reshape(1, -1)
    qg = (jnp.tile(q_norm_g[0], N_HEADS_SWA) * (HD ** -0.5)).reshape(1, -1)
    kg = jnp.tile(k_norm_g[0], 2).reshape(1, -1)
    bdq = jnp.asarray(_block_diag_ones(W_SQ, HD), BF16)
    bdk = jnp.asarray(_block_diag_ones(W_SK, HD), BF16)

    sq, sk, sv, gq, gk, gv, gg, gr = _proj(xp, xs, norm_mix_g[0].reshape(1, -1), w_main, w_ga, w_a2, b_a,
                                           qg, kg, bdq, bdk)

    cache_k = cache_swa_k[0].reshape(N_STREAMS, WINDOW, 2 * HD)
    cache_v = cache_swa_v[0].reshape(N_STREAMS, WINDOW, 2 * HD)
    o_swa_s = _swa_sample(sq, sk, sv, cache_k, cache_v, _swa_head_mask(T_SAMPLE),
                          _sink_rows(swa_sinks[0], T_SAMPLE))

    bd_state = jnp.asarray(_block_diag_ones(N_HEADS_GLA, 1).repeat(DV, axis=0).repeat(DK, axis=1), F32)
    gn = gla_norm_g[0].reshape(1, -1)
    tri_p, mask_p = _gla_masks(CHUNK, TM // CHUNK)
    tri_s, mask_s = _gla_masks(T_SAMPLE, 1)
    s0_p = jnp.zeros((1, N_HEADS_GLA * DV, N_HEADS_GLA * DK), F32)
    o_swa_p, og_p, sfin_p = _mixers_prompt(sq, sk, sv, _swa_head_mask(CHUNK), _sink_rows(swa_sinks[0], CHUNK),
                                           gq, gk, gv, gg, gr, s0_p, tri_p, mask_p, bd_state, gn)
    eye = jnp.eye(N_HEADS_GLA, dtype=F32)
    s0_s = jnp.einsum('bhde,hg->bhegd', state_gla[0].astype(F32), eye).reshape(
        N_STREAMS, N_HEADS_GLA * DV, N_HEADS_GLA * DK)
    og_s, sfin_s = _gla(gq, gk, gv, gg, gr, s0_s, tri_s, mask_s, bd_state, gn, t=T_SAMPLE, n_sub=1,
                        n_batch=N_STREAMS, n_steps=1, first_block=N_PROMPT // T_SAMPLE, name="gla_sample")

    def unpack_state(sfin):
        s = sfin.reshape(-1, N_HEADS_GLA, DV, N_HEADS_GLA, DK)
        s = jnp.stack([s[:, h, :, h, :] for h in range(N_HEADS_GLA)], axis=1)
        return jnp.transpose(s, (0, 1, 3, 2))[None]

    w_out0 = w_out[0].astype(BF16)
    wr = jnp.pad(w_router[0], ((0, 0), (0, LANES - N_EXPERTS)))
    wr_hi = wr.astype(BF16)
    wr_lo = (wr - wr_hi.astype(F32)).astype(BF16)
    br = jnp.pad(b_router[0].astype(F32), (0, LANES - N_EXPERTS)).reshape(1, -1)
    x_rows, h_rows, logits = _merge(o_swa_p, o_swa_s, og_p, og_s, xp, xs, w_out0[:W_SQ], w_out0[W_SQ:],
                                    norm_ffn_g[0].reshape(1, -1), wr_hi, wr_lo, br)
    earlier = jnp.asarray(np.triu(np.ones((RT, RT), np.float32), 1), BF16)
    w_rows, top_i, rank, counts = _router(logits, earlier)

    counts = counts[:, 0]
    padded = (counts + TM - 1) // TM * TM
    end = jnp.cumsum(padded)
    start = end - padded
    experts = jnp.arange(N_EXPERTS, dtype=I32)
    is_e = top_i[:TOP_K, :, None] == experts
    dest = (rank[:TOP_K] + jnp.sum(jnp.where(is_e, start, 0), axis=-1)).reshape(-1).astype(I32)
    n_used = (end[-1] // TM).astype(I32)
    tiles = jnp.minimum(jnp.arange(N_EXPERT_TILES, dtype=I32), n_used - 1)
    tile_expert = jnp.sum((tiles[:, None] * TM >= end[None, :]).astype(I32), axis=1)
    later_nonempty = (experts[None, :] > experts[:, None]) & (padded[None, :] > 0)
    next_expert = jnp.min(jnp.where(later_nonempty, experts[None, :], N_EXPERTS), axis=1)
    next_expert = jnp.where(next_expert < N_EXPERTS, next_expert, -1).astype(I32)

    xs_sorted = _dispatch(dest, end.astype(I32), x_rows)
    ys = _moe(tile_expert, n_used.reshape(1), next_expert, xs_sorted, w_gate[0], b_gate[0].reshape(N_EXPERTS, 1, -1),
              w_up[0], b_up[0].reshape(N_EXPERTS, 1, -1), w_down[0], b_down[0].reshape(N_EXPERTS, 1, -1))
    y_p, y_s = _combine(dest, h_rows, w_rows, ys)

    sk_s = sk[N_PROMPT:].reshape(N_STREAMS, T_SAMPLE, 2 * HD)
    sv_s = sv[N_PROMPT:].reshape(N_STREAMS, T_SAMPLE, 2 * HD)
    kc_s = jnp.concatenate([cache_k[:, T_SAMPLE:], sk_s], axis=1).reshape(1, N_STREAMS, WINDOW, 2, HD)
    vc_s = jnp.concatenate([cache_v[:, T_SAMPLE:], sv_s], axis=1).reshape(1, N_STREAMS, WINDOW, 2, HD)
    kc_p = sk[N_PROMPT - WINDOW:N_PROMPT].reshape(1, 1, WINDOW, 2, HD)
    vc_p = sv[N_PROMPT - WINDOW:N_PROMPT].reshape(1, 1, WINDOW, 2, HD)
    return (y_p.reshape(1, N_PROMPT, D_MODEL), y_s.reshape(N_STREAMS, T_SAMPLE, D_MODEL),
            unpack_state(sfin_p), kc_p, vc_p, unpack_state(sfin_s), kc_s, vc_s)
```

```python
import functools

import numpy as np
import jax
import jax.numpy as jnp
from jax import lax
from jax.experimental import pallas as pl
from jax.experimental.pallas import tpu as pltpu

F32 = jnp.float32
BF16 = jnp.bfloat16
I32 = jnp.int32
U32 = jnp.uint32

D_MODEL = 1024
N_PROMPT = 16384
N_STREAMS = 8
T_SAMPLE = 32
N_SAMPLE = N_STREAMS * T_SAMPLE
N_ROWS = N_PROMPT + N_SAMPLE
EPS = 1e-6

CHUNK = 64
SUB = 16
N_HEADS_SWA = 8
HD = 64
WINDOW = 128
N_HEADS_GLA = 4
DK = 64
DV = 128
GLA_TAU = 16.0
N_EXPERTS = 32
TOP_K = 4
SWIGLU_ALPHA = 1.702
SWIGLU_LIMIT = 7.0

TM = 256
N_TILES = N_ROWS // TM
N_PROMPT_TILES = N_PROMPT // TM
N_ASSIGN = N_ROWS * TOP_K
N_EXPERT_TILES = N_ASSIGN // TM + N_EXPERTS
N_SORTED_ROWS = N_EXPERT_TILES * TM
LANES = 128
SLOT_ROWS = 16
NEG_BIG = -1e30

W_SQ, W_SK, W_SV, W_GQ, W_GK, W_GV, W_GR, W_GA = 512, 128, 128, 256, 256, 512, 512, 16
OFF_SQ = 0
OFF_SK = OFF_SQ + W_SQ
OFF_SV = OFF_SK + W_SK
OFF_GQ = OFF_SV + W_SV
OFF_GK = OFF_GQ + W_GQ
OFF_GV = OFF_GK + W_GK
OFF_GR = OFF_GV + W_GV
OFF_GA = OFF_GR + W_GR
W_MAIN = OFF_GA


VMEM_MIB = dict(proj=40, front=48, mixer=32, merge=32, router=48, dispatch=32, experts=52, combine=40)


def _cparams(n_grid_axes, call):
    return pltpu.CompilerParams(dimension_semantics=("arbitrary",) * n_grid_axes,
                                vmem_limit_bytes=VMEM_MIB[call] * 2 ** 20)


def _dot(a, b):
    return jnp.dot(a, b, preferred_element_type=F32)


def _dot_nt(a, b):
    return lax.dot_general(a, b, (((1,), (1,)), ((), ())), preferred_element_type=F32)


def _dot_tn(a, b):
    return lax.dot_general(a, b, (((0,), (0,)), ((), ())), preferred_element_type=F32)


def _split_bf16(x):
    hi = x.astype(BF16)
    lo = (x - hi.astype(F32)).astype(BF16)
    return hi, lo


def _rms(x):
    return x * lax.rsqrt(jnp.mean(x * x, axis=-1, keepdims=True) + EPS)


ROW_SUB = 8
ROW_TILE = (ROW_SUB, LANES)


PROJ_OUTPUTS = ((W_SQ, BF16), (W_SK, F32), (W_SV, F32), (W_GQ, BF16), (W_GK, BF16), (W_GV, BF16),
                (W_GQ, F32), (W_GR, BF16))
N_PROJ_CONSTS = 9


def _proj_tile(x, g_ref, w_ref, wga_ref, wa2_ref, ba_ref, qg_ref, kg_ref, bdq_ref, bdk_ref,
               sq_ref, sk_ref, sv_ref, gq_ref, gk_ref, gv_ref, gg_ref, gr_ref):
    xb = (_rms(x) * g_ref[...]).astype(BF16)

    def seg(off, width):
        return _dot(xb, w_ref[:, off:off + width])

    def head_norm(u, bd_ref):
        hi, lo = _split_bf16(u * u)
        ss = _dot(hi, bd_ref[...]) + _dot(lo, bd_ref[...])
        return u * lax.rsqrt(ss * (1.0 / HD) + EPS)

    sq_ref[...] = (head_norm(seg(OFF_SQ, W_SQ), bdq_ref) * qg_ref[...]).astype(BF16)
    sk_ref[...] = head_norm(seg(OFF_SK, W_SK), bdk_ref) * kg_ref[...]
    sv_ref[...] = seg(OFF_SV, W_SV)
    gq_ref[...] = (seg(OFF_GQ, W_GQ) * (DK ** -0.5)).astype(BF16)
    gk_ref[...] = seg(OFF_GK, W_GK).astype(BF16)
    gv_ref[...] = seg(OFF_GV, W_GV).astype(BF16)
    gr_ref[...] = seg(OFF_GR, W_GR).astype(BF16)
    ga = _dot(xb, wga_ref[...]).astype(BF16)
    z = _dot(ga, wa2_ref[...]) + ba_ref[...]
    log_sig = jnp.minimum(z, 0.0) - jnp.log(1.0 + jnp.exp(-jnp.abs(z)))
    gg_ref[...] = log_sig * (1.0 / GLA_TAU)


def _proj_sample_kernel(x_ref, *refs):
    _proj_tile(x_ref[...], *refs)


def _proj_sample(xs, consts):
    def full(a):
        return pl.BlockSpec(a.shape, lambda i: (0,) * a.ndim)

    return pl.pallas_call(
        _proj_sample_kernel,
        grid=(1,),
        in_specs=[full(xs)] + [full(a) for a in consts],
        out_specs=[pl.BlockSpec((N_SAMPLE, w), lambda i: (0, 0)) for w, _ in PROJ_OUTPUTS],
        out_shape=[jax.ShapeDtypeStruct((N_SAMPLE, w), dt) for w, dt in PROJ_OUTPUTS],
        compiler_params=_cparams(1, "proj"),
        name="proj_sample",
    )(xs, *consts)


def _dup_kv_heads(x):
    r = pltpu.roll(x, HD, axis=1)
    lo = lax.broadcasted_iota(I32, x.shape, 1) < HD
    out = []
    for a in (jnp.where(lo, x, r), jnp.where(lo, r, x)):
        out.append(jnp.concatenate([a, a], axis=1).astype(BF16))
    return out


def _swa_blocks(q_blocks, k_blocks, v_blocks, sinks, valids, head_mask):
    tq = q_blocks[0].shape[0]
    n_q = 4 * tq
    scores = []
    for q, k, valid in zip(q_blocks, k_blocks, valids):
        s_t = _dot_nt(k, jnp.concatenate([q] * 4, axis=0) * head_mask)
        scores.append(s_t if valid is None else jnp.where(valid, s_t, -jnp.inf))
    eye = (lax.broadcasted_iota(I32, (n_q, n_q), 0) == lax.broadcasted_iota(I32, (n_q, n_q), 1)
           ).astype(F32).astype(BF16)
    probs = []
    for s_t, sink in zip(scores, sinks):
        m = jnp.maximum(jnp.max(s_t, axis=0, keepdims=True), sink)
        p_t = jnp.exp(s_t - m)
        den = jnp.sum(p_t, axis=0, keepdims=True) + jnp.exp(sink - m)
        probs.append(_dot_nt(eye, (p_t / den).astype(BF16)).astype(BF16))
    lane_head = lax.broadcasted_iota(I32, (tq, 4 * HD), 1) // HD
    outs = []
    for p, v in zip(probs, v_blocks):
        o_full = _dot(p, v)
        o = jnp.zeros((tq, 4 * HD), F32)
        for a in range(4):
            o = o + jnp.where(lane_head == a, o_full[a * tq:(a + 1) * tq], 0.0)
        outs.append(o)
    return outs


def _swa_prompt_tile(i, q_ref, kp_ref, kc_ref, vp_ref, vc_ref, hm_ref, sink_ref, o_ref):
    k_dup = _dup_kv_heads(jnp.concatenate([kp_ref[...], kc_ref[...]], axis=0))
    v_dup = _dup_kv_heads(jnp.concatenate([vp_ref[...], vc_ref[...]], axis=0))
    sink_row = [sink_ref[j][0:1, :] for j in range(2)]
    span = WINDOW + CHUNK
    key = lax.broadcasted_iota(I32, (span, 4 * CHUNK), 0)
    qs, ks, vs, sinks, valids, where = [], [], [], [], [], []
    for c in range(TM // CHUNK):
        lo = CHUNK * c
        valid = (i * TM - WINDOW + lo + key) >= 0
        for j in range(2):
            qs.append(q_ref[lo:lo + CHUNK, 4 * HD * j:4 * HD * (j + 1)])
            ks.append(k_dup[j][lo:lo + span])
            vs.append(v_dup[j][lo:lo + span])
            sinks.append(sink_row[j])
            valids.append(valid)
            where.append((lo, j))
    outs = _swa_blocks(qs, ks, vs, sinks, valids, hm_ref[...])
    for (lo, j), o in zip(where, outs):
        o_ref[lo:lo + CHUNK, 4 * HD * j:4 * HD * (j + 1)] = o.astype(BF16)


def _swa_sample_kernel(q_ref, kc_ref, kn_ref, vc_ref, vn_ref, hm_ref, sink_ref, o_ref):
    k_dup = _dup_kv_heads(jnp.concatenate([kc_ref[...], kn_ref[...]], axis=0))
    v_dup = _dup_kv_heads(jnp.concatenate([vc_ref[...], vn_ref[...]], axis=0))
    sink_row = [sink_ref[j][0:1, :] for j in range(2)]
    qs = [q_ref[:, 4 * HD * j:4 * HD * (j + 1)] for j in range(2)]
    outs = _swa_blocks(qs, k_dup, v_dup, sink_row, [None, None], hm_ref[...])
    o_ref[...] = jnp.concatenate(outs, axis=1).astype(BF16)


def _swa_sample(sq, sk, sv, cache_k, cache_v, head_mask, sink_b):
    new = lambda width: pl.BlockSpec((T_SAMPLE, width), lambda b: (b, 0))
    cache = pl.BlockSpec((None, WINDOW, 2 * HD), lambda b: (b, 0, 0))
    return pl.pallas_call(
        _swa_sample_kernel,
        grid=(N_STREAMS,),
        in_specs=[new(W_SQ), cache, new(2 * HD), cache, new(2 * HD),
                  pl.BlockSpec(head_mask.shape, lambda b: (0, 0)),
                  pl.BlockSpec(sink_b.shape, lambda b: (0, 0, 0))],
        out_specs=pl.BlockSpec((T_SAMPLE, W_SQ), lambda b: (b, 0)),
        out_shape=jax.ShapeDtypeStruct((N_SAMPLE, W_SQ), BF16),
        compiler_params=_cparams(1, "mixer"),
        name="swa_sample",
    )(sq, cache_k, sk, cache_v, sv, head_mask, sink_b)


GLA_SAFE_EXP = 80.0
GLA_SLOW_ROWS = 16


def _gla_step(c, q_ref, k_ref, v_ref, g_ref, gr_ref, s0_ref, tri_ref, m_ref, bd_ref, gn_ref,
              og_ref, sfin_ref, st_ref, o_ref, *, t, n_sub, companion=None):
    lanes = N_HEADS_GLA * DK

    @pl.when(c == 0)
    def _():
        st_ref[...] = s0_ref[...]

    n_rows = t * n_sub

    def group_row(x, period, offset):
        g = x.reshape(n_rows // period, period, lanes)[:, offset:offset + 1, :]
        return jnp.broadcast_to(g, (n_rows // period, period, lanes)).reshape(n_rows, lanes)

    g_hi, g_lo = _split_bf16(g_ref[...])
    b = _dot(tri_ref[...], g_hi) + _dot(tri_ref[...], g_lo)
    since_block_start = group_row(b, SUB, 0) - b
    blocked_is_safe = jnp.max(since_block_start) <= GLA_SAFE_EXP

    def emit(o):
        gr = gr_ref[...].astype(F32)
        gate = gr / (1.0 + jnp.exp(-gr))
        outs = []
        for h in range(N_HEADS_GLA):
            sl = slice(h * DV, (h + 1) * DV)
            outs.append(_rms(o[:, sl]) * gn_ref[...] * gate[:, sl])
        og_ref[...] = jnp.concatenate(outs, axis=1).astype(BF16)

    @pl.when(blocked_is_safe)
    def _():
        if companion is not None:
            companion()
        emit(_gla_blocked(q_ref, k_ref, v_ref, m_ref, bd_ref, st_ref, b, since_block_start, group_row,
                          t=t, n_sub=n_sub))

    @pl.when(jnp.logical_not(blocked_is_safe))
    def _():
        if companion is not None:
            companion()
        _gla_tokenwise(q_ref, k_ref, v_ref, g_ref, bd_ref, st_ref, o_ref, n_rows=n_rows)
        emit(o_ref[...])

    @pl.when(c == pl.num_programs(1) - 1)
    def _():
        sfin_ref[...] = st_ref[...]


def _gla_kernel(*refs, t, n_sub):
    _gla_step(pl.program_id(1), *refs, t=t, n_sub=n_sub)


def _front_prompt_kernel(*refs, t, n_sub):
    x_ref = refs[0]
    consts = refs[1:1 + N_PROJ_CONSTS]
    hm_ref, sink_ref, s0_ref, tri_ref, m_ref, bd_ref, gn_ref = refs[1 + N_PROJ_CONSTS:8 + N_PROJ_CONSTS]
    sk_out, sv_out, o_swa_ref, og_ref, sfin_ref = refs[8 + N_PROJ_CONSTS:13 + N_PROJ_CONSTS]
    sq_s, gq_s, gk_s, gv_s, gg_s, gr_s, sk_s, sv_s, st_ref, o_ref = refs[13 + N_PROJ_CONSTS:]
    c = pl.program_id(1)

    @pl.when(c == 0)
    def _():
        for ring in (sq_s, gq_s, gk_s, gv_s, gg_s, gr_s, sk_s, sv_s):
            ring[...] = jnp.zeros_like(ring)

    new2, old2 = lax.rem(c, 2), lax.rem(c + 1, 2)
    new3, old3, older3 = lax.rem(c, 3), lax.rem(c + 2, 3), lax.rem(c + 1, 3)
    half = pl.ds(TM // 2, TM // 2)

    def companion():
        _proj_tile(x_ref[...], *consts, sq_s.at[new2], sk_s.at[new3], sv_s.at[new3], gq_s.at[new2],
                   gk_s.at[new2], gv_s.at[new2], gg_s.at[new2], gr_s.at[new2])
        sk_out[...] = sk_s[new3]
        sv_out[...] = sv_s[new3]
        _swa_prompt_tile(c - 1, sq_s.at[old2], sk_s.at[older3, half], sk_s.at[old3], sv_s.at[older3, half],
                         sv_s.at[old3], hm_ref, sink_ref, o_swa_ref)

    _gla_step(c, gq_s.at[old2], gk_s.at[old2], gv_s.at[old2], gg_s.at[old2], gr_s.at[old2], s0_ref, tri_ref,
              m_ref, bd_ref, gn_ref, og_ref, sfin_ref, st_ref, o_ref, t=t, n_sub=n_sub, companion=companion)


def _gla_tokenwise(q_ref, k_ref, v_ref, g_ref, bd_ref, st_ref, o_ref, *, n_rows):
    row = lax.broadcasted_iota(I32, (GLA_SLOW_ROWS, 1), 0)

    def group(gi, carry):
        rows = pl.ds(pl.multiple_of(gi * GLA_SLOW_ROWS, GLA_SLOW_ROWS), GLA_SLOW_ROWS)
        q = q_ref[rows, :].astype(F32)
        k = k_ref[rows, :].astype(F32)
        v = v_ref[rows, :].astype(F32)
        decay = jnp.exp(g_ref[rows, :])
        o = jnp.zeros((GLA_SLOW_ROWS, N_HEADS_GLA * DV), F32)
        for j in range(GLA_SLOW_ROWS):
            only_j = row == j
            k_j = jnp.where(only_j, k, 0.0).astype(BF16)
            v_j = jnp.where(only_j, v, 0.0).astype(BF16)
            q_j = jnp.where(only_j, q, 0.0).astype(BF16)
            st = st_ref[...] * decay[j:j + 1, :] + _dot_tn(v_j, k_j) * bd_ref[...]
            st_ref[...] = st
            o = o + _dot_nt(q_j, st.astype(BF16))
        o_ref[rows, :] = o
        return carry

    lax.fori_loop(0, n_rows // GLA_SLOW_ROWS, group, 0)


def _gla_blocked(q_ref, k_ref, v_ref, m_ref, bd_ref, st_ref, b, since_block_start, group_row, *, t, n_sub):
    nb = t // SUB
    n_rows = t * n_sub
    lanes = N_HEADS_GLA * DK
    q = q_ref[...].astype(F32)
    k = k_ref[...].astype(F32)
    qd = (q * jnp.exp(-since_block_start)).astype(BF16)
    pos = lax.broadcasted_iota(I32, (n_rows, lanes), 0) & (t - 1)
    k_parts = []
    for blk in range(nb):
        arg = jnp.where(pos < SUB * (blk + 1), group_row(b, t, SUB * blk) - b, NEG_BIG)
        k_parts.append((k * jnp.exp(arg)).astype(BF16))
    k_cat = jnp.concatenate(k_parts, axis=1)
    qd_cat = jnp.concatenate([qd] * nb, axis=1)
    q_dec = (q * jnp.exp(b)).astype(BF16)
    k_last = (k * jnp.exp(group_row(b, t, t - 1) - b)).astype(BF16)
    row_a = lax.broadcasted_iota(I32, (N_HEADS_GLA * t, t), 0) & (t - 1)
    col_a = lax.broadcasted_iota(I32, (N_HEADS_GLA * t, t), 1)

    o_intra, q_decayed, state_add, state_decay = [], [], [], []
    for u in range(n_sub):
        rows = slice(u * t, (u + 1) * t)
        v = v_ref[rows, :]
        lhs = jnp.concatenate([qd_cat[rows]] * N_HEADS_GLA, axis=0) * m_ref[...]
        a = _dot_nt(lhs, k_cat[rows])
        a = jnp.where(row_a >= col_a, a, 0.0).astype(BF16)
        o_full = _dot(a, v)
        o_intra.append(jnp.concatenate(
            [o_full[h * t:(h + 1) * t, h * DV:(h + 1) * DV] for h in range(N_HEADS_GLA)], axis=1))
        q_decayed.append(q_dec[rows])
        state_add.append(_dot_tn(v, k_last[rows]) * bd_ref[...])
        state_decay.append(jnp.exp(b[(u + 1) * t - 1:(u + 1) * t, :]))

    st = st_ref[...]
    o_parts = []
    for u in range(n_sub):
        o_parts.append(o_intra[u] + _dot_nt(q_decayed[u], st.astype(BF16)))
        st = st * state_decay[u] + state_add[u]
    st_ref[...] = st
    return jnp.concatenate(o_parts, axis=0) if n_sub > 1 else o_parts[0]


def _gla(gq, gk, gv, gg, gr, s0, tri, mask, bd, gn, *, t, n_sub, n_batch, n_steps, first_block, name):
    rows_per_step = t * n_sub

    def rows(width):
        return pl.BlockSpec((rows_per_step, width), lambda b, c: (first_block + b * n_steps + c, 0))

    def full(a):
        return pl.BlockSpec(a.shape, lambda b, c: (0,) * a.ndim)

    state = pl.BlockSpec((None,) + s0.shape[1:], lambda b, c: (b, 0, 0))
    return pl.pallas_call(
        functools.partial(_gla_kernel, t=t, n_sub=n_sub),
        grid=(n_batch, n_steps),
        in_specs=[rows(W_GQ), rows(W_GK), rows(W_GV), rows(W_GQ), rows(W_GR), state,
                  full(tri), full(mask), full(bd), full(gn)],
        out_specs=[pl.BlockSpec((rows_per_step, W_GV), lambda b, c: (b * n_steps + c, 0)), state],
        out_shape=[jax.ShapeDtypeStruct((n_batch * n_steps * rows_per_step, W_GV), BF16),
                   jax.ShapeDtypeStruct(s0.shape, F32)],
        scratch_shapes=[pltpu.VMEM(s0.shape[1:], F32), pltpu.VMEM((rows_per_step, W_GV), F32)],
        compiler_params=_cparams(2, "mixer"),
        name=name,
    )(gq, gk, gv, gg, gr, s0, tri, mask, bd, gn)


def _front_prompt(xp, consts, head_mask, sink_b, s0, tri, mask, bd, gn):
    def full(a):
        return pl.BlockSpec(a.shape, lambda b, c: (0,) * a.ndim)

    def computed(width):
        return pl.BlockSpec((TM, width), lambda b, c: (jnp.minimum(c, N_PROMPT_TILES - 1), 0))

    def mixed(width):
        return pl.BlockSpec((TM, width), lambda b, c: (jnp.maximum(c - 1, 0), 0))

    state = pl.BlockSpec((None,) + s0.shape[1:], lambda b, c: (b, 0, 0))
    ring2 = [pltpu.VMEM((2, TM, w), dt) for w, dt in
             (PROJ_OUTPUTS[0], PROJ_OUTPUTS[3], PROJ_OUTPUTS[4], PROJ_OUTPUTS[5], PROJ_OUTPUTS[6], PROJ_OUTPUTS[7])]
    ring3 = [pltpu.VMEM((3, TM, w), dt) for w, dt in (PROJ_OUTPUTS[1], PROJ_OUTPUTS[2])]
    return pl.pallas_call(
        functools.partial(_front_prompt_kernel, t=CHUNK, n_sub=TM // CHUNK),
        grid=(1, N_PROMPT_TILES + 1),
        in_specs=[computed(D_MODEL)] + [full(a) for a in consts] + [full(head_mask), full(sink_b), state,
                                                                   full(tri), full(mask), full(bd), full(gn)],
        out_specs=[computed(W_SK), computed(W_SV), mixed(W_SQ), mixed(W_GV), state],
        out_shape=[jax.ShapeDtypeStruct((N_PROMPT, W_SK), F32), jax.ShapeDtypeStruct((N_PROMPT, W_SV), F32),
                   jax.ShapeDtypeStruct((N_PROMPT, W_SQ), BF16), jax.ShapeDtypeStruct((N_PROMPT, W_GV), BF16),
                   jax.ShapeDtypeStruct(s0.shape, F32)],
        scratch_shapes=ring2 + ring3 + [pltpu.VMEM(s0.shape[1:], F32), pltpu.VMEM((TM, W_GV), F32)],
        compiler_params=_cparams(2, "front"),
        name="front_prompt",
    )(xp, *consts, head_mask, sink_b, s0, tri, mask, bd, gn)


def _tile_row_copies(hbm_ref, tile, vmem_ref, sem, to_hbm, rows=TM):
    copies = []
    for a in range(ROW_SUB):
        h = hbm_ref.at[pl.ds(tile * rows, rows), a, :]
        v = vmem_ref.at[:, pl.ds(a * LANES, LANES)]
        copies.append(pltpu.make_async_copy(v, h, sem) if to_hbm else pltpu.make_async_copy(h, v, sem))
    return copies


def _store_tile_rows(i, n_steps, outputs, row_buf, row_sem, rows=TM):
    buf_slot = lax.rem(i, 2)

    def store(j, tile, s):
        return _tile_row_copies(outputs[j][0], tile, row_buf.at[j, s], row_sem.at[j, s], True, rows)

    for j, (_, value) in enumerate(outputs):
        @pl.when(i >= 2)
        def _():
            for c in store(j, i - 2, buf_slot):
                c.wait()

        row_buf[j, buf_slot] = value
        for c in store(j, i, buf_slot):
            c.start()

        @pl.when(i == n_steps - 1)
        def _():
            for c in store(j, i - 1, 1 - buf_slot) + store(j, i, buf_slot):
                c.wait()


def _merge_kernel(oswp_ref, osws_ref, ogp_ref, ogs_ref, xp_ref, xs_ref, wo1_ref, wo2_ref, gf_ref, wrh_ref, wrl_ref,
                  br_ref, xrow_ref, hrow_ref, lg_ref, row_buf, row_sem):
    i = pl.program_id(0)
    is_prompt = i < N_PROMPT_TILES
    x = jnp.where(is_prompt, xp_ref[...], xs_ref[...])
    o_swa = jnp.where(is_prompt, oswp_ref[...], osws_ref[...])
    og = jnp.where(is_prompt, ogp_ref[...], ogs_ref[...])
    h = x + (_dot(o_swa, wo1_ref[...]) + _dot(og, wo2_ref[...]))
    xn = _rms(h) * gf_ref[...]
    x_hi, x_lo = _split_bf16(xn)
    logits = _dot(x_hi, wrh_ref[...]) + _dot(x_lo, wrh_ref[...]) + _dot(x_hi, wrl_ref[...]) + br_ref[...]
    lg_ref[...] = logits.T[:N_EXPERTS]
    _store_tile_rows(i, N_TILES, ((xrow_ref, xn), (hrow_ref, h)), row_buf, row_sem)


def _merge(o_swa_p, o_swa_s, og_p, og_s, xp, xs, wo1, wo2, gf, wrh, wrl, br):
    def prompt_rows(width):
        return pl.BlockSpec((TM, width), lambda i: (jnp.minimum(i, N_PROMPT_TILES - 1), 0))

    def sample_rows(width):
        return pl.BlockSpec((TM, width), lambda i: (0, 0))

    def full(a):
        return pl.BlockSpec(a.shape, lambda i: (0,) * a.ndim)

    consts = (wo1, wo2, gf, wrh, wrl, br)
    return pl.pallas_call(
        _merge_kernel,
        grid=(N_TILES,),
        in_specs=[prompt_rows(W_SQ), sample_rows(W_SQ), prompt_rows(W_GV), sample_rows(W_GV),
                  prompt_rows(D_MODEL), sample_rows(D_MODEL)] + [full(a) for a in consts],
        out_specs=[pl.BlockSpec(memory_space=pl.ANY)] * 2 + [pl.BlockSpec((N_EXPERTS, TM), lambda i: (0, i))],
        out_shape=[jax.ShapeDtypeStruct((N_ROWS,) + ROW_TILE, F32)] * 2 + [
                   jax.ShapeDtypeStruct((N_EXPERTS, N_ROWS), F32)],
        scratch_shapes=[pltpu.VMEM((2, 2, TM, D_MODEL), F32), pltpu.SemaphoreType.DMA((2, 2))],
        compiler_params=_cparams(1, "merge"),
        name="merge",
    )(o_swa_p, o_swa_s, og_p, og_s, xp, xs, *consts)


RT = 1280
N_ROUTER_STEPS = N_ROWS // RT


def _router_kernel(lg_ref, tri_ref, wrow_ref, ti_ref, rk_ref, cnt_ref, base_ref, row_buf, row_sem):
    i = pl.program_id(0)

    @pl.when(i == 0)
    def _():
        base_ref[...] = jnp.zeros_like(base_ref)

    logits_t = lg_ref[...]
    expert = lax.broadcasted_iota(I32, logits_t.shape, 0)
    slot = lax.broadcasted_iota(I32, (SLOT_ROWS, RT), 0)
    vals, hots = [], []
    ti = jnp.zeros((SLOT_ROWS, RT), I32)
    for kk in range(TOP_K):
        m = jnp.max(logits_t, axis=0, keepdims=True)
        idx = jnp.min(jnp.where(logits_t == m, expert, N_EXPERTS), axis=0, keepdims=True)
        hot = expert == idx
        logits_t = jnp.where(hot, NEG_BIG, logits_t)
        vals.append(m)
        hots.append(hot)
        ti = jnp.where(slot == kk, idx, ti)
    ti_ref[...] = ti
    exps = [jnp.exp(v - vals[0]) for v in vals]
    den = exps[0] + exps[1] + exps[2] + exps[3]
    tw_t = jnp.zeros((SLOT_ROWS, RT), F32)
    for kk in range(TOP_K):
        tw_t = jnp.where(slot == kk, exps[kk] / den, tw_t)
    eye = (lax.broadcasted_iota(I32, (SLOT_ROWS, LANES), 0)
           == lax.broadcasted_iota(I32, (SLOT_ROWS, LANES), 1)).astype(F32).astype(BF16)
    w_hi = tw_t.astype(BF16)
    w_mid, w_lo = _split_bf16(tw_t - w_hi.astype(F32))
    tw_col = _dot_tn(w_hi, eye) + _dot_tn(w_mid, eye) + _dot_tn(w_lo, eye)

    onehot_t = jnp.zeros(logits_t.shape, F32)
    for hot in hots:
        onehot_t = onehot_t + jnp.where(hot, 1.0, 0.0)
    before_t = _dot(onehot_t.astype(BF16), tri_ref[...]) + base_ref[:, 0:1]
    rk = jnp.zeros((SLOT_ROWS, RT), I32)
    for kk in range(TOP_K):
        r = jnp.sum(jnp.where(hots[kk], before_t, 0.0), axis=0, keepdims=True).astype(I32)
        rk = jnp.where(slot == kk, r, rk)
    rk_ref[...] = rk
    total = base_ref[...] + jnp.sum(onehot_t, axis=1, keepdims=True)
    base_ref[...] = total
    cnt_ref[...] = total.astype(I32)

    w_lanes = [jnp.broadcast_to(tw_col[:, kk:kk + 1], (RT, LANES)) for kk in range(TOP_K)]
    w_rows = jnp.concatenate(w_lanes + [jnp.zeros((RT, D_MODEL - TOP_K * LANES), F32)], axis=1)
    _store_tile_rows(i, N_ROUTER_STEPS, ((wrow_ref, w_rows),), row_buf, row_sem, RT)


def _router(logits, tri):
    return pl.pallas_call(
        _router_kernel,
        grid=(N_ROUTER_STEPS,),
        in_specs=[pl.BlockSpec((N_EXPERTS, RT), lambda i: (0, i)), pl.BlockSpec(tri.shape, lambda i: (0, 0))],
        out_specs=[pl.BlockSpec(memory_space=pl.ANY),
                   pl.BlockSpec((SLOT_ROWS, RT), lambda i: (0, i)),
                   pl.BlockSpec((SLOT_ROWS, RT), lambda i: (0, i)),
                   pl.BlockSpec((N_EXPERTS, LANES), lambda i: (0, 0))],
        out_shape=[jax.ShapeDtypeStruct((N_ROWS,) + ROW_TILE, F32),
                   jax.ShapeDtypeStruct((SLOT_ROWS, N_ROWS), I32),
                   jax.ShapeDtypeStruct((SLOT_ROWS, N_ROWS), I32),
                   jax.ShapeDtypeStruct((N_EXPERTS, LANES), I32)],
        scratch_shapes=[pltpu.VMEM((N_EXPERTS, LANES), F32), pltpu.VMEM((1, 2, RT, D_MODEL), F32),
                        pltpu.SemaphoreType.DMA((1, 2))],
        compiler_params=_cparams(1, "router"),
        name="router",
    )(logits, tri)


ISSUE_UNROLL = 4


def _row_copy(src_ref, src_row, dst_ref, dst_row, sem):
    return pltpu.make_async_copy(src_ref.at[pl.ds(src_row, 1)], dst_ref.at[pl.ds(dst_row, 1)], sem)


def _dispatch_kernel(dest_ref, end_ref, x_ref, xs_ref, zero_ref, sem, zsem):
    i = pl.program_id(0)
    base = i * TM

    @pl.when(i == 0)
    def _():
        zero_ref[...] = jnp.zeros_like(zero_ref)

        def tail_copy(e):
            last = jnp.maximum(end_ref[e] - TM, 0)
            return pltpu.make_async_copy(zero_ref, xs_ref.at[pl.ds(pl.multiple_of(last, TM), TM)], zsem)

        def fill(e, carry):
            tail_copy(e).start()
            return carry

        def fill_wait(e, carry):
            tail_copy(e).wait()
            return carry

        lax.fori_loop(0, N_EXPERTS, fill, 0)
        lax.fori_loop(0, N_EXPERTS, fill_wait, 0)

        def unused_copy(t):
            return pltpu.make_async_copy(zero_ref, xs_ref.at[pl.ds(pl.multiple_of(t * TM, TM), TM)], zsem)

        def fill_unused(t, carry):
            unused_copy(t).start()
            unused_copy(t).wait()
            return carry

        lax.fori_loop(end_ref[N_EXPERTS - 1] // TM, N_EXPERT_TILES, fill_unused, 0)

    def issue(n, carry):
        for kk in range(TOP_K):
            _row_copy(x_ref, n, xs_ref, dest_ref[kk * N_ROWS + base + n], sem).start(priority=kk % 2)
        return carry

    lax.fori_loop(0, TM, issue, 0, unroll=ISSUE_UNROLL)

    for kk in range(TOP_K):
        pltpu.make_async_copy(x_ref, xs_ref.at[pl.ds(0, TM)], sem).wait()


def _dispatch(dest, end, x_packed):
    return pl.pallas_call(
        _dispatch_kernel,
        grid_spec=pltpu.PrefetchScalarGridSpec(
            num_scalar_prefetch=2,
            grid=(N_TILES,),
            in_specs=[pl.BlockSpec((TM,) + ROW_TILE, lambda i, d, e: (i, 0, 0))],
            out_specs=pl.BlockSpec(memory_space=pl.ANY),
            scratch_shapes=[pltpu.VMEM((TM,) + ROW_TILE, F32), pltpu.SemaphoreType.DMA,
                            pltpu.SemaphoreType.DMA],
        ),
        out_shape=jax.ShapeDtypeStruct((N_SORTED_ROWS,) + ROW_TILE, F32),
        compiler_params=_cparams(1, "dispatch"),
        name="dispatch",
    )(dest, end, x_packed)


CAST_ROWS = 128


def _moe_kernel(te_ref, nu_ref, nx_ref, xs_ref, wg_ref, bg_ref, wu_ref, bu_ref, wd_ref, bd_ref, ys_ref,
                w_stage, w_bf, x_buf, y_buf, zero_buf, w_sem, in_sem, out_sem, zero_sem):
    t = pl.program_id(0)
    n_used = nu_ref[0]
    slot = lax.rem(t, 2)
    e = te_ref[t]
    e_prev = te_ref[jnp.maximum(t - 1, 0)]

    def load(tile, s):
        return _tile_row_copies(xs_ref, tile, x_buf.at[s], in_sem.at[s], to_hbm=False)

    def store(tile, s):
        return _tile_row_copies(ys_ref, tile, y_buf.at[s], out_sem.at[s], to_hbm=True)

    def weight_copies(expert):
        return [pltpu.make_async_copy(w.at[expert], w_stage.at[j], w_sem.at[j])
                for j, w in enumerate((wg_ref, wu_ref, wd_ref))]

    @pl.when(t == 0)
    def _():
        for c in weight_copies(e) + load(0, 0):
            c.start()

    @pl.when(t + 1 < n_used)
    def _():
        for c in load(t + 1, 1 - slot):
            c.start()

    @pl.when((t == 0) | (e != e_prev))
    def _():
        for c in weight_copies(e):
            c.wait()

        def cast(r, carry):
            sl = pl.ds(pl.multiple_of(r * CAST_ROWS, CAST_ROWS), CAST_ROWS)
            for j in range(3):
                w_bf[j, sl, :] = w_stage[j, sl, :].astype(BF16)
            return carry

        lax.fori_loop(0, D_MODEL // CAST_ROWS, cast, 0)
        e_next = nx_ref[e]

        @pl.when(e_next >= 0)
        def _():
            for c in weight_copies(e_next):
                c.start()

    @pl.when(t < n_used)
    def _():
        for c in load(t, slot):
            c.wait()

        @pl.when(t >= 2)
        def _():
            for c in store(t - 2, slot):
                c.wait()

        x = x_buf[slot].astype(BF16)
        gate = jnp.minimum(_dot(x, w_bf[0]) + bg_ref[...], SWIGLU_LIMIT)
        up = jnp.clip(_dot(x, w_bf[1]) + bu_ref[...], -SWIGLU_LIMIT, SWIGLU_LIMIT)
        hdn = (up + 1.0) * gate * (1.0 / (1.0 + jnp.exp(-SWIGLU_ALPHA * gate)))
        y_buf[slot] = _dot(hdn.astype(BF16), w_bf[2]) + bd_ref[...]
        for c in store(t, slot):
            c.start()

    @pl.when(t >= n_used)
    def _():
        zero_buf[...] = jnp.zeros_like(zero_buf)
        fill = pltpu.make_async_copy(zero_buf, ys_ref.at[pl.ds(t * TM, TM)], zero_sem)
        fill.start()
        fill.wait()

    @pl.when(t == N_EXPERT_TILES - 1)
    def _():
        @pl.when(n_used >= 2)
        def _():
            for c in store(n_used - 2, lax.rem(n_used, 2)):
                c.wait()

        for c in store(n_used - 1, lax.rem(n_used - 1, 2)):
            c.wait()


def _moe(tile_expert, n_used, next_expert, xs, w_gate, b_gate, w_up, b_up, w_down, b_down):
    hbm = pl.BlockSpec(memory_space=pl.ANY)
    bias = pl.BlockSpec((None, 1, D_MODEL), lambda t, te, nu, nx: (te[t], 0, 0))
    return pl.pallas_call(
        _moe_kernel,
        grid_spec=pltpu.PrefetchScalarGridSpec(
            num_scalar_prefetch=3,
            grid=(N_EXPERT_TILES,),
            in_specs=[hbm, hbm, bias, hbm, bias, hbm, bias],
            out_specs=hbm,
            scratch_shapes=[pltpu.VMEM((3, D_MODEL, D_MODEL), F32), pltpu.VMEM((3, D_MODEL, D_MODEL), BF16),
                            pltpu.VMEM((2, TM, D_MODEL), F32), pltpu.VMEM((2, TM, D_MODEL), F32),
                            pltpu.VMEM((TM,) + ROW_TILE, F32), pltpu.SemaphoreType.DMA((3,)),
                            pltpu.SemaphoreType.DMA((2,)), pltpu.SemaphoreType.DMA((2,)),
                            pltpu.SemaphoreType.DMA],
        ),
        out_shape=jax.ShapeDtypeStruct((N_SORTED_ROWS,) + ROW_TILE, F32),
        compiler_params=_cparams(1, "experts"),
        name="experts",
    )(tile_expert, n_used, next_expert, xs, w_gate, b_gate, w_up, b_up, w_down, b_down)


def _combine_kernel(dest_ref, h_ref, w_ref, ys_ref, yp_ref, ysm_ref, g_ref, out_ref, sem):
    i = pl.program_id(0)
    slot = lax.rem(i, 2)

    def issue_tile(tile, s):
        base = tile * TM

        def issue(n, carry):
            for kk in range(TOP_K):
                _row_copy(ys_ref, dest_ref[kk * N_ROWS + base + n], g_ref.at[s], kk * TM + n,
                          sem.at[s]).start(priority=kk % 2)
            return carry

        lax.fori_loop(0, TM, issue, 0, unroll=ISSUE_UNROLL)

    @pl.when(i == 0)
    def _():
        issue_tile(0, 0)

    @pl.when(i + 1 < N_TILES)
    def _():
        issue_tile(i + 1, 1 - slot)

    for kk in range(TOP_K):
        pltpu.make_async_copy(ys_ref.at[pl.ds(0, TM)], g_ref.at[slot, pl.ds(kk * TM, TM)], sem.at[slot]).wait()

    w = w_ref[...]
    acc = None
    for kk in range(TOP_K):
        part = jnp.broadcast_to(w[:, kk:kk + 1, :], (TM,) + ROW_TILE) * g_ref[slot, kk * TM:(kk + 1) * TM]
        acc = part if acc is None else acc + part
    out_ref[...] = h_ref[...] + acc

    def write(y_ref):
        for a in range(ROW_SUB):
            y_ref[:, a * LANES:(a + 1) * LANES] = out_ref[:, a, :]

    @pl.when(i < N_PROMPT_TILES)
    def _():
        write(yp_ref)

    @pl.when(i >= N_PROMPT_TILES)
    def _():
        write(ysm_ref)


def _combine(dest, h_rows, w_rows, ys):
    tile = lambda index: pl.BlockSpec((TM,) + ROW_TILE, index)
    return pl.pallas_call(
        _combine_kernel,
        grid_spec=pltpu.PrefetchScalarGridSpec(
            num_scalar_prefetch=1,
            grid=(N_TILES,),
            in_specs=[tile(lambda i, d: (i, 0, 0)), tile(lambda i, d: (i, 0, 0)),
                      pl.BlockSpec(memory_space=pl.ANY)],
            out_specs=[pl.BlockSpec((TM, D_MODEL), lambda i, d: (jnp.minimum(i, N_PROMPT_TILES - 1), 0)),
                       pl.BlockSpec((TM, D_MODEL), lambda i, d: (0, 0))],
            scratch_shapes=[pltpu.VMEM((2, TOP_K * TM) + ROW_TILE, F32), pltpu.VMEM((TM,) + ROW_TILE, F32),
                            pltpu.SemaphoreType.DMA((2,))],
        ),
        out_shape=[jax.ShapeDtypeStruct((N_PROMPT, D_MODEL), F32),
                   jax.ShapeDtypeStruct((N_SAMPLE, D_MODEL), F32)],
        compiler_params=_cparams(1, "combine"),
        name="combine",
    )(dest, h_rows, w_rows, ys)


def _block_diag_ones(n, blk):
    idx = np.arange(n) // blk
    return (idx[:, None] == idx[None, :]).astype(np.float32)


def _swa_head_mask(tq):
    row_head = np.arange(4 * tq)[:, None] // tq
    lane_head = np.arange(4 * HD)[None, :] // HD
    return jnp.asarray((row_head == lane_head).astype(np.float32), BF16)


def _gla_masks(t, n_sub):
    nb = t // SUB
    lanes = N_HEADS_GLA * DK
    tri = jnp.asarray(np.kron(np.eye(n_sub, dtype=np.float32), np.tril(np.ones((t, t), np.float32))), BF16)
    row = np.arange(N_HEADS_GLA * t)
    col = np.arange(nb * lanes)
    same_head = (row[:, None] // t) == ((col[None, :] % lanes) // DK)
    same_blk = ((row[:, None] % t) // SUB) == (col[None, :] // lanes)
    mask = jnp.asarray((same_head & same_blk).astype(np.float32), BF16)
    return tri, mask


def _sink_rows(sinks, tq):
    s = jnp.repeat(sinks.astype(F32).reshape(2, 4), tq, axis=1)
    return jnp.broadcast_to(s[:, None, :], (2, 8, 4 * tq))


def kernel(x_prompt, x_sample, state_gla, cache_swa_k, cache_swa_v, norm_mix_g, w_in, w_gla_a2, b_gla_a, q_norm_g,
           k_norm_g, swa_sinks, gla_norm_g, w_out, norm_ffn_g, w_router, b_router, w_gate, b_gate, w_up, b_up,
           w_down, b_down):
    xp = x_prompt.reshape(N_PROMPT, D_MODEL)
    xs = x_sample.reshape(N_SAMPLE, D_MODEL)

    w_in0 = w_in[0]
    w_main = w_in0[:, :W_MAIN].astype(BF16)
    w_ga = jnp.pad(w_in0[:, OFF_GA:], ((0, 0), (0, LANES - W_GA))).astype(BF16)
    w_a2 = jnp.pad(w_gla_a2[0], ((0, LANES - W_GA), (0, 0))).astype(BF16)
    b_a = b_gla_a[0].reshape(1, -1)
    qg = (jnp.tile(q_norm_g[0], N_HEADS_SWA) * (HD ** -0.5)).reshape(1, -1)
    kg = jnp.tile(k_norm_g[0], 2).reshape(1, -1)
    bdq = jnp.asarray(_block_diag_ones(W_SQ, HD), BF16)
    bdk = jnp.asarray(_block_diag_ones(W_SK, HD), BF16)

    proj_consts = (norm_mix_g[0].reshape(1, -1), w_main, w_ga, w_a2, b_a, qg, kg, bdq, bdk)
    sq, sk, sv, gq, gk, gv, gg, gr = _proj_sample(xs, proj_consts)

    cache_k = cache_swa_k[0].reshape(N_STREAMS, WINDOW, 2 * HD)
    cache_v = cache_swa_v[0].reshape(N_STREAMS, WINDOW, 2 * HD)
    o_swa_s = _swa_sample(sq, sk, sv, cache_k, cache_v, _swa_head_mask(T_SAMPLE),
                          _sink_rows(swa_sinks[0], T_SAMPLE))

    bd_state = jnp.asarray(_block_diag_ones(N_HEADS_GLA, 1).repeat(DV, axis=0).repeat(DK, axis=1), F32)
    gn = gla_norm_g[0].reshape(1, -1)
    tri_p, mask_p = _gla_masks(CHUNK, TM // CHUNK)
    tri_s, mask_s = _gla_masks(T_SAMPLE, 1)
    s0_p = jnp.zeros((1, N_HEADS_GLA * DV, N_HEADS_GLA * DK), F32)
    sk_p, sv_p, o_swa_p, og_p, sfin_p = _front_prompt(xp, proj_consts, _swa_head_mask(CHUNK),
                                                      _sink_rows(swa_sinks[0], CHUNK), s0_p, tri_p, mask_p,
                                                      bd_state, gn)
    eye = jnp.eye(N_HEADS_GLA, dtype=F32)
    s0_s = jnp.einsum('bhde,hg->bhegd', state_gla[0].astype(F32), eye).reshape(
        N_STREAMS, N_HEADS_GLA * DV, N_HEADS_GLA * DK)
    og_s, sfin_s = _gla(gq, gk, gv, gg, gr, s0_s, tri_s, mask_s, bd_state, gn, t=T_SAMPLE, n_sub=1,
                        n_batch=N_STREAMS, n_steps=1, first_block=0, name="gla_sample")

    def unpack_state(sfin):
        s = sfin.reshape(-1, N_HEADS_GLA, DV, N_HEADS_GLA, DK)
        s = jnp.stack([s[:, h, :, h, :] for h in range(N_HEADS_GLA)], axis=1)
        return jnp.transpose(s, (0, 1, 3, 2))[None]

    w_out0 = w_out[0].astype(BF16)
    wr = jnp.pad(w_router[0], ((0, 0), (0, LANES - N_EXPERTS)))
    wr_hi = wr.astype(BF16)
    wr_lo = (wr - wr_hi.astype(F32)).astype(BF16)
    br = jnp.pad(b_router[0].astype(F32), (0, LANES - N_EXPERTS)).reshape(1, -1)
    x_rows, h_rows, logits = _merge(o_swa_p, o_swa_s, og_p, og_s, xp, xs, w_out0[:W_SQ], w_out0[W_SQ:],
                                    norm_ffn_g[0].reshape(1, -1), wr_hi, wr_lo, br)
    earlier = jnp.asarray(np.triu(np.ones((RT, RT), np.float32), 1), BF16)
    w_rows, top_i, rank, counts = _router(logits, earlier)

    counts = counts[:, 0]
    padded = (counts + TM - 1) // TM * TM
    end = jnp.cumsum(padded)
    start = end - padded
    experts = jnp.arange(N_EXPERTS, dtype=I32)
    is_e = top_i[:TOP_K, :, None] == experts
    dest = (rank[:TOP_K] + jnp.sum(jnp.where(is_e, start, 0), axis=-1)).reshape(-1).astype(I32)
    n_used = (end[-1] // TM).astype(I32)
    tiles = jnp.minimum(jnp.arange(N_EXPERT_TILES, dtype=I32), n_used - 1)
    tile_expert = jnp.sum((tiles[:, None] * TM >= end[None, :]).astype(I32), axis=1)
    later_nonempty = (experts[None, :] > experts[:, None]) & (padded[None, :] > 0)
    next_expert = jnp.min(jnp.where(later_nonempty, experts[None, :], N_EXPERTS), axis=1)
    next_expert = jnp.where(next_expert < N_EXPERTS, next_expert, -1).astype(I32)

    xs_sorted = _dispatch(dest, end.astype(I32), x_rows)
    ys = _moe(tile_expert, n_used.reshape(1), next_expert, xs_sorted, w_gate[0], b_gate[0].reshape(N_EXPERTS, 1, -1),
              w_up[0], b_up[0].reshape(N_EXPERTS, 1, -1), w_down[0], b_down[0].reshape(N_EXPERTS, 1, -1))
    y_p, y_s = _combine(dest, h_rows, w_rows, ys)

    sk_s = sk.reshape(N_STREAMS, T_SAMPLE, 2 * HD)
    sv_s = sv.reshape(N_STREAMS, T_SAMPLE, 2 * HD)
    kc_s = jnp.concatenate([cache_k[:, T_SAMPLE:], sk_s], axis=1).reshape(1, N_STREAMS, WINDOW, 2, HD)
    vc_s = jnp.concatenate([cache_v[:, T_SAMPLE:], sv_s], axis=1).reshape(1, N_STREAMS, WINDOW, 2, HD)
    kc_p = sk_p[N_PROMPT - WINDOW:].reshape(1, 1, WINDOW, 2, HD)
    vc_p = sv_p[N_PROMPT - WINDOW:].reshape(1, 1, WINDOW, 2, HD)
    return (y_p.reshape(1, N_PROMPT, D_MODEL), y_s.reshape(N_STREAMS, T_SAMPLE, D_MODEL),
            unpack_state(sfin_p), kc_p, vc_p, unpack_state(sfin_s), kc_s, vc_s)
```

```python
import functools

import numpy as np
import jax
import jax.numpy as jnp
from jax import lax
from jax.experimental import pallas as pl
from jax.experimental.pallas import tpu as pltpu

F32 = jnp.float32
BF16 = jnp.bfloat16
I32 = jnp.int32
U32 = jnp.uint32

D_MODEL = 1024
N_PROMPT = 16384
N_STREAMS = 8
T_SAMPLE = 32
N_SAMPLE = N_STREAMS * T_SAMPLE
N_ROWS = N_PROMPT + N_SAMPLE
EPS = 1e-6

CHUNK = 64
SUB = 16
N_HEADS_SWA = 8
HD = 64
WINDOW = 128
N_HEADS_GLA = 4
DK = 64
DV = 128
GLA_TAU = 16.0
N_EXPERTS = 32
TOP_K = 4
SWIGLU_ALPHA = 1.702
SWIGLU_LIMIT = 7.0

TM = 256
N_TILES = N_ROWS // TM
N_PROMPT_TILES = N_PROMPT // TM
N_ASSIGN = N_ROWS * TOP_K
N_EXPERT_TILES = N_ASSIGN // TM + N_EXPERTS
N_SORTED_ROWS = N_EXPERT_TILES * TM
LANES = 128
SLOT_ROWS = 16
NEG_BIG = -1e30

W_SQ, W_SK, W_SV, W_GQ, W_GK, W_GV, W_GR, W_GA = 512, 128, 128, 256, 256, 512, 512, 16
OFF_SQ = 0
OFF_SK = OFF_SQ + W_SQ
OFF_SV = OFF_SK + W_SK
OFF_GQ = OFF_SV + W_SV
OFF_GK = OFF_GQ + W_GQ
OFF_GV = OFF_GK + W_GK
OFF_GR = OFF_GV + W_GV
OFF_GA = OFF_GR + W_GR
W_MAIN = OFF_GA


VMEM_MIB = dict(proj=40, front=48, mixer=32, merge=32, router=48, dispatch=32, experts=52, combine=40)


def _cparams(n_grid_axes, call):
    return pltpu.CompilerParams(dimension_semantics=("arbitrary",) * n_grid_axes,
                                vmem_limit_bytes=VMEM_MIB[call] * 2 ** 20)


def _dot(a, b):
    return jnp.dot(a, b, preferred_element_type=F32)


def _dot_nt(a, b):
    return lax.dot_general(a, b, (((1,), (1,)), ((), ())), preferred_element_type=F32)


def _dot_tn(a, b):
    return lax.dot_general(a, b, (((0,), (0,)), ((), ())), preferred_element_type=F32)


def _split_bf16(x):
    hi = x.astype(BF16)
    lo = (x - hi.astype(F32)).astype(BF16)
    return hi, lo


def _rms(x):
    return x * lax.rsqrt(jnp.mean(x * x, axis=-1, keepdims=True) + EPS)


ROW_SUB = 8
ROW_TILE = (ROW_SUB, LANES)


PROJ_OUTPUTS = ((W_SQ, BF16), (W_SK, F32), (W_SV, F32), (W_GQ, BF16), (W_GK, BF16), (W_GV, BF16),
                (W_GQ, F32), (W_GR, BF16))
N_PROJ_CONSTS = 9


def _proj_tile(x, g_ref, w_ref, wga_ref, wa2_ref, ba_ref, qg_ref, kg_ref, bdq_ref, bdk_ref,
               sq_ref, sk_ref, sv_ref, gq_ref, gk_ref, gv_ref, gg_ref, gr_ref):
    xb = (_rms(x) * g_ref[...]).astype(BF16)

    def seg(off, width):
        return _dot(xb, w_ref[:, off:off + width])

    def head_norm(u, bd_ref):
        hi, lo = _split_bf16(u * u)
        ss = _dot(hi, bd_ref[...]) + _dot(lo, bd_ref[...])
        return u * lax.rsqrt(ss * (1.0 / HD) + EPS)

    sq_ref[...] = (head_norm(seg(OFF_SQ, W_SQ), bdq_ref) * qg_ref[...]).astype(BF16)
    sk_ref[...] = head_norm(seg(OFF_SK, W_SK), bdk_ref) * kg_ref[...]
    sv_ref[...] = seg(OFF_SV, W_SV)
    gq_ref[...] = (seg(OFF_GQ, W_GQ) * (DK ** -0.5)).astype(BF16)
    gk_ref[...] = seg(OFF_GK, W_GK).astype(BF16)
    gv_ref[...] = seg(OFF_GV, W_GV).astype(BF16)
    gr_ref[...] = seg(OFF_GR, W_GR).astype(BF16)
    ga = _dot(xb, wga_ref[...]).astype(BF16)
    z = _dot(ga, wa2_ref[...]) + ba_ref[...]
    log_sig = jnp.minimum(z, 0.0) - jnp.log(1.0 + jnp.exp(-jnp.abs(z)))
    gg_ref[...] = log_sig * (1.0 / GLA_TAU)


def _proj_sample_kernel(x_ref, *refs):
    _proj_tile(x_ref[...], *refs)


def _proj_sample(xs, consts):
    def full(a):
        return pl.BlockSpec(a.shape, lambda i: (0,) * a.ndim)

    return pl.pallas_call(
        _proj_sample_kernel,
        grid=(1,),
        in_specs=[full(xs)] + [full(a) for a in consts],
        out_specs=[pl.BlockSpec((N_SAMPLE, w), lambda i: (0, 0)) for w, _ in PROJ_OUTPUTS],
        out_shape=[jax.ShapeDtypeStruct((N_SAMPLE, w), dt) for w, dt in PROJ_OUTPUTS],
        compiler_params=_cparams(1, "proj"),
        name="proj_sample",
    )(xs, *consts)


def _dup_kv_heads(x):
    r = pltpu.roll(x, HD, axis=1)
    lo = lax.broadcasted_iota(I32, x.shape, 1) < HD
    out = []
    for a in (jnp.where(lo, x, r), jnp.where(lo, r, x)):
        out.append(jnp.concatenate([a, a], axis=1).astype(BF16))
    return out


def _swa_blocks(q_blocks, k_blocks, v_blocks, sinks, valids, head_mask):
    tq = q_blocks[0].shape[0]
    n_q = 4 * tq
    scores = []
    for q, k, valid in zip(q_blocks, k_blocks, valids):
        s_t = _dot_nt(k, jnp.concatenate([q] * 4, axis=0) * head_mask)
        scores.append(s_t if valid is None else jnp.where(valid, s_t, -jnp.inf))
    eye = (lax.broadcasted_iota(I32, (n_q, n_q), 0) == lax.broadcasted_iota(I32, (n_q, n_q), 1)
           ).astype(F32).astype(BF16)
    probs = []
    for s_t, sink in zip(scores, sinks):
        m = jnp.maximum(jnp.max(s_t, axis=0, keepdims=True), sink)
        p_t = jnp.exp(s_t - m)
        den = jnp.sum(p_t, axis=0, keepdims=True) + jnp.exp(sink - m)
        probs.append(_dot_nt(eye, (p_t / den).astype(BF16)).astype(BF16))
    lane_head = lax.broadcasted_iota(I32, (tq, 4 * HD), 1) // HD
    outs = []
    for p, v in zip(probs, v_blocks):
        o_full = _dot(p, v)
        o = jnp.zeros((tq, 4 * HD), F32)
        for a in range(4):
            o = o + jnp.where(lane_head == a, o_full[a * tq:(a + 1) * tq], 0.0)
        outs.append(o)
    return outs


def _swa_prompt_tile(i, q_ref, kp_ref, kc_ref, vp_ref, vc_ref, hm_ref, sink_ref, o_ref):
    k_dup = _dup_kv_heads(jnp.concatenate([kp_ref[...], kc_ref[...]], axis=0))
    v_dup = _dup_kv_heads(jnp.concatenate([vp_ref[...], vc_ref[...]], axis=0))
    sink_row = [sink_ref[j][0:1, :] for j in range(2)]
    span = WINDOW + CHUNK
    key = lax.broadcasted_iota(I32, (span, 4 * CHUNK), 0)
    qs, ks, vs, sinks, valids, where = [], [], [], [], [], []
    for c in range(TM // CHUNK):
        lo = CHUNK * c
        valid = (i * TM - WINDOW + lo + key) >= 0
        for j in range(2):
            qs.append(q_ref[lo:lo + CHUNK, 4 * HD * j:4 * HD * (j + 1)])
            ks.append(k_dup[j][lo:lo + span])
            vs.append(v_dup[j][lo:lo + span])
            sinks.append(sink_row[j])
            valids.append(valid)
            where.append((lo, j))
    outs = _swa_blocks(qs, ks, vs, sinks, valids, hm_ref[...])
    for (lo, j), o in zip(where, outs):
        o_ref[lo:lo + CHUNK, 4 * HD * j:4 * HD * (j + 1)] = o.astype(BF16)


def _swa_sample_kernel(q_ref, kc_ref, kn_ref, vc_ref, vn_ref, hm_ref, sink_ref, o_ref):
    k_dup = _dup_kv_heads(jnp.concatenate([kc_ref[...], kn_ref[...]], axis=0))
    v_dup = _dup_kv_heads(jnp.concatenate([vc_ref[...], vn_ref[...]], axis=0))
    sink_row = [sink_ref[j][0:1, :] for j in range(2)]
    qs = [q_ref[:, 4 * HD * j:4 * HD * (j + 1)] for j in range(2)]
    outs = _swa_blocks(qs, k_dup, v_dup, sink_row, [None, None], hm_ref[...])
    o_ref[...] = jnp.concatenate(outs, axis=1).astype(BF16)


def _swa_sample(sq, sk, sv, cache_k, cache_v, head_mask, sink_b):
    new = lambda width: pl.BlockSpec((T_SAMPLE, width), lambda b: (b, 0))
    cache = pl.BlockSpec((None, WINDOW, 2 * HD), lambda b: (b, 0, 0))
    return pl.pallas_call(
        _swa_sample_kernel,
        grid=(N_STREAMS,),
        in_specs=[new(W_SQ), cache, new(2 * HD), cache, new(2 * HD),
                  pl.BlockSpec(head_mask.shape, lambda b: (0, 0)),
                  pl.BlockSpec(sink_b.shape, lambda b: (0, 0, 0))],
        out_specs=pl.BlockSpec((T_SAMPLE, W_SQ), lambda b: (b, 0)),
        out_shape=jax.ShapeDtypeStruct((N_SAMPLE, W_SQ), BF16),
        compiler_params=_cparams(1, "mixer"),
        name="swa_sample",
    )(sq, cache_k, sk, cache_v, sv, head_mask, sink_b)


GLA_SAFE_EXP = 80.0
GLA_SLOW_ROWS = 16


def _gla_step(c, q_ref, k_ref, v_ref, g_ref, gr_ref, s0_ref, tri_ref, m_ref, bd_ref, gn_ref,
              og_ref, sfin_ref, st_ref, o_ref, *, t, n_sub, companion=None):
    @pl.when(c == 0)
    def _():
        st_ref[...] = s0_ref[...]

    n_rows = t * n_sub
    blocked_is_safe = (SUB - 1) * jnp.max(-g_ref[...]) <= GLA_SAFE_EXP

    def emit(o):
        gr = gr_ref[...].astype(F32)
        gate = gr / (1.0 + jnp.exp(-gr))
        outs = []
        for h in range(N_HEADS_GLA):
            sl = slice(h * DV, (h + 1) * DV)
            outs.append(_rms(o[:, sl]) * gn_ref[...] * gate[:, sl])
        og_ref[...] = jnp.concatenate(outs, axis=1).astype(BF16)

    @pl.when(blocked_is_safe)
    def _():
        if companion is not None:
            companion()
        emit(_gla_blocked(q_ref, k_ref, v_ref, g_ref, tri_ref, m_ref, bd_ref, st_ref, t=t, n_sub=n_sub))

    @pl.when(jnp.logical_not(blocked_is_safe))
    def _():
        if companion is not None:
            companion()
        _gla_tokenwise(q_ref, k_ref, v_ref, g_ref, bd_ref, st_ref, o_ref, n_rows=n_rows)
        emit(o_ref[...])

    @pl.when(c == pl.num_programs(1) - 1)
    def _():
        sfin_ref[...] = st_ref[...]


def _gla_kernel(*refs, t, n_sub):
    _gla_step(pl.program_id(1), *refs, t=t, n_sub=n_sub)


def _front_prompt_kernel(*refs, t, n_sub):
    x_ref = refs[0]
    consts = refs[1:1 + N_PROJ_CONSTS]
    hm_ref, sink_ref, s0_ref, tri_ref, m_ref, bd_ref, gn_ref = refs[1 + N_PROJ_CONSTS:8 + N_PROJ_CONSTS]
    sk_out, sv_out, o_swa_ref, og_ref, sfin_ref = refs[8 + N_PROJ_CONSTS:13 + N_PROJ_CONSTS]
    sq_s, gq_s, gk_s, gv_s, gg_s, gr_s, sk_s, sv_s, st_ref, o_ref = refs[13 + N_PROJ_CONSTS:]
    c = pl.program_id(1)

    @pl.when(c == 0)
    def _():
        for ring in (sq_s, gq_s, gk_s, gv_s, gg_s, gr_s, sk_s, sv_s):
            ring[...] = jnp.zeros_like(ring)

    new2, old2 = lax.rem(c, 2), lax.rem(c + 1, 2)
    new3, old3, older3 = lax.rem(c, 3), lax.rem(c + 2, 3), lax.rem(c + 1, 3)
    half = pl.ds(TM // 2, TM // 2)

    def companion():
        _proj_tile(x_ref[...], *consts, sq_s.at[new2], sk_s.at[new3], sv_s.at[new3], gq_s.at[new2],
                   gk_s.at[new2], gv_s.at[new2], gg_s.at[new2], gr_s.at[new2])
        sk_out[...] = sk_s[new3]
        sv_out[...] = sv_s[new3]
        _swa_prompt_tile(c - 1, sq_s.at[old2], sk_s.at[older3, half], sk_s.at[old3], sv_s.at[older3, half],
                         sv_s.at[old3], hm_ref, sink_ref, o_swa_ref)

    _gla_step(c, gq_s.at[old2], gk_s.at[old2], gv_s.at[old2], gg_s.at[old2], gr_s.at[old2], s0_ref, tri_ref,
              m_ref, bd_ref, gn_ref, og_ref, sfin_ref, st_ref, o_ref, t=t, n_sub=n_sub, companion=companion)


def _gla_tokenwise(q_ref, k_ref, v_ref, g_ref, bd_ref, st_ref, o_ref, *, n_rows):
    row = lax.broadcasted_iota(I32, (GLA_SLOW_ROWS, 1), 0)

    def group(gi, carry):
        rows = pl.ds(pl.multiple_of(gi * GLA_SLOW_ROWS, GLA_SLOW_ROWS), GLA_SLOW_ROWS)
        q = q_ref[rows, :].astype(F32)
        k = k_ref[rows, :].astype(F32)
        v = v_ref[rows, :].astype(F32)
        decay = jnp.exp(g_ref[rows, :])
        o = jnp.zeros((GLA_SLOW_ROWS, N_HEADS_GLA * DV), F32)
        for j in range(GLA_SLOW_ROWS):
            only_j = row == j
            k_j = jnp.where(only_j, k, 0.0).astype(BF16)
            v_j = jnp.where(only_j, v, 0.0).astype(BF16)
            q_j = jnp.where(only_j, q, 0.0).astype(BF16)
            st = st_ref[...] * decay[j:j + 1, :] + _dot_tn(v_j, k_j) * bd_ref[...]
            st_ref[...] = st
            o = o + _dot_nt(q_j, st.astype(BF16))
        o_ref[rows, :] = o
        return carry

    lax.fori_loop(0, n_rows // GLA_SLOW_ROWS, group, 0)


def _gla_blocked(q_ref, k_ref, v_ref, g_ref, tri_ref, m_ref, bd_ref, st_ref, *, t, n_sub):
    nb = t // SUB
    n_rows = t * n_sub
    lanes = N_HEADS_GLA * DK

    def group_row(x, period, offset):
        g = x.reshape(n_rows // period, period, lanes)[:, offset:offset + 1, :]
        return jnp.broadcast_to(g, (n_rows // period, period, lanes)).reshape(n_rows, lanes)

    g_hi, g_lo = _split_bf16(g_ref[...])
    b = _dot(tri_ref[...], g_hi) + _dot(tri_ref[...], g_lo)
    q = q_ref[...].astype(F32)
    k = k_ref[...].astype(F32)
    qd = (q * jnp.exp(b - group_row(b, SUB, 0))).astype(BF16)
    pos = lax.broadcasted_iota(I32, (n_rows, lanes), 0) & (t - 1)
    k_parts = []
    for blk in range(nb):
        arg = jnp.where(pos < SUB * (blk + 1), group_row(b, t, SUB * blk) - b, NEG_BIG)
        k_parts.append((k * jnp.exp(arg)).astype(BF16))
    k_cat = jnp.concatenate(k_parts, axis=1)
    qd_cat = jnp.concatenate([qd] * nb, axis=1)
    q_dec = (q * jnp.exp(b)).astype(BF16)
    k_last = (k * jnp.exp(group_row(b, t, t - 1) - b)).astype(BF16)
    row_a = lax.broadcasted_iota(I32, (N_HEADS_GLA * t, t), 0) & (t - 1)
    col_a = lax.broadcasted_iota(I32, (N_HEADS_GLA * t, t), 1)

    o_intra, q_decayed, state_add, state_decay = [], [], [], []
    for u in range(n_sub):
        rows = slice(u * t, (u + 1) * t)
        v = v_ref[rows, :]
        lhs = jnp.concatenate([qd_cat[rows]] * N_HEADS_GLA, axis=0) * m_ref[...]
        a = _dot_nt(lhs, k_cat[rows])
        a = jnp.where(row_a >= col_a, a, 0.0).astype(BF16)
        o_full = _dot(a, v)
        o_intra.append(jnp.concatenate(
            [o_full[h * t:(h + 1) * t, h * DV:(h + 1) * DV] for h in range(N_HEADS_GLA)], axis=1))
        q_decayed.append(q_dec[rows])
        state_add.append(_dot_tn(v, k_last[rows]) * bd_ref[...])
        state_decay.append(jnp.exp(b[(u + 1) * t - 1:(u + 1) * t, :]))

    st = st_ref[...]
    o_parts = []
    for u in range(n_sub):
        o_parts.append(o_intra[u] + _dot_nt(q_decayed[u], st.astype(BF16)))
        st = st * state_decay[u] + state_add[u]
    st_ref[...] = st
    return jnp.concatenate(o_parts, axis=0) if n_sub > 1 else o_parts[0]


def _gla(gq, gk, gv, gg, gr, s0, tri, mask, bd, gn, *, t, n_sub, n_batch, n_steps, first_block, name):
    rows_per_step = t * n_sub

    def rows(width):
        return pl.BlockSpec((rows_per_step, width), lambda b, c: (first_block + b * n_steps + c, 0))

    def full(a):
        return pl.BlockSpec(a.shape, lambda b, c: (0,) * a.ndim)

    state = pl.BlockSpec((None,) + s0.shape[1:], lambda b, c: (b, 0, 0))
    return pl.pallas_call(
        functools.partial(_gla_kernel, t=t, n_sub=n_sub),
        grid=(n_batch, n_steps),
        in_specs=[rows(W_GQ), rows(W_GK), rows(W_GV), rows(W_GQ), rows(W_GR), state,
                  full(tri), full(mask), full(bd), full(gn)],
        out_specs=[pl.BlockSpec((rows_per_step, W_GV), lambda b, c: (b * n_steps + c, 0)), state],
        out_shape=[jax.ShapeDtypeStruct((n_batch * n_steps * rows_per_step, W_GV), BF16),
                   jax.ShapeDtypeStruct(s0.shape, F32)],
        scratch_shapes=[pltpu.VMEM(s0.shape[1:], F32), pltpu.VMEM((rows_per_step, W_GV), F32)],
        compiler_params=_cparams(2, "mixer"),
        name=name,
    )(gq, gk, gv, gg, gr, s0, tri, mask, bd, gn)


def _front_prompt(xp, consts, head_mask, sink_b, s0, tri, mask, bd, gn):
    def full(a):
        return pl.BlockSpec(a.shape, lambda b, c: (0,) * a.ndim)

    def computed(width):
        return pl.BlockSpec((TM, width), lambda b, c: (jnp.minimum(c, N_PROMPT_TILES - 1), 0))

    def mixed(width):
        return pl.BlockSpec((TM, width), lambda b, c: (jnp.maximum(c - 1, 0), 0))

    state = pl.BlockSpec((None,) + s0.shape[1:], lambda b, c: (b, 0, 0))
    ring2 = [pltpu.VMEM((2, TM, w), dt) for w, dt in
             (PROJ_OUTPUTS[0], PROJ_OUTPUTS[3], PROJ_OUTPUTS[4], PROJ_OUTPUTS[5], PROJ_OUTPUTS[6], PROJ_OUTPUTS[7])]
    ring3 = [pltpu.VMEM((3, TM, w), dt) for w, dt in (PROJ_OUTPUTS[1], PROJ_OUTPUTS[2])]
    return pl.pallas_call(
        functools.partial(_front_prompt_kernel, t=CHUNK, n_sub=TM // CHUNK),
        grid=(1, N_PROMPT_TILES + 1),
        in_specs=[computed(D_MODEL)] + [full(a) for a in consts] + [full(head_mask), full(sink_b), state,
                                                                   full(tri), full(mask), full(bd), full(gn)],
        out_specs=[computed(W_SK), computed(W_SV), mixed(W_SQ), mixed(W_GV), state],
        out_shape=[jax.ShapeDtypeStruct((N_PROMPT, W_SK), F32), jax.ShapeDtypeStruct((N_PROMPT, W_SV), F32),
                   jax.ShapeDtypeStruct((N_PROMPT, W_SQ), BF16), jax.ShapeDtypeStruct((N_PROMPT, W_GV), BF16),
                   jax.ShapeDtypeStruct(s0.shape, F32)],
        scratch_shapes=ring2 + ring3 + [pltpu.VMEM(s0.shape[1:], F32), pltpu.VMEM((TM, W_GV), F32)],
        compiler_params=_cparams(2, "front"),
        name="front_prompt",
    )(xp, *consts, head_mask, sink_b, s0, tri, mask, bd, gn)


def _tile_row_copies(hbm_ref, tile, vmem_ref, sem, to_hbm, rows=TM):
    copies = []
    for a in range(ROW_SUB):
        h = hbm_ref.at[pl.ds(tile * rows, rows), a, :]
        v = vmem_ref.at[:, pl.ds(a * LANES, LANES)]
        copies.append(pltpu.make_async_copy(v, h, sem) if to_hbm else pltpu.make_async_copy(h, v, sem))
    return copies


def _store_tile_rows(i, n_steps, outputs, row_buf, row_sem, rows=TM):
    buf_slot = lax.rem(i, 2)

    def store(j, tile, s):
        return _tile_row_copies(outputs[j][0], tile, row_buf.at[j, s], row_sem.at[j, s], True, rows)

    for j, (_, value) in enumerate(outputs):
        @pl.when(i >= 2)
        def _():
            for c in store(j, i - 2, buf_slot):
                c.wait()

        row_buf[j, buf_slot] = value
        for c in store(j, i, buf_slot):
            c.start()

        @pl.when(i == n_steps - 1)
        def _():
            for c in store(j, i - 1, 1 - buf_slot) + store(j, i, buf_slot):
                c.wait()


def _merge_kernel(oswp_ref, osws_ref, ogp_ref, ogs_ref, xp_ref, xs_ref, wo1_ref, wo2_ref, gf_ref, wrh_ref, wrl_ref,
                  br_ref, xrow_ref, hrow_ref, lg_ref, row_buf, row_sem):
    i = pl.program_id(0)
    is_prompt = i < N_PROMPT_TILES
    x = jnp.where(is_prompt, xp_ref[...], xs_ref[...])
    o_swa = jnp.where(is_prompt, oswp_ref[...], osws_ref[...])
    og = jnp.where(is_prompt, ogp_ref[...], ogs_ref[...])
    h = x + (_dot(o_swa, wo1_ref[...]) + _dot(og, wo2_ref[...]))
    xn = _rms(h) * gf_ref[...]
    x_hi, x_lo = _split_bf16(xn)
    logits = _dot(x_hi, wrh_ref[...]) + _dot(x_lo, wrh_ref[...]) + _dot(x_hi, wrl_ref[...]) + br_ref[...]
    lg_ref[...] = logits.T[:N_EXPERTS]
    _store_tile_rows(i, N_TILES, ((xrow_ref, xn), (hrow_ref, h)), row_buf, row_sem)


def _merge(o_swa_p, o_swa_s, og_p, og_s, xp, xs, wo1, wo2, gf, wrh, wrl, br):
    def prompt_rows(width):
        return pl.BlockSpec((TM, width), lambda i: (jnp.minimum(i, N_PROMPT_TILES - 1), 0))

    def sample_rows(width):
        return pl.BlockSpec((TM, width), lambda i: (0, 0))

    def full(a):
        return pl.BlockSpec(a.shape, lambda i: (0,) * a.ndim)

    consts = (wo1, wo2, gf, wrh, wrl, br)
    return pl.pallas_call(
        _merge_kernel,
        grid=(N_TILES,),
        in_specs=[prompt_rows(W_SQ), sample_rows(W_SQ), prompt_rows(W_GV), sample_rows(W_GV),
                  prompt_rows(D_MODEL), sample_rows(D_MODEL)] + [full(a) for a in consts],
        out_specs=[pl.BlockSpec(memory_space=pl.ANY)] * 2 + [pl.BlockSpec((N_EXPERTS, TM), lambda i: (0, i))],
        out_shape=[jax.ShapeDtypeStruct((N_ROWS,) + ROW_TILE, F32)] * 2 + [
                   jax.ShapeDtypeStruct((N_EXPERTS, N_ROWS), F32)],
        scratch_shapes=[pltpu.VMEM((2, 2, TM, D_MODEL), F32), pltpu.SemaphoreType.DMA((2, 2))],
        compiler_params=_cparams(1, "merge"),
        name="merge",
    )(o_swa_p, o_swa_s, og_p, og_s, xp, xs, *consts)


RT = 1280
N_ROUTER_STEPS = N_ROWS // RT


def _router_kernel(lg_ref, tri_ref, wrow_ref, ti_ref, rk_ref, cnt_ref, base_ref, row_buf, row_sem):
    i = pl.program_id(0)

    @pl.when(i == 0)
    def _():
        base_ref[...] = jnp.zeros_like(base_ref)

    logits_t = lg_ref[...]
    expert = lax.broadcasted_iota(I32, logits_t.shape, 0)
    slot = lax.broadcasted_iota(I32, (SLOT_ROWS, RT), 0)
    vals, hots = [], []
    ti = jnp.zeros((SLOT_ROWS, RT), I32)
    for kk in range(TOP_K):
        m = jnp.max(logits_t, axis=0, keepdims=True)
        idx = jnp.min(jnp.where(logits_t == m, expert, N_EXPERTS), axis=0, keepdims=True)
        hot = expert == idx
        logits_t = jnp.where(hot, NEG_BIG, logits_t)
        vals.append(m)
        hots.append(hot)
        ti = jnp.where(slot == kk, idx, ti)
    ti_ref[...] = ti
    exps = [jnp.exp(v - vals[0]) for v in vals]
    den = exps[0] + exps[1] + exps[2] + exps[3]
    tw_t = jnp.zeros((SLOT_ROWS, RT), F32)
    for kk in range(TOP_K):
        tw_t = jnp.where(slot == kk, exps[kk] / den, tw_t)
    eye = (lax.broadcasted_iota(I32, (SLOT_ROWS, LANES), 0)
           == lax.broadcasted_iota(I32, (SLOT_ROWS, LANES), 1)).astype(F32).astype(BF16)
    w_hi = tw_t.astype(BF16)
    w_mid, w_lo = _split_bf16(tw_t - w_hi.astype(F32))
    tw_col = _dot_tn(w_hi, eye) + _dot_tn(w_mid, eye) + _dot_tn(w_lo, eye)

    onehot_t = jnp.zeros(logits_t.shape, F32)
    for hot in hots:
        onehot_t = onehot_t + jnp.where(hot, 1.0, 0.0)
    before_t = _dot(onehot_t.astype(BF16), tri_ref[...]) + base_ref[:, 0:1]
    rk = jnp.zeros((SLOT_ROWS, RT), I32)
    for kk in range(TOP_K):
        r = jnp.sum(jnp.where(hots[kk], before_t, 0.0), axis=0, keepdims=True).astype(I32)
        rk = jnp.where(slot == kk, r, rk)
    rk_ref[...] = rk
    total = base_ref[...] + jnp.sum(onehot_t, axis=1, keepdims=True)
    base_ref[...] = total
    cnt_ref[...] = total.astype(I32)

    w_lanes = [jnp.broadcast_to(tw_col[:, kk:kk + 1], (RT, LANES)) for kk in range(TOP_K)]
    w_rows = jnp.concatenate(w_lanes + [jnp.zeros((RT, D_MODEL - TOP_K * LANES), F32)], axis=1)
    _store_tile_rows(i, N_ROUTER_STEPS, ((wrow_ref, w_rows),), row_buf, row_sem, RT)


def _router(logits, tri):
    return pl.pallas_call(
        _router_kernel,
        grid=(N_ROUTER_STEPS,),
        in_specs=[pl.BlockSpec((N_EXPERTS, RT), lambda i: (0, i)), pl.BlockSpec(tri.shape, lambda i: (0, 0))],
        out_specs=[pl.BlockSpec(memory_space=pl.ANY),
                   pl.BlockSpec((SLOT_ROWS, RT), lambda i: (0, i)),
                   pl.BlockSpec((SLOT_ROWS, RT), lambda i: (0, i)),
                   pl.BlockSpec((N_EXPERTS, LANES), lambda i: (0, 0))],
        out_shape=[jax.ShapeDtypeStruct((N_ROWS,) + ROW_TILE, F32),
                   jax.ShapeDtypeStruct((SLOT_ROWS, N_ROWS), I32),
                   jax.ShapeDtypeStruct((SLOT_ROWS, N_ROWS), I32),
                   jax.ShapeDtypeStruct((N_EXPERTS, LANES), I32)],
        scratch_shapes=[pltpu.VMEM((N_EXPERTS, LANES), F32), pltpu.VMEM((1, 2, RT, D_MODEL), F32),
                        pltpu.SemaphoreType.DMA((1, 2))],
        compiler_params=_cparams(1, "router"),
        name="router",
    )(logits, tri)


ISSUE_UNROLL = 4


def _row_copy(src_ref, src_row, dst_ref, dst_row, sem):
    return pltpu.make_async_copy(src_ref.at[pl.ds(src_row, 1)], dst_ref.at[pl.ds(dst_row, 1)], sem)


def _dispatch_kernel(dest_ref, end_ref, x_ref, xs_ref, zero_ref, sem, zsem):
    i = pl.program_id(0)
    base = i * TM

    @pl.when(i == 0)
    def _():
        zero_ref[...] = jnp.zeros_like(zero_ref)

        def tail_copy(e):
            last = jnp.maximum(end_ref[e] - TM, 0)
            return pltpu.make_async_copy(zero_ref, xs_ref.at[pl.ds(pl.multiple_of(last, TM), TM)], zsem)

        def fill(e, carry):
            tail_copy(e).start()
            return carry

        def fill_wait(e, carry):
            tail_copy(e).wait()
            return carry

        lax.fori_loop(0, N_EXPERTS, fill, 0)
        lax.fori_loop(0, N_EXPERTS, fill_wait, 0)

        def unused_copy(t):
            return pltpu.make_async_copy(zero_ref, xs_ref.at[pl.ds(pl.multiple_of(t * TM, TM), TM)], zsem)

        def fill_unused(t, carry):
            unused_copy(t).start()
            unused_copy(t).wait()
            return carry

        lax.fori_loop(end_ref[N_EXPERTS - 1] // TM, N_EXPERT_TILES, fill_unused, 0)

    def issue(n, carry):
        for kk in range(TOP_K):
            _row_copy(x_ref, n, xs_ref, dest_ref[kk * N_ROWS + base + n], sem).start(priority=kk % 2)
        return carry

    lax.fori_loop(0, TM, issue, 0, unroll=ISSUE_UNROLL)

    for kk in range(TOP_K):
        pltpu.make_async_copy(x_ref, xs_ref.at[pl.ds(0, TM)], sem).wait()


def _dispatch(dest, end, x_packed):
    return pl.pallas_call(
        _dispatch_kernel,
        grid_spec=pltpu.PrefetchScalarGridSpec(
            num_scalar_prefetch=2,
            grid=(N_TILES,),
            in_specs=[pl.BlockSpec((TM,) + ROW_TILE, lambda i, d, e: (i, 0, 0))],
            out_specs=pl.BlockSpec(memory_space=pl.ANY),
            scratch_shapes=[pltpu.VMEM((TM,) + ROW_TILE, F32), pltpu.SemaphoreType.DMA,
                            pltpu.SemaphoreType.DMA],
        ),
        out_shape=jax.ShapeDtypeStruct((N_SORTED_ROWS,) + ROW_TILE, F32),
        compiler_params=_cparams(1, "dispatch"),
        name="dispatch",
    )(dest, end, x_packed)


CAST_ROWS = 128


def _moe_kernel(te_ref, nu_ref, nx_ref, xs_ref, wg_ref, bg_ref, wu_ref, bu_ref, wd_ref, bd_ref, ys_ref,
                w_stage, w_bf, x_buf, y_buf, zero_buf, w_sem, in_sem, out_sem, zero_sem):
    t = pl.program_id(0)
    n_used = nu_ref[0]
    slot = lax.rem(t, 2)
    e = te_ref[t]
    e_prev = te_ref[jnp.maximum(t - 1, 0)]

    def load(tile, s):
        return _tile_row_copies(xs_ref, tile, x_buf.at[s], in_sem.at[s], to_hbm=False)

    def store(tile, s):
        return _tile_row_copies(ys_ref, tile, y_buf.at[s], out_sem.at[s], to_hbm=True)

    def weight_copies(expert):
        return [pltpu.make_async_copy(w.at[expert], w_stage.at[j], w_sem.at[j])
                for j, w in enumerate((wg_ref, wu_ref, wd_ref))]

    @pl.when(t == 0)
    def _():
        for c in weight_copies(e) + load(0, 0):
            c.start()

    @pl.when(t + 1 < n_used)
    def _():
        for c in load(t + 1, 1 - slot):
            c.start()

    @pl.when((t == 0) | (e != e_prev))
    def _():
        for c in weight_copies(e):
            c.wait()

        def cast(r, carry):
            sl = pl.ds(pl.multiple_of(r * CAST_ROWS, CAST_ROWS), CAST_ROWS)
            for j in range(3):
                w_bf[j, sl, :] = w_stage[j, sl, :].astype(BF16)
            return carry

        lax.fori_loop(0, D_MODEL // CAST_ROWS, cast, 0)
        e_next = nx_ref[e]

        @pl.when(e_next >= 0)
        def _():
            for c in weight_copies(e_next):
                c.start()

    @pl.when(t < n_used)
    def _():
        for c in load(t, slot):
            c.wait()

        @pl.when(t >= 2)
        def _():
            for c in store(t - 2, slot):
                c.wait()

        x = x_buf[slot].astype(BF16)
        gate = jnp.minimum(_dot(x, w_bf[0]) + bg_ref[...], SWIGLU_LIMIT)
        up = jnp.clip(_dot(x, w_bf[1]) + bu_ref[...], -SWIGLU_LIMIT, SWIGLU_LIMIT)
        hdn = (up + 1.0) * gate * (1.0 / (1.0 + jnp.exp(-SWIGLU_ALPHA * gate)))
        y_buf[slot] = _dot(hdn.astype(BF16), w_bf[2]) + bd_ref[...]
        for c in store(t, slot):
            c.start()

    @pl.when(t >= n_used)
    def _():
        zero_buf[...] = jnp.zeros_like(zero_buf)
        fill = pltpu.make_async_copy(zero_buf, ys_ref.at[pl.ds(t * TM, TM)], zero_sem)
        fill.start()
        fill.wait()

    @pl.when(t == N_EXPERT_TILES - 1)
    def _():
        @pl.when(n_used >= 2)
        def _():
            for c in store(n_used - 2, lax.rem(n_used, 2)):
                c.wait()

        for c in store(n_used - 1, lax.rem(n_used - 1, 2)):
            c.wait()


def _moe(tile_expert, n_used, next_expert, xs, w_gate, b_gate, w_up, b_up, w_down, b_down):
    hbm = pl.BlockSpec(memory_space=pl.ANY)
    bias = pl.BlockSpec((None, 1, D_MODEL), lambda t, te, nu, nx: (te[t], 0, 0))
    return pl.pallas_call(
        _moe_kernel,
        grid_spec=pltpu.PrefetchScalarGridSpec(
            num_scalar_prefetch=3,
            grid=(N_EXPERT_TILES,),
            in_specs=[hbm, hbm, bias, hbm, bias, hbm, bias],
            out_specs=hbm,
            scratch_shapes=[pltpu.VMEM((3, D_MODEL, D_MODEL), F32), pltpu.VMEM((3, D_MODEL, D_MODEL), BF16),
                            pltpu.VMEM((2, TM, D_MODEL), F32), pltpu.VMEM((2, TM, D_MODEL), F32),
                            pltpu.VMEM((TM,) + ROW_TILE, F32), pltpu.SemaphoreType.DMA((3,)),
                            pltpu.SemaphoreType.DMA((2,)), pltpu.SemaphoreType.DMA((2,)),
                            pltpu.SemaphoreType.DMA],
        ),
        out_shape=jax.ShapeDtypeStruct((N_SORTED_ROWS,) + ROW_TILE, F32),
        compiler_params=_cparams(1, "experts"),
        name="experts",
    )(tile_expert, n_used, next_expert, xs, w_gate, b_gate, w_up, b_up, w_down, b_down)


def _combine_kernel(dest_ref, h_ref, w_ref, ys_ref, yp_ref, ysm_ref, g_ref, out_ref, sem):
    i = pl.program_id(0)
    slot = lax.rem(i, 2)

    def issue_tile(tile, s):
        base = tile * TM

        def issue(n, carry):
            for kk in range(TOP_K):
                _row_copy(ys_ref, dest_ref[kk * N_ROWS + base + n], g_ref.at[s], kk * TM + n,
                          sem.at[s]).start(priority=kk % 2)
            return carry

        lax.fori_loop(0, TM, issue, 0, unroll=ISSUE_UNROLL)

    @pl.when(i == 0)
    def _():
        issue_tile(0, 0)

    @pl.when(i + 1 < N_TILES)
    def _():
        issue_tile(i + 1, 1 - slot)

    for kk in range(TOP_K):
        pltpu.make_async_copy(ys_ref.at[pl.ds(0, TM)], g_ref.at[slot, pl.ds(kk * TM, TM)], sem.at[slot]).wait()

    w = w_ref[...]
    acc = None
    for kk in range(TOP_K):
        part = jnp.broadcast_to(w[:, kk:kk + 1, :], (TM,) + ROW_TILE) * g_ref[slot, kk * TM:(kk + 1) * TM]
        acc = part if acc is None else acc + part
    out_ref[...] = h_ref[...] + acc

    def write(y_ref):
        for a in range(ROW_SUB):
            y_ref[:, a * LANES:(a + 1) * LANES] = out_ref[:, a, :]

    @pl.when(i < N_PROMPT_TILES)
    def _():
        write(yp_ref)

    @pl.when(i >= N_PROMPT_TILES)
    def _():
        write(ysm_ref)


def _combine(dest, h_rows, w_rows, ys):
    tile = lambda index: pl.BlockSpec((TM,) + ROW_TILE, index)
    return pl.pallas_call(
        _combine_kernel,
        grid_spec=pltpu.PrefetchScalarGridSpec(
            num_scalar_prefetch=1,
            grid=(N_TILES,),
            in_specs=[tile(lambda i, d: (i, 0, 0)), tile(lambda i, d: (i, 0, 0)),
                      pl.BlockSpec(memory_space=pl.ANY)],
            out_specs=[pl.BlockSpec((TM, D_MODEL), lambda i, d: (jnp.minimum(i, N_PROMPT_TILES - 1), 0)),
                       pl.BlockSpec((TM, D_MODEL), lambda i, d: (0, 0))],
            scratch_shapes=[pltpu.VMEM((2, TOP_K * TM) + ROW_TILE, F32), pltpu.VMEM((TM,) + ROW_TILE, F32),
                            pltpu.SemaphoreType.DMA((2,))],
        ),
        out_shape=[jax.ShapeDtypeStruct((N_PROMPT, D_MODEL), F32),
                   jax.ShapeDtypeStruct((N_SAMPLE, D_MODEL), F32)],
        compiler_params=_cparams(1, "combine"),
        name="combine",
    )(dest, h_rows, w_rows, ys)


def _block_diag_ones(n, blk):
    idx = np.arange(n) // blk
    return (idx[:, None] == idx[None, :]).astype(np.float32)


def _swa_head_mask(tq):
    row_head = np.arange(4 * tq)[:, None] // tq
    lane_head = np.arange(4 * HD)[None, :] // HD
    return jnp.asarray((row_head == lane_head).astype(np.float32), BF16)


def _gla_masks(t, n_sub):
    nb = t // SUB
    lanes = N_HEADS_GLA * DK
    tri = jnp.asarray(np.kron(np.eye(n_sub, dtype=np.float32), np.tril(np.ones((t, t), np.float32))), BF16)
    row = np.arange(N_HEADS_GLA * t)
    col = np.arange(nb * lanes)
    same_head = (row[:, None] // t) == ((col[None, :] % lanes) // DK)
    same_blk = ((row[:, None] % t) // SUB) == (col[None, :] // lanes)
    mask = jnp.asarray((same_head & same_blk).astype(np.float32), BF16)
    return tri, mask


def _sink_rows(sinks, tq):
    s = jnp.repeat(sinks.astype(F32).reshape(2, 4), tq, axis=1)
    return jnp.broadcast_to(s[:, None, :], (2, 8, 4 * tq))


def kernel(x_prompt, x_sample, state_gla, cache_swa_k, cache_swa_v, norm_mix_g, w_in, w_gla_a2, b_gla_a, q_norm_g,
           k_norm_g, swa_sinks, gla_norm_g, w_out, norm_ffn_g, w_router, b_router, w_gate, b_gate, w_up, b_up,
           w_down, b_down):
    xp = x_prompt.reshape(N_PROMPT, D_MODEL)
    xs = x_sample.reshape(N_SAMPLE, D_MODEL)

    w_in0 = w_in[0]
    w_main = w_in0[:, :W_MAIN].astype(BF16)
    w_ga = jnp.pad(w_in0[:, OFF_GA:], ((0, 0), (0, LANES - W_GA))).astype(BF16)
    w_a2 = jnp.pad(w_gla_a2[0], ((0, LANES - W_GA), (0, 0))).astype(BF16)
    b_a = b_gla_a[0].reshape(1, -1)
    qg = (jnp.tile(q_norm_g[0], N_HEADS_SWA) * (HD ** -0.5)).reshape(1, -1)
    kg = jnp.tile(k_norm_g[0], 2).reshape(1, -1)
    bdq = jnp.asarray(_block_diag_ones(W_SQ, HD), BF16)
    bdk = jnp.asarray(_block_diag_ones(W_SK, HD), BF16)

    proj_consts = (norm_mix_g[0].reshape(1, -1), w_main, w_ga, w_a2, b_a, qg, kg, bdq, bdk)
    sq, sk, sv, gq, gk, gv, gg, gr = _proj_sample(xs, proj_consts)

    cache_k = cache_swa_k[0].reshape(N_STREAMS, WINDOW, 2 * HD)
    cache_v = cache_swa_v[0].reshape(N_STREAMS, WINDOW, 2 * HD)
    o_swa_s = _swa_sample(sq, sk, sv, cache_k, cache_v, _swa_head_mask(T_SAMPLE),
                          _sink_rows(swa_sinks[0], T_SAMPLE))

    bd_state = jnp.asarray(_block_diag_ones(N_HEADS_GLA, 1).repeat(DV, axis=0).repeat(DK, axis=1), F32)
    gn = gla_norm_g[0].reshape(1, -1)
    tri_p, mask_p = _gla_masks(CHUNK, TM // CHUNK)
    tri_s, mask_s = _gla_masks(T_SAMPLE, 1)
    s0_p = jnp.zeros((1, N_HEADS_GLA * DV, N_HEADS_GLA * DK), F32)
    sk_p, sv_p, o_swa_p, og_p, sfin_p = _front_prompt(xp, proj_consts, _swa_head_mask(CHUNK),
                                                      _sink_rows(swa_sinks[0], CHUNK), s0_p, tri_p, mask_p,
                                                      bd_state, gn)
    eye = jnp.eye(N_HEADS_GLA, dtype=F32)
    s0_s = jnp.einsum('bhde,hg->bhegd', state_gla[0].astype(F32), eye).reshape(
        N_STREAMS, N_HEADS_GLA * DV, N_HEADS_GLA * DK)
    og_s, sfin_s = _gla(gq, gk, gv, gg, gr, s0_s, tri_s, mask_s, bd_state, gn, t=T_SAMPLE, n_sub=1,
                        n_batch=N_STREAMS, n_steps=1, first_block=0, name="gla_sample")

    def unpack_state(sfin):
        s = sfin.reshape(-1, N_HEADS_GLA, DV, N_HEADS_GLA, DK)
        s = jnp.stack([s[:, h, :, h, :] for h in range(N_HEADS_GLA)], axis=1)
        return jnp.transpose(s, (0, 1, 3, 2))[None]

    w_out0 = w_out[0].astype(BF16)
    wr = jnp.pad(w_router[0], ((0, 0), (0, LANES - N_EXPERTS)))
    wr_hi = wr.astype(BF16)
    wr_lo = (wr - wr_hi.astype(F32)).astype(BF16)
    br = jnp.pad(b_router[0].astype(F32), (0, LANES - N_EXPERTS)).reshape(1, -1)
    x_rows, h_rows, logits = _merge(o_swa_p, o_swa_s, og_p, og_s, xp, xs, w_out0[:W_SQ], w_out0[W_SQ:],
                                    norm_ffn_g[0].reshape(1, -1), wr_hi, wr_lo, br)
    earlier = jnp.asarray(np.triu(np.ones((RT, RT), np.float32), 1), BF16)
    w_rows, top_i, rank, counts = _router(logits, earlier)

    counts = counts[:, 0]
    padded = (counts + TM - 1) // TM * TM
    end = jnp.cumsum(padded)
    start = end - padded
    experts = jnp.arange(N_EXPERTS, dtype=I32)
    is_e = top_i[:TOP_K, :, None] == experts
    dest = (rank[:TOP_K] + jnp.sum(jnp.where(is_e, start, 0), axis=-1)).reshape(-1).astype(I32)
    n_used = (end[-1] // TM).astype(I32)
    tiles = jnp.minimum(jnp.arange(N_EXPERT_TILES, dtype=I32), n_used - 1)
    tile_expert = jnp.sum((tiles[:, None] * TM >= end[None, :]).astype(I32), axis=1)
    later_nonempty = (experts[None, :] > experts[:, None]) & (padded[None, :] > 0)
    next_expert = jnp.min(jnp.where(later_nonempty, experts[None, :], N_EXPERTS), axis=1)
    next_expert = jnp.where(next_expert < N_EXPERTS, next_expert, -1).astype(I32)

    xs_sorted = _dispatch(dest, end.astype(I32), x_rows)
    ys = _moe(tile_expert, n_used.reshape(1), next_expert, xs_sorted, w_gate[0], b_gate[0].reshape(N_EXPERTS, 1, -1),
              w_up[0], b_up[0].reshape(N_EXPERTS, 1, -1), w_down[0], b_down[0].reshape(N_EXPERTS, 1, -1))
    y_p, y_s = _combine(dest, h_rows, w_rows, ys)

    sk_s = sk.reshape(N_STREAMS, T_SAMPLE, 2 * HD)
    sv_s = sv.reshape(N_STREAMS, T_SAMPLE, 2 * HD)
    kc_s = jnp.concatenate([cache_k[:, T_SAMPLE:], sk_s], axis=1).reshape(1, N_STREAMS, WINDOW, 2, HD)
    vc_s = jnp.concatenate([cache_v[:, T_SAMPLE:], sv_s], axis=1).reshape(1, N_STREAMS, WINDOW, 2, HD)
    kc_p = sk_p[N_PROMPT - WINDOW:].reshape(1, 1, WINDOW, 2, HD)
    vc_p = sv_p[N_PROMPT - WINDOW:].reshape(1, 1, WINDOW, 2, HD)
    return (y_p.reshape(1, N_PROMPT, D_MODEL), y_s.reshape(N_STREAMS, T_SAMPLE, D_MODEL),
            unpack_state(sfin_p), kc_p, vc_p, unpack_state(sfin_s), kc_s, vc_s)
```

```python
import functools

import numpy as np
import jax
import jax.numpy as jnp
from jax import lax
from jax.experimental import pallas as pl
from jax.experimental.pallas import tpu as pltpu

F32 = jnp.float32
BF16 = jnp.bfloat16
I32 = jnp.int32

D_MODEL = 1024
N_PROMPT = 16384
N_STREAMS = 8
T_SAMPLE = 32
N_SAMPLE = N_STREAMS * T_SAMPLE
N_ROWS = N_PROMPT + N_SAMPLE
EPS = 1e-6

CHUNK = 64
SUB = 16
N_HEADS_SWA = 8
HD = 64
WINDOW = 128
N_HEADS_GLA = 4
DK = 64
DV = 128
GLA_TAU = 16.0
N_EXPERTS = 32
TOP_K = 4
SWIGLU_ALPHA = 1.702
SWIGLU_LIMIT = 7.0

TM = 256
N_TILES = N_ROWS // TM
N_PROMPT_TILES = N_PROMPT // TM
N_ASSIGN = N_ROWS * TOP_K
N_EXPERT_TILES = N_ASSIGN // TM + N_EXPERTS
N_SORTED_ROWS = N_EXPERT_TILES * TM
LANES = 128
SLOT_ROWS = 16
NEG_BIG = -1e30

W_SQ, W_SK, W_SV, W_GQ, W_GK, W_GV, W_GR, W_GA = 512, 128, 128, 256, 256, 512, 512, 16
OFF_SQ = 0
OFF_SK = OFF_SQ + W_SQ
OFF_SV = OFF_SK + W_SK
OFF_GQ = OFF_SV + W_SV
OFF_GK = OFF_GQ + W_GQ
OFF_GV = OFF_GK + W_GK
OFF_GR = OFF_GV + W_GV
OFF_GA = OFF_GR + W_GR
W_MAIN = OFF_GA


VMEM_MIB = dict(proj=40, front=48, mixer=32, merge=32, router=48, dispatch=32, experts=52, combine=40)


def _cparams(n_grid_axes, call):
    return pltpu.CompilerParams(dimension_semantics=("arbitrary",) * n_grid_axes,
                                vmem_limit_bytes=VMEM_MIB[call] * 2 ** 20)


def _dot(a, b):
    return jnp.dot(a, b, preferred_element_type=F32)


def _dot_nt(a, b):
    return lax.dot_general(a, b, (((1,), (1,)), ((), ())), preferred_element_type=F32)


def _dot_tn(a, b):
    return lax.dot_general(a, b, (((0,), (0,)), ((), ())), preferred_element_type=F32)


def _split_bf16(x):
    hi = x.astype(BF16)
    lo = (x - hi.astype(F32)).astype(BF16)
    return hi, lo


def _rms(x):
    return x * lax.rsqrt(jnp.mean(x * x, axis=-1, keepdims=True) + EPS)


ROW_SUB = 8
ROW_TILE = (ROW_SUB, LANES)


PROJ_OUTPUTS = ((W_SQ, BF16), (W_SK, F32), (W_SV, F32), (W_GQ, BF16), (W_GK, BF16), (W_GV, BF16),
                (W_GQ, F32), (W_GR, BF16))
N_PROJ_CONSTS = 9


def _proj_tile(x, g_ref, w_ref, wga_ref, wa2_ref, ba_ref, qg_ref, kg_ref, bdq_ref, bdk_ref,
               sq_ref, sk_ref, sv_ref, gq_ref, gk_ref, gv_ref, gg_ref, gr_ref):
    xb = (_rms(x) * g_ref[...]).astype(BF16)

    def seg(off, width):
        return _dot(xb, w_ref[:, off:off + width])

    def head_norm(u, bd_ref):
        hi, lo = _split_bf16(u * u)
        ss = _dot(hi, bd_ref[...]) + _dot(lo, bd_ref[...])
        return u * lax.rsqrt(ss * (1.0 / HD) + EPS)

    sq_ref[...] = (head_norm(seg(OFF_SQ, W_SQ), bdq_ref) * qg_ref[...]).astype(BF16)
    sk_ref[...] = head_norm(seg(OFF_SK, W_SK), bdk_ref) * kg_ref[...]
    sv_ref[...] = seg(OFF_SV, W_SV)
    gq_ref[...] = (seg(OFF_GQ, W_GQ) * (DK ** -0.5)).astype(BF16)
    gk_ref[...] = seg(OFF_GK, W_GK).astype(BF16)
    gv_ref[...] = seg(OFF_GV, W_GV).astype(BF16)
    gr_ref[...] = seg(OFF_GR, W_GR).astype(BF16)
    ga = _dot(xb, wga_ref[...]).astype(BF16)
    z = _dot(ga, wa2_ref[...]) + ba_ref[...]
    log_sig = jnp.minimum(z, 0.0) - jnp.log(1.0 + jnp.exp(-jnp.abs(z)))
    gg_ref[...] = log_sig * (1.0 / GLA_TAU)


def _proj_sample_kernel(x_ref, *refs):
    _proj_tile(x_ref[...], *refs)


def _proj_sample(xs, consts):
    def full(a):
        return pl.BlockSpec(a.shape, lambda i: (0,) * a.ndim)

    return pl.pallas_call(
        _proj_sample_kernel,
        grid=(1,),
        in_specs=[full(xs)] + [full(a) for a in consts],
        out_specs=[pl.BlockSpec((N_SAMPLE, w), lambda i: (0, 0)) for w, _ in PROJ_OUTPUTS],
        out_shape=[jax.ShapeDtypeStruct((N_SAMPLE, w), dt) for w, dt in PROJ_OUTPUTS],
        compiler_params=_cparams(1, "proj"),
        name="proj_sample",
    )(xs, *consts)


def _dup_kv_heads(x):
    r = pltpu.roll(x, HD, axis=1)
    lo = lax.broadcasted_iota(I32, x.shape, 1) < HD
    out = []
    for a in (jnp.where(lo, x, r), jnp.where(lo, r, x)):
        out.append(jnp.concatenate([a, a], axis=1).astype(BF16))
    return out


def _swa_blocks(q_blocks, k_blocks, v_blocks, sinks, valids, head_mask):
    tq = q_blocks[0].shape[0]
    n_q = 4 * tq
    scores = []
    for q, k, valid in zip(q_blocks, k_blocks, valids):
        s_t = _dot_nt(k, jnp.concatenate([q] * 4, axis=0) * head_mask)
        scores.append(s_t if valid is None else jnp.where(valid, s_t, -jnp.inf))
    eye = (lax.broadcasted_iota(I32, (n_q, n_q), 0) == lax.broadcasted_iota(I32, (n_q, n_q), 1)
           ).astype(F32).astype(BF16)
    probs = []
    for s_t, sink in zip(scores, sinks):
        m = jnp.maximum(jnp.max(s_t, axis=0, keepdims=True), sink)
        p_t = jnp.exp(s_t - m)
        den = jnp.sum(p_t, axis=0, keepdims=True) + jnp.exp(sink - m)
        probs.append(_dot_nt(eye, (p_t / den).astype(BF16)).astype(BF16))
    lane_head = lax.broadcasted_iota(I32, (tq, 4 * HD), 1) // HD
    outs = []
    for p, v in zip(probs, v_blocks):
        o_full = _dot(p, v)
        o = jnp.zeros((tq, 4 * HD), F32)
        for a in range(4):
            o = o + jnp.where(lane_head == a, o_full[a * tq:(a + 1) * tq], 0.0)
        outs.append(o)
    return outs


def _swa_prompt_tile(i, q_ref, kp_ref, kc_ref, vp_ref, vc_ref, hm_ref, sink_ref, o_ref):
    k_dup = _dup_kv_heads(jnp.concatenate([kp_ref[...], kc_ref[...]], axis=0))
    v_dup = _dup_kv_heads(jnp.concatenate([vp_ref[...], vc_ref[...]], axis=0))
    sink_row = [sink_ref[j][0:1, :] for j in range(2)]
    span = WINDOW + CHUNK
    key = lax.broadcasted_iota(I32, (span, 4 * CHUNK), 0)
    qs, ks, vs, sinks, valids, where = [], [], [], [], [], []
    for c in range(TM // CHUNK):
        lo = CHUNK * c
        valid = (i * TM - WINDOW + lo + key) >= 0
        for j in range(2):
            qs.append(q_ref[lo:lo + CHUNK, 4 * HD * j:4 * HD * (j + 1)])
            ks.append(k_dup[j][lo:lo + span])
            vs.append(v_dup[j][lo:lo + span])
            sinks.append(sink_row[j])
            valids.append(valid)
            where.append((lo, j))
    outs = _swa_blocks(qs, ks, vs, sinks, valids, hm_ref[...])
    for (lo, j), o in zip(where, outs):
        o_ref[lo:lo + CHUNK, 4 * HD * j:4 * HD * (j + 1)] = o.astype(BF16)


def _swa_sample_kernel(q_ref, kc_ref, kn_ref, vc_ref, vn_ref, hm_ref, sink_ref, o_ref):
    k_dup = _dup_kv_heads(jnp.concatenate([kc_ref[...], kn_ref[...]], axis=0))
    v_dup = _dup_kv_heads(jnp.concatenate([vc_ref[...], vn_ref[...]], axis=0))
    sink_row = [sink_ref[j][0:1, :] for j in range(2)]
    qs = [q_ref[:, 4 * HD * j:4 * HD * (j + 1)] for j in range(2)]
    outs = _swa_blocks(qs, k_dup, v_dup, sink_row, [None, None], hm_ref[...])
    o_ref[...] = jnp.concatenate(outs, axis=1).astype(BF16)


def _swa_sample(sq, sk, sv, cache_k, cache_v, head_mask, sink_b):
    new = lambda width: pl.BlockSpec((T_SAMPLE, width), lambda b: (b, 0))
    cache = pl.BlockSpec((None, WINDOW, 2 * HD), lambda b: (b, 0, 0))
    return pl.pallas_call(
        _swa_sample_kernel,
        grid=(N_STREAMS,),
        in_specs=[new(W_SQ), cache, new(2 * HD), cache, new(2 * HD),
                  pl.BlockSpec(head_mask.shape, lambda b: (0, 0)),
                  pl.BlockSpec(sink_b.shape, lambda b: (0, 0, 0))],
        out_specs=pl.BlockSpec((T_SAMPLE, W_SQ), lambda b: (b, 0)),
        out_shape=jax.ShapeDtypeStruct((N_SAMPLE, W_SQ), BF16),
        compiler_params=_cparams(1, "mixer"),
        name="swa_sample",
    )(sq, cache_k, sk, cache_v, sv, head_mask, sink_b)


GLA_SAFE_EXP = 80.0
GLA_SLOW_ROWS = 16


def _gla_step(c, q_ref, k_ref, v_ref, g_ref, gr_ref, s0_ref, tri_ref, m_ref, bd_ref, gn_ref,
              og_ref, sfin_ref, st_ref, o_ref, *, t, n_sub, companion=None):
    @pl.when(c == 0)
    def _():
        st_ref[...] = s0_ref[...]

    n_rows = t * n_sub
    blocked_is_safe = (SUB - 1) * jnp.max(-g_ref[...]) <= GLA_SAFE_EXP

    def emit(o):
        gr = gr_ref[...].astype(F32)
        gate = gr / (1.0 + jnp.exp(-gr))
        outs = []
        for h in range(N_HEADS_GLA):
            sl = slice(h * DV, (h + 1) * DV)
            outs.append(_rms(o[:, sl]) * gn_ref[...] * gate[:, sl])
        og_ref[...] = jnp.concatenate(outs, axis=1).astype(BF16)

    @pl.when(blocked_is_safe)
    def _():
        if companion is not None:
            companion()
        emit(_gla_blocked(q_ref, k_ref, v_ref, g_ref, tri_ref, m_ref, bd_ref, st_ref, t=t, n_sub=n_sub))

    @pl.when(jnp.logical_not(blocked_is_safe))
    def _():
        if companion is not None:
            companion()
        _gla_tokenwise(q_ref, k_ref, v_ref, g_ref, bd_ref, st_ref, o_ref, n_rows=n_rows)
        emit(o_ref[...])

    @pl.when(c == pl.num_programs(1) - 1)
    def _():
        sfin_ref[...] = st_ref[...]


def _gla_kernel(*refs, t, n_sub):
    _gla_step(pl.program_id(1), *refs, t=t, n_sub=n_sub)


def _front_prompt_kernel(*refs, t, n_sub):
    x_ref = refs[0]
    consts = refs[1:1 + N_PROJ_CONSTS]
    hm_ref, sink_ref, s0_ref, tri_ref, m_ref, bd_ref, gn_ref = refs[1 + N_PROJ_CONSTS:8 + N_PROJ_CONSTS]
    sk_out, sv_out, o_swa_ref, og_ref, sfin_ref = refs[8 + N_PROJ_CONSTS:13 + N_PROJ_CONSTS]
    sq_s, gq_s, gk_s, gv_s, gg_s, gr_s, sk_s, sv_s, st_ref, o_ref = refs[13 + N_PROJ_CONSTS:]
    c = pl.program_id(1)

    @pl.when(c == 0)
    def _():
        for ring in (sq_s, gq_s, gk_s, gv_s, gg_s, gr_s, sk_s, sv_s):
            ring[...] = jnp.zeros_like(ring)

    new2, old2 = lax.rem(c, 2), lax.rem(c + 1, 2)
    new3, old3, older3 = lax.rem(c, 3), lax.rem(c + 2, 3), lax.rem(c + 1, 3)
    half = pl.ds(TM // 2, TM // 2)

    def companion():
        _proj_tile(x_ref[...], *consts, sq_s.at[new2], sk_s.at[new3], sv_s.at[new3], gq_s.at[new2],
                   gk_s.at[new2], gv_s.at[new2], gg_s.at[new2], gr_s.at[new2])
        sk_out[...] = sk_s[new3]
        sv_out[...] = sv_s[new3]
        _swa_prompt_tile(c - 1, sq_s.at[old2], sk_s.at[older3, half], sk_s.at[old3], sv_s.at[older3, half],
                         sv_s.at[old3], hm_ref, sink_ref, o_swa_ref)

    _gla_step(c, gq_s.at[old2], gk_s.at[old2], gv_s.at[old2], gg_s.at[old2], gr_s.at[old2], s0_ref, tri_ref,
              m_ref, bd_ref, gn_ref, og_ref, sfin_ref, st_ref, o_ref, t=t, n_sub=n_sub, companion=companion)


def _gla_tokenwise(q_ref, k_ref, v_ref, g_ref, bd_ref, st_ref, o_ref, *, n_rows):
    row = lax.broadcasted_iota(I32, (GLA_SLOW_ROWS, 1), 0)

    def group(gi, carry):
        rows = pl.ds(pl.multiple_of(gi * GLA_SLOW_ROWS, GLA_SLOW_ROWS), GLA_SLOW_ROWS)
        q = q_ref[rows, :].astype(F32)
        k = k_ref[rows, :].astype(F32)
        v = v_ref[rows, :].astype(F32)
        decay = jnp.exp(g_ref[rows, :])
        o = jnp.zeros((GLA_SLOW_ROWS, N_HEADS_GLA * DV), F32)
        for j in range(GLA_SLOW_ROWS):
            only_j = row == j
            k_j = jnp.where(only_j, k, 0.0).astype(BF16)
            v_j = jnp.where(only_j, v, 0.0).astype(BF16)
            q_j = jnp.where(only_j, q, 0.0).astype(BF16)
            st = st_ref[...] * decay[j:j + 1, :] + _dot_tn(v_j, k_j) * bd_ref[...]
            st_ref[...] = st
            o = o + _dot_nt(q_j, st.astype(BF16))
        o_ref[rows, :] = o
        return carry

    lax.fori_loop(0, n_rows // GLA_SLOW_ROWS, group, 0)


def _gla_blocked(q_ref, k_ref, v_ref, g_ref, tri_ref, m_ref, bd_ref, st_ref, *, t, n_sub):
    nb = t // SUB
    n_rows = t * n_sub
    lanes = N_HEADS_GLA * DK

    def group_row(x, period, offset):
        g = x.reshape(n_rows // period, period, lanes)[:, offset:offset + 1, :]
        return jnp.broadcast_to(g, (n_rows // period, period, lanes)).reshape(n_rows, lanes)

    g_hi, g_lo = _split_bf16(g_ref[...])
    b = _dot(tri_ref[...], g_hi) + _dot(tri_ref[...], g_lo)
    q = q_ref[...].astype(F32)
    k = k_ref[...].astype(F32)
    qd = (q * jnp.exp(b - group_row(b, SUB, 0))).astype(BF16)
    pos = lax.broadcasted_iota(I32, (n_rows, lanes), 0) & (t - 1)
    k_parts = []
    for blk in range(nb):
        arg = jnp.where(pos < SUB * (blk + 1), group_row(b, t, SUB * blk) - b, NEG_BIG)
        k_parts.append((k * jnp.exp(arg)).astype(BF16))
    k_cat = jnp.concatenate(k_parts, axis=1)
    qd_cat = jnp.concatenate([qd] * nb, axis=1)
    q_dec = (q * jnp.exp(b)).astype(BF16)
    k_last = (k * jnp.exp(group_row(b, t, t - 1) - b)).astype(BF16)
    row_a = lax.broadcasted_iota(I32, (N_HEADS_GLA * t, t), 0) & (t - 1)
    col_a = lax.broadcasted_iota(I32, (N_HEADS_GLA * t, t), 1)

    o_intra, q_decayed, state_add, state_decay = [], [], [], []
    for u in range(n_sub):
        rows = slice(u * t, (u + 1) * t)
        v = v_ref[rows, :]
        lhs = jnp.concatenate([qd_cat[rows]] * N_HEADS_GLA, axis=0) * m_ref[...]
        a = _dot_nt(lhs, k_cat[rows])
        a = jnp.where(row_a >= col_a, a, 0.0).astype(BF16)
        o_full = _dot(a, v)
        o_intra.append(jnp.concatenate(
            [o_full[h * t:(h + 1) * t, h * DV:(h + 1) * DV] for h in range(N_HEADS_GLA)], axis=1))
        q_decayed.append(q_dec[rows])
        state_add.append(_dot_tn(v, k_last[rows]) * bd_ref[...])
        state_decay.append(jnp.exp(b[(u + 1) * t - 1:(u + 1) * t, :]))

    st = st_ref[...]
    o_parts = []
    for u in range(n_sub):
        o_parts.append(o_intra[u] + _dot_nt(q_decayed[u], st.astype(BF16)))
        st = st * state_decay[u] + state_add[u]
    st_ref[...] = st
    return jnp.concatenate(o_parts, axis=0) if n_sub > 1 else o_parts[0]


def _gla(gq, gk, gv, gg, gr, s0, tri, mask, bd, gn, *, t, n_sub, n_batch, n_steps, first_block, name):
    rows_per_step = t * n_sub

    def rows(width):
        return pl.BlockSpec((rows_per_step, width), lambda b, c: (first_block + b * n_steps + c, 0))

    def full(a):
        return pl.BlockSpec(a.shape, lambda b, c: (0,) * a.ndim)

    state = pl.BlockSpec((None,) + s0.shape[1:], lambda b, c: (b, 0, 0))
    return pl.pallas_call(
        functools.partial(_gla_kernel, t=t, n_sub=n_sub),
        grid=(n_batch, n_steps),
        in_specs=[rows(W_GQ), rows(W_GK), rows(W_GV), rows(W_GQ), rows(W_GR), state,
                  full(tri), full(mask), full(bd), full(gn)],
        out_specs=[pl.BlockSpec((rows_per_step, W_GV), lambda b, c: (b * n_steps + c, 0)), state],
        out_shape=[jax.ShapeDtypeStruct((n_batch * n_steps * rows_per_step, W_GV), BF16),
                   jax.ShapeDtypeStruct(s0.shape, F32)],
        scratch_shapes=[pltpu.VMEM(s0.shape[1:], F32), pltpu.VMEM((rows_per_step, W_GV), F32)],
        compiler_params=_cparams(2, "mixer"),
        name=name,
    )(gq, gk, gv, gg, gr, s0, tri, mask, bd, gn)


def _front_prompt(xp, consts, head_mask, sink_b, s0, tri, mask, bd, gn):
    def full(a):
        return pl.BlockSpec(a.shape, lambda b, c: (0,) * a.ndim)

    def computed(width):
        return pl.BlockSpec((TM, width), lambda b, c: (jnp.minimum(c, N_PROMPT_TILES - 1), 0))

    def mixed(width):
        return pl.BlockSpec((TM, width), lambda b, c: (jnp.maximum(c - 1, 0), 0))

    state = pl.BlockSpec((None,) + s0.shape[1:], lambda b, c: (b, 0, 0))
    ring2 = [pltpu.VMEM((2, TM, w), dt) for w, dt in
             (PROJ_OUTPUTS[0], PROJ_OUTPUTS[3], PROJ_OUTPUTS[4], PROJ_OUTPUTS[5], PROJ_OUTPUTS[6], PROJ_OUTPUTS[7])]
    ring3 = [pltpu.VMEM((3, TM, w), dt) for w, dt in (PROJ_OUTPUTS[1], PROJ_OUTPUTS[2])]
    return pl.pallas_call(
        functools.partial(_front_prompt_kernel, t=CHUNK, n_sub=TM // CHUNK),
        grid=(1, N_PROMPT_TILES + 1),
        in_specs=[computed(D_MODEL)] + [full(a) for a in consts] + [full(head_mask), full(sink_b), state,
                                                                   full(tri), full(mask), full(bd), full(gn)],
        out_specs=[computed(W_SK), computed(W_SV), mixed(W_SQ), mixed(W_GV), state],
        out_shape=[jax.ShapeDtypeStruct((N_PROMPT, W_SK), F32), jax.ShapeDtypeStruct((N_PROMPT, W_SV), F32),
                   jax.ShapeDtypeStruct((N_PROMPT, W_SQ), BF16), jax.ShapeDtypeStruct((N_PROMPT, W_GV), BF16),
                   jax.ShapeDtypeStruct(s0.shape, F32)],
        scratch_shapes=ring2 + ring3 + [pltpu.VMEM(s0.shape[1:], F32), pltpu.VMEM((TM, W_GV), F32)],
        compiler_params=_cparams(2, "front"),
        name="front_prompt",
    )(xp, *consts, head_mask, sink_b, s0, tri, mask, bd, gn)


def _tile_row_copies(hbm_ref, tile, vmem_ref, sem, to_hbm, rows=TM):
    copies = []
    for a in range(ROW_SUB):
        h = hbm_ref.at[pl.ds(tile * rows, rows), a, :]
        v = vmem_ref.at[:, pl.ds(a * LANES, LANES)]
        copies.append(pltpu.make_async_copy(v, h, sem) if to_hbm else pltpu.make_async_copy(h, v, sem))
    return copies


def _store_tile_rows(i, n_steps, outputs, row_buf, row_sem, rows=TM):
    buf_slot = lax.rem(i, 2)

    def store(j, tile, s):
        return _tile_row_copies(outputs[j][0], tile, row_buf.at[j, s], row_sem.at[j, s], True, rows)

    for j, (_, value) in enumerate(outputs):
        @pl.when(i >= 2)
        def _():
            for c in store(j, i - 2, buf_slot):
                c.wait()

        row_buf[j, buf_slot] = value
        for c in store(j, i, buf_slot):
            c.start()

        @pl.when(i == n_steps - 1)
        def _():
            for c in store(j, i - 1, 1 - buf_slot) + store(j, i, buf_slot):
                c.wait()


def _merge_kernel(oswp_ref, osws_ref, ogp_ref, ogs_ref, xp_ref, xs_ref, wo1_ref, wo2_ref, gf_ref, wrh_ref, wrl_ref,
                  br_ref, xrow_ref, hrow_ref, lg_ref, row_buf, row_sem):
    i = pl.program_id(0)
    is_prompt = i < N_PROMPT_TILES
    x = jnp.where(is_prompt, xp_ref[...], xs_ref[...])
    o_swa = jnp.where(is_prompt, oswp_ref[...], osws_ref[...])
    og = jnp.where(is_prompt, ogp_ref[...], ogs_ref[...])
    h = x + (_dot(o_swa, wo1_ref[...]) + _dot(og, wo2_ref[...]))
    xn = _rms(h) * gf_ref[...]
    x_hi, x_lo = _split_bf16(xn)
    logits = _dot(x_hi, wrh_ref[...]) + _dot(x_lo, wrh_ref[...]) + _dot(x_hi, wrl_ref[...]) + br_ref[...]
    lg_ref[...] = logits.T[:N_EXPERTS]
    _store_tile_rows(i, N_TILES, ((xrow_ref, xn), (hrow_ref, h)), row_buf, row_sem)


def _merge(o_swa_p, o_swa_s, og_p, og_s, xp, xs, wo1, wo2, gf, wrh, wrl, br):
    def prompt_rows(width):
        return pl.BlockSpec((TM, width), lambda i: (jnp.minimum(i, N_PROMPT_TILES - 1), 0))

    def sample_rows(width):
        return pl.BlockSpec((TM, width), lambda i: (0, 0))

    def full(a):
        return pl.BlockSpec(a.shape, lambda i: (0,) * a.ndim)

    consts = (wo1, wo2, gf, wrh, wrl, br)
    return pl.pallas_call(
        _merge_kernel,
        grid=(N_TILES,),
        in_specs=[prompt_rows(W_SQ), sample_rows(W_SQ), prompt_rows(W_GV), sample_rows(W_GV),
                  prompt_rows(D_MODEL), sample_rows(D_MODEL)] + [full(a) for a in consts],
        out_specs=[pl.BlockSpec(memory_space=pl.ANY)] * 2 + [pl.BlockSpec((N_EXPERTS, TM), lambda i: (0, i))],
        out_shape=[jax.ShapeDtypeStruct((N_ROWS,) + ROW_TILE, F32)] * 2 + [
                   jax.ShapeDtypeStruct((N_EXPERTS, N_ROWS), F32)],
        scratch_shapes=[pltpu.VMEM((2, 2, TM, D_MODEL), F32), pltpu.SemaphoreType.DMA((2, 2))],
        compiler_params=_cparams(1, "merge"),
        name="merge",
    )(o_swa_p, o_swa_s, og_p, og_s, xp, xs, *consts)


RT = 1280
N_ROUTER_STEPS = N_ROWS // RT


def _router_kernel(lg_ref, tri_ref, wrow_ref, ti_ref, rk_ref, cnt_ref, base_ref, row_buf, row_sem):
    i = pl.program_id(0)

    @pl.when(i == 0)
    def _():
        base_ref[...] = jnp.zeros_like(base_ref)

    logits_t = lg_ref[...]
    expert = lax.broadcasted_iota(I32, logits_t.shape, 0)
    slot = lax.broadcasted_iota(I32, (SLOT_ROWS, RT), 0)
    vals, hots = [], []
    ti = jnp.zeros((SLOT_ROWS, RT), I32)
    for kk in range(TOP_K):
        m = jnp.max(logits_t, axis=0, keepdims=True)
        idx = jnp.min(jnp.where(logits_t == m, expert, N_EXPERTS), axis=0, keepdims=True)
        hot = expert == idx
        logits_t = jnp.where(hot, NEG_BIG, logits_t)
        vals.append(m)
        hots.append(hot)
        ti = jnp.where(slot == kk, idx, ti)
    ti_ref[...] = ti
    exps = [jnp.exp(v - vals[0]) for v in vals]
    den = exps[0] + exps[1] + exps[2] + exps[3]
    tw_t = jnp.zeros((SLOT_ROWS, RT), F32)
    for kk in range(TOP_K):
        tw_t = jnp.where(slot == kk, exps[kk] / den, tw_t)
    eye = (lax.broadcasted_iota(I32, (SLOT_ROWS, LANES), 0)
           == lax.broadcasted_iota(I32, (SLOT_ROWS, LANES), 1)).astype(F32).astype(BF16)
    w_hi = tw_t.astype(BF16)
    w_mid, w_lo = _split_bf16(tw_t - w_hi.astype(F32))
    tw_col = _dot_tn(w_hi, eye) + _dot_tn(w_mid, eye) + _dot_tn(w_lo, eye)

    onehot_t = jnp.zeros(logits_t.shape, F32)
    for hot in hots:
        onehot_t = onehot_t + jnp.where(hot, 1.0, 0.0)
    before_t = _dot(onehot_t.astype(BF16), tri_ref[...]) + base_ref[:, 0:1]
    rk = jnp.zeros((SLOT_ROWS, RT), I32)
    for kk in range(TOP_K):
        r = jnp.sum(jnp.where(hots[kk], before_t, 0.0), axis=0, keepdims=True).astype(I32)
        rk = jnp.where(slot == kk, r, rk)
    rk_ref[...] = rk
    total = base_ref[...] + jnp.sum(onehot_t, axis=1, keepdims=True)
    base_ref[...] = total
    cnt_ref[...] = total.astype(I32)

    w_lanes = [jnp.broadcast_to(tw_col[:, kk:kk + 1], (RT, LANES)) for kk in range(TOP_K)]
    w_rows = jnp.concatenate(w_lanes + [jnp.zeros((RT, D_MODEL - TOP_K * LANES), F32)], axis=1)
    _store_tile_rows(i, N_ROUTER_STEPS, ((wrow_ref, w_rows),), row_buf, row_sem, RT)


def _router(logits, tri):
    return pl.pallas_call(
        _router_kernel,
        grid=(N_ROUTER_STEPS,),
        in_specs=[pl.BlockSpec((N_EXPERTS, RT), lambda i: (0, i)), pl.BlockSpec(tri.shape, lambda i: (0, 0))],
        out_specs=[pl.BlockSpec(memory_space=pl.ANY),
                   pl.BlockSpec((SLOT_ROWS, RT), lambda i: (0, i)),
                   pl.BlockSpec((SLOT_ROWS, RT), lambda i: (0, i)),
                   pl.BlockSpec((N_EXPERTS, LANES), lambda i: (0, 0))],
        out_shape=[jax.ShapeDtypeStruct((N_ROWS,) + ROW_TILE, F32),
                   jax.ShapeDtypeStruct((SLOT_ROWS, N_ROWS), I32),
                   jax.ShapeDtypeStruct((SLOT_ROWS, N_ROWS), I32),
                   jax.ShapeDtypeStruct((N_EXPERTS, LANES), I32)],
        scratch_shapes=[pltpu.VMEM((N_EXPERTS, LANES), F32), pltpu.VMEM((1, 2, RT, D_MODEL), F32),
                        pltpu.SemaphoreType.DMA((1, 2))],
        compiler_params=_cparams(1, "router"),
        name="router",
    )(logits, tri)


ISSUE_UNROLL = 4


def _row_copy(src_ref, src_row, dst_ref, dst_row, sem):
    return pltpu.make_async_copy(src_ref.at[pl.ds(src_row, 1)], dst_ref.at[pl.ds(dst_row, 1)], sem)


def _dispatch_kernel(dest_ref, end_ref, x_ref, xs_ref, zero_ref, sem, zsem):
    i = pl.program_id(0)
    base = i * TM

    @pl.when(i == 0)
    def _():
        zero_ref[...] = jnp.zeros_like(zero_ref)

        def tail_copy(e):
            last = jnp.maximum(end_ref[e] - TM, 0)
            return pltpu.make_async_copy(zero_ref, xs_ref.at[pl.ds(pl.multiple_of(last, TM), TM)], zsem)

        def fill(e, carry):
            tail_copy(e).start()
            return carry

        def fill_wait(e, carry):
            tail_copy(e).wait()
            return carry

        lax.fori_loop(0, N_EXPERTS, fill, 0)
        lax.fori_loop(0, N_EXPERTS, fill_wait, 0)

        def unused_copy(t):
            return pltpu.make_async_copy(zero_ref, xs_ref.at[pl.ds(pl.multiple_of(t * TM, TM), TM)], zsem)

        def fill_unused(t, carry):
            unused_copy(t).start()
            unused_copy(t).wait()
            return carry

        lax.fori_loop(end_ref[N_EXPERTS - 1] // TM, N_EXPERT_TILES, fill_unused, 0)

    def issue(n, carry):
        for kk in range(TOP_K):
            _row_copy(x_ref, n, xs_ref, dest_ref[kk * N_ROWS + base + n], sem).start(priority=kk % 2)
        return carry

    lax.fori_loop(0, TM, issue, 0, unroll=ISSUE_UNROLL)

    for kk in range(TOP_K):
        pltpu.make_async_copy(x_ref, xs_ref.at[pl.ds(0, TM)], sem).wait()


def _dispatch(dest, end, x_packed):
    return pl.pallas_call(
        _dispatch_kernel,
        grid_spec=pltpu.PrefetchScalarGridSpec(
            num_scalar_prefetch=2,
            grid=(N_TILES,),
            in_specs=[pl.BlockSpec((TM,) + ROW_TILE, lambda i, d, e: (i, 0, 0))],
            out_specs=pl.BlockSpec(memory_space=pl.ANY),
            scratch_shapes=[pltpu.VMEM((TM,) + ROW_TILE, F32), pltpu.SemaphoreType.DMA,
                            pltpu.SemaphoreType.DMA],
        ),
        out_shape=jax.ShapeDtypeStruct((N_SORTED_ROWS,) + ROW_TILE, F32),
        compiler_params=_cparams(1, "dispatch"),
        name="dispatch",
    )(dest, end, x_packed)


CAST_ROWS = 128


def _moe_kernel(te_ref, nu_ref, nx_ref, xs_ref, wg_ref, bg_ref, wu_ref, bu_ref, wd_ref, bd_ref, ys_ref,
                w_stage, w_bf, x_buf, y_buf, zero_buf, w_sem, in_sem, out_sem, zero_sem):
    t = pl.program_id(0)
    n_used = nu_ref[0]
    slot = lax.rem(t, 2)
    e = te_ref[t]
    e_prev = te_ref[jnp.maximum(t - 1, 0)]

    def load(tile, s):
        return _tile_row_copies(xs_ref, tile, x_buf.at[s], in_sem.at[s], to_hbm=False)

    def store(tile, s):
        return _tile_row_copies(ys_ref, tile, y_buf.at[s], out_sem.at[s], to_hbm=True)

    def weight_copies(expert):
        return [pltpu.make_async_copy(w.at[expert], w_stage.at[j], w_sem.at[j])
                for j, w in enumerate((wg_ref, wu_ref, wd_ref))]

    @pl.when(t == 0)
    def _():
        for c in weight_copies(e) + load(0, 0):
            c.start()

    @pl.when(t + 1 < n_used)
    def _():
        for c in load(t + 1, 1 - slot):
            c.start()

    @pl.when((t == 0) | (e != e_prev))
    def _():
        for c in weight_copies(e):
            c.wait()

        def cast(r, carry):
            sl = pl.ds(pl.multiple_of(r * CAST_ROWS, CAST_ROWS), CAST_ROWS)
            for j in range(3):
                w_bf[j, sl, :] = w_stage[j, sl, :].astype(BF16)
            return carry

        lax.fori_loop(0, D_MODEL // CAST_ROWS, cast, 0)
        e_next = nx_ref[e]

        @pl.when(e_next >= 0)
        def _():
            for c in weight_copies(e_next):
                c.start()

    @pl.when(t < n_used)
    def _():
        for c in load(t, slot):
            c.wait()

        @pl.when(t >= 2)
        def _():
            for c in store(t - 2, slot):
                c.wait()

        x = x_buf[slot].astype(BF16)
        gate = jnp.minimum(_dot(x, w_bf[0]) + bg_ref[...], SWIGLU_LIMIT)
        up = jnp.clip(_dot(x, w_bf[1]) + bu_ref[...], -SWIGLU_LIMIT, SWIGLU_LIMIT)
        hdn = (up + 1.0) * gate * (1.0 / (1.0 + jnp.exp(-SWIGLU_ALPHA * gate)))
        y_buf[slot] = _dot(hdn.astype(BF16), w_bf[2]) + bd_ref[...]
        for c in store(t, slot):
            c.start()

    @pl.when(t >= n_used)
    def _():
        zero_buf[...] = jnp.zeros_like(zero_buf)
        fill = pltpu.make_async_copy(zero_buf, ys_ref.at[pl.ds(t * TM, TM)], zero_sem)
        fill.start()
        fill.wait()

    @pl.when(t == N_EXPERT_TILES - 1)
    def _():
        @pl.when(n_used >= 2)
        def _():
            for c in store(n_used - 2, lax.rem(n_used, 2)):
                c.wait()

        for c in store(n_used - 1, lax.rem(n_used - 1, 2)):
            c.wait()


def _moe(tile_expert, n_used, next_expert, xs, w_gate, b_gate, w_up, b_up, w_down, b_down):
    hbm = pl.BlockSpec(memory_space=pl.ANY)
    bias = pl.BlockSpec((None, 1, D_MODEL), lambda t, te, nu, nx: (te[t], 0, 0))
    return pl.pallas_call(
        _moe_kernel,
        grid_spec=pltpu.PrefetchScalarGridSpec(
            num_scalar_prefetch=3,
            grid=(N_EXPERT_TILES,),
            in_specs=[hbm, hbm, bias, hbm, bias, hbm, bias],
            out_specs=hbm,
            scratch_shapes=[pltpu.VMEM((3, D_MODEL, D_MODEL), F32), pltpu.VMEM((3, D_MODEL, D_MODEL), BF16),
                            pltpu.VMEM((2, TM, D_MODEL), F32), pltpu.VMEM((2, TM, D_MODEL), F32),
                            pltpu.VMEM((TM,) + ROW_TILE, F32), pltpu.SemaphoreType.DMA((3,)),
                            pltpu.SemaphoreType.DMA((2,)), pltpu.SemaphoreType.DMA((2,)),
                            pltpu.SemaphoreType.DMA],
        ),
        out_shape=jax.ShapeDtypeStruct((N_SORTED_ROWS,) + ROW_TILE, F32),
        compiler_params=_cparams(1, "experts"),
        name="experts",
    )(tile_expert, n_used, next_expert, xs, w_gate, b_gate, w_up, b_up, w_down, b_down)


def _combine_kernel(dest_ref, h_ref, w_ref, ys_ref, yp_ref, ysm_ref, g_ref, out_ref, sem):
    i = pl.program_id(0)
    slot = lax.rem(i, 2)

    def issue_tile(tile, s):
        base = tile * TM

        def issue(n, carry):
            for kk in range(TOP_K):
                _row_copy(ys_ref, dest_ref[kk * N_ROWS + base + n], g_ref.at[s], kk * TM + n,
                          sem.at[s]).start(priority=kk % 2)
            return carry

        lax.fori_loop(0, TM, issue, 0, unroll=ISSUE_UNROLL)

    @pl.when(i == 0)
    def _():
        issue_tile(0, 0)

    @pl.when(i + 1 < N_TILES)
    def _():
        issue_tile(i + 1, 1 - slot)

    for kk in range(TOP_K):
        pltpu.make_async_copy(ys_ref.at[pl.ds(0, TM)], g_ref.at[slot, pl.ds(kk * TM, TM)], sem.at[slot]).wait()

    w = w_ref[...]
    acc = None
    for kk in range(TOP_K):
        part = jnp.broadcast_to(w[:, kk:kk + 1, :], (TM,) + ROW_TILE) * g_ref[slot, kk * TM:(kk + 1) * TM]
        acc = part if acc is None else acc + part
    out_ref[...] = h_ref[...] + acc

    def write(y_ref):
        for a in range(ROW_SUB):
            y_ref[:, a * LANES:(a + 1) * LANES] = out_ref[:, a, :]

    @pl.when(i < N_PROMPT_TILES)
    def _():
        write(yp_ref)

    @pl.when(i >= N_PROMPT_TILES)
    def _():
        write(ysm_ref)


def _combine(dest, h_rows, w_rows, ys):
    tile = lambda index: pl.BlockSpec((TM,) + ROW_TILE, index)
    return pl.pallas_call(
        _combine_kernel,
        grid_spec=pltpu.PrefetchScalarGridSpec(
            num_scalar_prefetch=1,
            grid=(N_TILES,),
            in_specs=[tile(lambda i, d: (i, 0, 0)), tile(lambda i, d: (i, 0, 0)),
                      pl.BlockSpec(memory_space=pl.ANY)],
            out_specs=[pl.BlockSpec((TM, D_MODEL), lambda i, d: (jnp.minimum(i, N_PROMPT_TILES - 1), 0)),
                       pl.BlockSpec((TM, D_MODEL), lambda i, d: (0, 0))],
            scratch_shapes=[pltpu.VMEM((2, TOP_K * TM) + ROW_TILE, F32), pltpu.VMEM((TM,) + ROW_TILE, F32),
                            pltpu.SemaphoreType.DMA((2,))],
        ),
        out_shape=[jax.ShapeDtypeStruct((N_PROMPT, D_MODEL), F32),
                   jax.ShapeDtypeStruct((N_SAMPLE, D_MODEL), F32)],
        compiler_params=_cparams(1, "combine"),
        name="combine",
    )(dest, h_rows, w_rows, ys)


def _block_diag_ones(n, blk):
    idx = np.arange(n) // blk
    return (idx[:, None] == idx[None, :]).astype(np.float32)


def _swa_head_mask(tq):
    row_head = np.arange(4 * tq)[:, None] // tq
    lane_head = np.arange(4 * HD)[None, :] // HD
    return jnp.asarray((row_head == lane_head).astype(np.float32), BF16)


def _gla_masks(t, n_sub):
    nb = t // SUB
    lanes = N_HEADS_GLA * DK
    tri = jnp.asarray(np.kron(np.eye(n_sub, dtype=np.float32), np.tril(np.ones((t, t), np.float32))), BF16)
    row = np.arange(N_HEADS_GLA * t)
    col = np.arange(nb * lanes)
    same_head = (row[:, None] // t) == ((col[None, :] % lanes) // DK)
    same_blk = ((row[:, None] % t) // SUB) == (col[None, :] // lanes)
    mask = jnp.asarray((same_head & same_blk).astype(np.float32), BF16)
    return tri, mask


def _sink_rows(sinks, tq):
    s = jnp.repeat(sinks.astype(F32).reshape(2, 4), tq, axis=1)
    return jnp.broadcast_to(s[:, None, :], (2, 8, 4 * tq))


def kernel(x_prompt, x_sample, state_gla, cache_swa_k, cache_swa_v, norm_mix_g, w_in, w_gla_a2, b_gla_a, q_norm_g,
           k_norm_g, swa_sinks, gla_norm_g, w_out, norm_ffn_g, w_router, b_router, w_gate, b_gate, w_up, b_up,
           w_down, b_down):
    xp = x_prompt.reshape(N_PROMPT, D_MODEL)
    xs = x_sample.reshape(N_SAMPLE, D_MODEL)

    w_in0 = w_in[0]
    w_main = w_in0[:, :W_MAIN].astype(BF16)
    w_ga = jnp.pad(w_in0[:, OFF_GA:], ((0, 0), (0, LANES - W_GA))).astype(BF16)
    w_a2 = jnp.pad(w_gla_a2[0], ((0, LANES - W_GA), (0, 0))).astype(BF16)
    b_a = b_gla_a[0].reshape(1, -1)
    qg = (jnp.tile(q_norm_g[0], N_HEADS_SWA) * (HD ** -0.5)).reshape(1, -1)
    kg = jnp.tile(k_norm_g[0], 2).reshape(1, -1)
    bdq = jnp.asarray(_block_diag_ones(W_SQ, HD), BF16)
    bdk = jnp.asarray(_block_diag_ones(W_SK, HD), BF16)

    proj_consts = (norm_mix_g[0].reshape(1, -1), w_main, w_ga, w_a2, b_a, qg, kg, bdq, bdk)
    sq, sk, sv, gq, gk, gv, gg, gr = _proj_sample(xs, proj_consts)

    cache_k = cache_swa_k[0].reshape(N_STREAMS, WINDOW, 2 * HD)
    cache_v = cache_swa_v[0].reshape(N_STREAMS, WINDOW, 2 * HD)
    o_swa_s = _swa_sample(sq, sk, sv, cache_k, cache_v, _swa_head_mask(T_SAMPLE),
                          _sink_rows(swa_sinks[0], T_SAMPLE))

    bd_state = jnp.asarray(_block_diag_ones(N_HEADS_GLA, 1).repeat(DV, axis=0).repeat(DK, axis=1), F32)
    gn = gla_norm_g[0].reshape(1, -1)
    tri_p, mask_p = _gla_masks(CHUNK, TM // CHUNK)
    tri_s, mask_s = _gla_masks(T_SAMPLE, 1)
    s0_p = jnp.zeros((1, N_HEADS_GLA * DV, N_HEADS_GLA * DK), F32)
    sk_p, sv_p, o_swa_p, og_p, sfin_p = _front_prompt(xp, proj_consts, _swa_head_mask(CHUNK),
                                                      _sink_rows(swa_sinks[0], CHUNK), s0_p, tri_p, mask_p,
                                                      bd_state, gn)
    eye = jnp.eye(N_HEADS_GLA, dtype=F32)
    s0_s = jnp.einsum('bhde,hg->bhegd', state_gla[0].astype(F32), eye).reshape(
        N_STREAMS, N_HEADS_GLA * DV, N_HEADS_GLA * DK)
    og_s, sfin_s = _gla(gq, gk, gv, gg, gr, s0_s, tri_s, mask_s, bd_state, gn, t=T_SAMPLE, n_sub=1,
                        n_batch=N_STREAMS, n_steps=1, first_block=0, name="gla_sample")

    def unpack_state(sfin):
        s = sfin.reshape(-1, N_HEADS_GLA, DV, N_HEADS_GLA, DK)
        s = jnp.stack([s[:, h, :, h, :] for h in range(N_HEADS_GLA)], axis=1)
        return jnp.transpose(s, (0, 1, 3, 2))[None]

    w_out0 = w_out[0].astype(BF16)
    wr = jnp.pad(w_router[0], ((0, 0), (0, LANES - N_EXPERTS)))
    wr_hi = wr.astype(BF16)
    wr_lo = (wr - wr_hi.astype(F32)).astype(BF16)
    br = jnp.pad(b_router[0].astype(F32), (0, LANES - N_EXPERTS)).reshape(1, -1)
    x_rows, h_rows, logits = _merge(o_swa_p, o_swa_s, og_p, og_s, xp, xs, w_out0[:W_SQ], w_out0[W_SQ:],
                                    norm_ffn_g[0].reshape(1, -1), wr_hi, wr_lo, br)
    earlier = jnp.asarray(np.triu(np.ones((RT, RT), np.float32), 1), BF16)
    w_rows, top_i, rank, counts = _router(logits, earlier)

    counts = counts[:, 0]
    padded = (counts + TM - 1) // TM * TM
    end = jnp.cumsum(padded)
    start = end - padded
    experts = jnp.arange(N_EXPERTS, dtype=I32)
    is_e = top_i[:TOP_K, :, None] == experts
    dest = (rank[:TOP_K] + jnp.sum(jnp.where(is_e, start, 0), axis=-1)).reshape(-1).astype(I32)
    n_used = (end[-1] // TM).astype(I32)
    tiles = jnp.minimum(jnp.arange(N_EXPERT_TILES, dtype=I32), n_used - 1)
    tile_expert = jnp.sum((tiles[:, None] * TM >= end[None, :]).astype(I32), axis=1)
    later_nonempty = (experts[None, :] > experts[:, None]) & (padded[None, :] > 0)
    next_expert = jnp.min(jnp.where(later_nonempty, experts[None, :], N_EXPERTS), axis=1)
    next_expert = jnp.where(next_expert < N_EXPERTS, next_expert, -1).astype(I32)

    xs_sorted = _dispatch(dest, end.astype(I32), x_rows)
    ys = _moe(tile_expert, n_used.reshape(1), next_expert, xs_sorted, w_gate[0], b_gate[0].reshape(N_EXPERTS, 1, -1),
              w_up[0], b_up[0].reshape(N_EXPERTS, 1, -1), w_down[0], b_down[0].reshape(N_EXPERTS, 1, -1))
    y_p, y_s = _combine(dest, h_rows, w_rows, ys)

    sk_s = sk.reshape(N_STREAMS, T_SAMPLE, 2 * HD)
    sv_s = sv.reshape(N_STREAMS, T_SAMPLE, 2 * HD)
    kc_s = jnp.concatenate([cache_k[:, T_SAMPLE:], sk_s], axis=1).reshape(1, N_STREAMS, WINDOW, 2, HD)
    vc_s = jnp.concatenate([cache_v[:, T_SAMPLE:], sv_s], axis=1).reshape(1, N_STREAMS, WINDOW, 2, HD)
    kc_p = sk_p[N_PROMPT - WINDOW:].reshape(1, 1, WINDOW, 2, HD)
    vc_p = sv_p[N_PROMPT - WINDOW:].reshape(1, 1, WINDOW, 2, HD)
    return (y_p.reshape(1, N_PROMPT, D_MODEL), y_s.reshape(N_STREAMS, T_SAMPLE, D_MODEL),
            unpack_state(sfin_p), kc_p, vc_p, unpack_state(sfin_s), kc_s, vc_s)
```

```python
import functools

import numpy as np
import jax
import jax.numpy as jnp
from jax import lax
from jax.experimental import pallas as pl
from jax.experimental.pallas import tpu as pltpu

F32 = jnp.float32
BF16 = jnp.bfloat16
I32 = jnp.int32

D_MODEL = 1024
N_PROMPT = 16384
N_STREAMS = 8
T_SAMPLE = 32
N_SAMPLE = N_STREAMS * T_SAMPLE
N_ROWS = N_PROMPT + N_SAMPLE
EPS = 1e-6

CHUNK = 64
SUB = 16
N_HEADS_SWA = 8
HD = 64
WINDOW = 128
N_HEADS_GLA = 4
DK = 64
DV = 128
GLA_TAU = 16.0
N_EXPERTS = 32
TOP_K = 4
SWIGLU_ALPHA = 1.702
SWIGLU_LIMIT = 7.0

TM = 256
N_TILES = N_ROWS // TM
N_PROMPT_TILES = N_PROMPT // TM
N_ASSIGN = N_ROWS * TOP_K
N_EXPERT_TILES = N_ASSIGN // TM + N_EXPERTS
N_SORTED_ROWS = N_EXPERT_TILES * TM
LANES = 128
SLOT_ROWS = 16
NEG_BIG = -1e30

W_SQ, W_SK, W_SV, W_GQ, W_GK, W_GV, W_GR, W_GA = 512, 128, 128, 256, 256, 512, 512, 16
OFF_SQ = 0
OFF_SK = OFF_SQ + W_SQ
OFF_SV = OFF_SK + W_SK
OFF_GQ = OFF_SV + W_SV
OFF_GK = OFF_GQ + W_GQ
OFF_GV = OFF_GK + W_GK
OFF_GR = OFF_GV + W_GV
OFF_GA = OFF_GR + W_GR
W_MAIN = OFF_GA


VMEM_MIB = dict(proj=40, front=48, mixer=32, merge=32, router=48, dispatch=32, experts=52, combine=40)


def _cparams(n_grid_axes, call):
    return pltpu.CompilerParams(dimension_semantics=("arbitrary",) * n_grid_axes,
                                vmem_limit_bytes=VMEM_MIB[call] * 2 ** 20)


def _dot(a, b):
    return jnp.dot(a, b, preferred_element_type=F32)


def _dot_nt(a, b):
    return lax.dot_general(a, b, (((1,), (1,)), ((), ())), preferred_element_type=F32)


def _dot_tn(a, b):
    return lax.dot_general(a, b, (((0,), (0,)), ((), ())), preferred_element_type=F32)


def _split_bf16(x):
    hi = x.astype(BF16)
    lo = (x - hi.astype(F32)).astype(BF16)
    return hi, lo


def _rms(x):
    return x * lax.rsqrt(jnp.mean(x * x, axis=-1, keepdims=True) + EPS)


ROW_SUB = 8
ROW_TILE = (ROW_SUB, LANES)


PROJ_OUTPUTS = ((W_SQ, BF16), (W_SK, F32), (W_SV, F32), (W_GQ, BF16), (W_GK, BF16), (W_GV, BF16),
                (W_GQ, F32), (W_GR, BF16))
N_PROJ_CONSTS = 9


def _proj_tile(x, g_ref, w_ref, wga_ref, wa2_ref, ba_ref, qg_ref, kg_ref, bdq_ref, bdk_ref,
               sq_ref, sk_ref, sv_ref, gq_ref, gk_ref, gv_ref, gg_ref, gr_ref):
    xb = (_rms(x) * g_ref[...]).astype(BF16)

    def seg(off, width):
        return _dot(xb, w_ref[:, off:off + width])

    def head_norm(u, bd_ref):
        hi, lo = _split_bf16(u * u)
        ss = _dot(hi, bd_ref[...]) + _dot(lo, bd_ref[...])
        return u * lax.rsqrt(ss * (1.0 / HD) + EPS)

    sq_ref[...] = (head_norm(seg(OFF_SQ, W_SQ), bdq_ref) * qg_ref[...]).astype(BF16)
    sk_ref[...] = head_norm(seg(OFF_SK, W_SK), bdk_ref) * kg_ref[...]
    sv_ref[...] = seg(OFF_SV, W_SV)
    gq_ref[...] = (seg(OFF_GQ, W_GQ) * (DK ** -0.5)).astype(BF16)
    gk_ref[...] = seg(OFF_GK, W_GK).astype(BF16)
    gv_ref[...] = seg(OFF_GV, W_GV).astype(BF16)
    gr_ref[...] = seg(OFF_GR, W_GR).astype(BF16)
    ga = _dot(xb, wga_ref[...]).astype(BF16)
    z = _dot(ga, wa2_ref[...]) + ba_ref[...]
    log_sig = jnp.minimum(z, 0.0) - jnp.log(1.0 + jnp.exp(-jnp.abs(z)))
    gg_ref[...] = log_sig * (1.0 / GLA_TAU)


def _proj_sample_kernel(x_ref, *refs):
    _proj_tile(x_ref[...], *refs)


def _proj_sample(xs, consts):
    def full(a):
        return pl.BlockSpec(a.shape, lambda i: (0,) * a.ndim)

    return pl.pallas_call(
        _proj_sample_kernel,
        grid=(1,),
        in_specs=[full(xs)] + [full(a) for a in consts],
        out_specs=[pl.BlockSpec((N_SAMPLE, w), lambda i: (0, 0)) for w, _ in PROJ_OUTPUTS],
        out_shape=[jax.ShapeDtypeStruct((N_SAMPLE, w), dt) for w, dt in PROJ_OUTPUTS],
        compiler_params=_cparams(1, "proj"),
        name="proj_sample",
    )(xs, *consts)


def _dup_kv_heads(x):
    r = pltpu.roll(x, HD, axis=1)
    lo = lax.broadcasted_iota(I32, x.shape, 1) < HD
    out = []
    for a in (jnp.where(lo, x, r), jnp.where(lo, r, x)):
        out.append(jnp.concatenate([a, a], axis=1).astype(BF16))
    return out


def _swa_blocks(q_blocks, k_blocks, v_blocks, sinks, valids, head_mask):
    tq = q_blocks[0].shape[0]
    n_q = 4 * tq
    scores = []
    for q, k, valid in zip(q_blocks, k_blocks, valids):
        s_t = _dot_nt(k, jnp.concatenate([q] * 4, axis=0) * head_mask)
        scores.append(s_t if valid is None else jnp.where(valid, s_t, -jnp.inf))
    eye = (lax.broadcasted_iota(I32, (n_q, n_q), 0) == lax.broadcasted_iota(I32, (n_q, n_q), 1)
           ).astype(F32).astype(BF16)
    probs = []
    for s_t, sink in zip(scores, sinks):
        m = jnp.maximum(jnp.max(s_t, axis=0, keepdims=True), sink)
        p_t = jnp.exp(s_t - m)
        den = jnp.sum(p_t, axis=0, keepdims=True) + jnp.exp(sink - m)
        probs.append(_dot_nt(eye, (p_t / den).astype(BF16)).astype(BF16))
    lane_head = lax.broadcasted_iota(I32, (tq, 4 * HD), 1) // HD
    outs = []
    for p, v in zip(probs, v_blocks):
        o_full = _dot(p, v)
        o = jnp.zeros((tq, 4 * HD), F32)
        for a in range(4):
            o = o + jnp.where(lane_head == a, o_full[a * tq:(a + 1) * tq], 0.0)
        outs.append(o)
    return outs


def _swa_prompt_tile(i, q_ref, kp_ref, kc_ref, vp_ref, vc_ref, hm_ref, sink_ref, o_ref):
    k_dup = _dup_kv_heads(jnp.concatenate([kp_ref[...], kc_ref[...]], axis=0))
    v_dup = _dup_kv_heads(jnp.concatenate([vp_ref[...], vc_ref[...]], axis=0))
    sink_row = [sink_ref[j][0:1, :] for j in range(2)]
    span = WINDOW + CHUNK
    key = lax.broadcasted_iota(I32, (span, 4 * CHUNK), 0)
    qs, ks, vs, sinks, valids, where = [], [], [], [], [], []
    for c in range(TM // CHUNK):
        lo = CHUNK * c
        valid = (i * TM - WINDOW + lo + key) >= 0
        for j in range(2):
            qs.append(q_ref[lo:lo + CHUNK, 4 * HD * j:4 * HD * (j + 1)])
            ks.append(k_dup[j][lo:lo + span])
            vs.append(v_dup[j][lo:lo + span])
            sinks.append(sink_row[j])
            valids.append(valid)
            where.append((lo, j))
    outs = _swa_blocks(qs, ks, vs, sinks, valids, hm_ref[...])
    for (lo, j), o in zip(where, outs):
        o_ref[lo:lo + CHUNK, 4 * HD * j:4 * HD * (j + 1)] = o.astype(BF16)


def _swa_sample_kernel(q_ref, kc_ref, kn_ref, vc_ref, vn_ref, hm_ref, sink_ref, o_ref):
    k_dup = _dup_kv_heads(jnp.concatenate([kc_ref[...], kn_ref[...]], axis=0))
    v_dup = _dup_kv_heads(jnp.concatenate([vc_ref[...], vn_ref[...]], axis=0))
    sink_row = [sink_ref[j][0:1, :] for j in range(2)]
    qs = [q_ref[:, 4 * HD * j:4 * HD * (j + 1)] for j in range(2)]
    outs = _swa_blocks(qs, k_dup, v_dup, sink_row, [None, None], hm_ref[...])
    o_ref[...] = jnp.concatenate(outs, axis=1).astype(BF16)


def _swa_sample(sq, sk, sv, cache_k, cache_v, head_mask, sink_b):
    new = lambda width: pl.BlockSpec((T_SAMPLE, width), lambda b: (b, 0))
    cache = pl.BlockSpec((None, WINDOW, 2 * HD), lambda b: (b, 0, 0))
    return pl.pallas_call(
        _swa_sample_kernel,
        grid=(N_STREAMS,),
        in_specs=[new(W_SQ), cache, new(2 * HD), cache, new(2 * HD),
                  pl.BlockSpec(head_mask.shape, lambda b: (0, 0)),
                  pl.BlockSpec(sink_b.shape, lambda b: (0, 0, 0))],
        out_specs=pl.BlockSpec((T_SAMPLE, W_SQ), lambda b: (b, 0)),
        out_shape=jax.ShapeDtypeStruct((N_SAMPLE, W_SQ), BF16),
        compiler_params=_cparams(1, "mixer"),
        name="swa_sample",
    )(sq, cache_k, sk, cache_v, sv, head_mask, sink_b)


GLA_SAFE_EXP = 80.0
GLA_SLOW_ROWS = 16


def _gla_step(c, q_ref, k_ref, v_ref, g_ref, gr_ref, s0_ref, tri_ref, m_ref, bd_ref, gn_ref,
              og_ref, sfin_ref, st_ref, o_ref, *, t, n_sub, companion=None):
    @pl.when(c == 0)
    def _():
        st_ref[...] = s0_ref[...]

    n_rows = t * n_sub
    blocked_is_safe = (SUB - 1) * jnp.max(-g_ref[...]) <= GLA_SAFE_EXP

    def emit(o):
        gr = gr_ref[...].astype(F32)
        gate = gr / (1.0 + jnp.exp(-gr))
        outs = []
        for h in range(N_HEADS_GLA):
            sl = slice(h * DV, (h + 1) * DV)
            outs.append(_rms(o[:, sl]) * gn_ref[...] * gate[:, sl])
        og_ref[...] = jnp.concatenate(outs, axis=1).astype(BF16)

    @pl.when(blocked_is_safe)
    def _():
        if companion is not None:
            companion()
        emit(_gla_blocked(q_ref, k_ref, v_ref, g_ref, tri_ref, m_ref, bd_ref, st_ref, t=t, n_sub=n_sub))

    @pl.when(jnp.logical_not(blocked_is_safe))
    def _():
        if companion is not None:
            companion()
        _gla_tokenwise(q_ref, k_ref, v_ref, g_ref, bd_ref, st_ref, o_ref, n_rows=n_rows)
        emit(o_ref[...])

    @pl.when(c == pl.num_programs(1) - 1)
    def _():
        sfin_ref[...] = st_ref[...]


def _gla_kernel(*refs, t, n_sub):
    _gla_step(pl.program_id(1), *refs, t=t, n_sub=n_sub)


def _front_prompt_kernel(*refs, t, n_sub):
    x_ref = refs[0]
    consts = refs[1:1 + N_PROJ_CONSTS]
    hm_ref, sink_ref, s0_ref, tri_ref, m_ref, bd_ref, gn_ref = refs[1 + N_PROJ_CONSTS:8 + N_PROJ_CONSTS]
    sk_out, sv_out, o_swa_ref, og_ref, sfin_ref = refs[8 + N_PROJ_CONSTS:13 + N_PROJ_CONSTS]
    sq_s, gq_s, gk_s, gv_s, gg_s, gr_s, sk_s, sv_s, st_ref, o_ref = refs[13 + N_PROJ_CONSTS:]
    c = pl.program_id(1)

    @pl.when(c == 0)
    def _():
        for ring in (sq_s, gq_s, gk_s, gv_s, gg_s, gr_s, sk_s, sv_s):
            ring[...] = jnp.zeros_like(ring)

    new2, old2 = lax.rem(c, 2), lax.rem(c + 1, 2)
    new3, old3, older3 = lax.rem(c, 3), lax.rem(c + 2, 3), lax.rem(c + 1, 3)
    half = pl.ds(TM // 2, TM // 2)

    def companion():
        _proj_tile(x_ref[...], *consts, sq_s.at[new2], sk_s.at[new3], sv_s.at[new3], gq_s.at[new2],
                   gk_s.at[new2], gv_s.at[new2], gg_s.at[new2], gr_s.at[new2])
        sk_out[...] = sk_s[new3]
        sv_out[...] = sv_s[new3]
        _swa_prompt_tile(c - 1, sq_s.at[old2], sk_s.at[older3, half], sk_s.at[old3], sv_s.at[older3, half],
                         sv_s.at[old3], hm_ref, sink_ref, o_swa_ref)

    _gla_step(c, gq_s.at[old2], gk_s.at[old2], gv_s.at[old2], gg_s.at[old2], gr_s.at[old2], s0_ref, tri_ref,
              m_ref, bd_ref, gn_ref, og_ref, sfin_ref, st_ref, o_ref, t=t, n_sub=n_sub, companion=companion)


def _gla_tokenwise(q_ref, k_ref, v_ref, g_ref, bd_ref, st_ref, o_ref, *, n_rows):
    row = lax.broadcasted_iota(I32, (GLA_SLOW_ROWS, 1), 0)

    def group(gi, carry):
        rows = pl.ds(pl.multiple_of(gi * GLA_SLOW_ROWS, GLA_SLOW_ROWS), GLA_SLOW_ROWS)
        q = q_ref[rows, :].astype(F32)
        k = k_ref[rows, :].astype(F32)
        v = v_ref[rows, :].astype(F32)
        decay = jnp.exp(g_ref[rows, :])
        o = jnp.zeros((GLA_SLOW_ROWS, N_HEADS_GLA * DV), F32)
        for j in range(GLA_SLOW_ROWS):
            only_j = row == j
            k_j = jnp.where(only_j, k, 0.0).astype(BF16)
            v_j = jnp.where(only_j, v, 0.0).astype(BF16)
            q_j = jnp.where(only_j, q, 0.0).astype(BF16)
            st = st_ref[...] * decay[j:j + 1, :] + _dot_tn(v_j, k_j) * bd_ref[...]
            st_ref[...] = st
            o = o + _dot_nt(q_j, st.astype(BF16))
        o_ref[rows, :] = o
        return carry

    lax.fori_loop(0, n_rows // GLA_SLOW_ROWS, group, 0)


def _gla_blocked(q_ref, k_ref, v_ref, g_ref, tri_ref, m_ref, bd_ref, st_ref, *, t, n_sub):
    nb = t // SUB
    n_rows = t * n_sub
    lanes = N_HEADS_GLA * DK

    def group_row(x, period, offset):
        g = x.reshape(n_rows // period, period, lanes)[:, offset:offset + 1, :]
        return jnp.broadcast_to(g, (n_rows // period, period, lanes)).reshape(n_rows, lanes)

    g_hi, g_lo = _split_bf16(g_ref[...])
    b = _dot(tri_ref[...], g_hi) + _dot(tri_ref[...], g_lo)
    q = q_ref[...].astype(F32)
    k = k_ref[...].astype(F32)
    qd = (q * jnp.exp(b - group_row(b, SUB, 0))).astype(BF16)
    pos = lax.broadcasted_iota(I32, (n_rows, lanes), 0) & (t - 1)
    k_parts = []
    for blk in range(nb):
        arg = jnp.where(pos < SUB * (blk + 1), group_row(b, t, SUB * blk) - b, NEG_BIG)
        k_parts.append((k * jnp.exp(arg)).astype(BF16))
    k_cat = jnp.concatenate(k_parts, axis=1)
    qd_cat = jnp.concatenate([qd] * nb, axis=1)
    q_dec = (q * jnp.exp(b)).astype(BF16)
    k_last = (k * jnp.exp(group_row(b, t, t - 1) - b)).astype(BF16)
    row_a = lax.broadcasted_iota(I32, (N_HEADS_GLA * t, t), 0) & (t - 1)
    col_a = lax.broadcasted_iota(I32, (N_HEADS_GLA * t, t), 1)

    o_intra, q_decayed, state_add, state_decay = [], [], [], []
    for u in range(n_sub):
        rows = slice(u * t, (u + 1) * t)
        v = v_ref[rows, :]
        lhs = jnp.concatenate([qd_cat[rows]] * N_HEADS_GLA, axis=0) * m_ref[...]
        a = _dot_nt(lhs, k_cat[rows])
        a = jnp.where(row_a >= col_a, a, 0.0).astype(BF16)
        o_full = _dot(a, v)
        o_intra.append(jnp.concatenate(
            [o_full[h * t:(h + 1) * t, h * DV:(h + 1) * DV] for h in range(N_HEADS_GLA)], axis=1))
        q_decayed.append(q_dec[rows])
        state_add.append(_dot_tn(v, k_last[rows]) * bd_ref[...])
        state_decay.append(jnp.exp(b[(u + 1) * t - 1:(u + 1) * t, :]))

    st = st_ref[...]
    o_parts = []
    for u in range(n_sub):
        o_parts.append(o_intra[u] + _dot_nt(q_decayed[u], st.astype(BF16)))
        st = st * state_decay[u] + state_add[u]
    st_ref[...] = st
    return jnp.concatenate(o_parts, axis=0) if n_sub > 1 else o_parts[0]


def _gla(gq, gk, gv, gg, gr, s0, tri, mask, bd, gn, *, t, n_sub, n_batch, n_steps, first_block, name):
    rows_per_step = t * n_sub

    def rows(width):
        return pl.BlockSpec((rows_per_step, width), lambda b, c: (first_block + b * n_steps + c, 0))

    def full(a):
        return pl.BlockSpec(a.shape, lambda b, c: (0,) * a.ndim)

    state = pl.BlockSpec((None,) + s0.shape[1:], lambda b, c: (b, 0, 0))
    return pl.pallas_call(
        functools.partial(_gla_kernel, t=t, n_sub=n_sub),
        grid=(n_batch, n_steps),
        in_specs=[rows(W_GQ), rows(W_GK), rows(W_GV), rows(W_GQ), rows(W_GR), state,
                  full(tri), full(mask), full(bd), full(gn)],
        out_specs=[pl.BlockSpec((rows_per_step, W_GV), lambda b, c: (b * n_steps + c, 0)), state],
        out_shape=[jax.ShapeDtypeStruct((n_batch * n_steps * rows_per_step, W_GV), BF16),
                   jax.ShapeDtypeStruct(s0.shape, F32)],
        scratch_shapes=[pltpu.VMEM(s0.shape[1:], F32), pltpu.VMEM((rows_per_step, W_GV), F32)],
        compiler_params=_cparams(2, "mixer"),
        name=name,
    )(gq, gk, gv, gg, gr, s0, tri, mask, bd, gn)


def _front_prompt(xp, consts, head_mask, sink_b, s0, tri, mask, bd, gn):
    def full(a):
        return pl.BlockSpec(a.shape, lambda b, c: (0,) * a.ndim)

    def computed(width):
        return pl.BlockSpec((TM, width), lambda b, c: (jnp.minimum(c, N_PROMPT_TILES - 1), 0))

    def mixed(width):
        return pl.BlockSpec((TM, width), lambda b, c: (jnp.maximum(c - 1, 0), 0))

    state = pl.BlockSpec((None,) + s0.shape[1:], lambda b, c: (b, 0, 0))
    ring2 = [pltpu.VMEM((2, TM, w), dt) for w, dt in
             (PROJ_OUTPUTS[0], PROJ_OUTPUTS[3], PROJ_OUTPUTS[4], PROJ_OUTPUTS[5], PROJ_OUTPUTS[6], PROJ_OUTPUTS[7])]
    ring3 = [pltpu.VMEM((3, TM, w), dt) for w, dt in (PROJ_OUTPUTS[1], PROJ_OUTPUTS[2])]
    return pl.pallas_call(
        functools.partial(_front_prompt_kernel, t=CHUNK, n_sub=TM // CHUNK),
        grid=(1, N_PROMPT_TILES + 1),
        in_specs=[computed(D_MODEL)] + [full(a) for a in consts] + [full(head_mask), full(sink_b), state,
                                                                   full(tri), full(mask), full(bd), full(gn)],
        out_specs=[computed(W_SK), computed(W_SV), mixed(W_SQ), mixed(W_GV), state],
        out_shape=[jax.ShapeDtypeStruct((N_PROMPT, W_SK), F32), jax.ShapeDtypeStruct((N_PROMPT, W_SV), F32),
                   jax.ShapeDtypeStruct((N_PROMPT, W_SQ), BF16), jax.ShapeDtypeStruct((N_PROMPT, W_GV), BF16),
                   jax.ShapeDtypeStruct(s0.shape, F32)],
        scratch_shapes=ring2 + ring3 + [pltpu.VMEM(s0.shape[1:], F32), pltpu.VMEM((TM, W_GV), F32)],
        compiler_params=_cparams(2, "front"),
        name="front_prompt",
    )(xp, *consts, head_mask, sink_b, s0, tri, mask, bd, gn)


def _tile_row_copies(hbm_ref, tile, vmem_ref, sem, to_hbm, rows=TM):
    copies = []
    for a in range(ROW_SUB):
        h = hbm_ref.at[pl.ds(tile * rows, rows), a, :]
        v = vmem_ref.at[:, pl.ds(a * LANES, LANES)]
        copies.append(pltpu.make_async_copy(v, h, sem) if to_hbm else pltpu.make_async_copy(h, v, sem))
    return copies


def _store_tile_rows(i, n_steps, outputs, row_buf, row_sem, rows=TM):
    buf_slot = lax.rem(i, 2)

    def store(j, tile, s):
        return _tile_row_copies(outputs[j][0], tile, row_buf.at[j, s], row_sem.at[j, s], True, rows)

    for j, (_, value) in enumerate(outputs):
        @pl.when(i >= 2)
        def _():
            for c in store(j, i - 2, buf_slot):
                c.wait()

        row_buf[j, buf_slot] = value
        for c in store(j, i, buf_slot):
            c.start()

        @pl.when(i == n_steps - 1)
        def _():
            for c in store(j, i - 1, 1 - buf_slot) + store(j, i, buf_slot):
                c.wait()


def _merge_kernel(oswp_ref, osws_ref, ogp_ref, ogs_ref, xp_ref, xs_ref, wo1_ref, wo2_ref, gf_ref, wrh_ref, wrl_ref,
                  br_ref, xrow_ref, h_ref, lg_ref, row_buf, row_sem):
    i = pl.program_id(0)
    is_prompt = i < N_PROMPT_TILES
    x = jnp.where(is_prompt, xp_ref[...], xs_ref[...])
    o_swa = jnp.where(is_prompt, oswp_ref[...], osws_ref[...])
    og = jnp.where(is_prompt, ogp_ref[...], ogs_ref[...])
    h = x + (_dot(o_swa, wo1_ref[...]) + _dot(og, wo2_ref[...]))
    xn = _rms(h) * gf_ref[...]
    x_hi, x_lo = _split_bf16(xn)
    logits = _dot(x_hi, wrh_ref[...]) + _dot(x_lo, wrh_ref[...]) + _dot(x_hi, wrl_ref[...]) + br_ref[...]
    lg_ref[...] = logits.T[:N_EXPERTS]
    h_ref[...] = h
    _store_tile_rows(i, N_TILES, ((xrow_ref, xn),), row_buf, row_sem)


def _merge(o_swa_p, o_swa_s, og_p, og_s, xp, xs, wo1, wo2, gf, wrh, wrl, br):
    def prompt_rows(width):
        return pl.BlockSpec((TM, width), lambda i: (jnp.minimum(i, N_PROMPT_TILES - 1), 0))

    def sample_rows(width):
        return pl.BlockSpec((TM, width), lambda i: (0, 0))

    def full(a):
        return pl.BlockSpec(a.shape, lambda i: (0,) * a.ndim)

    consts = (wo1, wo2, gf, wrh, wrl, br)
    return pl.pallas_call(
        _merge_kernel,
        grid=(N_TILES,),
        in_specs=[prompt_rows(W_SQ), sample_rows(W_SQ), prompt_rows(W_GV), sample_rows(W_GV),
                  prompt_rows(D_MODEL), sample_rows(D_MODEL)] + [full(a) for a in consts],
        out_specs=[pl.BlockSpec(memory_space=pl.ANY), pl.BlockSpec((TM, D_MODEL), lambda i: (i, 0)),
                   pl.BlockSpec((N_EXPERTS, TM), lambda i: (0, i))],
        out_shape=[jax.ShapeDtypeStruct((N_ROWS,) + ROW_TILE, F32), jax.ShapeDtypeStruct((N_ROWS, D_MODEL), F32),
                   jax.ShapeDtypeStruct((N_EXPERTS, N_ROWS), F32)],
        scratch_shapes=[pltpu.VMEM((1, 2, TM, D_MODEL), F32), pltpu.SemaphoreType.DMA((1, 2))],
        compiler_params=_cparams(1, "merge"),
        name="merge",
    )(o_swa_p, o_swa_s, og_p, og_s, xp, xs, *consts)


RT = 1280
N_ROUTER_STEPS = N_ROWS // RT


def _router_kernel(lg_ref, tri_ref, wrow_ref, ti_ref, rk_ref, cnt_ref, base_ref, row_buf, row_sem):
    i = pl.program_id(0)

    @pl.when(i == 0)
    def _():
        base_ref[...] = jnp.zeros_like(base_ref)

    logits_t = lg_ref[...]
    expert = lax.broadcasted_iota(I32, logits_t.shape, 0)
    slot = lax.broadcasted_iota(I32, (SLOT_ROWS, RT), 0)
    vals, hots = [], []
    ti = jnp.zeros((SLOT_ROWS, RT), I32)
    for kk in range(TOP_K):
        m = jnp.max(logits_t, axis=0, keepdims=True)
        idx = jnp.min(jnp.where(logits_t == m, expert, N_EXPERTS), axis=0, keepdims=True)
        hot = expert == idx
        logits_t = jnp.where(hot, NEG_BIG, logits_t)
        vals.append(m)
        hots.append(hot)
        ti = jnp.where(slot == kk, idx, ti)
    ti_ref[...] = ti
    exps = [jnp.exp(v - vals[0]) for v in vals]
    den = exps[0] + exps[1] + exps[2] + exps[3]
    tw_t = jnp.zeros((SLOT_ROWS, RT), F32)
    for kk in range(TOP_K):
        tw_t = jnp.where(slot == kk, exps[kk] / den, tw_t)
    eye = (lax.broadcasted_iota(I32, (SLOT_ROWS, LANES), 0)
           == lax.broadcasted_iota(I32, (SLOT_ROWS, LANES), 1)).astype(F32).astype(BF16)
    w_hi = tw_t.astype(BF16)
    w_mid, w_lo = _split_bf16(tw_t - w_hi.astype(F32))
    tw_col = _dot_tn(w_hi, eye) + _dot_tn(w_mid, eye) + _dot_tn(w_lo, eye)

    onehot_t = jnp.zeros(logits_t.shape, F32)
    for hot in hots:
        onehot_t = onehot_t + jnp.where(hot, 1.0, 0.0)
    before_t = _dot(onehot_t.astype(BF16), tri_ref[...]) + base_ref[:, 0:1]
    rk = jnp.zeros((SLOT_ROWS, RT), I32)
    for kk in range(TOP_K):
        r = jnp.sum(jnp.where(hots[kk], before_t, 0.0), axis=0, keepdims=True).astype(I32)
        rk = jnp.where(slot == kk, r, rk)
    rk_ref[...] = rk
    total = base_ref[...] + jnp.sum(onehot_t, axis=1, keepdims=True)
    base_ref[...] = total
    cnt_ref[...] = total.astype(I32)

    w_lanes = [jnp.broadcast_to(tw_col[:, kk:kk + 1], (RT, LANES)) for kk in range(TOP_K)]
    w_rows = jnp.concatenate(w_lanes + [jnp.zeros((RT, D_MODEL - TOP_K * LANES), F32)], axis=1)
    _store_tile_rows(i, N_ROUTER_STEPS, ((wrow_ref, w_rows),), row_buf, row_sem, RT)


def _router(logits, tri):
    return pl.pallas_call(
        _router_kernel,
        grid=(N_ROUTER_STEPS,),
        in_specs=[pl.BlockSpec((N_EXPERTS, RT), lambda i: (0, i)), pl.BlockSpec(tri.shape, lambda i: (0, 0))],
        out_specs=[pl.BlockSpec(memory_space=pl.ANY),
                   pl.BlockSpec((SLOT_ROWS, RT), lambda i: (0, i)),
                   pl.BlockSpec((SLOT_ROWS, RT), lambda i: (0, i)),
                   pl.BlockSpec((N_EXPERTS, LANES), lambda i: (0, 0))],
        out_shape=[jax.ShapeDtypeStruct((N_ROWS,) + ROW_TILE, F32),
                   jax.ShapeDtypeStruct((SLOT_ROWS, N_ROWS), I32),
                   jax.ShapeDtypeStruct((SLOT_ROWS, N_ROWS), I32),
                   jax.ShapeDtypeStruct((N_EXPERTS, LANES), I32)],
        scratch_shapes=[pltpu.VMEM((N_EXPERTS, LANES), F32), pltpu.VMEM((1, 2, RT, D_MODEL), F32),
                        pltpu.SemaphoreType.DMA((1, 2))],
        compiler_params=_cparams(1, "router"),
        name="router",
    )(logits, tri)


ISSUE_UNROLL = 4


def _row_copy(src_ref, src_row, dst_ref, dst_row, sem):
    return pltpu.make_async_copy(src_ref.at[pl.ds(src_row, 1)], dst_ref.at[pl.ds(dst_row, 1)], sem)


def _dispatch_kernel(dest_ref, end_ref, x_ref, xs_ref, zero_ref, sem, zsem):
    i = pl.program_id(0)
    base = i * TM

    @pl.when(i == 0)
    def _():
        zero_ref[...] = jnp.zeros_like(zero_ref)

        def tail_copy(e):
            last = jnp.maximum(end_ref[e] - TM, 0)
            return pltpu.make_async_copy(zero_ref, xs_ref.at[pl.ds(pl.multiple_of(last, TM), TM)], zsem)

        def fill(e, carry):
            tail_copy(e).start()
            return carry

        def fill_wait(e, carry):
            tail_copy(e).wait()
            return carry

        lax.fori_loop(0, N_EXPERTS, fill, 0)
        lax.fori_loop(0, N_EXPERTS, fill_wait, 0)

        def unused_copy(t):
            return pltpu.make_async_copy(zero_ref, xs_ref.at[pl.ds(pl.multiple_of(t * TM, TM), TM)], zsem)

        def fill_unused(t, carry):
            unused_copy(t).start()
            unused_copy(t).wait()
            return carry

        lax.fori_loop(end_ref[N_EXPERTS - 1] // TM, N_EXPERT_TILES, fill_unused, 0)

    def issue(n, carry):
        for kk in range(TOP_K):
            _row_copy(x_ref, n, xs_ref, dest_ref[kk * N_ROWS + base + n], sem).start(priority=kk % 2)
        return carry

    lax.fori_loop(0, TM, issue, 0, unroll=ISSUE_UNROLL)

    for kk in range(TOP_K):
        pltpu.make_async_copy(x_ref, xs_ref.at[pl.ds(0, TM)], sem).wait()


def _dispatch(dest, end, x_packed):
    return pl.pallas_call(
        _dispatch_kernel,
        grid_spec=pltpu.PrefetchScalarGridSpec(
            num_scalar_prefetch=2,
            grid=(N_TILES,),
            in_specs=[pl.BlockSpec((TM,) + ROW_TILE, lambda i, d, e: (i, 0, 0))],
            out_specs=pl.BlockSpec(memory_space=pl.ANY),
            scratch_shapes=[pltpu.VMEM((TM,) + ROW_TILE, F32), pltpu.SemaphoreType.DMA,
                            pltpu.SemaphoreType.DMA],
        ),
        out_shape=jax.ShapeDtypeStruct((N_SORTED_ROWS,) + ROW_TILE, F32),
        compiler_params=_cparams(1, "dispatch"),
        name="dispatch",
    )(dest, end, x_packed)


CAST_ROWS = 128


def _moe_kernel(te_ref, nu_ref, nx_ref, xs_ref, wg_ref, bg_ref, wu_ref, bu_ref, wd_ref, bd_ref, ys_ref,
                w_stage, w_bf, x_buf, y_buf, zero_buf, w_sem, in_sem, out_sem, zero_sem):
    t = pl.program_id(0)
    n_used = nu_ref[0]
    slot = lax.rem(t, 2)
    e = te_ref[t]
    e_prev = te_ref[jnp.maximum(t - 1, 0)]

    def load(tile, s):
        return _tile_row_copies(xs_ref, tile, x_buf.at[s], in_sem.at[s], to_hbm=False)

    def store(tile, s):
        return _tile_row_copies(ys_ref, tile, y_buf.at[s], out_sem.at[s], to_hbm=True)

    def weight_copies(expert):
        return [pltpu.make_async_copy(w.at[expert], w_stage.at[j], w_sem.at[j])
                for j, w in enumerate((wg_ref, wu_ref, wd_ref))]

    @pl.when(t == 0)
    def _():
        for c in weight_copies(e) + load(0, 0):
            c.start()

    @pl.when(t + 1 < n_used)
    def _():
        for c in load(t + 1, 1 - slot):
            c.start()

    @pl.when((t == 0) | (e != e_prev))
    def _():
        for c in weight_copies(e):
            c.wait()

        def cast(r, carry):
            sl = pl.ds(pl.multiple_of(r * CAST_ROWS, CAST_ROWS), CAST_ROWS)
            for j in range(3):
                w_bf[j, sl, :] = w_stage[j, sl, :].astype(BF16)
            return carry

        lax.fori_loop(0, D_MODEL // CAST_ROWS, cast, 0)
        e_next = nx_ref[e]

        @pl.when(e_next >= 0)
        def _():
            for c in weight_copies(e_next):
                c.start()

    @pl.when(t < n_used)
    def _():
        for c in load(t, slot):
            c.wait()

        @pl.when(t >= 2)
        def _():
            for c in store(t - 2, slot):
                c.wait()

        x = x_buf[slot].astype(BF16)
        gate = jnp.minimum(_dot(x, w_bf[0]) + bg_ref[...], SWIGLU_LIMIT)
        up = jnp.clip(_dot(x, w_bf[1]) + bu_ref[...], -SWIGLU_LIMIT, SWIGLU_LIMIT)
        hdn = (up + 1.0) * gate * (1.0 / (1.0 + jnp.exp(-SWIGLU_ALPHA * gate)))
        y_buf[slot] = _dot(hdn.astype(BF16), w_bf[2]) + bd_ref[...]
        for c in store(t, slot):
            c.start()

    @pl.when(t >= n_used)
    def _():
        zero_buf[...] = jnp.zeros_like(zero_buf)
        fill = pltpu.make_async_copy(zero_buf, ys_ref.at[pl.ds(t * TM, TM)], zero_sem)
        fill.start()
        fill.wait()

    @pl.when(t == N_EXPERT_TILES - 1)
    def _():
        @pl.when(n_used >= 2)
        def _():
            for c in store(n_used - 2, lax.rem(n_used, 2)):
                c.wait()

        for c in store(n_used - 1, lax.rem(n_used - 1, 2)):
            c.wait()


def _moe(tile_expert, n_used, next_expert, xs, w_gate, b_gate, w_up, b_up, w_down, b_down):
    hbm = pl.BlockSpec(memory_space=pl.ANY)
    bias = pl.BlockSpec((None, 1, D_MODEL), lambda t, te, nu, nx: (te[t], 0, 0))
    return pl.pallas_call(
        _moe_kernel,
        grid_spec=pltpu.PrefetchScalarGridSpec(
            num_scalar_prefetch=3,
            grid=(N_EXPERT_TILES,),
            in_specs=[hbm, hbm, bias, hbm, bias, hbm, bias],
            out_specs=hbm,
            scratch_shapes=[pltpu.VMEM((3, D_MODEL, D_MODEL), F32), pltpu.VMEM((3, D_MODEL, D_MODEL), BF16),
                            pltpu.VMEM((2, TM, D_MODEL), F32), pltpu.VMEM((2, TM, D_MODEL), F32),
                            pltpu.VMEM((TM,) + ROW_TILE, F32), pltpu.SemaphoreType.DMA((3,)),
                            pltpu.SemaphoreType.DMA((2,)), pltpu.SemaphoreType.DMA((2,)),
                            pltpu.SemaphoreType.DMA],
        ),
        out_shape=jax.ShapeDtypeStruct((N_SORTED_ROWS,) + ROW_TILE, F32),
        compiler_params=_cparams(1, "experts"),
        name="experts",
    )(tile_expert, n_used, next_expert, xs, w_gate, b_gate, w_up, b_up, w_down, b_down)


def _combine_kernel(dest_ref, h_ref, w_ref, ys_ref, yp_ref, ysm_ref, g_ref, out_ref, sem):
    i = pl.program_id(0)
    slot = lax.rem(i, 2)

    def issue_tile(tile, s):
        base = tile * TM

        def issue(n, carry):
            for kk in range(TOP_K):
                _row_copy(ys_ref, dest_ref[kk * N_ROWS + base + n], g_ref.at[s], kk * TM + n,
                          sem.at[s]).start(priority=kk % 2)
            return carry

        lax.fori_loop(0, TM, issue, 0, unroll=ISSUE_UNROLL)

    @pl.when(i == 0)
    def _():
        issue_tile(0, 0)

    @pl.when(i + 1 < N_TILES)
    def _():
        issue_tile(i + 1, 1 - slot)

    for kk in range(TOP_K):
        pltpu.make_async_copy(ys_ref.at[pl.ds(0, TM)], g_ref.at[slot, pl.ds(kk * TM, TM)], sem.at[slot]).wait()

    w = w_ref[...]
    acc = None
    for kk in range(TOP_K):
        part = jnp.broadcast_to(w[:, kk:kk + 1, :], (TM,) + ROW_TILE) * g_ref[slot, kk * TM:(kk + 1) * TM]
        acc = part if acc is None else acc + part
    out_ref[...] = acc

    def write(y_ref):
        for a in range(ROW_SUB):
            cols = slice(a * LANES, (a + 1) * LANES)
            y_ref[:, cols] = h_ref[:, cols] + out_ref[:, a, :]

    @pl.when(i < N_PROMPT_TILES)
    def _():
        write(yp_ref)

    @pl.when(i >= N_PROMPT_TILES)
    def _():
        write(ysm_ref)


def _combine(dest, h, w_rows, ys):
    tile = lambda index: pl.BlockSpec((TM,) + ROW_TILE, index)
    return pl.pallas_call(
        _combine_kernel,
        grid_spec=pltpu.PrefetchScalarGridSpec(
            num_scalar_prefetch=1,
            grid=(N_TILES,),
            in_specs=[pl.BlockSpec((TM, D_MODEL), lambda i, d: (i, 0)), tile(lambda i, d: (i, 0, 0)),
                      pl.BlockSpec(memory_space=pl.ANY)],
            out_specs=[pl.BlockSpec((TM, D_MODEL), lambda i, d: (jnp.minimum(i, N_PROMPT_TILES - 1), 0)),
                       pl.BlockSpec((TM, D_MODEL), lambda i, d: (0, 0))],
            scratch_shapes=[pltpu.VMEM((2, TOP_K * TM) + ROW_TILE, F32), pltpu.VMEM((TM,) + ROW_TILE, F32),
                            pltpu.SemaphoreType.DMA((2,))],
        ),
        out_shape=[jax.ShapeDtypeStruct((N_PROMPT, D_MODEL), F32),
                   jax.ShapeDtypeStruct((N_SAMPLE, D_MODEL), F32)],
        compiler_params=_cparams(1, "combine"),
        name="combine",
    )(dest, h, w_rows, ys)


def _block_diag_ones(n, blk):
    idx = np.arange(n) // blk
    return (idx[:, None] == idx[None, :]).astype(np.float32)


def _swa_head_mask(tq):
    row_head = np.arange(4 * tq)[:, None] // tq
    lane_head = np.arange(4 * HD)[None, :] // HD
    return jnp.asarray((row_head == lane_head).astype(np.float32), BF16)


def _gla_masks(t, n_sub):
    nb = t // SUB
    lanes = N_HEADS_GLA * DK
    tri = jnp.asarray(np.kron(np.eye(n_sub, dtype=np.float32), np.tril(np.ones((t, t), np.float32))), BF16)
    row = np.arange(N_HEADS_GLA * t)
    col = np.arange(nb * lanes)
    same_head = (row[:, None] // t) == ((col[None, :] % lanes) // DK)
    same_blk = ((row[:, None] % t) // SUB) == (col[None, :] // lanes)
    mask = jnp.asarray((same_head & same_blk).astype(np.float32), BF16)
    return tri, mask


def _sink_rows(sinks, tq):
    s = jnp.repeat(sinks.astype(F32).reshape(2, 4), tq, axis=1)
    return jnp.broadcast_to(s[:, None, :], (2, 8, 4 * tq))


def kernel(x_prompt, x_sample, state_gla, cache_swa_k, cache_swa_v, norm_mix_g, w_in, w_gla_a2, b_gla_a, q_norm_g,
           k_norm_g, swa_sinks, gla_norm_g, w_out, norm_ffn_g, w_router, b_router, w_gate, b_gate, w_up, b_up,
           w_down, b_down):
    xp = x_prompt.reshape(N_PROMPT, D_MODEL)
    xs = x_sample.reshape(N_SAMPLE, D_MODEL)

    w_in0 = w_in[0]
    w_main = w_in0[:, :W_MAIN].astype(BF16)
    w_ga = jnp.pad(w_in0[:, OFF_GA:], ((0, 0), (0, LANES - W_GA))).astype(BF16)
    w_a2 = jnp.pad(w_gla_a2[0], ((0, LANES - W_GA), (0, 0))).astype(BF16)
    b_a = b_gla_a[0].reshape(1, -1)
    qg = (jnp.tile(q_norm_g[0], N_HEADS_SWA) * (HD ** -0.5)).reshape(1, -1)
    kg = jnp.tile(k_norm_g[0], 2).reshape(1, -1)
    bdq = jnp.asarray(_block_diag_ones(W_SQ, HD), BF16)
    bdk = jnp.asarray(_block_diag_ones(W_SK, HD), BF16)

    proj_consts = (norm_mix_g[0].reshape(1, -1), w_main, w_ga, w_a2, b_a, qg, kg, bdq, bdk)
    sq, sk, sv, gq, gk, gv, gg, gr = _proj_sample(xs, proj_consts)

    cache_k = cache_swa_k[0].reshape(N_STREAMS, WINDOW, 2 * HD)
    cache_v = cache_swa_v[0].reshape(N_STREAMS, WINDOW, 2 * HD)
    o_swa_s = _swa_sample(sq, sk, sv, cache_k, cache_v, _swa_head_mask(T_SAMPLE),
                          _sink_rows(swa_sinks[0], T_SAMPLE))

    bd_state = jnp.asarray(_block_diag_ones(N_HEADS_GLA, 1).repeat(DV, axis=0).repeat(DK, axis=1), F32)
    gn = gla_norm_g[0].reshape(1, -1)
    tri_p, mask_p = _gla_masks(CHUNK, TM // CHUNK)
    tri_s, mask_s = _gla_masks(T_SAMPLE, 1)
    s0_p = jnp.zeros((1, N_HEADS_GLA * DV, N_HEADS_GLA * DK), F32)
    sk_p, sv_p, o_swa_p, og_p, sfin_p = _front_prompt(xp, proj_consts, _swa_head_mask(CHUNK),
                                                      _sink_rows(swa_sinks[0], CHUNK), s0_p, tri_p, mask_p,
                                                      bd_state, gn)
    eye = jnp.eye(N_HEADS_GLA, dtype=F32)
    s0_s = jnp.einsum('bhde,hg->bhegd', state_gla[0].astype(F32), eye).reshape(
        N_STREAMS, N_HEADS_GLA * DV, N_HEADS_GLA * DK)
    og_s, sfin_s = _gla(gq, gk, gv, gg, gr, s0_s, tri_s, mask_s, bd_state, gn, t=T_SAMPLE, n_sub=1,
                        n_batch=N_STREAMS, n_steps=1, first_block=0, name="gla_sample")

    def unpack_state(sfin):
        s = sfin.reshape(-1, N_HEADS_GLA, DV, N_HEADS_GLA, DK)
        s = jnp.stack([s[:, h, :, h, :] for h in range(N_HEADS_GLA)], axis=1)
        return jnp.transpose(s, (0, 1, 3, 2))[None]

    w_out0 = w_out[0].astype(BF16)
    wr = jnp.pad(w_router[0], ((0, 0), (0, LANES - N_EXPERTS)))
    wr_hi = wr.astype(BF16)
    wr_lo = (wr - wr_hi.astype(F32)).astype(BF16)
    br = jnp.pad(b_router[0].astype(F32), (0, LANES - N_EXPERTS)).reshape(1, -1)
    x_rows, h, logits = _merge(o_swa_p, o_swa_s, og_p, og_s, xp, xs, w_out0[:W_SQ], w_out0[W_SQ:],
                               norm_ffn_g[0].reshape(1, -1), wr_hi, wr_lo, br)
    earlier = jnp.asarray(np.triu(np.ones((RT, RT), np.float32), 1), BF16)
    w_rows, top_i, rank, counts = _router(logits, earlier)

    counts = counts[:, 0]
    padded = (counts + TM - 1) // TM * TM
    end = jnp.cumsum(padded)
    start = end - padded
    experts = jnp.arange(N_EXPERTS, dtype=I32)
    is_e = top_i[:TOP_K, :, None] == experts
    dest = (rank[:TOP_K] + jnp.sum(jnp.where(is_e, start, 0), axis=-1)).reshape(-1).astype(I32)
    n_used = (end[-1] // TM).astype(I32)
    tiles = jnp.minimum(jnp.arange(N_EXPERT_TILES, dtype=I32), n_used - 1)
    tile_expert = jnp.sum((tiles[:, None] * TM >= end[None, :]).astype(I32), axis=1)
    later_nonempty = (experts[None, :] > experts[:, None]) & (padded[None, :] > 0)
    next_expert = jnp.min(jnp.where(later_nonempty, experts[None, :], N_EXPERTS), axis=1)
    next_expert = jnp.where(next_expert < N_EXPERTS, next_expert, -1).astype(I32)

    xs_sorted = _dispatch(dest, end.astype(I32), x_rows)
    ys = _moe(tile_expert, n_used.reshape(1), next_expert, xs_sorted, w_gate[0], b_gate[0].reshape(N_EXPERTS, 1, -1),
              w_up[0], b_up[0].reshape(N_EXPERTS, 1, -1), w_down[0], b_down[0].reshape(N_EXPERTS, 1, -1))
    y_p, y_s = _combine(dest, h, w_rows, ys)

    sk_s = sk.reshape(N_STREAMS, T_SAMPLE, 2 * HD)
    sv_s = sv.reshape(N_STREAMS, T_SAMPLE, 2 * HD)
    kc_s = jnp.concatenate([cache_k[:, T_SAMPLE:], sk_s], axis=1).reshape(1, N_STREAMS, WINDOW, 2, HD)
    vc_s = jnp.concatenate([cache_v[:, T_SAMPLE:], sv_s], axis=1).reshape(1, N_STREAMS, WINDOW, 2, HD)
    kc_p = sk_p[N_PROMPT - WINDOW:].reshape(1, 1, WINDOW, 2, HD)
    vc_p = sv_p[N_PROMPT - WINDOW:].reshape(1, 1, WINDOW, 2, HD)
    return (y_p.reshape(1, N_PROMPT, D_MODEL), y_s.reshape(N_STREAMS, T_SAMPLE, D_MODEL),
            unpack_state(sfin_p), kc_p, vc_p, unpack_state(sfin_s), kc_s, vc_s)
```

```python
import functools

import numpy as np
import jax
import jax.numpy as jnp
from jax import lax
from jax.experimental import pallas as pl
from jax.experimental.pallas import tpu as pltpu

F32 = jnp.float32
BF16 = jnp.bfloat16
I32 = jnp.int32

D_MODEL = 1024
N_PROMPT = 16384
N_STREAMS = 8
T_SAMPLE = 32
N_SAMPLE = N_STREAMS * T_SAMPLE
N_ROWS = N_PROMPT + N_SAMPLE
EPS = 1e-6

CHUNK = 64
SUB = 16
N_HEADS_SWA = 8
HD = 64
WINDOW = 128
N_HEADS_GLA = 4
DK = 64
DV = 128
GLA_TAU = 16.0
N_EXPERTS = 32
TOP_K = 4
SWIGLU_ALPHA = 1.702
SWIGLU_LIMIT = 7.0

TM = 256
N_TILES = N_ROWS // TM
N_PROMPT_TILES = N_PROMPT // TM
N_ASSIGN = N_ROWS * TOP_K
N_EXPERT_TILES = N_ASSIGN // TM + N_EXPERTS
N_SORTED_ROWS = N_EXPERT_TILES * TM
LANES = 128
SLOT_ROWS = 16
NEG_BIG = -1e30

W_SQ, W_SK, W_SV, W_GQ, W_GK, W_GV, W_GR, W_GA = 512, 128, 128, 256, 256, 512, 512, 16
OFF_SQ = 0
OFF_SK = OFF_SQ + W_SQ
OFF_SV = OFF_SK + W_SK
OFF_GQ = OFF_SV + W_SV
OFF_GK = OFF_GQ + W_GQ
OFF_GV = OFF_GK + W_GK
OFF_GR = OFF_GV + W_GV
OFF_GA = OFF_GR + W_GR
W_MAIN = OFF_GA


VMEM_MIB = dict(proj=40, front=48, mixer=32, merge=32, router=48, dispatch=32, experts=52, combine=40)


def _cparams(n_grid_axes, call):
    return pltpu.CompilerParams(dimension_semantics=("arbitrary",) * n_grid_axes,
                                vmem_limit_bytes=VMEM_MIB[call] * 2 ** 20)


def _dot(a, b):
    return jnp.dot(a, b, preferred_element_type=F32)


def _dot_nt(a, b):
    return lax.dot_general(a, b, (((1,), (1,)), ((), ())), preferred_element_type=F32)


def _dot_tn(a, b):
    return lax.dot_general(a, b, (((0,), (0,)), ((), ())), preferred_element_type=F32)


def _split_bf16(x):
    hi = x.astype(BF16)
    lo = (x - hi.astype(F32)).astype(BF16)
    return hi, lo


def _rms(x):
    return x * lax.rsqrt(jnp.mean(x * x, axis=-1, keepdims=True) + EPS)


ROW_SUB = 8
ROW_TILE = (ROW_SUB, LANES)


PROJ_OUTPUTS = ((W_SQ, BF16), (W_SK, F32), (W_SV, F32), (W_GQ, BF16), (W_GK, BF16), (W_GV, BF16),
                (W_GQ, F32), (W_GR, BF16))
N_PROJ_CONSTS = 9


def _proj_tile(x, g_ref, w_ref, wga_ref, wa2_ref, ba_ref, qg_ref, kg_ref, bdq_ref, bdk_ref,
               sq_ref, sk_ref, sv_ref, gq_ref, gk_ref, gv_ref, gg_ref, gr_ref):
    xb = (_rms(x) * g_ref[...]).astype(BF16)

    def seg(off, width):
        return _dot(xb, w_ref[:, off:off + width])

    def head_norm(u, bd_ref):
        hi, lo = _split_bf16(u * u)
        ss = _dot(hi, bd_ref[...]) + _dot(lo, bd_ref[...])
        return u * lax.rsqrt(ss * (1.0 / HD) + EPS)

    sq_ref[...] = (head_norm(seg(OFF_SQ, W_SQ), bdq_ref) * qg_ref[...]).astype(BF16)
    sk_ref[...] = head_norm(seg(OFF_SK, W_SK), bdk_ref) * kg_ref[...]
    sv_ref[...] = seg(OFF_SV, W_SV)
    gq_ref[...] = (seg(OFF_GQ, W_GQ) * (DK ** -0.5)).astype(BF16)
    gk_ref[...] = seg(OFF_GK, W_GK).astype(BF16)
    gv_ref[...] = seg(OFF_GV, W_GV).astype(BF16)
    gr_ref[...] = seg(OFF_GR, W_GR).astype(BF16)
    ga = _dot(xb, wga_ref[...]).astype(BF16)
    z = _dot(ga, wa2_ref[...]) + ba_ref[...]
    log_sig = jnp.minimum(z, 0.0) - jnp.log(1.0 + jnp.exp(-jnp.abs(z)))
    gg_ref[...] = log_sig * (1.0 / GLA_TAU)


def _proj_sample_kernel(x_ref, *refs):
    _proj_tile(x_ref[...], *refs)


def _proj_sample(xs, consts):
    def full(a):
        return pl.BlockSpec(a.shape, lambda i: (0,) * a.ndim)

    return pl.pallas_call(
        _proj_sample_kernel,
        grid=(1,),
        in_specs=[full(xs)] + [full(a) for a in consts],
        out_specs=[pl.BlockSpec((N_SAMPLE, w), lambda i: (0, 0)) for w, _ in PROJ_OUTPUTS],
        out_shape=[jax.ShapeDtypeStruct((N_SAMPLE, w), dt) for w, dt in PROJ_OUTPUTS],
        compiler_params=_cparams(1, "proj"),
        name="proj_sample",
    )(xs, *consts)


def _dup_kv_heads(x):
    r = pltpu.roll(x, HD, axis=1)
    lo = lax.broadcasted_iota(I32, x.shape, 1) < HD
    out = []
    for a in (jnp.where(lo, x, r), jnp.where(lo, r, x)):
        out.append(jnp.concatenate([a, a], axis=1).astype(BF16))
    return out


def _swa_blocks(q_blocks, k_blocks, v_blocks, sinks, valids, head_mask):
    tq = q_blocks[0].shape[0]
    scores = []
    for q, k, valid in zip(q_blocks, k_blocks, valids):
        s_t = _dot_nt(k, jnp.concatenate([q] * 4, axis=0) * head_mask)
        scores.append(s_t if valid is None else jnp.where(valid, s_t, -jnp.inf))
    probs = []
    for s_t, sink in zip(scores, sinks):
        m = jnp.maximum(jnp.max(s_t, axis=0, keepdims=True), sink)
        p_t = jnp.exp(s_t - m)
        den = jnp.sum(p_t, axis=0, keepdims=True) + jnp.exp(sink - m)
        probs.append((p_t / den).astype(BF16))
    lane_head = lax.broadcasted_iota(I32, (tq, 4 * HD), 1) // HD
    outs = []
    for p_t, v in zip(probs, v_blocks):
        o_full = _dot_tn(p_t, v)
        o = jnp.zeros((tq, 4 * HD), F32)
        for a in range(4):
            o = o + jnp.where(lane_head == a, o_full[a * tq:(a + 1) * tq], 0.0)
        outs.append(o)
    return outs


def _swa_prompt_tile(i, q_ref, kp_ref, kc_ref, vp_ref, vc_ref, hm_ref, sink_ref, o_ref):
    k_dup = _dup_kv_heads(jnp.concatenate([kp_ref[...], kc_ref[...]], axis=0))
    v_dup = _dup_kv_heads(jnp.concatenate([vp_ref[...], vc_ref[...]], axis=0))
    sink_row = [sink_ref[j][0:1, :] for j in range(2)]
    span = WINDOW + CHUNK
    key = lax.broadcasted_iota(I32, (span, 4 * CHUNK), 0)
    qs, ks, vs, sinks, valids, where = [], [], [], [], [], []
    for c in range(TM // CHUNK):
        lo = CHUNK * c
        valid = (i * TM - WINDOW + lo + key) >= 0
        for j in range(2):
            qs.append(q_ref[lo:lo + CHUNK, 4 * HD * j:4 * HD * (j + 1)])
            ks.append(k_dup[j][lo:lo + span])
            vs.append(v_dup[j][lo:lo + span])
            sinks.append(sink_row[j])
            valids.append(valid)
            where.append((lo, j))
    outs = _swa_blocks(qs, ks, vs, sinks, valids, hm_ref[...])
    for (lo, j), o in zip(where, outs):
        o_ref[lo:lo + CHUNK, 4 * HD * j:4 * HD * (j + 1)] = o.astype(BF16)


def _swa_sample_kernel(q_ref, kc_ref, kn_ref, vc_ref, vn_ref, hm_ref, sink_ref, o_ref):
    k_dup = _dup_kv_heads(jnp.concatenate([kc_ref[...], kn_ref[...]], axis=0))
    v_dup = _dup_kv_heads(jnp.concatenate([vc_ref[...], vn_ref[...]], axis=0))
    sink_row = [sink_ref[j][0:1, :] for j in range(2)]
    qs = [q_ref[:, 4 * HD * j:4 * HD * (j + 1)] for j in range(2)]
    outs = _swa_blocks(qs, k_dup, v_dup, sink_row, [None, None], hm_ref[...])
    o_ref[...] = jnp.concatenate(outs, axis=1).astype(BF16)


def _swa_sample(sq, sk, sv, cache_k, cache_v, head_mask, sink_b):
    new = lambda width: pl.BlockSpec((T_SAMPLE, width), lambda b: (b, 0))
    cache = pl.BlockSpec((None, WINDOW, 2 * HD), lambda b: (b, 0, 0))
    return pl.pallas_call(
        _swa_sample_kernel,
        grid=(N_STREAMS,),
        in_specs=[new(W_SQ), cache, new(2 * HD), cache, new(2 * HD),
                  pl.BlockSpec(head_mask.shape, lambda b: (0, 0)),
                  pl.BlockSpec(sink_b.shape, lambda b: (0, 0, 0))],
        out_specs=pl.BlockSpec((T_SAMPLE, W_SQ), lambda b: (b, 0)),
        out_shape=jax.ShapeDtypeStruct((N_SAMPLE, W_SQ), BF16),
        compiler_params=_cparams(1, "mixer"),
        name="swa_sample",
    )(sq, cache_k, sk, cache_v, sv, head_mask, sink_b)


GLA_SAFE_EXP = 80.0
GLA_SLOW_ROWS = 16


def _gla_step(c, q_ref, k_ref, v_ref, g_ref, gr_ref, s0_ref, tri_ref, m_ref, bd_ref, gn_ref,
              og_ref, sfin_ref, st_ref, o_ref, *, t, n_sub, companion=None):
    @pl.when(c == 0)
    def _():
        st_ref[...] = s0_ref[...]

    n_rows = t * n_sub
    blocked_is_safe = (SUB - 1) * jnp.max(-g_ref[...]) <= GLA_SAFE_EXP

    def emit(o):
        gr = gr_ref[...].astype(F32)
        gate = gr / (1.0 + jnp.exp(-gr))
        outs = []
        for h in range(N_HEADS_GLA):
            sl = slice(h * DV, (h + 1) * DV)
            outs.append(_rms(o[:, sl]) * gn_ref[...] * gate[:, sl])
        og_ref[...] = jnp.concatenate(outs, axis=1).astype(BF16)

    @pl.when(blocked_is_safe)
    def _():
        if companion is not None:
            companion()
        emit(_gla_blocked(q_ref, k_ref, v_ref, g_ref, tri_ref, m_ref, bd_ref, st_ref, t=t, n_sub=n_sub))

    @pl.when(jnp.logical_not(blocked_is_safe))
    def _():
        if companion is not None:
            companion()
        _gla_tokenwise(q_ref, k_ref, v_ref, g_ref, bd_ref, st_ref, o_ref, n_rows=n_rows)
        emit(o_ref[...])

    @pl.when(c == pl.num_programs(1) - 1)
    def _():
        sfin_ref[...] = st_ref[...]


def _gla_kernel(*refs, t, n_sub):
    _gla_step(pl.program_id(1), *refs, t=t, n_sub=n_sub)


def _front_prompt_kernel(*refs, t, n_sub):
    x_ref = refs[0]
    consts = refs[1:1 + N_PROJ_CONSTS]
    hm_ref, sink_ref, s0_ref, tri_ref, m_ref, bd_ref, gn_ref = refs[1 + N_PROJ_CONSTS:8 + N_PROJ_CONSTS]
    sk_out, sv_out, o_swa_ref, og_ref, sfin_ref = refs[8 + N_PROJ_CONSTS:13 + N_PROJ_CONSTS]
    sq_s, gq_s, gk_s, gv_s, gg_s, gr_s, sk_s, sv_s, st_ref, o_ref = refs[13 + N_PROJ_CONSTS:]
    c = pl.program_id(1)

    @pl.when(c == 0)
    def _():
        for ring in (sq_s, gq_s, gk_s, gv_s, gg_s, gr_s, sk_s, sv_s):
            ring[...] = jnp.zeros_like(ring)

    new2, old2 = lax.rem(c, 2), lax.rem(c + 1, 2)
    new3, old3, older3 = lax.rem(c, 3), lax.rem(c + 2, 3), lax.rem(c + 1, 3)
    half = pl.ds(TM // 2, TM // 2)

    def companion():
        _proj_tile(x_ref[...], *consts, sq_s.at[new2], sk_s.at[new3], sv_s.at[new3], gq_s.at[new2],
                   gk_s.at[new2], gv_s.at[new2], gg_s.at[new2], gr_s.at[new2])
        sk_out[...] = sk_s[new3]
        sv_out[...] = sv_s[new3]
        _swa_prompt_tile(c - 1, sq_s.at[old2], sk_s.at[older3, half], sk_s.at[old3], sv_s.at[older3, half],
                         sv_s.at[old3], hm_ref, sink_ref, o_swa_ref)

    _gla_step(c, gq_s.at[old2], gk_s.at[old2], gv_s.at[old2], gg_s.at[old2], gr_s.at[old2], s0_ref, tri_ref,
              m_ref, bd_ref, gn_ref, og_ref, sfin_ref, st_ref, o_ref, t=t, n_sub=n_sub, companion=companion)


def _gla_tokenwise(q_ref, k_ref, v_ref, g_ref, bd_ref, st_ref, o_ref, *, n_rows):
    row = lax.broadcasted_iota(I32, (GLA_SLOW_ROWS, 1), 0)

    def group(gi, carry):
        rows = pl.ds(pl.multiple_of(gi * GLA_SLOW_ROWS, GLA_SLOW_ROWS), GLA_SLOW_ROWS)
        q = q_ref[rows, :].astype(F32)
        k = k_ref[rows, :].astype(F32)
        v = v_ref[rows, :].astype(F32)
        decay = jnp.exp(g_ref[rows, :])
        o = jnp.zeros((GLA_SLOW_ROWS, N_HEADS_GLA * DV), F32)
        for j in range(GLA_SLOW_ROWS):
            only_j = row == j
            k_j = jnp.where(only_j, k, 0.0).astype(BF16)
            v_j = jnp.where(only_j, v, 0.0).astype(BF16)
            q_j = jnp.where(only_j, q, 0.0).astype(BF16)
            st = st_ref[...] * decay[j:j + 1, :] + _dot_tn(v_j, k_j) * bd_ref[...]
            st_ref[...] = st
            o = o + _dot_nt(q_j, st.astype(BF16))
        o_ref[rows, :] = o
        return carry

    lax.fori_loop(0, n_rows // GLA_SLOW_ROWS, group, 0)


def _gla_blocked(q_ref, k_ref, v_ref, g_ref, tri_ref, m_ref, bd_ref, st_ref, *, t, n_sub):
    nb = t // SUB
    n_rows = t * n_sub
    lanes = N_HEADS_GLA * DK

    def group_row(x, period, offset):
        g = x.reshape(n_rows // period, period, lanes)[:, offset:offset + 1, :]
        return jnp.broadcast_to(g, (n_rows // period, period, lanes)).reshape(n_rows, lanes)

    g_hi, g_lo = _split_bf16(g_ref[...])
    b = _dot(tri_ref[...], g_hi) + _dot(tri_ref[...], g_lo)
    q = q_ref[...].astype(F32)
    k = k_ref[...].astype(F32)
    qd = (q * jnp.exp(b - group_row(b, SUB, 0))).astype(BF16)
    pos = lax.broadcasted_iota(I32, (n_rows, lanes), 0) & (t - 1)
    k_parts = []
    for blk in range(nb):
        arg = jnp.where(pos < SUB * (blk + 1), group_row(b, t, SUB * blk) - b, NEG_BIG)
        k_parts.append((k * jnp.exp(arg)).astype(BF16))
    k_cat = jnp.concatenate(k_parts, axis=1)
    qd_cat = jnp.concatenate([qd] * nb, axis=1)
    q_dec = (q * jnp.exp(b)).astype(BF16)
    k_last = (k * jnp.exp(group_row(b, t, t - 1) - b)).astype(BF16)
    row_a = lax.broadcasted_iota(I32, (N_HEADS_GLA * t, t), 0) & (t - 1)
    col_a = lax.broadcasted_iota(I32, (N_HEADS_GLA * t, t), 1)

    o_intra, q_decayed, state_add, state_decay = [], [], [], []
    for u in range(n_sub):
        rows = slice(u * t, (u + 1) * t)
        v = v_ref[rows, :]
        lhs = jnp.concatenate([qd_cat[rows]] * N_HEADS_GLA, axis=0) * m_ref[...]
        a = _dot_nt(lhs, k_cat[rows])
        a = jnp.where(row_a >= col_a, a, 0.0).astype(BF16)
        o_full = _dot(a, v)
        o_intra.append(jnp.concatenate(
            [o_full[h * t:(h + 1) * t, h * DV:(h + 1) * DV] for h in range(N_HEADS_GLA)], axis=1))
        q_decayed.append(q_dec[rows])
        state_add.append(_dot_tn(v, k_last[rows]) * bd_ref[...])
        state_decay.append(jnp.exp(b[(u + 1) * t - 1:(u + 1) * t, :]))

    st = st_ref[...]
    o_parts = []
    for u in range(n_sub):
        o_parts.append(o_intra[u] + _dot_nt(q_decayed[u], st.astype(BF16)))
        st = st * state_decay[u] + state_add[u]
    st_ref[...] = st
    return jnp.concatenate(o_parts, axis=0) if n_sub > 1 else o_parts[0]


def _gla(gq, gk, gv, gg, gr, s0, tri, mask, bd, gn, *, t, n_sub, n_batch, n_steps, first_block, name):
    rows_per_step = t * n_sub

    def rows(width):
        return pl.BlockSpec((rows_per_step, width), lambda b, c: (first_block + b * n_steps + c, 0))

    def full(a):
        return pl.BlockSpec(a.shape, lambda b, c: (0,) * a.ndim)

    state = pl.BlockSpec((None,) + s0.shape[1:], lambda b, c: (b, 0, 0))
    return pl.pallas_call(
        functools.partial(_gla_kernel, t=t, n_sub=n_sub),
        grid=(n_batch, n_steps),
        in_specs=[rows(W_GQ), rows(W_GK), rows(W_GV), rows(W_GQ), rows(W_GR), state,
                  full(tri), full(mask), full(bd), full(gn)],
        out_specs=[pl.BlockSpec((rows_per_step, W_GV), lambda b, c: (b * n_steps + c, 0)), state],
        out_shape=[jax.ShapeDtypeStruct((n_batch * n_steps * rows_per_step, W_GV), BF16),
                   jax.ShapeDtypeStruct(s0.shape, F32)],
        scratch_shapes=[pltpu.VMEM(s0.shape[1:], F32), pltpu.VMEM((rows_per_step, W_GV), F32)],
        compiler_params=_cparams(2, "mixer"),
        name=name,
    )(gq, gk, gv, gg, gr, s0, tri, mask, bd, gn)


def _front_prompt(xp, consts, head_mask, sink_b, s0, tri, mask, bd, gn):
    def full(a):
        return pl.BlockSpec(a.shape, lambda b, c: (0,) * a.ndim)

    def computed(width):
        return pl.BlockSpec((TM, width), lambda b, c: (jnp.minimum(c, N_PROMPT_TILES - 1), 0))

    def mixed(width):
        return pl.BlockSpec((TM, width), lambda b, c: (jnp.maximum(c - 1, 0), 0))

    state = pl.BlockSpec((None,) + s0.shape[1:], lambda b, c: (b, 0, 0))
    ring2 = [pltpu.VMEM((2, TM, w), dt) for w, dt in
             (PROJ_OUTPUTS[0], PROJ_OUTPUTS[3], PROJ_OUTPUTS[4], PROJ_OUTPUTS[5], PROJ_OUTPUTS[6], PROJ_OUTPUTS[7])]
    ring3 = [pltpu.VMEM((3, TM, w), dt) for w, dt in (PROJ_OUTPUTS[1], PROJ_OUTPUTS[2])]
    return pl.pallas_call(
        functools.partial(_front_prompt_kernel, t=CHUNK, n_sub=TM // CHUNK),
        grid=(1, N_PROMPT_TILES + 1),
        in_specs=[computed(D_MODEL)] + [full(a) for a in consts] + [full(head_mask), full(sink_b), state,
                                                                   full(tri), full(mask), full(bd), full(gn)],
        out_specs=[computed(W_SK), computed(W_SV), mixed(W_SQ), mixed(W_GV), state],
        out_shape=[jax.ShapeDtypeStruct((N_PROMPT, W_SK), F32), jax.ShapeDtypeStruct((N_PROMPT, W_SV), F32),
                   jax.ShapeDtypeStruct((N_PROMPT, W_SQ), BF16), jax.ShapeDtypeStruct((N_PROMPT, W_GV), BF16),
                   jax.ShapeDtypeStruct(s0.shape, F32)],
        scratch_shapes=ring2 + ring3 + [pltpu.VMEM(s0.shape[1:], F32), pltpu.VMEM((TM, W_GV), F32)],
        compiler_params=_cparams(2, "front"),
        name="front_prompt",
    )(xp, *consts, head_mask, sink_b, s0, tri, mask, bd, gn)


def _tile_row_copies(hbm_ref, tile, vmem_ref, sem, to_hbm, rows=TM):
    copies = []
    for a in range(ROW_SUB):
        h = hbm_ref.at[pl.ds(tile * rows, rows), a, :]
        v = vmem_ref.at[:, pl.ds(a * LANES, LANES)]
        copies.append(pltpu.make_async_copy(v, h, sem) if to_hbm else pltpu.make_async_copy(h, v, sem))
    return copies


def _store_tile_rows(i, n_steps, outputs, row_buf, row_sem, rows=TM):
    buf_slot = lax.rem(i, 2)

    def store(j, tile, s):
        return _tile_row_copies(outputs[j][0], tile, row_buf.at[j, s], row_sem.at[j, s], True, rows)

    for j, (_, value) in enumerate(outputs):
        @pl.when(i >= 2)
        def _():
            for c in store(j, i - 2, buf_slot):
                c.wait()

        row_buf[j, buf_slot] = value
        for c in store(j, i, buf_slot):
            c.start()

        @pl.when(i == n_steps - 1)
        def _():
            for c in store(j, i - 1, 1 - buf_slot) + store(j, i, buf_slot):
                c.wait()


def _merge_kernel(oswp_ref, osws_ref, ogp_ref, ogs_ref, xp_ref, xs_ref, wo1_ref, wo2_ref, gf_ref, wrh_ref, wrl_ref,
                  br_ref, xrow_ref, hrow_ref, lg_ref, row_buf, row_sem):
    i = pl.program_id(0)
    is_prompt = i < N_PROMPT_TILES
    x = jnp.where(is_prompt, xp_ref[...], xs_ref[...])
    o_swa = jnp.where(is_prompt, oswp_ref[...], osws_ref[...])
    og = jnp.where(is_prompt, ogp_ref[...], ogs_ref[...])
    h = x + (_dot(o_swa, wo1_ref[...]) + _dot(og, wo2_ref[...]))
    xn = _rms(h) * gf_ref[...]
    x_hi, x_lo = _split_bf16(xn)
    logits = _dot(x_hi, wrh_ref[...]) + _dot(x_lo, wrh_ref[...]) + _dot(x_hi, wrl_ref[...]) + br_ref[...]
    lg_ref[...] = logits.T[:N_EXPERTS]
    _store_tile_rows(i, N_TILES, ((xrow_ref, xn), (hrow_ref, h)), row_buf, row_sem)


def _merge(o_swa_p, o_swa_s, og_p, og_s, xp, xs, wo1, wo2, gf, wrh, wrl, br):
    def prompt_rows(width):
        return pl.BlockSpec((TM, width), lambda i: (jnp.minimum(i, N_PROMPT_TILES - 1), 0))

    def sample_rows(width):
        return pl.BlockSpec((TM, width), lambda i: (0, 0))

    def full(a):
        return pl.BlockSpec(a.shape, lambda i: (0,) * a.ndim)

    consts = (wo1, wo2, gf, wrh, wrl, br)
    return pl.pallas_call(
        _merge_kernel,
        grid=(N_TILES,),
        in_specs=[prompt_rows(W_SQ), sample_rows(W_SQ), prompt_rows(W_GV), sample_rows(W_GV),
                  prompt_rows(D_MODEL), sample_rows(D_MODEL)] + [full(a) for a in consts],
        out_specs=[pl.BlockSpec(memory_space=pl.ANY)] * 2 + [pl.BlockSpec((N_EXPERTS, TM), lambda i: (0, i))],
        out_shape=[jax.ShapeDtypeStruct((N_ROWS,) + ROW_TILE, F32)] * 2 + [
                   jax.ShapeDtypeStruct((N_EXPERTS, N_ROWS), F32)],
        scratch_shapes=[pltpu.VMEM((2, 2, TM, D_MODEL), F32), pltpu.SemaphoreType.DMA((2, 2))],
        compiler_params=_cparams(1, "merge"),
        name="merge",
    )(o_swa_p, o_swa_s, og_p, og_s, xp, xs, *consts)


RT = 1280
N_ROUTER_STEPS = N_ROWS // RT


def _router_kernel(lg_ref, tri_ref, wrow_ref, ti_ref, rk_ref, cnt_ref, base_ref, row_buf, row_sem):
    i = pl.program_id(0)

    @pl.when(i == 0)
    def _():
        base_ref[...] = jnp.zeros_like(base_ref)

    logits_t = lg_ref[...]
    expert = lax.broadcasted_iota(I32, logits_t.shape, 0)
    slot = lax.broadcasted_iota(I32, (SLOT_ROWS, RT), 0)
    vals, hots = [], []
    ti = jnp.zeros((SLOT_ROWS, RT), I32)
    for kk in range(TOP_K):
        m = jnp.max(logits_t, axis=0, keepdims=True)
        idx = jnp.min(jnp.where(logits_t == m, expert, N_EXPERTS), axis=0, keepdims=True)
        hot = expert == idx
        logits_t = jnp.where(hot, NEG_BIG, logits_t)
        vals.append(m)
        hots.append(hot)
        ti = jnp.where(slot == kk, idx, ti)
    ti_ref[...] = ti
    exps = [jnp.exp(v - vals[0]) for v in vals]
    den = exps[0] + exps[1] + exps[2] + exps[3]
    tw_t = jnp.zeros((SLOT_ROWS, RT), F32)
    for kk in range(TOP_K):
        tw_t = jnp.where(slot == kk, exps[kk] / den, tw_t)
    eye = (lax.broadcasted_iota(I32, (SLOT_ROWS, LANES), 0)
           == lax.broadcasted_iota(I32, (SLOT_ROWS, LANES), 1)).astype(F32).astype(BF16)
    w_hi = tw_t.astype(BF16)
    w_mid, w_lo = _split_bf16(tw_t - w_hi.astype(F32))
    tw_col = _dot_tn(w_hi, eye) + _dot_tn(w_mid, eye) + _dot_tn(w_lo, eye)

    onehot_t = jnp.zeros(logits_t.shape, F32)
    for hot in hots:
        onehot_t = onehot_t + jnp.where(hot, 1.0, 0.0)
    before_t = _dot(onehot_t.astype(BF16), tri_ref[...]) + base_ref[:, 0:1]
    rk = jnp.zeros((SLOT_ROWS, RT), I32)
    for kk in range(TOP_K):
        r = jnp.sum(jnp.where(hots[kk], before_t, 0.0), axis=0, keepdims=True).astype(I32)
        rk = jnp.where(slot == kk, r, rk)
    rk_ref[...] = rk
    total = base_ref[...] + jnp.sum(onehot_t, axis=1, keepdims=True)
    base_ref[...] = total
    cnt_ref[...] = total.astype(I32)

    w_lanes = [jnp.broadcast_to(tw_col[:, kk:kk + 1], (RT, LANES)) for kk in range(TOP_K)]
    w_rows = jnp.concatenate(w_lanes + [jnp.zeros((RT, D_MODEL - TOP_K * LANES), F32)], axis=1)
    _store_tile_rows(i, N_ROUTER_STEPS, ((wrow_ref, w_rows),), row_buf, row_sem, RT)


def _router(logits, tri):
    return pl.pallas_call(
        _router_kernel,
        grid=(N_ROUTER_STEPS,),
        in_specs=[pl.BlockSpec((N_EXPERTS, RT), lambda i: (0, i)), pl.BlockSpec(tri.shape, lambda i: (0, 0))],
        out_specs=[pl.BlockSpec(memory_space=pl.ANY),
                   pl.BlockSpec((SLOT_ROWS, RT), lambda i: (0, i)),
                   pl.BlockSpec((SLOT_ROWS, RT), lambda i: (0, i)),
                   pl.BlockSpec((N_EXPERTS, LANES), lambda i: (0, 0))],
        out_shape=[jax.ShapeDtypeStruct((N_ROWS,) + ROW_TILE, F32),
                   jax.ShapeDtypeStruct((SLOT_ROWS, N_ROWS), I32),
                   jax.ShapeDtypeStruct((SLOT_ROWS, N_ROWS), I32),
                   jax.ShapeDtypeStruct((N_EXPERTS, LANES), I32)],
        scratch_shapes=[pltpu.VMEM((N_EXPERTS, LANES), F32), pltpu.VMEM((1, 2, RT, D_MODEL), F32),
                        pltpu.SemaphoreType.DMA((1, 2))],
        compiler_params=_cparams(1, "router"),
        name="router",
    )(logits, tri)


ISSUE_UNROLL = 4


def _row_copy(src_ref, src_row, dst_ref, dst_row, sem):
    return pltpu.make_async_copy(src_ref.at[pl.ds(src_row, 1)], dst_ref.at[pl.ds(dst_row, 1)], sem)


def _dispatch_kernel(dest_ref, end_ref, x_ref, xs_ref, zero_ref, sem, zsem):
    i = pl.program_id(0)
    base = i * TM

    @pl.when(i == 0)
    def _():
        zero_ref[...] = jnp.zeros_like(zero_ref)

        def tail_copy(e):
            last = jnp.maximum(end_ref[e] - TM, 0)
            return pltpu.make_async_copy(zero_ref, xs_ref.at[pl.ds(pl.multiple_of(last, TM), TM)], zsem)

        def fill(e, carry):
            tail_copy(e).start()
            return carry

        def fill_wait(e, carry):
            tail_copy(e).wait()
            return carry

        lax.fori_loop(0, N_EXPERTS, fill, 0)
        lax.fori_loop(0, N_EXPERTS, fill_wait, 0)

        def unused_copy(t):
            return pltpu.make_async_copy(zero_ref, xs_ref.at[pl.ds(pl.multiple_of(t * TM, TM), TM)], zsem)

        def fill_unused(t, carry):
            unused_copy(t).start()
            unused_copy(t).wait()
            return carry

        lax.fori_loop(end_ref[N_EXPERTS - 1] // TM, N_EXPERT_TILES, fill_unused, 0)

    def issue(n, carry):
        for kk in range(TOP_K):
            _row_copy(x_ref, n, xs_ref, dest_ref[kk * N_ROWS + base + n], sem).start(priority=kk % 2)
        return carry

    lax.fori_loop(0, TM, issue, 0, unroll=ISSUE_UNROLL)

    for kk in range(TOP_K):
        pltpu.make_async_copy(x_ref, xs_ref.at[pl.ds(0, TM)], sem).wait()


def _dispatch(dest, end, x_packed):
    return pl.pallas_call(
        _dispatch_kernel,
        grid_spec=pltpu.PrefetchScalarGridSpec(
            num_scalar_prefetch=2,
            grid=(N_TILES,),
            in_specs=[pl.BlockSpec((TM,) + ROW_TILE, lambda i, d, e: (i, 0, 0))],
            out_specs=pl.BlockSpec(memory_space=pl.ANY),
            scratch_shapes=[pltpu.VMEM((TM,) + ROW_TILE, F32), pltpu.SemaphoreType.DMA,
                            pltpu.SemaphoreType.DMA],
        ),
        out_shape=jax.ShapeDtypeStruct((N_SORTED_ROWS,) + ROW_TILE, F32),
        compiler_params=_cparams(1, "dispatch"),
        name="dispatch",
    )(dest, end, x_packed)


CAST_ROWS = 128


def _moe_kernel(te_ref, nu_ref, nx_ref, xs_ref, wg_ref, bg_ref, wu_ref, bu_ref, wd_ref, bd_ref, ys_ref,
                w_stage, w_bf, x_buf, y_buf, zero_buf, w_sem, in_sem, out_sem, zero_sem):
    t = pl.program_id(0)
    n_used = nu_ref[0]
    slot = lax.rem(t, 2)
    e = te_ref[t]
    e_prev = te_ref[jnp.maximum(t - 1, 0)]

    def load(tile, s):
        return _tile_row_copies(xs_ref, tile, x_buf.at[s], in_sem.at[s], to_hbm=False)

    def store(tile, s):
        return _tile_row_copies(ys_ref, tile, y_buf.at[s], out_sem.at[s], to_hbm=True)

    def weight_copies(expert):
        return [pltpu.make_async_copy(w.at[expert], w_stage.at[j], w_sem.at[j])
                for j, w in enumerate((wg_ref, wu_ref, wd_ref))]

    @pl.when(t == 0)
    def _():
        for c in weight_copies(e) + load(0, 0):
            c.start()

    @pl.when(t + 1 < n_used)
    def _():
        for c in load(t + 1, 1 - slot):
            c.start()

    @pl.when((t == 0) | (e != e_prev))
    def _():
        for c in weight_copies(e):
            c.wait()

        def cast(r, carry):
            sl = pl.ds(pl.multiple_of(r * CAST_ROWS, CAST_ROWS), CAST_ROWS)
            for j in range(3):
                w_bf[j, sl, :] = w_stage[j, sl, :].astype(BF16)
            return carry

        lax.fori_loop(0, D_MODEL // CAST_ROWS, cast, 0)
        e_next = nx_ref[e]

        @pl.when(e_next >= 0)
        def _():
            for c in weight_copies(e_next):
                c.start()

    @pl.when(t < n_used)
    def _():
        for c in load(t, slot):
            c.wait()

        @pl.when(t >= 2)
        def _():
            for c in store(t - 2, slot):
                c.wait()

        x = x_buf[slot].astype(BF16)
        gate = jnp.minimum(_dot(x, w_bf[0]) + bg_ref[...], SWIGLU_LIMIT)
        up = jnp.clip(_dot(x, w_bf[1]) + bu_ref[...], -SWIGLU_LIMIT, SWIGLU_LIMIT)
        hdn = (up + 1.0) * gate * (1.0 / (1.0 + jnp.exp(-SWIGLU_ALPHA * gate)))
        y_buf[slot] = _dot(hdn.astype(BF16), w_bf[2]) + bd_ref[...]
        for c in store(t, slot):
            c.start()

    @pl.when(t >= n_used)
    def _():
        zero_buf[...] = jnp.zeros_like(zero_buf)
        fill = pltpu.make_async_copy(zero_buf, ys_ref.at[pl.ds(t * TM, TM)], zero_sem)
        fill.start()
        fill.wait()

    @pl.when(t == N_EXPERT_TILES - 1)
    def _():
        @pl.when(n_used >= 2)
        def _():
            for c in store(n_used - 2, lax.rem(n_used, 2)):
                c.wait()

        for c in store(n_used - 1, lax.rem(n_used - 1, 2)):
            c.wait()


def _moe(tile_expert, n_used, next_expert, xs, w_gate, b_gate, w_up, b_up, w_down, b_down):
    hbm = pl.BlockSpec(memory_space=pl.ANY)
    bias = pl.BlockSpec((None, 1, D_MODEL), lambda t, te, nu, nx: (te[t], 0, 0))
    return pl.pallas_call(
        _moe_kernel,
        grid_spec=pltpu.PrefetchScalarGridSpec(
            num_scalar_prefetch=3,
            grid=(N_EXPERT_TILES,),
            in_specs=[hbm, hbm, bias, hbm, bias, hbm, bias],
            out_specs=hbm,
            scratch_shapes=[pltpu.VMEM((3, D_MODEL, D_MODEL), F32), pltpu.VMEM((3, D_MODEL, D_MODEL), BF16),
                            pltpu.VMEM((2, TM, D_MODEL), F32), pltpu.VMEM((2, TM, D_MODEL), F32),
                            pltpu.VMEM((TM,) + ROW_TILE, F32), pltpu.SemaphoreType.DMA((3,)),
                            pltpu.SemaphoreType.DMA((2,)), pltpu.SemaphoreType.DMA((2,)),
                            pltpu.SemaphoreType.DMA],
        ),
        out_shape=jax.ShapeDtypeStruct((N_SORTED_ROWS,) + ROW_TILE, F32),
        compiler_params=_cparams(1, "experts"),
        name="experts",
    )(tile_expert, n_used, next_expert, xs, w_gate, b_gate, w_up, b_up, w_down, b_down)


def _combine_kernel(dest_ref, h_ref, w_ref, ys_ref, yp_ref, ysm_ref, g_ref, out_ref, sem):
    i = pl.program_id(0)
    slot = lax.rem(i, 2)

    def issue_tile(tile, s):
        base = tile * TM

        def issue(n, carry):
            for kk in range(TOP_K):
                _row_copy(ys_ref, dest_ref[kk * N_ROWS + base + n], g_ref.at[s], kk * TM + n,
                          sem.at[s]).start(priority=kk % 2)
            return carry

        lax.fori_loop(0, TM, issue, 0, unroll=ISSUE_UNROLL)

    @pl.when(i == 0)
    def _():
        issue_tile(0, 0)

    @pl.when(i + 1 < N_TILES)
    def _():
        issue_tile(i + 1, 1 - slot)

    for kk in range(TOP_K):
        pltpu.make_async_copy(ys_ref.at[pl.ds(0, TM)], g_ref.at[slot, pl.ds(kk * TM, TM)], sem.at[slot]).wait()

    w = w_ref[...]
    acc = None
    for kk in range(TOP_K):
        part = jnp.broadcast_to(w[:, kk:kk + 1, :], (TM,) + ROW_TILE) * g_ref[slot, kk * TM:(kk + 1) * TM]
        acc = part if acc is None else acc + part
    out_ref[...] = h_ref[...] + acc

    def write(y_ref):
        for a in range(ROW_SUB):
            y_ref[:, a * LANES:(a + 1) * LANES] = out_ref[:, a, :]

    @pl.when(i < N_PROMPT_TILES)
    def _():
        write(yp_ref)

    @pl.when(i >= N_PROMPT_TILES)
    def _():
        write(ysm_ref)


def _combine(dest, h_rows, w_rows, ys):
    tile = lambda index: pl.BlockSpec((TM,) + ROW_TILE, index)
    return pl.pallas_call(
        _combine_kernel,
        grid_spec=pltpu.PrefetchScalarGridSpec(
            num_scalar_prefetch=1,
            grid=(N_TILES,),
            in_specs=[tile(lambda i, d: (i, 0, 0)), tile(lambda i, d: (i, 0, 0)),
                      pl.BlockSpec(memory_space=pl.ANY)],
            out_specs=[pl.BlockSpec((TM, D_MODEL), lambda i, d: (jnp.minimum(i, N_PROMPT_TILES - 1), 0)),
                       pl.BlockSpec((TM, D_MODEL), lambda i, d: (0, 0))],
            scratch_shapes=[pltpu.VMEM((2, TOP_K * TM) + ROW_TILE, F32), pltpu.VMEM((TM,) + ROW_TILE, F32),
                            pltpu.SemaphoreType.DMA((2,))],
        ),
        out_shape=[jax.ShapeDtypeStruct((N_PROMPT, D_MODEL), F32),
                   jax.ShapeDtypeStruct((N_SAMPLE, D_MODEL), F32)],
        compiler_params=_cparams(1, "combine"),
        name="combine",
    )(dest, h_rows, w_rows, ys)


def _block_diag_ones(n, blk):
    idx = np.arange(n) // blk
    return (idx[:, None] == idx[None, :]).astype(np.float32)


def _swa_head_mask(tq):
    row_head = np.arange(4 * tq)[:, None] // tq
    lane_head = np.arange(4 * HD)[None, :] // HD
    return jnp.asarray((row_head == lane_head).astype(np.float32), BF16)


def _gla_masks(t, n_sub):
    nb = t // SUB
    lanes = N_HEADS_GLA * DK
    tri = jnp.asarray(np.kron(np.eye(n_sub, dtype=np.float32), np.tril(np.ones((t, t), np.float32))), BF16)
    row = np.arange(N_HEADS_GLA * t)
    col = np.arange(nb * lanes)
    same_head = (row[:, None] // t) == ((col[None, :] % lanes) // DK)
    same_blk = ((row[:, None] % t) // SUB) == (col[None, :] // lanes)
    mask = jnp.asarray((same_head & same_blk).astype(np.float32), BF16)
    return tri, mask


def _sink_rows(sinks, tq):
    s = jnp.repeat(sinks.astype(F32).reshape(2, 4), tq, axis=1)
    return jnp.broadcast_to(s[:, None, :], (2, 8, 4 * tq))


def kernel(x_prompt, x_sample, state_gla, cache_swa_k, cache_swa_v, norm_mix_g, w_in, w_gla_a2, b_gla_a, q_norm_g,
           k_norm_g, swa_sinks, gla_norm_g, w_out, norm_ffn_g, w_router, b_router, w_gate, b_gate, w_up, b_up,
           w_down, b_down):
    xp = x_prompt.reshape(N_PROMPT, D_MODEL)
    xs = x_sample.reshape(N_SAMPLE, D_MODEL)

    w_in0 = w_in[0]
    w_main = w_in0[:, :W_MAIN].astype(BF16)
    w_ga = jnp.pad(w_in0[:, OFF_GA:], ((0, 0), (0, LANES - W_GA))).astype(BF16)
    w_a2 = jnp.pad(w_gla_a2[0], ((0, LANES - W_GA), (0, 0))).astype(BF16)
    b_a = b_gla_a[0].reshape(1, -1)
    qg = (jnp.tile(q_norm_g[0], N_HEADS_SWA) * (HD ** -0.5)).reshape(1, -1)
    kg = jnp.tile(k_norm_g[0], 2).reshape(1, -1)
    bdq = jnp.asarray(_block_diag_ones(W_SQ, HD), BF16)
    bdk = jnp.asarray(_block_diag_ones(W_SK, HD), BF16)

    proj_consts = (norm_mix_g[0].reshape(1, -1), w_main, w_ga, w_a2, b_a, qg, kg, bdq, bdk)
    sq, sk, sv, gq, gk, gv, gg, gr = _proj_sample(xs, proj_consts)

    cache_k = cache_swa_k[0].reshape(N_STREAMS, WINDOW, 2 * HD)
    cache_v = cache_swa_v[0].reshape(N_STREAMS, WINDOW, 2 * HD)
    o_swa_s = _swa_sample(sq, sk, sv, cache_k, cache_v, _swa_head_mask(T_SAMPLE),
                          _sink_rows(swa_sinks[0], T_SAMPLE))

    bd_state = jnp.asarray(_block_diag_ones(N_HEADS_GLA, 1).repeat(DV, axis=0).repeat(DK, axis=1), F32)
    gn = gla_norm_g[0].reshape(1, -1)
    tri_p, mask_p = _gla_masks(CHUNK, TM // CHUNK)
    tri_s, mask_s = _gla_masks(T_SAMPLE, 1)
    s0_p = jnp.zeros((1, N_HEADS_GLA * DV, N_HEADS_GLA * DK), F32)
    sk_p, sv_p, o_swa_p, og_p, sfin_p = _front_prompt(xp, proj_consts, _swa_head_mask(CHUNK),
                                                      _sink_rows(swa_sinks[0], CHUNK), s0_p, tri_p, mask_p,
                                                      bd_state, gn)
    eye = jnp.eye(N_HEADS_GLA, dtype=F32)
    s0_s = jnp.einsum('bhde,hg->bhegd', state_gla[0].astype(F32), eye).reshape(
        N_STREAMS, N_HEADS_GLA * DV, N_HEADS_GLA * DK)
    og_s, sfin_s = _gla(gq, gk, gv, gg, gr, s0_s, tri_s, mask_s, bd_state, gn, t=T_SAMPLE, n_sub=1,
                        n_batch=N_STREAMS, n_steps=1, first_block=0, name="gla_sample")

    def unpack_state(sfin):
        s = sfin.reshape(-1, N_HEADS_GLA, DV, N_HEADS_GLA, DK)
        s = jnp.stack([s[:, h, :, h, :] for h in range(N_HEADS_GLA)], axis=1)
        return jnp.transpose(s, (0, 1, 3, 2))[None]

    w_out0 = w_out[0].astype(BF16)
    wr = jnp.pad(w_router[0], ((0, 0), (0, LANES - N_EXPERTS)))
    wr_hi = wr.astype(BF16)
    wr_lo = (wr - wr_hi.astype(F32)).astype(BF16)
    br = jnp.pad(b_router[0].astype(F32), (0, LANES - N_EXPERTS)).reshape(1, -1)
    x_rows, h_rows, logits = _merge(o_swa_p, o_swa_s, og_p, og_s, xp, xs, w_out0[:W_SQ], w_out0[W_SQ:],
                                    norm_ffn_g[0].reshape(1, -1), wr_hi, wr_lo, br)
    earlier = jnp.asarray(np.triu(np.ones((RT, RT), np.float32), 1), BF16)
    w_rows, top_i, rank, counts = _router(logits, earlier)

    counts = counts[:, 0]
    padded = (counts + TM - 1) // TM * TM
    end = jnp.cumsum(padded)
    start = end - padded
    experts = jnp.arange(N_EXPERTS, dtype=I32)
    is_e = top_i[:TOP_K, :, None] == experts
    dest = (rank[:TOP_K] + jnp.sum(jnp.where(is_e, start, 0), axis=-1)).reshape(-1).astype(I32)
    n_used = (end[-1] // TM).astype(I32)
    tiles = jnp.minimum(jnp.arange(N_EXPERT_TILES, dtype=I32), n_used - 1)
    tile_expert = jnp.sum((tiles[:, None] * TM >= end[None, :]).astype(I32), axis=1)
    later_nonempty = (experts[None, :] > experts[:, None]) & (padded[None, :] > 0)
    next_expert = jnp.min(jnp.where(later_nonempty, experts[None, :], N_EXPERTS), axis=1)
    next_expert = jnp.where(next_expert < N_EXPERTS, next_expert, -1).astype(I32)

    xs_sorted = _dispatch(dest, end.astype(I32), x_rows)
    ys = _moe(tile_expert, n_used.reshape(1), next_expert, xs_sorted, w_gate[0], b_gate[0].reshape(N_EXPERTS, 1, -1),
              w_up[0], b_up[0].reshape(N_EXPERTS, 1, -1), w_down[0], b_down[0].reshape(N_EXPERTS, 1, -1))
    y_p, y_s = _combine(dest, h_rows, w_rows, ys)

    sk_s = sk.reshape(N_STREAMS, T_SAMPLE, 2 * HD)
    sv_s = sv.reshape(N_STREAMS, T_SAMPLE, 2 * HD)
    kc_s = jnp.concatenate([cache_k[:, T_SAMPLE:], sk_s], axis=1).reshape(1, N_STREAMS, WINDOW, 2, HD)
    vc_s = jnp.concatenate([cache_v[:, T_SAMPLE:], sv_s], axis=1).reshape(1, N_STREAMS, WINDOW, 2, HD)
    kc_p = sk_p[N_PROMPT - WINDOW:].reshape(1, 1, WINDOW, 2, HD)
    vc_p = sv_p[N_PROMPT - WINDOW:].reshape(1, 1, WINDOW, 2, HD)
    return (y_p.reshape(1, N_PROMPT, D_MODEL), y_s.reshape(N_STREAMS, T_SAMPLE, D_MODEL),
            unpack_state(sfin_p), kc_p, vc_p, unpack_state(sfin_s), kc_s, vc_s)
```

```python
import functools

import numpy as np
import jax
import jax.numpy as jnp
from jax import lax
from jax.experimental import pallas as pl
from jax.experimental.pallas import tpu as pltpu

F32 = jnp.float32
BF16 = jnp.bfloat16
I32 = jnp.int32

D_MODEL = 1024
N_PROMPT = 16384
N_STREAMS = 8
T_SAMPLE = 32
N_SAMPLE = N_STREAMS * T_SAMPLE
N_ROWS = N_PROMPT + N_SAMPLE
EPS = 1e-6

CHUNK = 64
SUB = 16
N_HEADS_SWA = 8
HD = 64
WINDOW = 128
N_HEADS_GLA = 4
DK = 64
DV = 128
GLA_TAU = 16.0
N_EXPERTS = 32
TOP_K = 4
SWIGLU_ALPHA = 1.702
SWIGLU_LIMIT = 7.0

TM = 256
N_TILES = N_ROWS // TM
N_PROMPT_TILES = N_PROMPT // TM
N_ASSIGN = N_ROWS * TOP_K
N_EXPERT_TILES = N_ASSIGN // TM + N_EXPERTS
N_SORTED_ROWS = N_EXPERT_TILES * TM
LANES = 128
SLOT_ROWS = 16
NEG_BIG = -1e30

W_SQ, W_SK, W_SV, W_GQ, W_GK, W_GV, W_GR, W_GA = 512, 128, 128, 256, 256, 512, 512, 16
OFF_SQ = 0
OFF_SK = OFF_SQ + W_SQ
OFF_SV = OFF_SK + W_SK
OFF_GQ = OFF_SV + W_SV
OFF_GK = OFF_GQ + W_GQ
OFF_GV = OFF_GK + W_GK
OFF_GR = OFF_GV + W_GV
OFF_GA = OFF_GR + W_GR
W_MAIN = OFF_GA


VMEM_MIB = dict(proj=40, front=48, mixer=32, merge=32, router=48, dispatch=32, experts=52, combine=40)


def _cparams(n_grid_axes, call):
    return pltpu.CompilerParams(dimension_semantics=("arbitrary",) * n_grid_axes,
                                vmem_limit_bytes=VMEM_MIB[call] * 2 ** 20)


def _dot(a, b):
    return jnp.dot(a, b, preferred_element_type=F32)


def _dot_nt(a, b):
    return lax.dot_general(a, b, (((1,), (1,)), ((), ())), preferred_element_type=F32)


def _dot_tn(a, b):
    return lax.dot_general(a, b, (((0,), (0,)), ((), ())), preferred_element_type=F32)


def _split_bf16(x):
    hi = x.astype(BF16)
    lo = (x - hi.astype(F32)).astype(BF16)
    return hi, lo


def _rms(x):
    return x * lax.rsqrt(jnp.mean(x * x, axis=-1, keepdims=True) + EPS)


ROW_SUB = 8
ROW_TILE = (ROW_SUB, LANES)


PROJ_OUTPUTS = ((W_SQ, BF16), (W_SK, F32), (W_SV, F32), (W_GQ, BF16), (W_GK, BF16), (W_GV, BF16),
                (W_GQ, F32), (W_GR, BF16))
N_PROJ_CONSTS = 9


def _proj_tile(x, g_ref, w_ref, wga_ref, wa2_ref, ba_ref, qg_ref, kg_ref, bdq_ref, bdk_ref,
               sq_ref, sk_ref, sv_ref, gq_ref, gk_ref, gv_ref, gg_ref, gr_ref):
    xb = (_rms(x) * g_ref[...]).astype(BF16)

    def seg(off, width):
        return _dot(xb, w_ref[:, off:off + width])

    def head_norm(u, bd_ref):
        hi, lo = _split_bf16(u * u)
        ss = _dot(hi, bd_ref[...]) + _dot(lo, bd_ref[...])
        return u * lax.rsqrt(ss * (1.0 / HD) + EPS)

    sq_ref[...] = (head_norm(seg(OFF_SQ, W_SQ), bdq_ref) * qg_ref[...]).astype(BF16)
    sk_ref[...] = head_norm(seg(OFF_SK, W_SK), bdk_ref) * kg_ref[...]
    sv_ref[...] = seg(OFF_SV, W_SV)
    gq_ref[...] = (seg(OFF_GQ, W_GQ) * (DK ** -0.5)).astype(BF16)
    gk_ref[...] = seg(OFF_GK, W_GK).astype(BF16)
    gv_ref[...] = seg(OFF_GV, W_GV).astype(BF16)
    gr_ref[...] = seg(OFF_GR, W_GR).astype(BF16)
    ga = _dot(xb, wga_ref[...]).astype(BF16)
    z = _dot(ga, wa2_ref[...]) + ba_ref[...]
    log_sig = jnp.minimum(z, 0.0) - jnp.log(1.0 + jnp.exp(-jnp.abs(z)))
    gg_ref[...] = log_sig * (1.0 / GLA_TAU)


def _proj_sample_kernel(x_ref, *refs):
    _proj_tile(x_ref[...], *refs)


def _proj_sample(xs, consts):
    def full(a):
        return pl.BlockSpec(a.shape, lambda i: (0,) * a.ndim)

    return pl.pallas_call(
        _proj_sample_kernel,
        grid=(1,),
        in_specs=[full(xs)] + [full(a) for a in consts],
        out_specs=[pl.BlockSpec((N_SAMPLE, w), lambda i: (0, 0)) for w, _ in PROJ_OUTPUTS],
        out_shape=[jax.ShapeDtypeStruct((N_SAMPLE, w), dt) for w, dt in PROJ_OUTPUTS],
        compiler_params=_cparams(1, "proj"),
        name="proj_sample",
    )(xs, *consts)


def _dup_kv_heads(x):
    r = pltpu.roll(x, HD, axis=1)
    lo = lax.broadcasted_iota(I32, x.shape, 1) < HD
    out = []
    for a in (jnp.where(lo, x, r), jnp.where(lo, r, x)):
        out.append(jnp.concatenate([a, a], axis=1).astype(BF16))
    return out


def _swa_blocks(q_blocks, k_blocks, v_blocks, sinks, valids, head_mask):
    tq = q_blocks[0].shape[0]
    scores = []
    for q, k, valid in zip(q_blocks, k_blocks, valids):
        s_t = _dot_nt(k, jnp.concatenate([q] * 4, axis=0) * head_mask)
        scores.append(s_t if valid is None else jnp.where(valid, s_t, -jnp.inf))
    probs = []
    for s_t, sink in zip(scores, sinks):
        m = jnp.maximum(jnp.max(s_t, axis=0, keepdims=True), sink)
        p_t = jnp.exp(s_t - m)
        den = jnp.sum(p_t, axis=0, keepdims=True) + jnp.exp(sink - m)
        probs.append((p_t / den).astype(BF16))
    lane_head = lax.broadcasted_iota(I32, (tq, 4 * HD), 1) // HD
    outs = []
    for p_t, v in zip(probs, v_blocks):
        o_full = _dot_tn(p_t, v)
        o = jnp.zeros((tq, 4 * HD), F32)
        for a in range(4):
            o = o + jnp.where(lane_head == a, o_full[a * tq:(a + 1) * tq], 0.0)
        outs.append(o)
    return outs


def _swa_prompt_tile(i, q_ref, kp_ref, kc_ref, vp_ref, vc_ref, hm_ref, sink_ref, o_ref):
    k_dup = _dup_kv_heads(jnp.concatenate([kp_ref[...], kc_ref[...]], axis=0))
    v_dup = _dup_kv_heads(jnp.concatenate([vp_ref[...], vc_ref[...]], axis=0))
    sink_row = [sink_ref[j][0:1, :] for j in range(2)]
    span = WINDOW + CHUNK
    key = lax.broadcasted_iota(I32, (span, 4 * CHUNK), 0)
    qs, ks, vs, sinks, valids, where = [], [], [], [], [], []
    for c in range(TM // CHUNK):
        lo = CHUNK * c
        valid = (i * TM - WINDOW + lo + key) >= 0
        for j in range(2):
            qs.append(q_ref[lo:lo + CHUNK, 4 * HD * j:4 * HD * (j + 1)])
            ks.append(k_dup[j][lo:lo + span])
            vs.append(v_dup[j][lo:lo + span])
            sinks.append(sink_row[j])
            valids.append(valid)
            where.append((lo, j))
    outs = _swa_blocks(qs, ks, vs, sinks, valids, hm_ref[...])
    for (lo, j), o in zip(where, outs):
        o_ref[lo:lo + CHUNK, 4 * HD * j:4 * HD * (j + 1)] = o.astype(BF16)


def _swa_sample_kernel(q_ref, kc_ref, kn_ref, vc_ref, vn_ref, hm_ref, sink_ref, o_ref):
    k_dup = _dup_kv_heads(jnp.concatenate([kc_ref[...], kn_ref[...]], axis=0))
    v_dup = _dup_kv_heads(jnp.concatenate([vc_ref[...], vn_ref[...]], axis=0))
    sink_row = [sink_ref[j][0:1, :] for j in range(2)]
    qs = [q_ref[:, 4 * HD * j:4 * HD * (j + 1)] for j in range(2)]
    outs = _swa_blocks(qs, k_dup, v_dup, sink_row, [None, None], hm_ref[...])
    o_ref[...] = jnp.concatenate(outs, axis=1).astype(BF16)


def _swa_sample(sq, sk, sv, cache_k, cache_v, head_mask, sink_b):
    new = lambda width: pl.BlockSpec((T_SAMPLE, width), lambda b: (b, 0))
    cache = pl.BlockSpec((None, WINDOW, 2 * HD), lambda b: (b, 0, 0))
    return pl.pallas_call(
        _swa_sample_kernel,
        grid=(N_STREAMS,),
        in_specs=[new(W_SQ), cache, new(2 * HD), cache, new(2 * HD),
                  pl.BlockSpec(head_mask.shape, lambda b: (0, 0)),
                  pl.BlockSpec(sink_b.shape, lambda b: (0, 0, 0))],
        out_specs=pl.BlockSpec((T_SAMPLE, W_SQ), lambda b: (b, 0)),
        out_shape=jax.ShapeDtypeStruct((N_SAMPLE, W_SQ), BF16),
        compiler_params=_cparams(1, "mixer"),
        name="swa_sample",
    )(sq, cache_k, sk, cache_v, sv, head_mask, sink_b)


GLA_SAFE_EXP = 80.0
GLA_SLOW_ROWS = 16


def _gla_step(c, q_ref, k_ref, v_ref, g_ref, gr_ref, s0_ref, tri_ref, m_ref, bd_ref, gn_ref,
              og_ref, sfin_ref, st_ref, o_ref, *, t, n_sub, companion=None):
    @pl.when(c == 0)
    def _():
        st_ref[...] = s0_ref[...]

    n_rows = t * n_sub
    blocked_is_safe = (SUB - 1) * jnp.max(-g_ref[...]) <= GLA_SAFE_EXP

    def emit(o):
        gr = gr_ref[...].astype(F32)
        gate = gr / (1.0 + jnp.exp(-gr))
        outs = []
        for h in range(N_HEADS_GLA):
            sl = slice(h * DV, (h + 1) * DV)
            outs.append(_rms(o[:, sl]) * gn_ref[...] * gate[:, sl])
        og_ref[...] = jnp.concatenate(outs, axis=1).astype(BF16)

    @pl.when(blocked_is_safe)
    def _():
        if companion is not None:
            companion()
        emit(_gla_blocked(q_ref, k_ref, v_ref, g_ref, tri_ref, m_ref, bd_ref, st_ref, t=t, n_sub=n_sub))

    @pl.when(jnp.logical_not(blocked_is_safe))
    def _():
        if companion is not None:
            companion()
        _gla_tokenwise(q_ref, k_ref, v_ref, g_ref, bd_ref, st_ref, o_ref, n_rows=n_rows)
        emit(o_ref[...])

    @pl.when(c == pl.num_programs(1) - 1)
    def _():
        sfin_ref[...] = st_ref[...]


def _gla_kernel(*refs, t, n_sub):
    _gla_step(pl.program_id(1), *refs, t=t, n_sub=n_sub)


def _front_prompt_kernel(*refs, t, n_sub):
    x_ref = refs[0]
    consts = refs[1:1 + N_PROJ_CONSTS]
    hm_ref, sink_ref, s0_ref, tri_ref, m_ref, bd_ref, gn_ref = refs[1 + N_PROJ_CONSTS:8 + N_PROJ_CONSTS]
    sk_out, sv_out, o_swa_ref, og_ref, sfin_ref = refs[8 + N_PROJ_CONSTS:13 + N_PROJ_CONSTS]
    sq_s, gq_s, gk_s, gv_s, gg_s, gr_s, sk_s, sv_s, st_ref, o_ref = refs[13 + N_PROJ_CONSTS:]
    c = pl.program_id(1)

    @pl.when(c == 0)
    def _():
        for ring in (sq_s, gq_s, gk_s, gv_s, gg_s, gr_s, sk_s, sv_s):
            ring[...] = jnp.zeros_like(ring)

    new2, old2 = lax.rem(c, 2), lax.rem(c + 1, 2)
    new3, old3, older3 = lax.rem(c, 3), lax.rem(c + 2, 3), lax.rem(c + 1, 3)
    half = pl.ds(TM // 2, TM // 2)

    def companion():
        _proj_tile(x_ref[...], *consts, sq_s.at[new2], sk_s.at[new3], sv_s.at[new3], gq_s.at[new2],
                   gk_s.at[new2], gv_s.at[new2], gg_s.at[new2], gr_s.at[new2])
        sk_out[...] = sk_s[new3]
        sv_out[...] = sv_s[new3]
        _swa_prompt_tile(c - 1, sq_s.at[old2], sk_s.at[older3, half], sk_s.at[old3], sv_s.at[older3, half],
                         sv_s.at[old3], hm_ref, sink_ref, o_swa_ref)

    _gla_step(c, gq_s.at[old2], gk_s.at[old2], gv_s.at[old2], gg_s.at[old2], gr_s.at[old2], s0_ref, tri_ref,
              m_ref, bd_ref, gn_ref, og_ref, sfin_ref, st_ref, o_ref, t=t, n_sub=n_sub, companion=companion)


def _gla_tokenwise(q_ref, k_ref, v_ref, g_ref, bd_ref, st_ref, o_ref, *, n_rows):
    row = lax.broadcasted_iota(I32, (GLA_SLOW_ROWS, 1), 0)

    def group(gi, carry):
        rows = pl.ds(pl.multiple_of(gi * GLA_SLOW_ROWS, GLA_SLOW_ROWS), GLA_SLOW_ROWS)
        q = q_ref[rows, :].astype(F32)
        k = k_ref[rows, :].astype(F32)
        v = v_ref[rows, :].astype(F32)
        decay = jnp.exp(g_ref[rows, :])
        o = jnp.zeros((GLA_SLOW_ROWS, N_HEADS_GLA * DV), F32)
        for j in range(GLA_SLOW_ROWS):
            only_j = row == j
            k_j = jnp.where(only_j, k, 0.0).astype(BF16)
            v_j = jnp.where(only_j, v, 0.0).astype(BF16)
            q_j = jnp.where(only_j, q, 0.0).astype(BF16)
            st = st_ref[...] * decay[j:j + 1, :] + _dot_tn(v_j, k_j) * bd_ref[...]
            st_ref[...] = st
            o = o + _dot_nt(q_j, st.astype(BF16))
        o_ref[rows, :] = o
        return carry

    lax.fori_loop(0, n_rows // GLA_SLOW_ROWS, group, 0)


def _gla_blocked(q_ref, k_ref, v_ref, g_ref, tri_ref, m_ref, bd_ref, st_ref, *, t, n_sub):
    nb = t // SUB
    n_rows = t * n_sub
    lanes = N_HEADS_GLA * DK

    def group_row(x, period, offset):
        g = x.reshape(n_rows // period, period, lanes)[:, offset:offset + 1, :]
        return jnp.broadcast_to(g, (n_rows // period, period, lanes)).reshape(n_rows, lanes)

    g_hi, g_lo = _split_bf16(g_ref[...])
    b = _dot(tri_ref[...], g_hi) + _dot(tri_ref[...], g_lo)
    q = q_ref[...].astype(F32)
    k = k_ref[...].astype(F32)
    qd = (q * jnp.exp(b - group_row(b, SUB, 0))).astype(BF16)
    pos = lax.broadcasted_iota(I32, (n_rows, lanes), 0) & (t - 1)
    k_parts = []
    for blk in range(nb):
        arg = jnp.where(pos < SUB * (blk + 1), group_row(b, t, SUB * blk) - b, NEG_BIG)
        k_parts.append((k * jnp.exp(arg)).astype(BF16))
    head_rows = jnp.concatenate([m_ref[h * t:h * t + SUB, 0:lanes] for h in range(N_HEADS_GLA)], axis=0)
    q_dec = (q * jnp.exp(b)).astype(BF16)
    k_last = (k * jnp.exp(group_row(b, t, t - 1) - b)).astype(BF16)
    row_a = lax.broadcasted_iota(I32, (N_HEADS_GLA * t, t), 0) & (t - 1)
    col_a = lax.broadcasted_iota(I32, (N_HEADS_GLA * t, t), 1)

    o_intra, q_decayed, state_add, state_decay = [], [], [], []
    for u in range(n_sub):
        rows = slice(u * t, (u + 1) * t)
        v = v_ref[rows, :]
        pieces = []
        for blk in range(nb):
            q_blk = qd[u * t + blk * SUB:u * t + (blk + 1) * SUB]
            pieces.append(_dot_nt(jnp.concatenate([q_blk] * N_HEADS_GLA, axis=0) * head_rows, k_parts[blk][rows]))
        a = jnp.concatenate([pieces[blk][h * SUB:(h + 1) * SUB] for h in range(N_HEADS_GLA) for blk in range(nb)],
                            axis=0)
        a = jnp.where(row_a >= col_a, a, 0.0).astype(BF16)
        o_full = _dot(a, v)
        o_intra.append(jnp.concatenate(
            [o_full[h * t:(h + 1) * t, h * DV:(h + 1) * DV] for h in range(N_HEADS_GLA)], axis=1))
        q_decayed.append(q_dec[rows])
        state_add.append(_dot_tn(v, k_last[rows]) * bd_ref[...])
        state_decay.append(jnp.exp(b[(u + 1) * t - 1:(u + 1) * t, :]))

    st = st_ref[...]
    o_parts = []
    for u in range(n_sub):
        o_parts.append(o_intra[u] + _dot_nt(q_decayed[u], st.astype(BF16)))
        st = st * state_decay[u] + state_add[u]
    st_ref[...] = st
    return jnp.concatenate(o_parts, axis=0) if n_sub > 1 else o_parts[0]


def _gla(gq, gk, gv, gg, gr, s0, tri, mask, bd, gn, *, t, n_sub, n_batch, n_steps, first_block, name):
    rows_per_step = t * n_sub

    def rows(width):
        return pl.BlockSpec((rows_per_step, width), lambda b, c: (first_block + b * n_steps + c, 0))

    def full(a):
        return pl.BlockSpec(a.shape, lambda b, c: (0,) * a.ndim)

    state = pl.BlockSpec((None,) + s0.shape[1:], lambda b, c: (b, 0, 0))
    return pl.pallas_call(
        functools.partial(_gla_kernel, t=t, n_sub=n_sub),
        grid=(n_batch, n_steps),
        in_specs=[rows(W_GQ), rows(W_GK), rows(W_GV), rows(W_GQ), rows(W_GR), state,
                  full(tri), full(mask), full(bd), full(gn)],
        out_specs=[pl.BlockSpec((rows_per_step, W_GV), lambda b, c: (b * n_steps + c, 0)), state],
        out_shape=[jax.ShapeDtypeStruct((n_batch * n_steps * rows_per_step, W_GV), BF16),
                   jax.ShapeDtypeStruct(s0.shape, F32)],
        scratch_shapes=[pltpu.VMEM(s0.shape[1:], F32), pltpu.VMEM((rows_per_step, W_GV), F32)],
        compiler_params=_cparams(2, "mixer"),
        name=name,
    )(gq, gk, gv, gg, gr, s0, tri, mask, bd, gn)


def _front_prompt(xp, consts, head_mask, sink_b, s0, tri, mask, bd, gn):
    def full(a):
        return pl.BlockSpec(a.shape, lambda b, c: (0,) * a.ndim)

    def computed(width):
        return pl.BlockSpec((TM, width), lambda b, c: (jnp.minimum(c, N_PROMPT_TILES - 1), 0))

    def mixed(width):
        return pl.BlockSpec((TM, width), lambda b, c: (jnp.maximum(c - 1, 0), 0))

    state = pl.BlockSpec((None,) + s0.shape[1:], lambda b, c: (b, 0, 0))
    ring2 = [pltpu.VMEM((2, TM, w), dt) for w, dt in
             (PROJ_OUTPUTS[0], PROJ_OUTPUTS[3], PROJ_OUTPUTS[4], PROJ_OUTPUTS[5], PROJ_OUTPUTS[6], PROJ_OUTPUTS[7])]
    ring3 = [pltpu.VMEM((3, TM, w), dt) for w, dt in (PROJ_OUTPUTS[1], PROJ_OUTPUTS[2])]
    return pl.pallas_call(
        functools.partial(_front_prompt_kernel, t=CHUNK, n_sub=TM // CHUNK),
        grid=(1, N_PROMPT_TILES + 1),
        in_specs=[computed(D_MODEL)] + [full(a) for a in consts] + [full(head_mask), full(sink_b), state,
                                                                   full(tri), full(mask), full(bd), full(gn)],
        out_specs=[computed(W_SK), computed(W_SV), mixed(W_SQ), mixed(W_GV), state],
        out_shape=[jax.ShapeDtypeStruct((N_PROMPT, W_SK), F32), jax.ShapeDtypeStruct((N_PROMPT, W_SV), F32),
                   jax.ShapeDtypeStruct((N_PROMPT, W_SQ), BF16), jax.ShapeDtypeStruct((N_PROMPT, W_GV), BF16),
                   jax.ShapeDtypeStruct(s0.shape, F32)],
        scratch_shapes=ring2 + ring3 + [pltpu.VMEM(s0.shape[1:], F32), pltpu.VMEM((TM, W_GV), F32)],
        compiler_params=_cparams(2, "front"),
        name="front_prompt",
    )(xp, *consts, head_mask, sink_b, s0, tri, mask, bd, gn)


def _tile_row_copies(hbm_ref, tile, vmem_ref, sem, to_hbm, rows=TM):
    copies = []
    for a in range(ROW_SUB):
        h = hbm_ref.at[pl.ds(tile * rows, rows), a, :]
        v = vmem_ref.at[:, pl.ds(a * LANES, LANES)]
        copies.append(pltpu.make_async_copy(v, h, sem) if to_hbm else pltpu.make_async_copy(h, v, sem))
    return copies


def _store_tile_rows(i, n_steps, outputs, row_buf, row_sem, rows=TM):
    buf_slot = lax.rem(i, 2)

    def store(j, tile, s):
        return _tile_row_copies(outputs[j][0], tile, row_buf.at[j, s], row_sem.at[j, s], True, rows)

    for j, (_, value) in enumerate(outputs):
        @pl.when(i >= 2)
        def _():
            for c in store(j, i - 2, buf_slot):
                c.wait()

        row_buf[j, buf_slot] = value
        for c in store(j, i, buf_slot):
            c.start()

        @pl.when(i == n_steps - 1)
        def _():
            for c in store(j, i - 1, 1 - buf_slot) + store(j, i, buf_slot):
                c.wait()


def _merge_kernel(oswp_ref, osws_ref, ogp_ref, ogs_ref, xp_ref, xs_ref, wo1_ref, wo2_ref, gf_ref, wrh_ref, wrl_ref,
                  br_ref, xrow_ref, hrow_ref, lg_ref, row_buf, row_sem):
    i = pl.program_id(0)
    is_prompt = i < N_PROMPT_TILES
    x = jnp.where(is_prompt, xp_ref[...], xs_ref[...])
    o_swa = jnp.where(is_prompt, oswp_ref[...], osws_ref[...])
    og = jnp.where(is_prompt, ogp_ref[...], ogs_ref[...])
    h = x + (_dot(o_swa, wo1_ref[...]) + _dot(og, wo2_ref[...]))
    xn = _rms(h) * gf_ref[...]
    x_hi, x_lo = _split_bf16(xn)
    logits = _dot(x_hi, wrh_ref[...]) + _dot(x_lo, wrh_ref[...]) + _dot(x_hi, wrl_ref[...]) + br_ref[...]
    lg_ref[...] = logits.T[:N_EXPERTS]
    _store_tile_rows(i, N_TILES, ((xrow_ref, xn), (hrow_ref, h)), row_buf, row_sem)


def _merge(o_swa_p, o_swa_s, og_p, og_s, xp, xs, wo1, wo2, gf, wrh, wrl, br):
    def prompt_rows(width):
        return pl.BlockSpec((TM, width), lambda i: (jnp.minimum(i, N_PROMPT_TILES - 1), 0))

    def sample_rows(width):
        return pl.BlockSpec((TM, width), lambda i: (0, 0))

    def full(a):
        return pl.BlockSpec(a.shape, lambda i: (0,) * a.ndim)

    consts = (wo1, wo2, gf, wrh, wrl, br)
    return pl.pallas_call(
        _merge_kernel,
        grid=(N_TILES,),
        in_specs=[prompt_rows(W_SQ), sample_rows(W_SQ), prompt_rows(W_GV), sample_rows(W_GV),
                  prompt_rows(D_MODEL), sample_rows(D_MODEL)] + [full(a) for a in consts],
        out_specs=[pl.BlockSpec(memory_space=pl.ANY)] * 2 + [pl.BlockSpec((N_EXPERTS, TM), lambda i: (0, i))],
        out_shape=[jax.ShapeDtypeStruct((N_ROWS,) + ROW_TILE, F32)] * 2 + [
                   jax.ShapeDtypeStruct((N_EXPERTS, N_ROWS), F32)],
        scratch_shapes=[pltpu.VMEM((2, 2, TM, D_MODEL), F32), pltpu.SemaphoreType.DMA((2, 2))],
        compiler_params=_cparams(1, "merge"),
        name="merge",
    )(o_swa_p, o_swa_s, og_p, og_s, xp, xs, *consts)


RT = 1280
N_ROUTER_STEPS = N_ROWS // RT


def _router_kernel(lg_ref, tri_ref, wrow_ref, ti_ref, rk_ref, cnt_ref, base_ref, row_buf, row_sem):
    i = pl.program_id(0)

    @pl.when(i == 0)
    def _():
        base_ref[...] = jnp.zeros_like(base_ref)

    logits_t = lg_ref[...]
    expert = lax.broadcasted_iota(I32, logits_t.shape, 0)
    slot = lax.broadcasted_iota(I32, (SLOT_ROWS, RT), 0)
    vals, hots = [], []
    ti = jnp.zeros((SLOT_ROWS, RT), I32)
    for kk in range(TOP_K):
        m = jnp.max(logits_t, axis=0, keepdims=True)
        idx = jnp.min(jnp.where(logits_t == m, expert, N_EXPERTS), axis=0, keepdims=True)
        hot = expert == idx
        logits_t = jnp.where(hot, NEG_BIG, logits_t)
        vals.append(m)
        hots.append(hot)
        ti = jnp.where(slot == kk, idx, ti)
    ti_ref[...] = ti
    exps = [jnp.exp(v - vals[0]) for v in vals]
    den = exps[0] + exps[1] + exps[2] + exps[3]
    tw_t = jnp.zeros((SLOT_ROWS, RT), F32)
    for kk in range(TOP_K):
        tw_t = jnp.where(slot == kk, exps[kk] / den, tw_t)
    eye = (lax.broadcasted_iota(I32, (SLOT_ROWS, LANES), 0)
           == lax.broadcasted_iota(I32, (SLOT_ROWS, LANES), 1)).astype(F32).astype(BF16)
    w_hi = tw_t.astype(BF16)
    w_mid, w_lo = _split_bf16(tw_t - w_hi.astype(F32))
    tw_col = _dot_tn(w_hi, eye) + _dot_tn(w_mid, eye) + _dot_tn(w_lo, eye)

    onehot_t = jnp.zeros(logits_t.shape, F32)
    for hot in hots:
        onehot_t = onehot_t + jnp.where(hot, 1.0, 0.0)
    before_t = _dot(onehot_t.astype(BF16), tri_ref[...]) + base_ref[:, 0:1]
    rk = jnp.zeros((SLOT_ROWS, RT), I32)
    for kk in range(TOP_K):
        r = jnp.sum(jnp.where(hots[kk], before_t, 0.0), axis=0, keepdims=True).astype(I32)
        rk = jnp.where(slot == kk, r, rk)
    rk_ref[...] = rk
    total = base_ref[...] + jnp.sum(onehot_t, axis=1, keepdims=True)
    base_ref[...] = total
    cnt_ref[...] = total.astype(I32)

    w_lanes = [jnp.broadcast_to(tw_col[:, kk:kk + 1], (RT, LANES)) for kk in range(TOP_K)]
    w_rows = jnp.concatenate(w_lanes + [jnp.zeros((RT, D_MODEL - TOP_K * LANES), F32)], axis=1)
    _store_tile_rows(i, N_ROUTER_STEPS, ((wrow_ref, w_rows),), row_buf, row_sem, RT)


def _router(logits, tri):
    return pl.pallas_call(
        _router_kernel,
        grid=(N_ROUTER_STEPS,),
        in_specs=[pl.BlockSpec((N_EXPERTS, RT), lambda i: (0, i)), pl.BlockSpec(tri.shape, lambda i: (0, 0))],
        out_specs=[pl.BlockSpec(memory_space=pl.ANY),
                   pl.BlockSpec((SLOT_ROWS, RT), lambda i: (0, i)),
                   pl.BlockSpec((SLOT_ROWS, RT), lambda i: (0, i)),
                   pl.BlockSpec((N_EXPERTS, LANES), lambda i: (0, 0))],
        out_shape=[jax.ShapeDtypeStruct((N_ROWS,) + ROW_TILE, F32),
                   jax.ShapeDtypeStruct((SLOT_ROWS, N_ROWS), I32),
                   jax.ShapeDtypeStruct((SLOT_ROWS, N_ROWS), I32),
                   jax.ShapeDtypeStruct((N_EXPERTS, LANES), I32)],
        scratch_shapes=[pltpu.VMEM((N_EXPERTS, LANES), F32), pltpu.VMEM((1, 2, RT, D_MODEL), F32),
                        pltpu.SemaphoreType.DMA((1, 2))],
        compiler_params=_cparams(1, "router"),
        name="router",
    )(logits, tri)


ISSUE_UNROLL = 4


def _row_copy(src_ref, src_row, dst_ref, dst_row, sem):
    return pltpu.make_async_copy(src_ref.at[pl.ds(src_row, 1)], dst_ref.at[pl.ds(dst_row, 1)], sem)


def _dispatch_kernel(dest_ref, end_ref, x_ref, xs_ref, zero_ref, sem, zsem):
    i = pl.program_id(0)
    base = i * TM

    @pl.when(i == 0)
    def _():
        zero_ref[...] = jnp.zeros_like(zero_ref)

        def tail_copy(e):
            last = jnp.maximum(end_ref[e] - TM, 0)
            return pltpu.make_async_copy(zero_ref, xs_ref.at[pl.ds(pl.multiple_of(last, TM), TM)], zsem)

        def fill(e, carry):
            tail_copy(e).start()
            return carry

        def fill_wait(e, carry):
            tail_copy(e).wait()
            return carry

        lax.fori_loop(0, N_EXPERTS, fill, 0)
        lax.fori_loop(0, N_EXPERTS, fill_wait, 0)

        def unused_copy(t):
            return pltpu.make_async_copy(zero_ref, xs_ref.at[pl.ds(pl.multiple_of(t * TM, TM), TM)], zsem)

        def fill_unused(t, carry):
            unused_copy(t).start()
            unused_copy(t).wait()
            return carry

        lax.fori_loop(end_ref[N_EXPERTS - 1] // TM, N_EXPERT_TILES, fill_unused, 0)

    def issue(n, carry):
        for kk in range(TOP_K):
            _row_copy(x_ref, n, xs_ref, dest_ref[kk * N_ROWS + base + n], sem).start(priority=kk % 2)
        return carry

    lax.fori_loop(0, TM, issue, 0, unroll=ISSUE_UNROLL)

    for kk in range(TOP_K):
        pltpu.make_async_copy(x_ref, xs_ref.at[pl.ds(0, TM)], sem).wait()


def _dispatch(dest, end, x_packed):
    return pl.pallas_call(
        _dispatch_kernel,
        grid_spec=pltpu.PrefetchScalarGridSpec(
            num_scalar_prefetch=2,
            grid=(N_TILES,),
            in_specs=[pl.BlockSpec((TM,) + ROW_TILE, lambda i, d, e: (i, 0, 0))],
            out_specs=pl.BlockSpec(memory_space=pl.ANY),
            scratch_shapes=[pltpu.VMEM((TM,) + ROW_TILE, F32), pltpu.SemaphoreType.DMA,
                            pltpu.SemaphoreType.DMA],
        ),
        out_shape=jax.ShapeDtypeStruct((N_SORTED_ROWS,) + ROW_TILE, F32),
        compiler_params=_cparams(1, "dispatch"),
        name="dispatch",
    )(dest, end, x_packed)


CAST_ROWS = 128


def _moe_kernel(te_ref, nu_ref, nx_ref, xs_ref, wg_ref, bg_ref, wu_ref, bu_ref, wd_ref, bd_ref, ys_ref,
                w_stage, w_bf, x_buf, y_buf, zero_buf, w_sem, in_sem, out_sem, zero_sem):
    t = pl.program_id(0)
    n_used = nu_ref[0]
    slot = lax.rem(t, 2)
    e = te_ref[t]
    e_prev = te_ref[jnp.maximum(t - 1, 0)]

    def load(tile, s):
        return _tile_row_copies(xs_ref, tile, x_buf.at[s], in_sem.at[s], to_hbm=False)

    def store(tile, s):
        return _tile_row_copies(ys_ref, tile, y_buf.at[s], out_sem.at[s], to_hbm=True)

    def weight_copies(expert):
        return [pltpu.make_async_copy(w.at[expert], w_stage.at[j], w_sem.at[j])
                for j, w in enumerate((wg_ref, wu_ref, wd_ref))]

    @pl.when(t == 0)
    def _():
        for c in weight_copies(e) + load(0, 0):
            c.start()

    @pl.when(t + 1 < n_used)
    def _():
        for c in load(t + 1, 1 - slot):
            c.start()

    @pl.when((t == 0) | (e != e_prev))
    def _():
        for c in weight_copies(e):
            c.wait()

        def cast(r, carry):
            sl = pl.ds(pl.multiple_of(r * CAST_ROWS, CAST_ROWS), CAST_ROWS)
            for j in range(3):
                w_bf[j, sl, :] = w_stage[j, sl, :].astype(BF16)
            return carry

        lax.fori_loop(0, D_MODEL // CAST_ROWS, cast, 0)
        e_next = nx_ref[e]

        @pl.when(e_next >= 0)
        def _():
            for c in weight_copies(e_next):
                c.start()

    @pl.when(t < n_used)
    def _():
        for c in load(t, slot):
            c.wait()

        @pl.when(t >= 2)
        def _():
            for c in store(t - 2, slot):
                c.wait()

        x = x_buf[slot].astype(BF16)
        gate = jnp.minimum(_dot(x, w_bf[0]) + bg_ref[...], SWIGLU_LIMIT)
        up = jnp.clip(_dot(x, w_bf[1]) + bu_ref[...], -SWIGLU_LIMIT, SWIGLU_LIMIT)
        hdn = (up + 1.0) * gate * (1.0 / (1.0 + jnp.exp(-SWIGLU_ALPHA * gate)))
        y_buf[slot] = _dot(hdn.astype(BF16), w_bf[2]) + bd_ref[...]
        for c in store(t, slot):
            c.start()

    @pl.when(t >= n_used)
    def _():
        zero_buf[...] = jnp.zeros_like(zero_buf)
        fill = pltpu.make_async_copy(zero_buf, ys_ref.at[pl.ds(t * TM, TM)], zero_sem)
        fill.start()
        fill.wait()

    @pl.when(t == N_EXPERT_TILES - 1)
    def _():
        @pl.when(n_used >= 2)
        def _():
            for c in store(n_used - 2, lax.rem(n_used, 2)):
                c.wait()

        for c in store(n_used - 1, lax.rem(n_used - 1, 2)):
            c.wait()


def _moe(tile_expert, n_used, next_expert, xs, w_gate, b_gate, w_up, b_up, w_down, b_down):
    hbm = pl.BlockSpec(memory_space=pl.ANY)
    bias = pl.BlockSpec((None, 1, D_MODEL), lambda t, te, nu, nx: (te[t], 0, 0))
    return pl.pallas_call(
        _moe_kernel,
        grid_spec=pltpu.PrefetchScalarGridSpec(
            num_scalar_prefetch=3,
            grid=(N_EXPERT_TILES,),
            in_specs=[hbm, hbm, bias, hbm, bias, hbm, bias],
            out_specs=hbm,
            scratch_shapes=[pltpu.VMEM((3, D_MODEL, D_MODEL), F32), pltpu.VMEM((3, D_MODEL, D_MODEL), BF16),
                            pltpu.VMEM((2, TM, D_MODEL), F32), pltpu.VMEM((2, TM, D_MODEL), F32),
                            pltpu.VMEM((TM,) + ROW_TILE, F32), pltpu.SemaphoreType.DMA((3,)),
                            pltpu.SemaphoreType.DMA((2,)), pltpu.SemaphoreType.DMA((2,)),
                            pltpu.SemaphoreType.DMA],
        ),
        out_shape=jax.ShapeDtypeStruct((N_SORTED_ROWS,) + ROW_TILE, F32),
        compiler_params=_cparams(1, "experts"),
        name="experts",
    )(tile_expert, n_used, next_expert, xs, w_gate, b_gate, w_up, b_up, w_down, b_down)


def _combine_kernel(dest_ref, h_ref, w_ref, ys_ref, yp_ref, ysm_ref, g_ref, out_ref, sem):
    i = pl.program_id(0)
    slot = lax.rem(i, 2)

    def issue_tile(tile, s):
        base = tile * TM

        def issue(n, carry):
            for kk in range(TOP_K):
                _row_copy(ys_ref, dest_ref[kk * N_ROWS + base + n], g_ref.at[s], kk * TM + n,
                          sem.at[s]).start(priority=kk % 2)
            return carry

        lax.fori_loop(0, TM, issue, 0, unroll=ISSUE_UNROLL)

    @pl.when(i == 0)
    def _():
        issue_tile(0, 0)

    @pl.when(i + 1 < N_TILES)
    def _():
        issue_tile(i + 1, 1 - slot)

    for kk in range(TOP_K):
        pltpu.make_async_copy(ys_ref.at[pl.ds(0, TM)], g_ref.at[slot, pl.ds(kk * TM, TM)], sem.at[slot]).wait()

    w = w_ref[...]
    acc = None
    for kk in range(TOP_K):
        part = jnp.broadcast_to(w[:, kk:kk + 1, :], (TM,) + ROW_TILE) * g_ref[slot, kk * TM:(kk + 1) * TM]
        acc = part if acc is None else acc + part
    out_ref[...] = h_ref[...] + acc

    def write(y_ref):
        for a in range(ROW_SUB):
            y_ref[:, a * LANES:(a + 1) * LANES] = out_ref[:, a, :]

    @pl.when(i < N_PROMPT_TILES)
    def _():
        write(yp_ref)

    @pl.when(i >= N_PROMPT_TILES)
    def _():
        write(ysm_ref)


def _combine(dest, h_rows, w_rows, ys):
    tile = lambda index: pl.BlockSpec((TM,) + ROW_TILE, index)
    return pl.pallas_call(
        _combine_kernel,
        grid_spec=pltpu.PrefetchScalarGridSpec(
            num_scalar_prefetch=1,
            grid=(N_TILES,),
            in_specs=[tile(lambda i, d: (i, 0, 0)), tile(lambda i, d: (i, 0, 0)),
                      pl.BlockSpec(memory_space=pl.ANY)],
            out_specs=[pl.BlockSpec((TM, D_MODEL), lambda i, d: (jnp.minimum(i, N_PROMPT_TILES - 1), 0)),
                       pl.BlockSpec((TM, D_MODEL), lambda i, d: (0, 0))],
            scratch_shapes=[pltpu.VMEM((2, TOP_K * TM) + ROW_TILE, F32), pltpu.VMEM((TM,) + ROW_TILE, F32),
                            pltpu.SemaphoreType.DMA((2,))],
        ),
        out_shape=[jax.ShapeDtypeStruct((N_PROMPT, D_MODEL), F32),
                   jax.ShapeDtypeStruct((N_SAMPLE, D_MODEL), F32)],
        compiler_params=_cparams(1, "combine"),
        name="combine",
    )(dest, h_rows, w_rows, ys)


def _block_diag_ones(n, blk):
    idx = np.arange(n) // blk
    return (idx[:, None] == idx[None, :]).astype(np.float32)


def _swa_head_mask(tq):
    row_head = np.arange(4 * tq)[:, None] // tq
    lane_head = np.arange(4 * HD)[None, :] // HD
    return jnp.asarray((row_head == lane_head).astype(np.float32), BF16)


def _gla_masks(t, n_sub):
    nb = t // SUB
    lanes = N_HEADS_GLA * DK
    tri = jnp.asarray(np.kron(np.eye(n_sub, dtype=np.float32), np.tril(np.ones((t, t), np.float32))), BF16)
    row = np.arange(N_HEADS_GLA * t)
    col = np.arange(nb * lanes)
    same_head = (row[:, None] // t) == ((col[None, :] % lanes) // DK)
    same_blk = ((row[:, None] % t) // SUB) == (col[None, :] // lanes)
    mask = jnp.asarray((same_head & same_blk).astype(np.float32), BF16)
    return tri, mask


def _sink_rows(sinks, tq):
    s = jnp.repeat(sinks.astype(F32).reshape(2, 4), tq, axis=1)
    return jnp.broadcast_to(s[:, None, :], (2, 8, 4 * tq))


def kernel(x_prompt, x_sample, state_gla, cache_swa_k, cache_swa_v, norm_mix_g, w_in, w_gla_a2, b_gla_a, q_norm_g,
           k_norm_g, swa_sinks, gla_norm_g, w_out, norm_ffn_g, w_router, b_router, w_gate, b_gate, w_up, b_up,
           w_down, b_down):
    xp = x_prompt.reshape(N_PROMPT, D_MODEL)
    xs = x_sample.reshape(N_SAMPLE, D_MODEL)

    w_in0 = w_in[0]
    w_main = w_in0[:, :W_MAIN].astype(BF16)
    w_ga = jnp.pad(w_in0[:, OFF_GA:], ((0, 0), (0, LANES - W_GA))).astype(BF16)
    w_a2 = jnp.pad(w_gla_a2[0], ((0, LANES - W_GA), (0, 0))).astype(BF16)
    b_a = b_gla_a[0].reshape(1, -1)
    qg = (jnp.tile(q_norm_g[0], N_HEADS_SWA) * (HD ** -0.5)).reshape(1, -1)
    kg = jnp.tile(k_norm_g[0], 2).reshape(1, -1)
    bdq = jnp.asarray(_block_diag_ones(W_SQ, HD), BF16)
    bdk = jnp.asarray(_block_diag_ones(W_SK, HD), BF16)

    proj_consts = (norm_mix_g[0].reshape(1, -1), w_main, w_ga, w_a2, b_a, qg, kg, bdq, bdk)
    sq, sk, sv, gq, gk, gv, gg, gr = _proj_sample(xs, proj_consts)

    cache_k = cache_swa_k[0].reshape(N_STREAMS, WINDOW, 2 * HD)
    cache_v = cache_swa_v[0].reshape(N_STREAMS, WINDOW, 2 * HD)
    o_swa_s = _swa_sample(sq, sk, sv, cache_k, cache_v, _swa_head_mask(T_SAMPLE),
                          _sink_rows(swa_sinks[0], T_SAMPLE))

    bd_state = jnp.asarray(_block_diag_ones(N_HEADS_GLA, 1).repeat(DV, axis=0).repeat(DK, axis=1), F32)
    gn = gla_norm_g[0].reshape(1, -1)
    tri_p, mask_p = _gla_masks(CHUNK, TM // CHUNK)
    tri_s, mask_s = _gla_masks(T_SAMPLE, 1)
    s0_p = jnp.zeros((1, N_HEADS_GLA * DV, N_HEADS_GLA * DK), F32)
    sk_p, sv_p, o_swa_p, og_p, sfin_p = _front_prompt(xp, proj_consts, _swa_head_mask(CHUNK),
                                                      _sink_rows(swa_sinks[0], CHUNK), s0_p, tri_p, mask_p,
                                                      bd_state, gn)
    eye = jnp.eye(N_HEADS_GLA, dtype=F32)
    s0_s = jnp.einsum('bhde,hg->bhegd', state_gla[0].astype(F32), eye).reshape(
        N_STREAMS, N_HEADS_GLA * DV, N_HEADS_GLA * DK)
    og_s, sfin_s = _gla(gq, gk, gv, gg, gr, s0_s, tri_s, mask_s, bd_state, gn, t=T_SAMPLE, n_sub=1,
                        n_batch=N_STREAMS, n_steps=1, first_block=0, name="gla_sample")

    def unpack_state(sfin):
        s = sfin.reshape(-1, N_HEADS_GLA, DV, N_HEADS_GLA, DK)
        s = jnp.stack([s[:, h, :, h, :] for h in range(N_HEADS_GLA)], axis=1)
        return jnp.transpose(s, (0, 1, 3, 2))[None]

    w_out0 = w_out[0].astype(BF16)
    wr = jnp.pad(w_router[0], ((0, 0), (0, LANES - N_EXPERTS)))
    wr_hi = wr.astype(BF16)
    wr_lo = (wr - wr_hi.astype(F32)).astype(BF16)
    br = jnp.pad(b_router[0].astype(F32), (0, LANES - N_EXPERTS)).reshape(1, -1)
    x_rows, h_rows, logits = _merge(o_swa_p, o_swa_s, og_p, og_s, xp, xs, w_out0[:W_SQ], w_out0[W_SQ:],
                                    norm_ffn_g[0].reshape(1, -1), wr_hi, wr_lo, br)
    earlier = jnp.asarray(np.triu(np.ones((RT, RT), np.float32), 1), BF16)
    w_rows, top_i, rank, counts = _router(logits, earlier)

    counts = counts[:, 0]
    padded = (counts + TM - 1) // TM * TM
    end = jnp.cumsum(padded)
    start = end - padded
    experts = jnp.arange(N_EXPERTS, dtype=I32)
    is_e = top_i[:TOP_K, :, None] == experts
    dest = (rank[:TOP_K] + jnp.sum(jnp.where(is_e, start, 0), axis=-1)).reshape(-1).astype(I32)
    n_used = (end[-1] // TM).astype(I32)
    tiles = jnp.minimum(jnp.arange(N_EXPERT_TILES, dtype=I32), n_used - 1)
    tile_expert = jnp.sum((tiles[:, None] * TM >= end[None, :]).astype(I32), axis=1)
    later_nonempty = (experts[None, :] > experts[:, None]) & (padded[None, :] > 0)
    next_expert = jnp.min(jnp.where(later_nonempty, experts[None, :], N_EXPERTS), axis=1)
    next_expert = jnp.where(next_expert < N_EXPERTS, next_expert, -1).astype(I32)

    xs_sorted = _dispatch(dest, end.astype(I32), x_rows)
    ys = _moe(tile_expert, n_used.reshape(1), next_expert, xs_sorted, w_gate[0], b_gate[0].reshape(N_EXPERTS, 1, -1),
              w_up[0], b_up[0].reshape(N_EXPERTS, 1, -1), w_down[0], b_down[0].reshape(N_EXPERTS, 1, -1))
    y_p, y_s = _combine(dest, h_rows, w_rows, ys)

    sk_s = sk.reshape(N_STREAMS, T_SAMPLE, 2 * HD)
    sv_s = sv.reshape(N_STREAMS, T_SAMPLE, 2 * HD)
    kc_s = jnp.concatenate([cache_k[:, T_SAMPLE:], sk_s], axis=1).reshape(1, N_STREAMS, WINDOW, 2, HD)
    vc_s = jnp.concatenate([cache_v[:, T_SAMPLE:], sv_s], axis=1).reshape(1, N_STREAMS, WINDOW, 2, HD)
    kc_p = sk_p[N_PROMPT - WINDOW:].reshape(1, 1, WINDOW, 2, HD)
    vc_p = sv_p[N_PROMPT - WINDOW:].reshape(1, 1, WINDOW, 2, HD)
    return (y_p.reshape(1, N_PROMPT, D_MODEL), y_s.reshape(N_STREAMS, T_SAMPLE, D_MODEL),
            unpack_state(sfin_p), kc_p, vc_p, unpack_state(sfin_s), kc_s, vc_s)
```

```python
import functools

import numpy as np
import jax
import jax.numpy as jnp
from jax import lax
from jax.experimental import pallas as pl
from jax.experimental.pallas import tpu as pltpu

F32 = jnp.float32
BF16 = jnp.bfloat16
I32 = jnp.int32

D_MODEL = 1024
N_PROMPT = 16384
N_STREAMS = 8
T_SAMPLE = 32
N_SAMPLE = N_STREAMS * T_SAMPLE
N_ROWS = N_PROMPT + N_SAMPLE
EPS = 1e-6

CHUNK = 64
SUB = 16
N_HEADS_SWA = 8
HD = 64
WINDOW = 128
N_HEADS_GLA = 4
DK = 64
DV = 128
GLA_TAU = 16.0
N_EXPERTS = 32
TOP_K = 4
SWIGLU_ALPHA = 1.702
SWIGLU_LIMIT = 7.0

TM = 256
N_TILES = N_ROWS // TM
N_PROMPT_TILES = N_PROMPT // TM
N_ASSIGN = N_ROWS * TOP_K
N_EXPERT_TILES = N_ASSIGN // TM + N_EXPERTS
N_SORTED_ROWS = N_EXPERT_TILES * TM
LANES = 128
SLOT_ROWS = 16
NEG_BIG = -1e30

W_SQ, W_SK, W_SV, W_GQ, W_GK, W_GV, W_GR, W_GA = 512, 128, 128, 256, 256, 512, 512, 16
OFF_SQ = 0
OFF_SK = OFF_SQ + W_SQ
OFF_SV = OFF_SK + W_SK
OFF_GQ = OFF_SV + W_SV
OFF_GK = OFF_GQ + W_GQ
OFF_GV = OFF_GK + W_GK
OFF_GR = OFF_GV + W_GV
OFF_GA = OFF_GR + W_GR
W_MAIN = OFF_GA


VMEM_MIB = dict(proj=40, front=48, mixer=32, merge=32, router=48, dispatch=32, experts=52, combine=40)


def _cparams(n_grid_axes, call):
    return pltpu.CompilerParams(dimension_semantics=("arbitrary",) * n_grid_axes,
                                vmem_limit_bytes=VMEM_MIB[call] * 2 ** 20)


def _dot(a, b):
    return jnp.dot(a, b, preferred_element_type=F32)


def _dot_nt(a, b):
    return lax.dot_general(a, b, (((1,), (1,)), ((), ())), preferred_element_type=F32)


def _dot_tn(a, b):
    return lax.dot_general(a, b, (((0,), (0,)), ((), ())), preferred_element_type=F32)


def _split_bf16(x):
    hi = x.astype(BF16)
    lo = (x - hi.astype(F32)).astype(BF16)
    return hi, lo


def _rms(x):
    return x * lax.rsqrt(jnp.mean(x * x, axis=-1, keepdims=True) + EPS)


ROW_SUB = 8
ROW_TILE = (ROW_SUB, LANES)


PROJ_OUTPUTS = ((W_SQ, BF16), (W_SK, F32), (W_SV, F32), (W_GQ, BF16), (W_GK, BF16), (W_GV, BF16),
                (W_GQ, F32), (W_GR, BF16))
N_PROJ_CONSTS = 9


def _proj_tile(x, g_ref, w_ref, wga_ref, wa2_ref, ba_ref, qg_ref, kg_ref, bdq_ref, bdk_ref,
               sq_ref, sk_ref, sv_ref, gq_ref, gk_ref, gv_ref, gg_ref, gr_ref):
    xb = (_rms(x) * g_ref[...]).astype(BF16)

    def seg(off, width):
        return _dot(xb, w_ref[:, off:off + width])

    def head_norm(u, bd_ref):
        hi, lo = _split_bf16(u * u)
        w = bd_ref.shape[0]
        ss = jnp.concatenate([_dot(hi[:, c:c + w], bd_ref[...]) + _dot(lo[:, c:c + w], bd_ref[...])
                              for c in range(0, u.shape[1], w)], axis=1)
        return u * lax.rsqrt(ss * (1.0 / HD) + EPS)

    sq_ref[...] = (head_norm(seg(OFF_SQ, W_SQ), bdq_ref) * qg_ref[...]).astype(BF16)
    sk_ref[...] = head_norm(seg(OFF_SK, W_SK), bdk_ref) * kg_ref[...]
    sv_ref[...] = seg(OFF_SV, W_SV)
    gq_ref[...] = (seg(OFF_GQ, W_GQ) * (DK ** -0.5)).astype(BF16)
    gk_ref[...] = seg(OFF_GK, W_GK).astype(BF16)
    gv_ref[...] = seg(OFF_GV, W_GV).astype(BF16)
    gr_ref[...] = seg(OFF_GR, W_GR).astype(BF16)
    ga = _dot(xb, wga_ref[...]).astype(BF16)
    z = _dot(ga, wa2_ref[...]) + ba_ref[...]
    log_sig = jnp.minimum(z, 0.0) - jnp.log(1.0 + jnp.exp(-jnp.abs(z)))
    gg_ref[...] = log_sig * (1.0 / GLA_TAU)


def _proj_sample_kernel(x_ref, *refs):
    _proj_tile(x_ref[...], *refs)


def _proj_sample(xs, consts):
    def full(a):
        return pl.BlockSpec(a.shape, lambda i: (0,) * a.ndim)

    return pl.pallas_call(
        _proj_sample_kernel,
        grid=(1,),
        in_specs=[full(xs)] + [full(a) for a in consts],
        out_specs=[pl.BlockSpec((N_SAMPLE, w), lambda i: (0, 0)) for w, _ in PROJ_OUTPUTS],
        out_shape=[jax.ShapeDtypeStruct((N_SAMPLE, w), dt) for w, dt in PROJ_OUTPUTS],
        compiler_params=_cparams(1, "proj"),
        name="proj_sample",
    )(xs, *consts)


def _dup_kv_heads(x):
    r = pltpu.roll(x, HD, axis=1)
    lo = lax.broadcasted_iota(I32, x.shape, 1) < HD
    out = []
    for a in (jnp.where(lo, x, r), jnp.where(lo, r, x)):
        out.append(jnp.concatenate([a, a], axis=1).astype(BF16))
    return out


def _swa_blocks(q_blocks, k_blocks, v_blocks, sinks, valids, head_mask):
    tq = q_blocks[0].shape[0]
    scores = []
    for q, k, valid in zip(q_blocks, k_blocks, valids):
        s_t = _dot_nt(k, jnp.concatenate([q] * 4, axis=0) * head_mask)
        scores.append(s_t if valid is None else jnp.where(valid, s_t, -jnp.inf))
    probs = []
    for s_t, sink in zip(scores, sinks):
        m = jnp.maximum(jnp.max(s_t, axis=0, keepdims=True), sink)
        p_t = jnp.exp(s_t - m)
        den = jnp.sum(p_t, axis=0, keepdims=True) + jnp.exp(sink - m)
        probs.append((p_t / den).astype(BF16))
    lane_head = lax.broadcasted_iota(I32, (tq, 4 * HD), 1) // HD
    outs = []
    for p_t, v in zip(probs, v_blocks):
        o_full = _dot_tn(p_t, v)
        o = jnp.zeros((tq, 4 * HD), F32)
        for a in range(4):
            o = o + jnp.where(lane_head == a, o_full[a * tq:(a + 1) * tq], 0.0)
        outs.append(o)
    return outs


def _swa_prompt_tile(i, q_ref, kp_ref, kc_ref, vp_ref, vc_ref, hm_ref, sink_ref, o_ref):
    k_dup = _dup_kv_heads(jnp.concatenate([kp_ref[...], kc_ref[...]], axis=0))
    v_dup = _dup_kv_heads(jnp.concatenate([vp_ref[...], vc_ref[...]], axis=0))
    sink_row = [sink_ref[j][0:1, :] for j in range(2)]
    span = WINDOW + CHUNK
    key = lax.broadcasted_iota(I32, (span, 4 * CHUNK), 0)
    qs, ks, vs, sinks, valids, where = [], [], [], [], [], []
    for c in range(TM // CHUNK):
        lo = CHUNK * c
        valid = (i * TM - WINDOW + lo + key) >= 0
        for j in range(2):
            qs.append(q_ref[lo:lo + CHUNK, 4 * HD * j:4 * HD * (j + 1)])
            ks.append(k_dup[j][lo:lo + span])
            vs.append(v_dup[j][lo:lo + span])
            sinks.append(sink_row[j])
            valids.append(valid)
            where.append((lo, j))
    outs = _swa_blocks(qs, ks, vs, sinks, valids, hm_ref[...])
    for (lo, j), o in zip(where, outs):
        o_ref[lo:lo + CHUNK, 4 * HD * j:4 * HD * (j + 1)] = o.astype(BF16)


def _swa_sample_kernel(q_ref, kc_ref, kn_ref, vc_ref, vn_ref, hm_ref, sink_ref, o_ref):
    k_dup = _dup_kv_heads(jnp.concatenate([kc_ref[...], kn_ref[...]], axis=0))
    v_dup = _dup_kv_heads(jnp.concatenate([vc_ref[...], vn_ref[...]], axis=0))
    sink_row = [sink_ref[j][0:1, :] for j in range(2)]
    qs = [q_ref[:, 4 * HD * j:4 * HD * (j + 1)] for j in range(2)]
    outs = _swa_blocks(qs, k_dup, v_dup, sink_row, [None, None], hm_ref[...])
    o_ref[...] = jnp.concatenate(outs, axis=1).astype(BF16)


def _swa_sample(sq, sk, sv, cache_k, cache_v, head_mask, sink_b):
    new = lambda width: pl.BlockSpec((T_SAMPLE, width), lambda b: (b, 0))
    cache = pl.BlockSpec((None, WINDOW, 2 * HD), lambda b: (b, 0, 0))
    return pl.pallas_call(
        _swa_sample_kernel,
        grid=(N_STREAMS,),
        in_specs=[new(W_SQ), cache, new(2 * HD), cache, new(2 * HD),
                  pl.BlockSpec(head_mask.shape, lambda b: (0, 0)),
                  pl.BlockSpec(sink_b.shape, lambda b: (0, 0, 0))],
        out_specs=pl.BlockSpec((T_SAMPLE, W_SQ), lambda b: (b, 0)),
        out_shape=jax.ShapeDtypeStruct((N_SAMPLE, W_SQ), BF16),
        compiler_params=_cparams(1, "mixer"),
        name="swa_sample",
    )(sq, cache_k, sk, cache_v, sv, head_mask, sink_b)


GLA_SAFE_EXP = 80.0
GLA_SLOW_ROWS = 16


def _gla_step(c, q_ref, k_ref, v_ref, g_ref, gr_ref, s0_ref, tri_ref, m_ref, bd_ref, gn_ref,
              og_ref, sfin_ref, st_ref, o_ref, *, t, n_sub, companion=None):
    @pl.when(c == 0)
    def _():
        st_ref[...] = s0_ref[...]

    n_rows = t * n_sub
    blocked_is_safe = (SUB - 1) * jnp.max(-g_ref[...]) <= GLA_SAFE_EXP

    def emit(o):
        gr = gr_ref[...].astype(F32)
        gate = gr / (1.0 + jnp.exp(-gr))
        outs = []
        for h in range(N_HEADS_GLA):
            sl = slice(h * DV, (h + 1) * DV)
            outs.append(_rms(o[:, sl]) * gn_ref[...] * gate[:, sl])
        og_ref[...] = jnp.concatenate(outs, axis=1).astype(BF16)

    @pl.when(blocked_is_safe)
    def _():
        if companion is not None:
            companion()
        emit(_gla_blocked(q_ref, k_ref, v_ref, g_ref, tri_ref, m_ref, bd_ref, st_ref, t=t, n_sub=n_sub))

    @pl.when(jnp.logical_not(blocked_is_safe))
    def _():
        if companion is not None:
            companion()
        _gla_tokenwise(q_ref, k_ref, v_ref, g_ref, bd_ref, st_ref, o_ref, n_rows=n_rows)
        emit(o_ref[...])

    @pl.when(c == pl.num_programs(1) - 1)
    def _():
        sfin_ref[...] = st_ref[...]


def _gla_kernel(*refs, t, n_sub):
    _gla_step(pl.program_id(1), *refs, t=t, n_sub=n_sub)


def _front_prompt_kernel(*refs, t, n_sub):
    x_ref = refs[0]
    consts = refs[1:1 + N_PROJ_CONSTS]
    hm_ref, sink_ref, s0_ref, tri_ref, m_ref, bd_ref, gn_ref = refs[1 + N_PROJ_CONSTS:8 + N_PROJ_CONSTS]
    sk_out, sv_out, o_swa_ref, og_ref, sfin_ref = refs[8 + N_PROJ_CONSTS:13 + N_PROJ_CONSTS]
    sq_s, gq_s, gk_s, gv_s, gg_s, gr_s, sk_s, sv_s, st_ref, o_ref = refs[13 + N_PROJ_CONSTS:]
    c = pl.program_id(1)

    @pl.when(c == 0)
    def _():
        for ring in (sq_s, gq_s, gk_s, gv_s, gg_s, gr_s, sk_s, sv_s):
            ring[...] = jnp.zeros_like(ring)

    new2, old2 = lax.rem(c, 2), lax.rem(c + 1, 2)
    new3, old3, older3 = lax.rem(c, 3), lax.rem(c + 2, 3), lax.rem(c + 1, 3)
    half = pl.ds(TM // 2, TM // 2)

    def companion():
        _proj_tile(x_ref[...], *consts, sq_s.at[new2], sk_s.at[new3], sv_s.at[new3], gq_s.at[new2],
                   gk_s.at[new2], gv_s.at[new2], gg_s.at[new2], gr_s.at[new2])
        sk_out[...] = sk_s[new3]
        sv_out[...] = sv_s[new3]
        _swa_prompt_tile(c - 1, sq_s.at[old2], sk_s.at[older3, half], sk_s.at[old3], sv_s.at[older3, half],
                         sv_s.at[old3], hm_ref, sink_ref, o_swa_ref)

    _gla_step(c, gq_s.at[old2], gk_s.at[old2], gv_s.at[old2], gg_s.at[old2], gr_s.at[old2], s0_ref, tri_ref,
              m_ref, bd_ref, gn_ref, og_ref, sfin_ref, st_ref, o_ref, t=t, n_sub=n_sub, companion=companion)


def _gla_tokenwise(q_ref, k_ref, v_ref, g_ref, bd_ref, st_ref, o_ref, *, n_rows):
    row = lax.broadcasted_iota(I32, (GLA_SLOW_ROWS, 1), 0)

    def group(gi, carry):
        rows = pl.ds(pl.multiple_of(gi * GLA_SLOW_ROWS, GLA_SLOW_ROWS), GLA_SLOW_ROWS)
        q = q_ref[rows, :].astype(F32)
        k = k_ref[rows, :].astype(F32)
        v = v_ref[rows, :].astype(F32)
        decay = jnp.exp(g_ref[rows, :])
        o = jnp.zeros((GLA_SLOW_ROWS, N_HEADS_GLA * DV), F32)
        for j in range(GLA_SLOW_ROWS):
            only_j = row == j
            k_j = jnp.where(only_j, k, 0.0).astype(BF16)
            v_j = jnp.where(only_j, v, 0.0).astype(BF16)
            q_j = jnp.where(only_j, q, 0.0).astype(BF16)
            st = st_ref[...] * decay[j:j + 1, :] + _dot_tn(v_j, k_j) * bd_ref[...]
            st_ref[...] = st
            o = o + _dot_nt(q_j, st.astype(BF16))
        o_ref[rows, :] = o
        return carry

    lax.fori_loop(0, n_rows // GLA_SLOW_ROWS, group, 0)


def _gla_blocked(q_ref, k_ref, v_ref, g_ref, tri_ref, m_ref, bd_ref, st_ref, *, t, n_sub):
    nb = t // SUB
    n_rows = t * n_sub
    lanes = N_HEADS_GLA * DK

    def group_row(x, period, offset):
        g = x.reshape(n_rows // period, period, lanes)[:, offset:offset + 1, :]
        return jnp.broadcast_to(g, (n_rows // period, period, lanes)).reshape(n_rows, lanes)

    g_hi, g_lo = _split_bf16(g_ref[...])
    b = _dot(tri_ref[...], g_hi) + _dot(tri_ref[...], g_lo)
    q = q_ref[...].astype(F32)
    k = k_ref[...].astype(F32)
    qd = (q * jnp.exp(b - group_row(b, SUB, 0))).astype(BF16)
    pos = lax.broadcasted_iota(I32, (n_rows, lanes), 0) & (t - 1)
    k_parts = []
    for blk in range(nb):
        arg = jnp.where(pos < SUB * (blk + 1), group_row(b, t, SUB * blk) - b, NEG_BIG)
        k_parts.append((k * jnp.exp(arg)).astype(BF16))
    head_rows = jnp.concatenate([m_ref[h * t:h * t + SUB, 0:lanes] for h in range(N_HEADS_GLA)], axis=0)
    q_dec = (q * jnp.exp(b)).astype(BF16)
    k_last = (k * jnp.exp(group_row(b, t, t - 1) - b)).astype(BF16)
    row_a = lax.broadcasted_iota(I32, (N_HEADS_GLA * t, t), 0) & (t - 1)
    col_a = lax.broadcasted_iota(I32, (N_HEADS_GLA * t, t), 1)

    o_intra, q_decayed, state_add, state_decay = [], [], [], []
    for u in range(n_sub):
        rows = slice(u * t, (u + 1) * t)
        v = v_ref[rows, :]
        pieces = []
        for blk in range(nb):
            q_blk = qd[u * t + blk * SUB:u * t + (blk + 1) * SUB]
            pieces.append(_dot_nt(jnp.concatenate([q_blk] * N_HEADS_GLA, axis=0) * head_rows, k_parts[blk][rows]))
        a = jnp.concatenate([pieces[blk][h * SUB:(h + 1) * SUB] for h in range(N_HEADS_GLA) for blk in range(nb)],
                            axis=0)
        a = jnp.where(row_a >= col_a, a, 0.0).astype(BF16)
        o_full = _dot(a, v)
        o_intra.append(jnp.concatenate(
            [o_full[h * t:(h + 1) * t, h * DV:(h + 1) * DV] for h in range(N_HEADS_GLA)], axis=1))
        q_decayed.append(q_dec[rows])
        state_add.append(_dot_tn(v, k_last[rows]) * bd_ref[...])
        state_decay.append(jnp.exp(b[(u + 1) * t - 1:(u + 1) * t, :]))

    st = st_ref[...]
    o_parts = []
    for u in range(n_sub):
        o_parts.append(o_intra[u] + _dot_nt(q_decayed[u], st.astype(BF16)))
        st = st * state_decay[u] + state_add[u]
    st_ref[...] = st
    return jnp.concatenate(o_parts, axis=0) if n_sub > 1 else o_parts[0]


def _gla(gq, gk, gv, gg, gr, s0, tri, mask, bd, gn, *, t, n_sub, n_batch, n_steps, first_block, name):
    rows_per_step = t * n_sub

    def rows(width):
        return pl.BlockSpec((rows_per_step, width), lambda b, c: (first_block + b * n_steps + c, 0))

    def full(a):
        return pl.BlockSpec(a.shape, lambda b, c: (0,) * a.ndim)

    state = pl.BlockSpec((None,) + s0.shape[1:], lambda b, c: (b, 0, 0))
    return pl.pallas_call(
        functools.partial(_gla_kernel, t=t, n_sub=n_sub),
        grid=(n_batch, n_steps),
        in_specs=[rows(W_GQ), rows(W_GK), rows(W_GV), rows(W_GQ), rows(W_GR), state,
                  full(tri), full(mask), full(bd), full(gn)],
        out_specs=[pl.BlockSpec((rows_per_step, W_GV), lambda b, c: (b * n_steps + c, 0)), state],
        out_shape=[jax.ShapeDtypeStruct((n_batch * n_steps * rows_per_step, W_GV), BF16),
                   jax.ShapeDtypeStruct(s0.shape, F32)],
        scratch_shapes=[pltpu.VMEM(s0.shape[1:], F32), pltpu.VMEM((rows_per_step, W_GV), F32)],
        compiler_params=_cparams(2, "mixer"),
        name=name,
    )(gq, gk, gv, gg, gr, s0, tri, mask, bd, gn)


def _front_prompt(xp, consts, head_mask, sink_b, s0, tri, mask, bd, gn):
    def full(a):
        return pl.BlockSpec(a.shape, lambda b, c: (0,) * a.ndim)

    def computed(width):
        return pl.BlockSpec((TM, width), lambda b, c: (jnp.minimum(c, N_PROMPT_TILES - 1), 0))

    def mixed(width):
        return pl.BlockSpec((TM, width), lambda b, c: (jnp.maximum(c - 1, 0), 0))

    state = pl.BlockSpec((None,) + s0.shape[1:], lambda b, c: (b, 0, 0))
    ring2 = [pltpu.VMEM((2, TM, w), dt) for w, dt in
             (PROJ_OUTPUTS[0], PROJ_OUTPUTS[3], PROJ_OUTPUTS[4], PROJ_OUTPUTS[5], PROJ_OUTPUTS[6], PROJ_OUTPUTS[7])]
    ring3 = [pltpu.VMEM((3, TM, w), dt) for w, dt in (PROJ_OUTPUTS[1], PROJ_OUTPUTS[2])]
    return pl.pallas_call(
        functools.partial(_front_prompt_kernel, t=CHUNK, n_sub=TM // CHUNK),
        grid=(1, N_PROMPT_TILES + 1),
        in_specs=[computed(D_MODEL)] + [full(a) for a in consts] + [full(head_mask), full(sink_b), state,
                                                                   full(tri), full(mask), full(bd), full(gn)],
        out_specs=[computed(W_SK), computed(W_SV), mixed(W_SQ), mixed(W_GV), state],
        out_shape=[jax.ShapeDtypeStruct((N_PROMPT, W_SK), F32), jax.ShapeDtypeStruct((N_PROMPT, W_SV), F32),
                   jax.ShapeDtypeStruct((N_PROMPT, W_SQ), BF16), jax.ShapeDtypeStruct((N_PROMPT, W_GV), BF16),
                   jax.ShapeDtypeStruct(s0.shape, F32)],
        scratch_shapes=ring2 + ring3 + [pltpu.VMEM(s0.shape[1:], F32), pltpu.VMEM((TM, W_GV), F32)],
        compiler_params=_cparams(2, "front"),
        name="front_prompt",
    )(xp, *consts, head_mask, sink_b, s0, tri, mask, bd, gn)


def _tile_row_copies(hbm_ref, tile, vmem_ref, sem, to_hbm, rows=TM):
    copies = []
    for a in range(ROW_SUB):
        h = hbm_ref.at[pl.ds(tile * rows, rows), a, :]
        v = vmem_ref.at[:, pl.ds(a * LANES, LANES)]
        copies.append(pltpu.make_async_copy(v, h, sem) if to_hbm else pltpu.make_async_copy(h, v, sem))
    return copies


def _store_tile_rows(i, n_steps, outputs, row_buf, row_sem, rows=TM):
    buf_slot = lax.rem(i, 2)

    def store(j, tile, s):
        return _tile_row_copies(outputs[j][0], tile, row_buf.at[j, s], row_sem.at[j, s], True, rows)

    for j, (_, value) in enumerate(outputs):
        @pl.when(i >= 2)
        def _():
            for c in store(j, i - 2, buf_slot):
                c.wait()

        row_buf[j, buf_slot] = value
        for c in store(j, i, buf_slot):
            c.start()

        @pl.when(i == n_steps - 1)
        def _():
            for c in store(j, i - 1, 1 - buf_slot) + store(j, i, buf_slot):
                c.wait()


def _merge_kernel(oswp_ref, osws_ref, ogp_ref, ogs_ref, xp_ref, xs_ref, wo1_ref, wo2_ref, gf_ref, wrh_ref, wrl_ref,
                  br_ref, xrow_ref, hrow_ref, lg_ref, row_buf, row_sem):
    i = pl.program_id(0)
    is_prompt = i < N_PROMPT_TILES
    x = jnp.where(is_prompt, xp_ref[...], xs_ref[...])
    o_swa = jnp.where(is_prompt, oswp_ref[...], osws_ref[...])
    og = jnp.where(is_prompt, ogp_ref[...], ogs_ref[...])
    h = x + (_dot(o_swa, wo1_ref[...]) + _dot(og, wo2_ref[...]))
    xn = _rms(h) * gf_ref[...]
    x_hi, x_lo = _split_bf16(xn)
    logits = _dot(x_hi, wrh_ref[...]) + _dot(x_lo, wrh_ref[...]) + _dot(x_hi, wrl_ref[...]) + br_ref[...]
    lg_ref[...] = logits.T[:N_EXPERTS]
    _store_tile_rows(i, N_TILES, ((xrow_ref, xn), (hrow_ref, h)), row_buf, row_sem)


def _merge(o_swa_p, o_swa_s, og_p, og_s, xp, xs, wo1, wo2, gf, wrh, wrl, br):
    def prompt_rows(width):
        return pl.BlockSpec((TM, width), lambda i: (jnp.minimum(i, N_PROMPT_TILES - 1), 0))

    def sample_rows(width):
        return pl.BlockSpec((TM, width), lambda i: (0, 0))

    def full(a):
        return pl.BlockSpec(a.shape, lambda i: (0,) * a.ndim)

    consts = (wo1, wo2, gf, wrh, wrl, br)
    return pl.pallas_call(
        _merge_kernel,
        grid=(N_TILES,),
        in_specs=[prompt_rows(W_SQ), sample_rows(W_SQ), prompt_rows(W_GV), sample_rows(W_GV),
                  prompt_rows(D_MODEL), sample_rows(D_MODEL)] + [full(a) for a in consts],
        out_specs=[pl.BlockSpec(memory_space=pl.ANY)] * 2 + [pl.BlockSpec((N_EXPERTS, TM), lambda i: (0, i))],
        out_shape=[jax.ShapeDtypeStruct((N_ROWS,) + ROW_TILE, F32)] * 2 + [
                   jax.ShapeDtypeStruct((N_EXPERTS, N_ROWS), F32)],
        scratch_shapes=[pltpu.VMEM((2, 2, TM, D_MODEL), F32), pltpu.SemaphoreType.DMA((2, 2))],
        compiler_params=_cparams(1, "merge"),
        name="merge",
    )(o_swa_p, o_swa_s, og_p, og_s, xp, xs, *consts)


RT = 1280
N_ROUTER_STEPS = N_ROWS // RT


def _router_kernel(lg_ref, tri_ref, wrow_ref, ti_ref, rk_ref, cnt_ref, base_ref, row_buf, row_sem):
    i = pl.program_id(0)

    @pl.when(i == 0)
    def _():
        base_ref[...] = jnp.zeros_like(base_ref)

    logits_t = lg_ref[...]
    expert = lax.broadcasted_iota(I32, logits_t.shape, 0)
    slot = lax.broadcasted_iota(I32, (SLOT_ROWS, RT), 0)
    vals, hots = [], []
    ti = jnp.zeros((SLOT_ROWS, RT), I32)
    for kk in range(TOP_K):
        m = jnp.max(logits_t, axis=0, keepdims=True)
        idx = jnp.min(jnp.where(logits_t == m, expert, N_EXPERTS), axis=0, keepdims=True)
        hot = expert == idx
        logits_t = jnp.where(hot, NEG_BIG, logits_t)
        vals.append(m)
        hots.append(hot)
        ti = jnp.where(slot == kk, idx, ti)
    ti_ref[...] = ti
    exps = [jnp.exp(v - vals[0]) for v in vals]
    den = exps[0] + exps[1] + exps[2] + exps[3]
    tw_t = jnp.zeros((SLOT_ROWS, RT), F32)
    for kk in range(TOP_K):
        tw_t = jnp.where(slot == kk, exps[kk] / den, tw_t)
    eye = (lax.broadcasted_iota(I32, (SLOT_ROWS, LANES), 0)
           == lax.broadcasted_iota(I32, (SLOT_ROWS, LANES), 1)).astype(F32).astype(BF16)
    w_hi = tw_t.astype(BF16)
    w_mid, w_lo = _split_bf16(tw_t - w_hi.astype(F32))
    tw_col = _dot_tn(w_hi, eye) + _dot_tn(w_mid, eye) + _dot_tn(w_lo, eye)

    onehot_t = jnp.zeros(logits_t.shape, F32)
    for hot in hots:
        onehot_t = onehot_t + jnp.where(hot, 1.0, 0.0)
    before_t = _dot(onehot_t.astype(BF16), tri_ref[...]) + base_ref[:, 0:1]
    rk = jnp.zeros((SLOT_ROWS, RT), I32)
    for kk in range(TOP_K):
        r = jnp.sum(jnp.where(hots[kk], before_t, 0.0), axis=0, keepdims=True).astype(I32)
        rk = jnp.where(slot == kk, r, rk)
    rk_ref[...] = rk
    total = base_ref[...] + jnp.sum(onehot_t, axis=1, keepdims=True)
    base_ref[...] = total
    cnt_ref[...] = total.astype(I32)

    w_lanes = [jnp.broadcast_to(tw_col[:, kk:kk + 1], (RT, LANES)) for kk in range(TOP_K)]
    w_rows = jnp.concatenate(w_lanes + [jnp.zeros((RT, D_MODEL - TOP_K * LANES), F32)], axis=1)
    _store_tile_rows(i, N_ROUTER_STEPS, ((wrow_ref, w_rows),), row_buf, row_sem, RT)


def _router(logits, tri):
    return pl.pallas_call(
        _router_kernel,
        grid=(N_ROUTER_STEPS,),
        in_specs=[pl.BlockSpec((N_EXPERTS, RT), lambda i: (0, i)), pl.BlockSpec(tri.shape, lambda i: (0, 0))],
        out_specs=[pl.BlockSpec(memory_space=pl.ANY),
                   pl.BlockSpec((SLOT_ROWS, RT), lambda i: (0, i)),
                   pl.BlockSpec((SLOT_ROWS, RT), lambda i: (0, i)),
                   pl.BlockSpec((N_EXPERTS, LANES), lambda i: (0, 0))],
        out_shape=[jax.ShapeDtypeStruct((N_ROWS,) + ROW_TILE, F32),
                   jax.ShapeDtypeStruct((SLOT_ROWS, N_ROWS), I32),
                   jax.ShapeDtypeStruct((SLOT_ROWS, N_ROWS), I32),
                   jax.ShapeDtypeStruct((N_EXPERTS, LANES), I32)],
        scratch_shapes=[pltpu.VMEM((N_EXPERTS, LANES), F32), pltpu.VMEM((1, 2, RT, D_MODEL), F32),
                        pltpu.SemaphoreType.DMA((1, 2))],
        compiler_params=_cparams(1, "router"),
        name="router",
    )(logits, tri)


ISSUE_UNROLL = 4


def _row_copy(src_ref, src_row, dst_ref, dst_row, sem):
    return pltpu.make_async_copy(src_ref.at[pl.ds(src_row, 1)], dst_ref.at[pl.ds(dst_row, 1)], sem)


def _dispatch_kernel(dest_ref, end_ref, x_ref, xs_ref, zero_ref, sem, zsem):
    i = pl.program_id(0)
    base = i * TM

    @pl.when(i == 0)
    def _():
        zero_ref[...] = jnp.zeros_like(zero_ref)

        def tail_copy(e):
            last = jnp.maximum(end_ref[e] - TM, 0)
            return pltpu.make_async_copy(zero_ref, xs_ref.at[pl.ds(pl.multiple_of(last, TM), TM)], zsem)

        def fill(e, carry):
            tail_copy(e).start()
            return carry

        def fill_wait(e, carry):
            tail_copy(e).wait()
            return carry

        lax.fori_loop(0, N_EXPERTS, fill, 0)
        lax.fori_loop(0, N_EXPERTS, fill_wait, 0)

        def unused_copy(t):
            return pltpu.make_async_copy(zero_ref, xs_ref.at[pl.ds(pl.multiple_of(t * TM, TM), TM)], zsem)

        def fill_unused(t, carry):
            unused_copy(t).start()
            unused_copy(t).wait()
            return carry

        lax.fori_loop(end_ref[N_EXPERTS - 1] // TM, N_EXPERT_TILES, fill_unused, 0)

    def issue(n, carry):
        for kk in range(TOP_K):
            _row_copy(x_ref, n, xs_ref, dest_ref[kk * N_ROWS + base + n], sem).start(priority=kk % 2)
        return carry

    lax.fori_loop(0, TM, issue, 0, unroll=ISSUE_UNROLL)

    for kk in range(TOP_K):
        pltpu.make_async_copy(x_ref, xs_ref.at[pl.ds(0, TM)], sem).wait()


def _dispatch(dest, end, x_packed):
    return pl.pallas_call(
        _dispatch_kernel,
        grid_spec=pltpu.PrefetchScalarGridSpec(
            num_scalar_prefetch=2,
            grid=(N_TILES,),
            in_specs=[pl.BlockSpec((TM,) + ROW_TILE, lambda i, d, e: (i, 0, 0))],
            out_specs=pl.BlockSpec(memory_space=pl.ANY),
            scratch_shapes=[pltpu.VMEM((TM,) + ROW_TILE, F32), pltpu.SemaphoreType.DMA,
                            pltpu.SemaphoreType.DMA],
        ),
        out_shape=jax.ShapeDtypeStruct((N_SORTED_ROWS,) + ROW_TILE, F32),
        compiler_params=_cparams(1, "dispatch"),
        name="dispatch",
    )(dest, end, x_packed)


CAST_ROWS = 128


def _moe_kernel(te_ref, nu_ref, nx_ref, xs_ref, wg_ref, bg_ref, wu_ref, bu_ref, wd_ref, bd_ref, ys_ref,
                w_stage, w_bf, x_buf, y_buf, zero_buf, w_sem, in_sem, out_sem, zero_sem):
    t = pl.program_id(0)
    n_used = nu_ref[0]
    slot = lax.rem(t, 2)
    e = te_ref[t]
    e_prev = te_ref[jnp.maximum(t - 1, 0)]

    def load(tile, s):
        return _tile_row_copies(xs_ref, tile, x_buf.at[s], in_sem.at[s], to_hbm=False)

    def store(tile, s):
        return _tile_row_copies(ys_ref, tile, y_buf.at[s], out_sem.at[s], to_hbm=True)

    def weight_copies(expert):
        return [pltpu.make_async_copy(w.at[expert], w_stage.at[j], w_sem.at[j])
                for j, w in enumerate((wg_ref, wu_ref, wd_ref))]

    @pl.when(t == 0)
    def _():
        for c in weight_copies(e) + load(0, 0):
            c.start()

    @pl.when(t + 1 < n_used)
    def _():
        for c in load(t + 1, 1 - slot):
            c.start()

    @pl.when((t == 0) | (e != e_prev))
    def _():
        for c in weight_copies(e):
            c.wait()

        def cast(r, carry):
            sl = pl.ds(pl.multiple_of(r * CAST_ROWS, CAST_ROWS), CAST_ROWS)
            for j in range(3):
                w_bf[j, sl, :] = w_stage[j, sl, :].astype(BF16)
            return carry

        lax.fori_loop(0, D_MODEL // CAST_ROWS, cast, 0)
        e_next = nx_ref[e]

        @pl.when(e_next >= 0)
        def _():
            for c in weight_copies(e_next):
                c.start()

    @pl.when(t < n_used)
    def _():
        for c in load(t, slot):
            c.wait()

        @pl.when(t >= 2)
        def _():
            for c in store(t - 2, slot):
                c.wait()

        x = x_buf[slot].astype(BF16)
        gate = jnp.minimum(_dot(x, w_bf[0]) + bg_ref[...], SWIGLU_LIMIT)
        up = jnp.clip(_dot(x, w_bf[1]) + bu_ref[...], -SWIGLU_LIMIT, SWIGLU_LIMIT)
        hdn = (up + 1.0) * gate * (1.0 / (1.0 + jnp.exp(-SWIGLU_ALPHA * gate)))
        y_buf[slot] = _dot(hdn.astype(BF16), w_bf[2]) + bd_ref[...]
        for c in store(t, slot):
            c.start()

    @pl.when(t >= n_used)
    def _():
        zero_buf[...] = jnp.zeros_like(zero_buf)
        fill = pltpu.make_async_copy(zero_buf, ys_ref.at[pl.ds(t * TM, TM)], zero_sem)
        fill.start()
        fill.wait()

    @pl.when(t == N_EXPERT_TILES - 1)
    def _():
        @pl.when(n_used >= 2)
        def _():
            for c in store(n_used - 2, lax.rem(n_used, 2)):
                c.wait()

        for c in store(n_used - 1, lax.rem(n_used - 1, 2)):
            c.wait()


def _moe(tile_expert, n_used, next_expert, xs, w_gate, b_gate, w_up, b_up, w_down, b_down):
    hbm = pl.BlockSpec(memory_space=pl.ANY)
    bias = pl.BlockSpec((None, 1, D_MODEL), lambda t, te, nu, nx: (te[t], 0, 0))
    return pl.pallas_call(
        _moe_kernel,
        grid_spec=pltpu.PrefetchScalarGridSpec(
            num_scalar_prefetch=3,
            grid=(N_EXPERT_TILES,),
            in_specs=[hbm, hbm, bias, hbm, bias, hbm, bias],
            out_specs=hbm,
            scratch_shapes=[pltpu.VMEM((3, D_MODEL, D_MODEL), F32), pltpu.VMEM((3, D_MODEL, D_MODEL), BF16),
                            pltpu.VMEM((2, TM, D_MODEL), F32), pltpu.VMEM((2, TM, D_MODEL), F32),
                            pltpu.VMEM((TM,) + ROW_TILE, F32), pltpu.SemaphoreType.DMA((3,)),
                            pltpu.SemaphoreType.DMA((2,)), pltpu.SemaphoreType.DMA((2,)),
                            pltpu.SemaphoreType.DMA],
        ),
        out_shape=jax.ShapeDtypeStruct((N_SORTED_ROWS,) + ROW_TILE, F32),
        compiler_params=_cparams(1, "experts"),
        name="experts",
    )(tile_expert, n_used, next_expert, xs, w_gate, b_gate, w_up, b_up, w_down, b_down)


def _combine_kernel(dest_ref, h_ref, w_ref, ys_ref, yp_ref, ysm_ref, g_ref, out_ref, sem):
    i = pl.program_id(0)
    slot = lax.rem(i, 2)

    def issue_tile(tile, s):
        base = tile * TM

        def issue(n, carry):
            for kk in range(TOP_K):
                _row_copy(ys_ref, dest_ref[kk * N_ROWS + base + n], g_ref.at[s], kk * TM + n,
                          sem.at[s]).start(priority=kk % 2)
            return carry

        lax.fori_loop(0, TM, issue, 0, unroll=ISSUE_UNROLL)

    @pl.when(i == 0)
    def _():
        issue_tile(0, 0)

    @pl.when(i + 1 < N_TILES)
    def _():
        issue_tile(i + 1, 1 - slot)

    for kk in range(TOP_K):
        pltpu.make_async_copy(ys_ref.at[pl.ds(0, TM)], g_ref.at[slot, pl.ds(kk * TM, TM)], sem.at[slot]).wait()

    w = w_ref[...]
    acc = None
    for kk in range(TOP_K):
        part = jnp.broadcast_to(w[:, kk:kk + 1, :], (TM,) + ROW_TILE) * g_ref[slot, kk * TM:(kk + 1) * TM]
        acc = part if acc is None else acc + part
    out_ref[...] = h_ref[...] + acc

    def write(y_ref):
        for a in range(ROW_SUB):
            y_ref[:, a * LANES:(a + 1) * LANES] = out_ref[:, a, :]

    @pl.when(i < N_PROMPT_TILES)
    def _():
        write(yp_ref)

    @pl.when(i >= N_PROMPT_TILES)
    def _():
        write(ysm_ref)


def _combine(dest, h_rows, w_rows, ys):
    tile = lambda index: pl.BlockSpec((TM,) + ROW_TILE, index)
    return pl.pallas_call(
        _combine_kernel,
        grid_spec=pltpu.PrefetchScalarGridSpec(
            num_scalar_prefetch=1,
            grid=(N_TILES,),
            in_specs=[tile(lambda i, d: (i, 0, 0)), tile(lambda i, d: (i, 0, 0)),
                      pl.BlockSpec(memory_space=pl.ANY)],
            out_specs=[pl.BlockSpec((TM, D_MODEL), lambda i, d: (jnp.minimum(i, N_PROMPT_TILES - 1), 0)),
                       pl.BlockSpec((TM, D_MODEL), lambda i, d: (0, 0))],
            scratch_shapes=[pltpu.VMEM((2, TOP_K * TM) + ROW_TILE, F32), pltpu.VMEM((TM,) + ROW_TILE, F32),
                            pltpu.SemaphoreType.DMA((2,))],
        ),
        out_shape=[jax.ShapeDtypeStruct((N_PROMPT, D_MODEL), F32),
                   jax.ShapeDtypeStruct((N_SAMPLE, D_MODEL), F32)],
        compiler_params=_cparams(1, "combine"),
        name="combine",
    )(dest, h_rows, w_rows, ys)


def _block_diag_ones(n, blk):
    idx = np.arange(n) // blk
    return (idx[:, None] == idx[None, :]).astype(np.float32)


def _swa_head_mask(tq):
    row_head = np.arange(4 * tq)[:, None] // tq
    lane_head = np.arange(4 * HD)[None, :] // HD
    return jnp.asarray((row_head == lane_head).astype(np.float32), BF16)


def _gla_masks(t, n_sub):
    nb = t // SUB
    lanes = N_HEADS_GLA * DK
    tri = jnp.asarray(np.kron(np.eye(n_sub, dtype=np.float32), np.tril(np.ones((t, t), np.float32))), BF16)
    row = np.arange(N_HEADS_GLA * t)
    col = np.arange(nb * lanes)
    same_head = (row[:, None] // t) == ((col[None, :] % lanes) // DK)
    same_blk = ((row[:, None] % t) // SUB) == (col[None, :] // lanes)
    mask = jnp.asarray((same_head & same_blk).astype(np.float32), BF16)
    return tri, mask


def _sink_rows(sinks, tq):
    s = jnp.repeat(sinks.astype(F32).reshape(2, 4), tq, axis=1)
    return jnp.broadcast_to(s[:, None, :], (2, 8, 4 * tq))


def kernel(x_prompt, x_sample, state_gla, cache_swa_k, cache_swa_v, norm_mix_g, w_in, w_gla_a2, b_gla_a, q_norm_g,
           k_norm_g, swa_sinks, gla_norm_g, w_out, norm_ffn_g, w_router, b_router, w_gate, b_gate, w_up, b_up,
           w_down, b_down):
    xp = x_prompt.reshape(N_PROMPT, D_MODEL)
    xs = x_sample.reshape(N_SAMPLE, D_MODEL)

    w_in0 = w_in[0]
    w_main = w_in0[:, :W_MAIN].astype(BF16)
    w_ga = jnp.pad(w_in0[:, OFF_GA:], ((0, 0), (0, LANES - W_GA))).astype(BF16)
    w_a2 = jnp.pad(w_gla_a2[0], ((0, LANES - W_GA), (0, 0))).astype(BF16)
    b_a = b_gla_a[0].reshape(1, -1)
    qg = (jnp.tile(q_norm_g[0], N_HEADS_SWA) * (HD ** -0.5)).reshape(1, -1)
    kg = jnp.tile(k_norm_g[0], 2).reshape(1, -1)
    bdq = jnp.asarray(_block_diag_ones(W_SQ // 2, HD), BF16)
    bdk = jnp.asarray(_block_diag_ones(W_SK, HD), BF16)

    proj_consts = (norm_mix_g[0].reshape(1, -1), w_main, w_ga, w_a2, b_a, qg, kg, bdq, bdk)
    sq, sk, sv, gq, gk, gv, gg, gr = _proj_sample(xs, proj_consts)

    cache_k = cache_swa_k[0].reshape(N_STREAMS, WINDOW, 2 * HD)
    cache_v = cache_swa_v[0].reshape(N_STREAMS, WINDOW, 2 * HD)
    o_swa_s = _swa_sample(sq, sk, sv, cache_k, cache_v, _swa_head_mask(T_SAMPLE),
                          _sink_rows(swa_sinks[0], T_SAMPLE))

    bd_state = jnp.asarray(_block_diag_ones(N_HEADS_GLA, 1).repeat(DV, axis=0).repeat(DK, axis=1), F32)
    gn = gla_norm_g[0].reshape(1, -1)
    tri_p, mask_p = _gla_masks(CHUNK, TM // CHUNK)
    tri_s, mask_s = _gla_masks(T_SAMPLE, 1)
    s0_p = jnp.zeros((1, N_HEADS_GLA * DV, N_HEADS_GLA * DK), F32)
    sk_p, sv_p, o_swa_p, og_p, sfin_p = _front_prompt(xp, proj_consts, _swa_head_mask(CHUNK),
                                                      _sink_rows(swa_sinks[0], CHUNK), s0_p, tri_p, mask_p,
                                                      bd_state, gn)
    eye = jnp.eye(N_HEADS_GLA, dtype=F32)
    s0_s = jnp.einsum('bhde,hg->bhegd', state_gla[0].astype(F32), eye).reshape(
        N_STREAMS, N_HEADS_GLA * DV, N_HEADS_GLA * DK)
    og_s, sfin_s = _gla(gq, gk, gv, gg, gr, s0_s, tri_s, mask_s, bd_state, gn, t=T_SAMPLE, n_sub=1,
                        n_batch=N_STREAMS, n_steps=1, first_block=0, name="gla_sample")

    def unpack_state(sfin):
        s = sfin.reshape(-1, N_HEADS_GLA, DV, N_HEADS_GLA, DK)
        s = jnp.stack([s[:, h, :, h, :] for h in range(N_HEADS_GLA)], axis=1)
        return jnp.transpose(s, (0, 1, 3, 2))[None]

    w_out0 = w_out[0].astype(BF16)
    wr = jnp.pad(w_router[0], ((0, 0), (0, LANES - N_EXPERTS)))
    wr_hi = wr.astype(BF16)
    wr_lo = (wr - wr_hi.astype(F32)).astype(BF16)
    br = jnp.pad(b_router[0].astype(F32), (0, LANES - N_EXPERTS)).reshape(1, -1)
    x_rows, h_rows, logits = _merge(o_swa_p, o_swa_s, og_p, og_s, xp, xs, w_out0[:W_SQ], w_out0[W_SQ:],
                                    norm_ffn_g[0].reshape(1, -1), wr_hi, wr_lo, br)
    earlier = jnp.asarray(np.triu(np.ones((RT, RT), np.float32), 1), BF16)
    w_rows, top_i, rank, counts = _router(logits, earlier)

    counts = counts[:, 0]
    padded = (counts + TM - 1) // TM * TM
    end = jnp.cumsum(padded)
    start = end - padded
    experts = jnp.arange(N_EXPERTS, dtype=I32)
    is_e = top_i[:TOP_K, :, None] == experts
    dest = (rank[:TOP_K] + jnp.sum(jnp.where(is_e, start, 0), axis=-1)).reshape(-1).astype(I32)
    n_used = (end[-1] // TM).astype(I32)
    tiles = jnp.minimum(jnp.arange(N_EXPERT_TILES, dtype=I32), n_used - 1)
    tile_expert = jnp.sum((tiles[:, None] * TM >= end[None, :]).astype(I32), axis=1)
    later_nonempty = (experts[None, :] > experts[:, None]) & (padded[None, :] > 0)
    next_expert = jnp.min(jnp.where(later_nonempty, experts[None, :], N_EXPERTS), axis=1)
    next_expert = jnp.where(next_expert < N_EXPERTS, next_expert, -1).astype(I32)

    xs_sorted = _dispatch(dest, end.astype(I32), x_rows)
    ys = _moe(tile_expert, n_used.reshape(1), next_expert, xs_sorted, w_gate[0], b_gate[0].reshape(N_EXPERTS, 1, -1),
              w_up[0], b_up[0].reshape(N_EXPERTS, 1, -1), w_down[0], b_down[0].reshape(N_EXPERTS, 1, -1))
    y_p, y_s = _combine(dest, h_rows, w_rows, ys)

    sk_s = sk.reshape(N_STREAMS, T_SAMPLE, 2 * HD)
    sv_s = sv.reshape(N_STREAMS, T_SAMPLE, 2 * HD)
    kc_s = jnp.concatenate([cache_k[:, T_SAMPLE:], sk_s], axis=1).reshape(1, N_STREAMS, WINDOW, 2, HD)
    vc_s = jnp.concatenate([cache_v[:, T_SAMPLE:], sv_s], axis=1).reshape(1, N_STREAMS, WINDOW, 2, HD)
    kc_p = sk_p[N_PROMPT - WINDOW:].reshape(1, 1, WINDOW, 2, HD)
    vc_p = sv_p[N_PROMPT - WINDOW:].reshape(1, 1, WINDOW, 2, HD)
    return (y_p.reshape(1, N_PROMPT, D_MODEL), y_s.reshape(N_STREAMS, T_SAMPLE, D_MODEL),
            unpack_state(sfin_p), kc_p, vc_p, unpack_state(sfin_s), kc_s, vc_s)
```

```python
import functools

import numpy as np
import jax
import jax.numpy as jnp
from jax import lax
from jax.experimental import pallas as pl
from jax.experimental.pallas import tpu as pltpu

F32 = jnp.float32
BF16 = jnp.bfloat16
I32 = jnp.int32

D_MODEL = 1024
N_PROMPT = 16384
N_STREAMS = 8
T_SAMPLE = 32
N_SAMPLE = N_STREAMS * T_SAMPLE
N_ROWS = N_PROMPT + N_SAMPLE
EPS = 1e-6

CHUNK = 64
SUB = 16
N_HEADS_SWA = 8
HD = 64
WINDOW = 128
N_HEADS_GLA = 4
DK = 64
DV = 128
GLA_TAU = 16.0
N_EXPERTS = 32
TOP_K = 4
SWIGLU_ALPHA = 1.702
SWIGLU_LIMIT = 7.0

TM = 256
N_TILES = N_ROWS // TM
N_PROMPT_TILES = N_PROMPT // TM
N_ASSIGN = N_ROWS * TOP_K
N_EXPERT_TILES = N_ASSIGN // TM + N_EXPERTS
N_SORTED_ROWS = N_EXPERT_TILES * TM
LANES = 128
SLOT_ROWS = 16
NEG_BIG = -1e30

W_SQ, W_SK, W_SV, W_GQ, W_GK, W_GV, W_GR, W_GA = 512, 128, 128, 256, 256, 512, 512, 16
OFF_SQ = 0
OFF_SK = OFF_SQ + W_SQ
OFF_SV = OFF_SK + W_SK
OFF_GQ = OFF_SV + W_SV
OFF_GK = OFF_GQ + W_GQ
OFF_GV = OFF_GK + W_GK
OFF_GR = OFF_GV + W_GV
OFF_GA = OFF_GR + W_GR
W_MAIN = OFF_GA


VMEM_MIB = dict(proj=40, front=48, mixer=32, merge=32, router=48, dispatch=32, experts=52, combine=40)


def _cparams(n_grid_axes, call):
    return pltpu.CompilerParams(dimension_semantics=("arbitrary",) * n_grid_axes,
                                vmem_limit_bytes=VMEM_MIB[call] * 2 ** 20)


def _dot(a, b):
    return jnp.dot(a, b, preferred_element_type=F32)


def _dot_nt(a, b):
    return lax.dot_general(a, b, (((1,), (1,)), ((), ())), preferred_element_type=F32)


def _dot_tn(a, b):
    return lax.dot_general(a, b, (((0,), (0,)), ((), ())), preferred_element_type=F32)


def _split_bf16(x):
    hi = x.astype(BF16)
    lo = (x - hi.astype(F32)).astype(BF16)
    return hi, lo


def _rms(x):
    return x * lax.rsqrt(jnp.mean(x * x, axis=-1, keepdims=True) + EPS)


ROW_SUB = 8
ROW_TILE = (ROW_SUB, LANES)


PROJ_OUTPUTS = ((W_SQ, BF16), (W_SK, F32), (W_SV, F32), (W_GQ, BF16), (W_GK, BF16), (W_GV, BF16),
                (W_GQ, F32), (W_GR, BF16))
N_PROJ_CONSTS = 9


def _proj_tile(x, g_ref, w_ref, wga_ref, wa2_ref, ba_ref, qg_ref, kg_ref, bdq_ref, bdk_ref,
               sq_ref, sk_ref, sv_ref, gq_ref, gk_ref, gv_ref, gg_ref, gr_ref):
    xb = (_rms(x) * g_ref[...]).astype(BF16)

    def seg(off, width):
        return _dot(xb, w_ref[:, off:off + width])

    def head_norm(u, bd_ref):
        hi, lo = _split_bf16(u * u)
        w = bd_ref.shape[0]
        ss = jnp.concatenate([_dot(hi[:, c:c + w], bd_ref[...]) + _dot(lo[:, c:c + w], bd_ref[...])
                              for c in range(0, u.shape[1], w)], axis=1)
        return u * lax.rsqrt(ss * (1.0 / HD) + EPS)

    sq_ref[...] = (head_norm(seg(OFF_SQ, W_SQ), bdq_ref) * qg_ref[...]).astype(BF16)
    kv = seg(OFF_SK, W_SK + W_SV)
    sk_ref[...] = head_norm(kv[:, :W_SK], bdk_ref) * kg_ref[...]
    sv_ref[...] = kv[:, W_SK:]
    gq_ref[...] = (seg(OFF_GQ, W_GQ) * (DK ** -0.5)).astype(BF16)
    gk_ref[...] = seg(OFF_GK, W_GK).astype(BF16)
    gv_ref[...] = seg(OFF_GV, W_GV).astype(BF16)
    gr_ref[...] = seg(OFF_GR, W_GR).astype(BF16)
    ga = _dot(xb, wga_ref[...]).astype(BF16)
    z = _dot(ga, wa2_ref[...]) + ba_ref[...]
    log_sig = jnp.minimum(z, 0.0) - jnp.log(1.0 + jnp.exp(-jnp.abs(z)))
    gg_ref[...] = log_sig * (1.0 / GLA_TAU)


def _proj_sample_kernel(x_ref, *refs):
    _proj_tile(x_ref[...], *refs)


def _proj_sample(xs, consts):
    def full(a):
        return pl.BlockSpec(a.shape, lambda i: (0,) * a.ndim)

    return pl.pallas_call(
        _proj_sample_kernel,
        grid=(1,),
        in_specs=[full(xs)] + [full(a) for a in consts],
        out_specs=[pl.BlockSpec((N_SAMPLE, w), lambda i: (0, 0)) for w, _ in PROJ_OUTPUTS],
        out_shape=[jax.ShapeDtypeStruct((N_SAMPLE, w), dt) for w, dt in PROJ_OUTPUTS],
        compiler_params=_cparams(1, "proj"),
        name="proj_sample",
    )(xs, *consts)


def _dup_kv_heads(x):
    r = pltpu.roll(x, HD, axis=1)
    lo = lax.broadcasted_iota(I32, x.shape, 1) < HD
    out = []
    for a in (jnp.where(lo, x, r), jnp.where(lo, r, x)):
        out.append(jnp.concatenate([a, a], axis=1).astype(BF16))
    return out


def _swa_blocks(q_blocks, k_blocks, v_blocks, sinks, valids, head_mask):
    tq = q_blocks[0].shape[0]
    scores = []
    for q, k, valid in zip(q_blocks, k_blocks, valids):
        s_t = _dot_nt(k, jnp.concatenate([q] * 4, axis=0) * head_mask)
        scores.append(s_t if valid is None else jnp.where(valid, s_t, -jnp.inf))
    probs = []
    for s_t, sink in zip(scores, sinks):
        m = jnp.maximum(jnp.max(s_t, axis=0, keepdims=True), sink)
        p_t = jnp.exp(s_t - m)
        den = jnp.sum(p_t, axis=0, keepdims=True) + jnp.exp(sink - m)
        probs.append((p_t / den).astype(BF16))
    lane_head = lax.broadcasted_iota(I32, (tq, 4 * HD), 1) // HD
    outs = []
    for p_t, v in zip(probs, v_blocks):
        o_full = _dot_tn(p_t, v)
        o = jnp.zeros((tq, 4 * HD), F32)
        for a in range(4):
            o = o + jnp.where(lane_head == a, o_full[a * tq:(a + 1) * tq], 0.0)
        outs.append(o)
    return outs


def _swa_prompt_tile(i, q_ref, kp_ref, kc_ref, vp_ref, vc_ref, hm_ref, sink_ref, o_ref):
    k_dup = _dup_kv_heads(jnp.concatenate([kp_ref[...], kc_ref[...]], axis=0))
    v_dup = _dup_kv_heads(jnp.concatenate([vp_ref[...], vc_ref[...]], axis=0))
    sink_row = [sink_ref[j][0:1, :] for j in range(2)]
    span = WINDOW + CHUNK
    key = lax.broadcasted_iota(I32, (span, 4 * CHUNK), 0)
    qs, ks, vs, sinks, valids, where = [], [], [], [], [], []
    for c in range(TM // CHUNK):
        lo = CHUNK * c
        valid = (i * TM - WINDOW + lo + key) >= 0
        for j in range(2):
            qs.append(q_ref[lo:lo + CHUNK, 4 * HD * j:4 * HD * (j + 1)])
            ks.append(k_dup[j][lo:lo + span])
            vs.append(v_dup[j][lo:lo + span])
            sinks.append(sink_row[j])
            valids.append(valid)
            where.append((lo, j))
    outs = _swa_blocks(qs, ks, vs, sinks, valids, hm_ref[...])
    for (lo, j), o in zip(where, outs):
        o_ref[lo:lo + CHUNK, 4 * HD * j:4 * HD * (j + 1)] = o.astype(BF16)


def _swa_sample_kernel(q_ref, kc_ref, kn_ref, vc_ref, vn_ref, hm_ref, sink_ref, o_ref):
    k_dup = _dup_kv_heads(jnp.concatenate([kc_ref[...], kn_ref[...]], axis=0))
    v_dup = _dup_kv_heads(jnp.concatenate([vc_ref[...], vn_ref[...]], axis=0))
    sink_row = [sink_ref[j][0:1, :] for j in range(2)]
    qs = [q_ref[:, 4 * HD * j:4 * HD * (j + 1)] for j in range(2)]
    outs = _swa_blocks(qs, k_dup, v_dup, sink_row, [None, None], hm_ref[...])
    o_ref[...] = jnp.concatenate(outs, axis=1).astype(BF16)


def _swa_sample(sq, sk, sv, cache_k, cache_v, head_mask, sink_b):
    new = lambda width: pl.BlockSpec((T_SAMPLE, width), lambda b: (b, 0))
    cache = pl.BlockSpec((None, WINDOW, 2 * HD), lambda b: (b, 0, 0))
    return pl.pallas_call(
        _swa_sample_kernel,
        grid=(N_STREAMS,),
        in_specs=[new(W_SQ), cache, new(2 * HD), cache, new(2 * HD),
                  pl.BlockSpec(head_mask.shape, lambda b: (0, 0)),
                  pl.BlockSpec(sink_b.shape, lambda b: (0, 0, 0))],
        out_specs=pl.BlockSpec((T_SAMPLE, W_SQ), lambda b: (b, 0)),
        out_shape=jax.ShapeDtypeStruct((N_SAMPLE, W_SQ), BF16),
        compiler_params=_cparams(1, "mixer"),
        name="swa_sample",
    )(sq, cache_k, sk, cache_v, sv, head_mask, sink_b)


GLA_SAFE_EXP = 80.0
GLA_SLOW_ROWS = 16


def _gla_step(c, q_ref, k_ref, v_ref, g_ref, gr_ref, s0_ref, tri_ref, m_ref, bd_ref, gn_ref,
              og_ref, sfin_ref, st_ref, o_ref, *, t, n_sub, companion=None):
    @pl.when(c == 0)
    def _():
        st_ref[...] = s0_ref[...]

    n_rows = t * n_sub
    blocked_is_safe = (SUB - 1) * jnp.max(-g_ref[...]) <= GLA_SAFE_EXP

    def emit(o):
        gr = gr_ref[...].astype(F32)
        gate = gr / (1.0 + jnp.exp(-gr))
        outs = []
        for h in range(N_HEADS_GLA):
            sl = slice(h * DV, (h + 1) * DV)
            outs.append(_rms(o[:, sl]) * gn_ref[...] * gate[:, sl])
        og_ref[...] = jnp.concatenate(outs, axis=1).astype(BF16)

    @pl.when(blocked_is_safe)
    def _():
        if companion is not None:
            companion()
        emit(_gla_blocked(q_ref, k_ref, v_ref, g_ref, tri_ref, m_ref, bd_ref, st_ref, t=t, n_sub=n_sub))

    @pl.when(jnp.logical_not(blocked_is_safe))
    def _():
        if companion is not None:
            companion()
        _gla_tokenwise(q_ref, k_ref, v_ref, g_ref, bd_ref, st_ref, o_ref, n_rows=n_rows)
        emit(o_ref[...])

    @pl.when(c == pl.num_programs(1) - 1)
    def _():
        sfin_ref[...] = st_ref[...]


def _gla_kernel(*refs, t, n_sub):
    _gla_step(pl.program_id(1), *refs, t=t, n_sub=n_sub)


def _front_prompt_kernel(*refs, t, n_sub):
    x_ref = refs[0]
    consts = refs[1:1 + N_PROJ_CONSTS]
    hm_ref, sink_ref, s0_ref, tri_ref, m_ref, bd_ref, gn_ref = refs[1 + N_PROJ_CONSTS:8 + N_PROJ_CONSTS]
    sk_out, sv_out, o_swa_ref, og_ref, sfin_ref = refs[8 + N_PROJ_CONSTS:13 + N_PROJ_CONSTS]
    sq_s, gq_s, gk_s, gv_s, gg_s, gr_s, sk_s, sv_s, st_ref, o_ref = refs[13 + N_PROJ_CONSTS:]
    c = pl.program_id(1)

    @pl.when(c == 0)
    def _():
        for ring in (sq_s, gq_s, gk_s, gv_s, gg_s, gr_s, sk_s, sv_s):
            ring[...] = jnp.zeros_like(ring)

    new2, old2 = lax.rem(c, 2), lax.rem(c + 1, 2)
    new3, old3, older3 = lax.rem(c, 3), lax.rem(c + 2, 3), lax.rem(c + 1, 3)
    half = pl.ds(TM // 2, TM // 2)

    def companion():
        _proj_tile(x_ref[...], *consts, sq_s.at[new2], sk_s.at[new3], sv_s.at[new3], gq_s.at[new2],
                   gk_s.at[new2], gv_s.at[new2], gg_s.at[new2], gr_s.at[new2])
        sk_out[...] = sk_s[new3]
        sv_out[...] = sv_s[new3]
        _swa_prompt_tile(c - 1, sq_s.at[old2], sk_s.at[older3, half], sk_s.at[old3], sv_s.at[older3, half],
                         sv_s.at[old3], hm_ref, sink_ref, o_swa_ref)

    _gla_step(c, gq_s.at[old2], gk_s.at[old2], gv_s.at[old2], gg_s.at[old2], gr_s.at[old2], s0_ref, tri_ref,
              m_ref, bd_ref, gn_ref, og_ref, sfin_ref, st_ref, o_ref, t=t, n_sub=n_sub, companion=companion)


def _gla_tokenwise(q_ref, k_ref, v_ref, g_ref, bd_ref, st_ref, o_ref, *, n_rows):
    row = lax.broadcasted_iota(I32, (GLA_SLOW_ROWS, 1), 0)

    def group(gi, carry):
        rows = pl.ds(pl.multiple_of(gi * GLA_SLOW_ROWS, GLA_SLOW_ROWS), GLA_SLOW_ROWS)
        q = q_ref[rows, :].astype(F32)
        k = k_ref[rows, :].astype(F32)
        v = v_ref[rows, :].astype(F32)
        decay = jnp.exp(g_ref[rows, :])
        o = jnp.zeros((GLA_SLOW_ROWS, N_HEADS_GLA * DV), F32)
        for j in range(GLA_SLOW_ROWS):
            only_j = row == j
            k_j = jnp.where(only_j, k, 0.0).astype(BF16)
            v_j = jnp.where(only_j, v, 0.0).astype(BF16)
            q_j = jnp.where(only_j, q, 0.0).astype(BF16)
            st = st_ref[...] * decay[j:j + 1, :] + _dot_tn(v_j, k_j) * bd_ref[...]
            st_ref[...] = st
            o = o + _dot_nt(q_j, st.astype(BF16))
        o_ref[rows, :] = o
        return carry

    lax.fori_loop(0, n_rows // GLA_SLOW_ROWS, group, 0)


def _gla_blocked(q_ref, k_ref, v_ref, g_ref, tri_ref, m_ref, bd_ref, st_ref, *, t, n_sub):
    nb = t // SUB
    n_rows = t * n_sub
    lanes = N_HEADS_GLA * DK

    def group_row(x, period, offset):
        g = x.reshape(n_rows // period, period, lanes)[:, offset:offset + 1, :]
        return jnp.broadcast_to(g, (n_rows // period, period, lanes)).reshape(n_rows, lanes)

    g_hi, g_lo = _split_bf16(g_ref[...])
    b = _dot(tri_ref[...], g_hi) + _dot(tri_ref[...], g_lo)
    q = q_ref[...].astype(F32)
    k = k_ref[...].astype(F32)
    qd = (q * jnp.exp(b - group_row(b, SUB, 0))).astype(BF16)
    pos = lax.broadcasted_iota(I32, (n_rows, lanes), 0) & (t - 1)
    k_parts = []
    for blk in range(nb):
        arg = jnp.where(pos < SUB * (blk + 1), group_row(b, t, SUB * blk) - b, NEG_BIG)
        k_parts.append((k * jnp.exp(arg)).astype(BF16))
    head_rows = jnp.concatenate([m_ref[h * t:h * t + SUB, 0:lanes] for h in range(N_HEADS_GLA)], axis=0)
    q_dec = (q * jnp.exp(b)).astype(BF16)
    k_last = (k * jnp.exp(group_row(b, t, t - 1) - b)).astype(BF16)
    row_a = lax.broadcasted_iota(I32, (N_HEADS_GLA * t, t), 0) & (t - 1)
    col_a = lax.broadcasted_iota(I32, (N_HEADS_GLA * t, t), 1)

    o_intra, q_decayed, state_add, state_decay = [], [], [], []
    for u in range(n_sub):
        rows = slice(u * t, (u + 1) * t)
        v = v_ref[rows, :]
        pieces = []
        for blk in range(nb):
            q_blk = qd[u * t + blk * SUB:u * t + (blk + 1) * SUB]
            pieces.append(_dot_nt(jnp.concatenate([q_blk] * N_HEADS_GLA, axis=0) * head_rows, k_parts[blk][rows]))
        a = jnp.concatenate([pieces[blk][h * SUB:(h + 1) * SUB] for h in range(N_HEADS_GLA) for blk in range(nb)],
                            axis=0)
        a = jnp.where(row_a >= col_a, a, 0.0).astype(BF16)
        o_full = _dot(a, v)
        o_intra.append(jnp.concatenate(
            [o_full[h * t:(h + 1) * t, h * DV:(h + 1) * DV] for h in range(N_HEADS_GLA)], axis=1))
        q_decayed.append(q_dec[rows])
        state_add.append(_dot_tn(v, k_last[rows]) * bd_ref[...])
        state_decay.append(jnp.exp(b[(u + 1) * t - 1:(u + 1) * t, :]))

    st = st_ref[...]
    o_parts = []
    for u in range(n_sub):
        o_parts.append(o_intra[u] + _dot_nt(q_decayed[u], st.astype(BF16)))
        st = st * state_decay[u] + state_add[u]
    st_ref[...] = st
    return jnp.concatenate(o_parts, axis=0) if n_sub > 1 else o_parts[0]


def _gla(gq, gk, gv, gg, gr, s0, tri, mask, bd, gn, *, t, n_sub, n_batch, n_steps, first_block, name):
    rows_per_step = t * n_sub

    def rows(width):
        return pl.BlockSpec((rows_per_step, width), lambda b, c: (first_block + b * n_steps + c, 0))

    def full(a):
        return pl.BlockSpec(a.shape, lambda b, c: (0,) * a.ndim)

    state = pl.BlockSpec((None,) + s0.shape[1:], lambda b, c: (b, 0, 0))
    return pl.pallas_call(
        functools.partial(_gla_kernel, t=t, n_sub=n_sub),
        grid=(n_batch, n_steps),
        in_specs=[rows(W_GQ), rows(W_GK), rows(W_GV), rows(W_GQ), rows(W_GR), state,
                  full(tri), full(mask), full(bd), full(gn)],
        out_specs=[pl.BlockSpec((rows_per_step, W_GV), lambda b, c: (b * n_steps + c, 0)), state],
        out_shape=[jax.ShapeDtypeStruct((n_batch * n_steps * rows_per_step, W_GV), BF16),
                   jax.ShapeDtypeStruct(s0.shape, F32)],
        scratch_shapes=[pltpu.VMEM(s0.shape[1:], F32), pltpu.VMEM((rows_per_step, W_GV), F32)],
        compiler_params=_cparams(2, "mixer"),
        name=name,
    )(gq, gk, gv, gg, gr, s0, tri, mask, bd, gn)


def _front_prompt(xp, consts, head_mask, sink_b, s0, tri, mask, bd, gn):
    def full(a):
        return pl.BlockSpec(a.shape, lambda b, c: (0,) * a.ndim)

    def computed(width):
        return pl.BlockSpec((TM, width), lambda b, c: (jnp.minimum(c, N_PROMPT_TILES - 1), 0))

    def mixed(width):
        return pl.BlockSpec((TM, width), lambda b, c: (jnp.maximum(c - 1, 0), 0))

    state = pl.BlockSpec((None,) + s0.shape[1:], lambda b, c: (b, 0, 0))
    ring2 = [pltpu.VMEM((2, TM, w), dt) for w, dt in
             (PROJ_OUTPUTS[0], PROJ_OUTPUTS[3], PROJ_OUTPUTS[4], PROJ_OUTPUTS[5], PROJ_OUTPUTS[6], PROJ_OUTPUTS[7])]
    ring3 = [pltpu.VMEM((3, TM, w), dt) for w, dt in (PROJ_OUTPUTS[1], PROJ_OUTPUTS[2])]
    return pl.pallas_call(
        functools.partial(_front_prompt_kernel, t=CHUNK, n_sub=TM // CHUNK),
        grid=(1, N_PROMPT_TILES + 1),
        in_specs=[computed(D_MODEL)] + [full(a) for a in consts] + [full(head_mask), full(sink_b), state,
                                                                   full(tri), full(mask), full(bd), full(gn)],
        out_specs=[computed(W_SK), computed(W_SV), mixed(W_SQ), mixed(W_GV), state],
        out_shape=[jax.ShapeDtypeStruct((N_PROMPT, W_SK), F32), jax.ShapeDtypeStruct((N_PROMPT, W_SV), F32),
                   jax.ShapeDtypeStruct((N_PROMPT, W_SQ), BF16), jax.ShapeDtypeStruct((N_PROMPT, W_GV), BF16),
                   jax.ShapeDtypeStruct(s0.shape, F32)],
        scratch_shapes=ring2 + ring3 + [pltpu.VMEM(s0.shape[1:], F32), pltpu.VMEM((TM, W_GV), F32)],
        compiler_params=_cparams(2, "front"),
        name="front_prompt",
    )(xp, *consts, head_mask, sink_b, s0, tri, mask, bd, gn)


def _tile_row_copies(hbm_ref, tile, vmem_ref, sem, to_hbm, rows=TM):
    copies = []
    for a in range(ROW_SUB):
        h = hbm_ref.at[pl.ds(tile * rows, rows), a, :]
        v = vmem_ref.at[:, pl.ds(a * LANES, LANES)]
        copies.append(pltpu.make_async_copy(v, h, sem) if to_hbm else pltpu.make_async_copy(h, v, sem))
    return copies


def _store_tile_rows(i, n_steps, outputs, row_buf, row_sem, rows=TM):
    buf_slot = lax.rem(i, 2)

    def store(j, tile, s):
        return _tile_row_copies(outputs[j][0], tile, row_buf.at[j, s], row_sem.at[j, s], True, rows)

    for j, (_, value) in enumerate(outputs):
        @pl.when(i >= 2)
        def _():
            for c in store(j, i - 2, buf_slot):
                c.wait()

        row_buf[j, buf_slot] = value
        for c in store(j, i, buf_slot):
            c.start()

        @pl.when(i == n_steps - 1)
        def _():
            for c in store(j, i - 1, 1 - buf_slot) + store(j, i, buf_slot):
                c.wait()


def _merge_kernel(oswp_ref, osws_ref, ogp_ref, ogs_ref, xp_ref, xs_ref, wo1_ref, wo2_ref, gf_ref, wrh_ref, wrl_ref,
                  br_ref, xrow_ref, hrow_ref, lg_ref, row_buf, row_sem):
    i = pl.program_id(0)
    is_prompt = i < N_PROMPT_TILES
    x = jnp.where(is_prompt, xp_ref[...], xs_ref[...])
    o_swa = jnp.where(is_prompt, oswp_ref[...], osws_ref[...])
    og = jnp.where(is_prompt, ogp_ref[...], ogs_ref[...])
    h = x + (_dot(o_swa, wo1_ref[...]) + _dot(og, wo2_ref[...]))
    xn = _rms(h) * gf_ref[...]
    x_hi, x_lo = _split_bf16(xn)
    logits = _dot(x_hi, wrh_ref[...]) + _dot(x_lo, wrh_ref[...]) + _dot(x_hi, wrl_ref[...]) + br_ref[...]
    lg_ref[...] = logits.T[:N_EXPERTS]
    _store_tile_rows(i, N_TILES, ((xrow_ref, xn), (hrow_ref, h)), row_buf, row_sem)


def _merge(o_swa_p, o_swa_s, og_p, og_s, xp, xs, wo1, wo2, gf, wrh, wrl, br):
    def prompt_rows(width):
        return pl.BlockSpec((TM, width), lambda i: (jnp.minimum(i, N_PROMPT_TILES - 1), 0))

    def sample_rows(width):
        return pl.BlockSpec((TM, width), lambda i: (0, 0))

    def full(a):
        return pl.BlockSpec(a.shape, lambda i: (0,) * a.ndim)

    consts = (wo1, wo2, gf, wrh, wrl, br)
    return pl.pallas_call(
        _merge_kernel,
        grid=(N_TILES,),
        in_specs=[prompt_rows(W_SQ), sample_rows(W_SQ), prompt_rows(W_GV), sample_rows(W_GV),
                  prompt_rows(D_MODEL), sample_rows(D_MODEL)] + [full(a) for a in consts],
        out_specs=[pl.BlockSpec(memory_space=pl.ANY)] * 2 + [pl.BlockSpec((N_EXPERTS, TM), lambda i: (0, i))],
        out_shape=[jax.ShapeDtypeStruct((N_ROWS,) + ROW_TILE, F32)] * 2 + [
                   jax.ShapeDtypeStruct((N_EXPERTS, N_ROWS), F32)],
        scratch_shapes=[pltpu.VMEM((2, 2, TM, D_MODEL), F32), pltpu.SemaphoreType.DMA((2, 2))],
        compiler_params=_cparams(1, "merge"),
        name="merge",
    )(o_swa_p, o_swa_s, og_p, og_s, xp, xs, *consts)


RT = 1280
N_ROUTER_STEPS = N_ROWS // RT


def _router_kernel(lg_ref, tri_ref, wrow_ref, ti_ref, rk_ref, cnt_ref, base_ref, row_buf, row_sem):
    i = pl.program_id(0)

    @pl.when(i == 0)
    def _():
        base_ref[...] = jnp.zeros_like(base_ref)

    logits_t = lg_ref[...]
    expert = lax.broadcasted_iota(I32, logits_t.shape, 0)
    slot = lax.broadcasted_iota(I32, (SLOT_ROWS, RT), 0)
    vals, hots = [], []
    ti = jnp.zeros((SLOT_ROWS, RT), I32)
    for kk in range(TOP_K):
        m = jnp.max(logits_t, axis=0, keepdims=True)
        idx = jnp.min(jnp.where(logits_t == m, expert, N_EXPERTS), axis=0, keepdims=True)
        hot = expert == idx
        logits_t = jnp.where(hot, NEG_BIG, logits_t)
        vals.append(m)
        hots.append(hot)
        ti = jnp.where(slot == kk, idx, ti)
    ti_ref[...] = ti
    exps = [jnp.exp(v - vals[0]) for v in vals]
    den = exps[0] + exps[1] + exps[2] + exps[3]
    tw_t = jnp.zeros((SLOT_ROWS, RT), F32)
    for kk in range(TOP_K):
        tw_t = jnp.where(slot == kk, exps[kk] / den, tw_t)
    eye = (lax.broadcasted_iota(I32, (SLOT_ROWS, LANES), 0)
           == lax.broadcasted_iota(I32, (SLOT_ROWS, LANES), 1)).astype(F32).astype(BF16)
    w_hi = tw_t.astype(BF16)
    w_mid, w_lo = _split_bf16(tw_t - w_hi.astype(F32))
    tw_col = _dot_tn(w_hi, eye) + _dot_tn(w_mid, eye) + _dot_tn(w_lo, eye)

    onehot_t = jnp.zeros(logits_t.shape, F32)
    for hot in hots:
        onehot_t = onehot_t + jnp.where(hot, 1.0, 0.0)
    before_t = _dot(onehot_t.astype(BF16), tri_ref[...]) + base_ref[:, 0:1]
    rk = jnp.zeros((SLOT_ROWS, RT), I32)
    for kk in range(TOP_K):
        r = jnp.sum(jnp.where(hots[kk], before_t, 0.0), axis=0, keepdims=True).astype(I32)
        rk = jnp.where(slot == kk, r, rk)
    rk_ref[...] = rk
    total = base_ref[...] + jnp.sum(onehot_t, axis=1, keepdims=True)
    base_ref[...] = total
    cnt_ref[...] = total.astype(I32)

    w_lanes = [jnp.broadcast_to(tw_col[:, kk:kk + 1], (RT, LANES)) for kk in range(TOP_K)]
    w_rows = jnp.concatenate(w_lanes + [jnp.zeros((RT, D_MODEL - TOP_K * LANES), F32)], axis=1)
    _store_tile_rows(i, N_ROUTER_STEPS, ((wrow_ref, w_rows),), row_buf, row_sem, RT)


def _router(logits, tri):
    return pl.pallas_call(
        _router_kernel,
        grid=(N_ROUTER_STEPS,),
        in_specs=[pl.BlockSpec((N_EXPERTS, RT), lambda i: (0, i)), pl.BlockSpec(tri.shape, lambda i: (0, 0))],
        out_specs=[pl.BlockSpec(memory_space=pl.ANY),
                   pl.BlockSpec((SLOT_ROWS, RT), lambda i: (0, i)),
                   pl.BlockSpec((SLOT_ROWS, RT), lambda i: (0, i)),
                   pl.BlockSpec((N_EXPERTS, LANES), lambda i: (0, 0))],
        out_shape=[jax.ShapeDtypeStruct((N_ROWS,) + ROW_TILE, F32),
                   jax.ShapeDtypeStruct((SLOT_ROWS, N_ROWS), I32),
                   jax.ShapeDtypeStruct((SLOT_ROWS, N_ROWS), I32),
                   jax.ShapeDtypeStruct((N_EXPERTS, LANES), I32)],
        scratch_shapes=[pltpu.VMEM((N_EXPERTS, LANES), F32), pltpu.VMEM((1, 2, RT, D_MODEL), F32),
                        pltpu.SemaphoreType.DMA((1, 2))],
        compiler_params=_cparams(1, "router"),
        name="router",
    )(logits, tri)


ISSUE_UNROLL = 4


def _row_copy(src_ref, src_row, dst_ref, dst_row, sem):
    return pltpu.make_async_copy(src_ref.at[pl.ds(src_row, 1)], dst_ref.at[pl.ds(dst_row, 1)], sem)


def _dispatch_kernel(dest_ref, end_ref, x_ref, xs_ref, zero_ref, sem, zsem):
    i = pl.program_id(0)
    base = i * TM

    @pl.when(i == 0)
    def _():
        zero_ref[...] = jnp.zeros_like(zero_ref)

        def tail_copy(e):
            last = jnp.maximum(end_ref[e] - TM, 0)
            return pltpu.make_async_copy(zero_ref, xs_ref.at[pl.ds(pl.multiple_of(last, TM), TM)], zsem)

        def fill(e, carry):
            tail_copy(e).start()
            return carry

        def fill_wait(e, carry):
            tail_copy(e).wait()
            return carry

        lax.fori_loop(0, N_EXPERTS, fill, 0)
        lax.fori_loop(0, N_EXPERTS, fill_wait, 0)

        def unused_copy(t):
            return pltpu.make_async_copy(zero_ref, xs_ref.at[pl.ds(pl.multiple_of(t * TM, TM), TM)], zsem)

        def fill_unused(t, carry):
            unused_copy(t).start()
            unused_copy(t).wait()
            return carry

        lax.fori_loop(end_ref[N_EXPERTS - 1] // TM, N_EXPERT_TILES, fill_unused, 0)

    def issue(n, carry):
        for kk in range(TOP_K):
            _row_copy(x_ref, n, xs_ref, dest_ref[kk * N_ROWS + base + n], sem).start(priority=kk % 2)
        return carry

    lax.fori_loop(0, TM, issue, 0, unroll=ISSUE_UNROLL)

    for kk in range(TOP_K):
        pltpu.make_async_copy(x_ref, xs_ref.at[pl.ds(0, TM)], sem).wait()


def _dispatch(dest, end, x_packed):
    return pl.pallas_call(
        _dispatch_kernel,
        grid_spec=pltpu.PrefetchScalarGridSpec(
            num_scalar_prefetch=2,
            grid=(N_TILES,),
            in_specs=[pl.BlockSpec((TM,) + ROW_TILE, lambda i, d, e: (i, 0, 0))],
            out_specs=pl.BlockSpec(memory_space=pl.ANY),
            scratch_shapes=[pltpu.VMEM((TM,) + ROW_TILE, F32), pltpu.SemaphoreType.DMA,
                            pltpu.SemaphoreType.DMA],
        ),
        out_shape=jax.ShapeDtypeStruct((N_SORTED_ROWS,) + ROW_TILE, F32),
        compiler_params=_cparams(1, "dispatch"),
        name="dispatch",
    )(dest, end, x_packed)


CAST_ROWS = 128


def _moe_kernel(te_ref, nu_ref, nx_ref, xs_ref, wg_ref, bg_ref, wu_ref, bu_ref, wd_ref, bd_ref, ys_ref,
                w_stage, w_bf, x_buf, y_buf, zero_buf, w_sem, in_sem, out_sem, zero_sem):
    t = pl.program_id(0)
    n_used = nu_ref[0]
    slot = lax.rem(t, 2)
    e = te_ref[t]
    e_prev = te_ref[jnp.maximum(t - 1, 0)]

    def load(tile, s):
        return _tile_row_copies(xs_ref, tile, x_buf.at[s], in_sem.at[s], to_hbm=False)

    def store(tile, s):
        return _tile_row_copies(ys_ref, tile, y_buf.at[s], out_sem.at[s], to_hbm=True)

    def weight_copies(expert):
        return [pltpu.make_async_copy(w.at[expert], w_stage.at[j], w_sem.at[j])
                for j, w in enumerate((wg_ref, wu_ref, wd_ref))]

    @pl.when(t == 0)
    def _():
        for c in weight_copies(e) + load(0, 0):
            c.start()

    @pl.when(t + 1 < n_used)
    def _():
        for c in load(t + 1, 1 - slot):
            c.start()

    @pl.when((t == 0) | (e != e_prev))
    def _():
        for c in weight_copies(e):
            c.wait()

        def cast(r, carry):
            sl = pl.ds(pl.multiple_of(r * CAST_ROWS, CAST_ROWS), CAST_ROWS)
            for j in range(3):
                w_bf[j, sl, :] = w_stage[j, sl, :].astype(BF16)
            return carry

        lax.fori_loop(0, D_MODEL // CAST_ROWS, cast, 0)
        e_next = nx_ref[e]

        @pl.when(e_next >= 0)
        def _():
            for c in weight_copies(e_next):
                c.start()

    @pl.when(t < n_used)
    def _():
        for c in load(t, slot):
            c.wait()

        @pl.when(t >= 2)
        def _():
            for c in store(t - 2, slot):
                c.wait()

        x = x_buf[slot].astype(BF16)
        gate = jnp.minimum(_dot(x, w_bf[0]) + bg_ref[...], SWIGLU_LIMIT)
        up = jnp.clip(_dot(x, w_bf[1]) + bu_ref[...], -SWIGLU_LIMIT, SWIGLU_LIMIT)
        hdn = (up + 1.0) * gate * (1.0 / (1.0 + jnp.exp(-SWIGLU_ALPHA * gate)))
        y_buf[slot] = _dot(hdn.astype(BF16), w_bf[2]) + bd_ref[...]
        for c in store(t, slot):
            c.start()

    @pl.when(t >= n_used)
    def _():
        zero_buf[...] = jnp.zeros_like(zero_buf)
        fill = pltpu.make_async_copy(zero_buf, ys_ref.at[pl.ds(t * TM, TM)], zero_sem)
        fill.start()
        fill.wait()

    @pl.when(t == N_EXPERT_TILES - 1)
    def _():
        @pl.when(n_used >= 2)
        def _():
            for c in store(n_used - 2, lax.rem(n_used, 2)):
                c.wait()

        for c in store(n_used - 1, lax.rem(n_used - 1, 2)):
            c.wait()


def _moe(tile_expert, n_used, next_expert, xs, w_gate, b_gate, w_up, b_up, w_down, b_down):
    hbm = pl.BlockSpec(memory_space=pl.ANY)
    bias = pl.BlockSpec((None, 1, D_MODEL), lambda t, te, nu, nx: (te[t], 0, 0))
    return pl.pallas_call(
        _moe_kernel,
        grid_spec=pltpu.PrefetchScalarGridSpec(
            num_scalar_prefetch=3,
            grid=(N_EXPERT_TILES,),
            in_specs=[hbm, hbm, bias, hbm, bias, hbm, bias],
            out_specs=hbm,
            scratch_shapes=[pltpu.VMEM((3, D_MODEL, D_MODEL), F32), pltpu.VMEM((3, D_MODEL, D_MODEL), BF16),
                            pltpu.VMEM((2, TM, D_MODEL), F32), pltpu.VMEM((2, TM, D_MODEL), F32),
                            pltpu.VMEM((TM,) + ROW_TILE, F32), pltpu.SemaphoreType.DMA((3,)),
                            pltpu.SemaphoreType.DMA((2,)), pltpu.SemaphoreType.DMA((2,)),
                            pltpu.SemaphoreType.DMA],
        ),
        out_shape=jax.ShapeDtypeStruct((N_SORTED_ROWS,) + ROW_TILE, F32),
        compiler_params=_cparams(1, "experts"),
        name="experts",
    )(tile_expert, n_used, next_expert, xs, w_gate, b_gate, w_up, b_up, w_down, b_down)


def _combine_kernel(dest_ref, h_ref, w_ref, ys_ref, yp_ref, ysm_ref, g_ref, out_ref, sem):
    i = pl.program_id(0)
    slot = lax.rem(i, 2)

    def issue_tile(tile, s):
        base = tile * TM

        def issue(n, carry):
            for kk in range(TOP_K):
                _row_copy(ys_ref, dest_ref[kk * N_ROWS + base + n], g_ref.at[s], kk * TM + n,
                          sem.at[s]).start(priority=kk % 2)
            return carry

        lax.fori_loop(0, TM, issue, 0, unroll=ISSUE_UNROLL)

    @pl.when(i == 0)
    def _():
        issue_tile(0, 0)

    @pl.when(i + 1 < N_TILES)
    def _():
        issue_tile(i + 1, 1 - slot)

    for kk in range(TOP_K):
        pltpu.make_async_copy(ys_ref.at[pl.ds(0, TM)], g_ref.at[slot, pl.ds(kk * TM, TM)], sem.at[slot]).wait()

    w = w_ref[...]
    acc = None
    for kk in range(TOP_K):
        part = jnp.broadcast_to(w[:, kk:kk + 1, :], (TM,) + ROW_TILE) * g_ref[slot, kk * TM:(kk + 1) * TM]
        acc = part if acc is None else acc + part
    out_ref[...] = h_ref[...] + acc

    def write(y_ref):
        for a in range(ROW_SUB):
            y_ref[:, a * LANES:(a + 1) * LANES] = out_ref[:, a, :]

    @pl.when(i < N_PROMPT_TILES)
    def _():
        write(yp_ref)

    @pl.when(i >= N_PROMPT_TILES)
    def _():
        write(ysm_ref)


def _combine(dest, h_rows, w_rows, ys):
    tile = lambda index: pl.BlockSpec((TM,) + ROW_TILE, index)
    return pl.pallas_call(
        _combine_kernel,
        grid_spec=pltpu.PrefetchScalarGridSpec(
            num_scalar_prefetch=1,
            grid=(N_TILES,),
            in_specs=[tile(lambda i, d: (i, 0, 0)), tile(lambda i, d: (i, 0, 0)),
                      pl.BlockSpec(memory_space=pl.ANY)],
            out_specs=[pl.BlockSpec((TM, D_MODEL), lambda i, d: (jnp.minimum(i, N_PROMPT_TILES - 1), 0)),
                       pl.BlockSpec((TM, D_MODEL), lambda i, d: (0, 0))],
            scratch_shapes=[pltpu.VMEM((2, TOP_K * TM) + ROW_TILE, F32), pltpu.VMEM((TM,) + ROW_TILE, F32),
                            pltpu.SemaphoreType.DMA((2,))],
        ),
        out_shape=[jax.ShapeDtypeStruct((N_PROMPT, D_MODEL), F32),
                   jax.ShapeDtypeStruct((N_SAMPLE, D_MODEL), F32)],
        compiler_params=_cparams(1, "combine"),
        name="combine",
    )(dest, h_rows, w_rows, ys)


def _block_diag_ones(n, blk):
    idx = np.arange(n) // blk
    return (idx[:, None] == idx[None, :]).astype(np.float32)


def _swa_head_mask(tq):
    row_head = np.arange(4 * tq)[:, None] // tq
    lane_head = np.arange(4 * HD)[None, :] // HD
    return jnp.asarray((row_head == lane_head).astype(np.float32), BF16)


def _gla_masks(t, n_sub):
    nb = t // SUB
    lanes = N_HEADS_GLA * DK
    tri = jnp.asarray(np.kron(np.eye(n_sub, dtype=np.float32), np.tril(np.ones((t, t), np.float32))), BF16)
    row = np.arange(N_HEADS_GLA * t)
    col = np.arange(nb * lanes)
    same_head = (row[:, None] // t) == ((col[None, :] % lanes) // DK)
    same_blk = ((row[:, None] % t) // SUB) == (col[None, :] // lanes)
    mask = jnp.asarray((same_head & same_blk).astype(np.float32), BF16)
    return tri, mask


def _sink_rows(sinks, tq):
    s = jnp.repeat(sinks.astype(F32).reshape(2, 4), tq, axis=1)
    return jnp.broadcast_to(s[:, None, :], (2, 8, 4 * tq))


def kernel(x_prompt, x_sample, state_gla, cache_swa_k, cache_swa_v, norm_mix_g, w_in, w_gla_a2, b_gla_a, q_norm_g,
           k_norm_g, swa_sinks, gla_norm_g, w_out, norm_ffn_g, w_router, b_router, w_gate, b_gate, w_up, b_up,
           w_down, b_down):
    xp = x_prompt.reshape(N_PROMPT, D_MODEL)
    xs = x_sample.reshape(N_SAMPLE, D_MODEL)

    w_in0 = w_in[0]
    w_main = w_in0[:, :W_MAIN].astype(BF16)
    w_ga = jnp.pad(w_in0[:, OFF_GA:], ((0, 0), (0, LANES - W_GA))).astype(BF16)
    w_a2 = jnp.pad(w_gla_a2[0], ((0, LANES - W_GA), (0, 0))).astype(BF16)
    b_a = b_gla_a[0].reshape(1, -1)
    qg = (jnp.tile(q_norm_g[0], N_HEADS_SWA) * (HD ** -0.5)).reshape(1, -1)
    kg = jnp.tile(k_norm_g[0], 2).reshape(1, -1)
    bdq = jnp.asarray(_block_diag_ones(W_SQ // 2, HD), BF16)
    bdk = jnp.asarray(_block_diag_ones(W_SK, HD), BF16)

    proj_consts = (norm_mix_g[0].reshape(1, -1), w_main, w_ga, w_a2, b_a, qg, kg, bdq, bdk)
    sq, sk, sv, gq, gk, gv, gg, gr = _proj_sample(xs, proj_consts)

    cache_k = cache_swa_k[0].reshape(N_STREAMS, WINDOW, 2 * HD)
    cache_v = cache_swa_v[0].reshape(N_STREAMS, WINDOW, 2 * HD)
    o_swa_s = _swa_sample(sq, sk, sv, cache_k, cache_v, _swa_head_mask(T_SAMPLE),
                          _sink_rows(swa_sinks[0], T_SAMPLE))

    bd_state = jnp.asarray(_block_diag_ones(N_HEADS_GLA, 1).repeat(DV, axis=0).repeat(DK, axis=1), F32)
    gn = gla_norm_g[0].reshape(1, -1)
    tri_p, mask_p = _gla_masks(CHUNK, TM // CHUNK)
    tri_s, mask_s = _gla_masks(T_SAMPLE, 1)
    s0_p = jnp.zeros((1, N_HEADS_GLA * DV, N_HEADS_GLA * DK), F32)
    sk_p, sv_p, o_swa_p, og_p, sfin_p = _front_prompt(xp, proj_consts, _swa_head_mask(CHUNK),
                                                      _sink_rows(swa_sinks[0], CHUNK), s0_p, tri_p, mask_p,
                                                      bd_state, gn)
    eye = jnp.eye(N_HEADS_GLA, dtype=F32)
    s0_s = jnp.einsum('bhde,hg->bhegd', state_gla[0].astype(F32), eye).reshape(
        N_STREAMS, N_HEADS_GLA * DV, N_HEADS_GLA * DK)
    og_s, sfin_s = _gla(gq, gk, gv, gg, gr, s0_s, tri_s, mask_s, bd_state, gn, t=T_SAMPLE, n_sub=1,
                        n_batch=N_STREAMS, n_steps=1, first_block=0, name="gla_sample")

    def unpack_state(sfin):
        s = sfin.reshape(-1, N_HEADS_GLA, DV, N_HEADS_GLA, DK)
        s = jnp.stack([s[:, h, :, h, :] for h in range(N_HEADS_GLA)], axis=1)
        return jnp.transpose(s, (0, 1, 3, 2))[None]

    w_out0 = w_out[0].astype(BF16)
    wr = jnp.pad(w_router[0], ((0, 0), (0, LANES - N_EXPERTS)))
    wr_hi = wr.astype(BF16)
    wr_lo = (wr - wr_hi.astype(F32)).astype(BF16)
    br = jnp.pad(b_router[0].astype(F32), (0, LANES - N_EXPERTS)).reshape(1, -1)
    x_rows, h_rows, logits = _merge(o_swa_p, o_swa_s, og_p, og_s, xp, xs, w_out0[:W_SQ], w_out0[W_SQ:],
                                    norm_ffn_g[0].reshape(1, -1), wr_hi, wr_lo, br)
    earlier = jnp.asarray(np.triu(np.ones((RT, RT), np.float32), 1), BF16)
    w_rows, top_i, rank, counts = _router(logits, earlier)

    counts = counts[:, 0]
    padded = (counts + TM - 1) // TM * TM
    end = jnp.cumsum(padded)
    start = end - padded
    experts = jnp.arange(N_EXPERTS, dtype=I32)
    is_e = top_i[:TOP_K, :, None] == experts
    dest = (rank[:TOP_K] + jnp.sum(jnp.where(is_e, start, 0), axis=-1)).reshape(-1).astype(I32)
    n_used = (end[-1] // TM).astype(I32)
    tiles = jnp.minimum(jnp.arange(N_EXPERT_TILES, dtype=I32), n_used - 1)
    tile_expert = jnp.sum((tiles[:, None] * TM >= end[None, :]).astype(I32), axis=1)
    later_nonempty = (experts[None, :] > experts[:, None]) & (padded[None, :] > 0)
    next_expert = jnp.min(jnp.where(later_nonempty, experts[None, :], N_EXPERTS), axis=1)
    next_expert = jnp.where(next_expert < N_EXPERTS, next_expert, -1).astype(I32)

    xs_sorted = _dispatch(dest, end.astype(I32), x_rows)
    ys = _moe(tile_expert, n_used.reshape(1), next_expert, xs_sorted, w_gate[0], b_gate[0].reshape(N_EXPERTS, 1, -1),
              w_up[0], b_up[0].reshape(N_EXPERTS, 1, -1), w_down[0], b_down[0].reshape(N_EXPERTS, 1, -1))
    y_p, y_s = _combine(dest, h_rows, w_rows, ys)

    sk_s = sk.reshape(N_STREAMS, T_SAMPLE, 2 * HD)
    sv_s = sv.reshape(N_STREAMS, T_SAMPLE, 2 * HD)
    kc_s = jnp.concatenate([cache_k[:, T_SAMPLE:], sk_s], axis=1).reshape(1, N_STREAMS, WINDOW, 2, HD)
    vc_s = jnp.concatenate([cache_v[:, T_SAMPLE:], sv_s], axis=1).reshape(1, N_STREAMS, WINDOW, 2, HD)
    kc_p = sk_p[N_PROMPT - WINDOW:].reshape(1, 1, WINDOW, 2, HD)
    vc_p = sv_p[N_PROMPT - WINDOW:].reshape(1, 1, WINDOW, 2, HD)
    return (y_p.reshape(1, N_PROMPT, D_MODEL), y_s.reshape(N_STREAMS, T_SAMPLE, D_MODEL),
            unpack_state(sfin_p), kc_p, vc_p, unpack_state(sfin_s), kc_s, vc_s)
```

```python
import functools

import numpy as np
import jax
import jax.numpy as jnp
from jax import lax
from jax.experimental import pallas as pl
from jax.experimental.pallas import tpu as pltpu

F32 = jnp.float32
BF16 = jnp.bfloat16
I32 = jnp.int32

D_MODEL = 1024
N_PROMPT = 16384
N_STREAMS = 8
T_SAMPLE = 32
N_SAMPLE = N_STREAMS * T_SAMPLE
N_ROWS = N_PROMPT + N_SAMPLE
EPS = 1e-6

CHUNK = 64
SUB = 16
N_HEADS_SWA = 8
HD = 64
WINDOW = 128
N_HEADS_GLA = 4
DK = 64
DV = 128
GLA_TAU = 16.0
N_EXPERTS = 32
TOP_K = 4
SWIGLU_ALPHA = 1.702
SWIGLU_LIMIT = 7.0

TM = 256
N_TILES = N_ROWS // TM
N_PROMPT_TILES = N_PROMPT // TM
N_ASSIGN = N_ROWS * TOP_K
N_EXPERT_TILES = N_ASSIGN // TM + N_EXPERTS
N_SORTED_ROWS = N_EXPERT_TILES * TM
LANES = 128
SLOT_ROWS = 16
NEG_BIG = -1e30

W_SQ, W_SK, W_SV, W_GQ, W_GK, W_GV, W_GR, W_GA = 512, 128, 128, 256, 256, 512, 512, 16
OFF_SQ = 0
OFF_SK = OFF_SQ + W_SQ
OFF_SV = OFF_SK + W_SK
OFF_GQ = OFF_SV + W_SV
OFF_GK = OFF_GQ + W_GQ
OFF_GV = OFF_GK + W_GK
OFF_GR = OFF_GV + W_GV
OFF_GA = OFF_GR + W_GR
W_MAIN = OFF_GA


VMEM_MIB = dict(proj=40, front=48, mixer=32, merge=32, router=48, dispatch=32, experts=52, combine=40)


def _cparams(n_grid_axes, call):
    return pltpu.CompilerParams(dimension_semantics=("arbitrary",) * n_grid_axes,
                                vmem_limit_bytes=VMEM_MIB[call] * 2 ** 20)


def _dot(a, b):
    return jnp.dot(a, b, preferred_element_type=F32)


def _dot_nt(a, b):
    return lax.dot_general(a, b, (((1,), (1,)), ((), ())), preferred_element_type=F32)


def _dot_tn(a, b):
    return lax.dot_general(a, b, (((0,), (0,)), ((), ())), preferred_element_type=F32)


def _split_bf16(x):
    hi = x.astype(BF16)
    lo = (x - hi.astype(F32)).astype(BF16)
    return hi, lo


def _rms(x):
    return x * lax.rsqrt(jnp.mean(x * x, axis=-1, keepdims=True) + EPS)


ROW_SUB = 8
ROW_TILE = (ROW_SUB, LANES)


PROJ_OUTPUTS = ((W_SQ, BF16), (W_SK, F32), (W_SV, F32), (W_GQ, BF16), (W_GK, BF16), (W_GV, BF16),
                (W_GQ, F32), (W_GR, BF16))
N_PROJ_CONSTS = 9


def _proj_tile(x, g_ref, w_ref, wga_ref, wa2_ref, ba_ref, qg_ref, kg_ref, bdq_ref, bdk_ref,
               sq_ref, sk_ref, sv_ref, gq_ref, gk_ref, gv_ref, gg_ref, gr_ref):
    xb = (_rms(x) * g_ref[...]).astype(BF16)

    def seg(off, width):
        return _dot(xb, w_ref[:, off:off + width])

    def head_norm(u, bd_ref):
        hi, lo = _split_bf16(u * u)
        w = bd_ref.shape[0]
        ss = jnp.concatenate([_dot(hi[:, c:c + w], bd_ref[...]) + _dot(lo[:, c:c + w], bd_ref[...])
                              for c in range(0, u.shape[1], w)], axis=1)
        return u * lax.rsqrt(ss * (1.0 / HD) + EPS)

    sq_ref[...] = (head_norm(seg(OFF_SQ, W_SQ), bdq_ref) * qg_ref[...]).astype(BF16)
    kv = seg(OFF_SK, W_SK + W_SV)
    sk_ref[...] = head_norm(kv[:, :W_SK], bdk_ref) * kg_ref[...]
    sv_ref[...] = kv[:, W_SK:]
    gq_ref[...] = (seg(OFF_GQ, W_GQ) * (DK ** -0.5)).astype(BF16)
    gk_ref[...] = seg(OFF_GK, W_GK).astype(BF16)
    gv_ref[...] = seg(OFF_GV, W_GV).astype(BF16)
    gr_ref[...] = seg(OFF_GR, W_GR).astype(BF16)
    ga = _dot(xb, wga_ref[...]).astype(BF16)
    z = _dot(ga, wa2_ref[...]) + ba_ref[...]
    log_sig = jnp.minimum(z, 0.0) - jnp.log(1.0 + jnp.exp(-jnp.abs(z)))
    gg_ref[...] = log_sig * (1.0 / GLA_TAU)


def _proj_sample_kernel(x_ref, *refs):
    _proj_tile(x_ref[...], *refs)


def _proj_sample(xs, consts):
    def full(a):
        return pl.BlockSpec(a.shape, lambda i: (0,) * a.ndim)

    return pl.pallas_call(
        _proj_sample_kernel,
        grid=(1,),
        in_specs=[full(xs)] + [full(a) for a in consts],
        out_specs=[pl.BlockSpec((N_SAMPLE, w), lambda i: (0, 0)) for w, _ in PROJ_OUTPUTS],
        out_shape=[jax.ShapeDtypeStruct((N_SAMPLE, w), dt) for w, dt in PROJ_OUTPUTS],
        compiler_params=_cparams(1, "proj"),
        name="proj_sample",
    )(xs, *consts)


def _dup_kv_heads(x):
    r = pltpu.roll(x, HD, axis=1)
    lo = lax.broadcasted_iota(I32, x.shape, 1) < HD
    out = []
    for a in (jnp.where(lo, x, r), jnp.where(lo, r, x)):
        out.append(jnp.concatenate([a, a], axis=1).astype(BF16))
    return out


def _swa_blocks(q_blocks, k_blocks, v_blocks, sinks, valids, head_mask):
    tq = q_blocks[0].shape[0]
    scores = []
    for q, k, valid in zip(q_blocks, k_blocks, valids):
        s_t = _dot_nt(k, jnp.concatenate([q] * 4, axis=0) * head_mask)
        scores.append(s_t if valid is None else jnp.where(valid, s_t, -jnp.inf))
    probs = []
    for s_t, sink in zip(scores, sinks):
        m = jnp.maximum(jnp.max(s_t, axis=0, keepdims=True), sink)
        p_t = jnp.exp(s_t - m)
        den = jnp.sum(p_t, axis=0, keepdims=True) + jnp.exp(sink - m)
        probs.append((p_t / den).astype(BF16))
    lane_head = lax.broadcasted_iota(I32, (tq, 4 * HD), 1) // HD
    outs = []
    for p_t, v in zip(probs, v_blocks):
        o_full = _dot_tn(p_t, v)
        o = jnp.zeros((tq, 4 * HD), F32)
        for a in range(4):
            o = o + jnp.where(lane_head == a, o_full[a * tq:(a + 1) * tq], 0.0)
        outs.append(o)
    return outs


def _swa_prompt_tile(i, q_ref, kp_ref, kc_ref, vp_ref, vc_ref, hm_ref, sink_ref, o_ref):
    k_dup = _dup_kv_heads(jnp.concatenate([kp_ref[...], kc_ref[...]], axis=0))
    v_dup = _dup_kv_heads(jnp.concatenate([vp_ref[...], vc_ref[...]], axis=0))
    sink_row = [sink_ref[j][0:1, :] for j in range(2)]
    span = WINDOW + CHUNK
    key = lax.broadcasted_iota(I32, (span, 4 * CHUNK), 0)
    qs, ks, vs, sinks, valids, where = [], [], [], [], [], []
    for c in range(TM // CHUNK):
        lo = CHUNK * c
        valid = (i * TM - WINDOW + lo + key) >= 0
        for j in range(2):
            qs.append(q_ref[lo:lo + CHUNK, 4 * HD * j:4 * HD * (j + 1)])
            ks.append(k_dup[j][lo:lo + span])
            vs.append(v_dup[j][lo:lo + span])
            sinks.append(sink_row[j])
            valids.append(valid)
            where.append((lo, j))
    outs = _swa_blocks(qs, ks, vs, sinks, valids, hm_ref[...])
    for (lo, j), o in zip(where, outs):
        o_ref[lo:lo + CHUNK, 4 * HD * j:4 * HD * (j + 1)] = o.astype(BF16)


def _swa_sample_kernel(q_ref, kc_ref, kn_ref, vc_ref, vn_ref, hm_ref, sink_ref, o_ref):
    k_dup = _dup_kv_heads(jnp.concatenate([kc_ref[...], kn_ref[...]], axis=0))
    v_dup = _dup_kv_heads(jnp.concatenate([vc_ref[...], vn_ref[...]], axis=0))
    sink_row = [sink_ref[j][0:1, :] for j in range(2)]
    qs = [q_ref[:, 4 * HD * j:4 * HD * (j + 1)] for j in range(2)]
    outs = _swa_blocks(qs, k_dup, v_dup, sink_row, [None, None], hm_ref[...])
    o_ref[...] = jnp.concatenate(outs, axis=1).astype(BF16)


def _swa_sample(sq, sk, sv, cache_k, cache_v, head_mask, sink_b):
    new = lambda width: pl.BlockSpec((T_SAMPLE, width), lambda b: (b, 0))
    cache = pl.BlockSpec((None, WINDOW, 2 * HD), lambda b: (b, 0, 0))
    return pl.pallas_call(
        _swa_sample_kernel,
        grid=(N_STREAMS,),
        in_specs=[new(W_SQ), cache, new(2 * HD), cache, new(2 * HD),
                  pl.BlockSpec(head_mask.shape, lambda b: (0, 0)),
                  pl.BlockSpec(sink_b.shape, lambda b: (0, 0, 0))],
        out_specs=pl.BlockSpec((T_SAMPLE, W_SQ), lambda b: (b, 0)),
        out_shape=jax.ShapeDtypeStruct((N_SAMPLE, W_SQ), BF16),
        compiler_params=_cparams(1, "mixer"),
        name="swa_sample",
    )(sq, cache_k, sk, cache_v, sv, head_mask, sink_b)


GLA_SAFE_EXP = 80.0
GLA_SLOW_ROWS = 16


def _gla_step(c, q_ref, k_ref, v_ref, g_ref, gr_ref, s0_ref, tri_ref, m_ref, bd_ref, gn_ref,
              og_ref, sfin_ref, st_ref, o_ref, *, t, n_sub, companion=None):
    @pl.when(c == 0)
    def _():
        st_ref[...] = s0_ref[...]

    n_rows = t * n_sub
    blocked_is_safe = (SUB - 1) * jnp.max(-g_ref[...]) <= GLA_SAFE_EXP

    def emit(o):
        gr = gr_ref[...].astype(F32)
        gate = gr / (1.0 + jnp.exp(-gr))
        outs = []
        for h in range(N_HEADS_GLA):
            sl = slice(h * DV, (h + 1) * DV)
            outs.append(_rms(o[:, sl]) * gn_ref[...] * gate[:, sl])
        og_ref[...] = jnp.concatenate(outs, axis=1).astype(BF16)

    @pl.when(blocked_is_safe)
    def _():
        if companion is not None:
            companion()
        emit(_gla_blocked(q_ref, k_ref, v_ref, g_ref, tri_ref, m_ref, bd_ref, st_ref, t=t, n_sub=n_sub))

    @pl.when(jnp.logical_not(blocked_is_safe))
    def _():
        if companion is not None:
            companion()
        _gla_tokenwise(q_ref, k_ref, v_ref, g_ref, bd_ref, st_ref, o_ref, n_rows=n_rows)
        emit(o_ref[...])

    @pl.when(c == pl.num_programs(1) - 1)
    def _():
        sfin_ref[...] = st_ref[...]


def _gla_kernel(*refs, t, n_sub):
    _gla_step(pl.program_id(1), *refs, t=t, n_sub=n_sub)


def _front_prompt_kernel(*refs, t, n_sub):
    x_ref = refs[0]
    consts = refs[1:1 + N_PROJ_CONSTS]
    hm_ref, sink_ref, s0_ref, tri_ref, m_ref, bd_ref, gn_ref = refs[1 + N_PROJ_CONSTS:8 + N_PROJ_CONSTS]
    sk_out, sv_out, o_swa_ref, og_ref, sfin_ref = refs[8 + N_PROJ_CONSTS:13 + N_PROJ_CONSTS]
    sq_s, gq_s, gk_s, gv_s, gg_s, gr_s, sk_s, sv_s, st_ref, o_ref = refs[13 + N_PROJ_CONSTS:]
    c = pl.program_id(1)

    @pl.when(c == 0)
    def _():
        for ring in (sq_s, gq_s, gk_s, gv_s, gg_s, gr_s, sk_s, sv_s):
            ring[...] = jnp.zeros_like(ring)

    new2, old2 = lax.rem(c, 2), lax.rem(c + 1, 2)
    new3, old3, older3 = lax.rem(c, 3), lax.rem(c + 2, 3), lax.rem(c + 1, 3)
    half = pl.ds(TM // 2, TM // 2)

    def companion():
        _proj_tile(x_ref[...], *consts, sq_s.at[new2], sk_s.at[new3], sv_s.at[new3], gq_s.at[new2],
                   gk_s.at[new2], gv_s.at[new2], gg_s.at[new2], gr_s.at[new2])
        sk_out[...] = sk_s[new3]
        sv_out[...] = sv_s[new3]
        _swa_prompt_tile(c - 1, sq_s.at[old2], sk_s.at[older3, half], sk_s.at[old3], sv_s.at[older3, half],
                         sv_s.at[old3], hm_ref, sink_ref, o_swa_ref)

    _gla_step(c, gq_s.at[old2], gk_s.at[old2], gv_s.at[old2], gg_s.at[old2], gr_s.at[old2], s0_ref, tri_ref,
              m_ref, bd_ref, gn_ref, og_ref, sfin_ref, st_ref, o_ref, t=t, n_sub=n_sub, companion=companion)


def _gla_tokenwise(q_ref, k_ref, v_ref, g_ref, bd_ref, st_ref, o_ref, *, n_rows):
    row = lax.broadcasted_iota(I32, (GLA_SLOW_ROWS, 1), 0)

    def group(gi, carry):
        rows = pl.ds(pl.multiple_of(gi * GLA_SLOW_ROWS, GLA_SLOW_ROWS), GLA_SLOW_ROWS)
        q = q_ref[rows, :].astype(F32)
        k = k_ref[rows, :].astype(F32)
        v = v_ref[rows, :].astype(F32)
        decay = jnp.exp(g_ref[rows, :])
        o = jnp.zeros((GLA_SLOW_ROWS, N_HEADS_GLA * DV), F32)
        for j in range(GLA_SLOW_ROWS):
            only_j = row == j
            k_j = jnp.where(only_j, k, 0.0).astype(BF16)
            v_j = jnp.where(only_j, v, 0.0).astype(BF16)
            q_j = jnp.where(only_j, q, 0.0).astype(BF16)
            st = st_ref[...] * decay[j:j + 1, :] + _dot_tn(v_j, k_j) * bd_ref[...]
            st_ref[...] = st
            o = o + _dot_nt(q_j, st.astype(BF16))
        o_ref[rows, :] = o
        return carry

    lax.fori_loop(0, n_rows // GLA_SLOW_ROWS, group, 0)


def _gla_blocked(q_ref, k_ref, v_ref, g_ref, tri_ref, m_ref, bd_ref, st_ref, *, t, n_sub):
    nb = t // SUB
    n_rows = t * n_sub
    lanes = N_HEADS_GLA * DK

    def group_row(x, period, offset):
        g = x.reshape(n_rows // period, period, lanes)[:, offset:offset + 1, :]
        return jnp.broadcast_to(g, (n_rows // period, period, lanes)).reshape(n_rows, lanes)

    g_hi, g_lo = _split_bf16(g_ref[...])
    b = _dot(tri_ref[...], g_hi) + _dot(tri_ref[...], g_lo)
    q = q_ref[...].astype(F32)
    k = k_ref[...].astype(F32)
    qd = (q * jnp.exp(b - group_row(b, SUB, 0))).astype(BF16)
    pos = lax.broadcasted_iota(I32, (n_rows, lanes), 0) & (t - 1)
    k_parts = []
    for blk in range(nb):
        arg = jnp.where(pos < SUB * (blk + 1), group_row(b, t, SUB * blk) - b, NEG_BIG)
        k_parts.append((k * jnp.exp(arg)).astype(BF16))
    head_rows = jnp.concatenate([m_ref[h * t:h * t + SUB, 0:lanes] for h in range(N_HEADS_GLA)], axis=0)
    q_dec = (q * jnp.exp(b)).astype(BF16)
    k_last = (k * jnp.exp(group_row(b, t, t - 1) - b)).astype(BF16)
    row_a = lax.broadcasted_iota(I32, (N_HEADS_GLA * t, t), 0) & (t - 1)
    col_a = lax.broadcasted_iota(I32, (N_HEADS_GLA * t, t), 1)

    o_intra, q_decayed, state_add, state_decay = [], [], [], []
    for u in range(n_sub):
        rows = slice(u * t, (u + 1) * t)
        v = v_ref[rows, :]
        pieces = []
        for blk in range(nb):
            q_blk = qd[u * t + blk * SUB:u * t + (blk + 1) * SUB]
            pieces.append(_dot_nt(jnp.concatenate([q_blk] * N_HEADS_GLA, axis=0) * head_rows, k_parts[blk][rows]))
        a = jnp.concatenate([pieces[blk][h * SUB:(h + 1) * SUB] for h in range(N_HEADS_GLA) for blk in range(nb)],
                            axis=0)
        a = jnp.where(row_a >= col_a, a, 0.0).astype(BF16)
        o_full = _dot(a, v)
        o_intra.append(jnp.concatenate(
            [o_full[h * t:(h + 1) * t, h * DV:(h + 1) * DV] for h in range(N_HEADS_GLA)], axis=1))
        q_decayed.append(q_dec[rows])
        state_add.append(_dot_tn(v, k_last[rows]) * bd_ref[...])
        state_decay.append(jnp.exp(b[(u + 1) * t - 1:(u + 1) * t, :]))

    st = st_ref[...]
    o_parts = []
    for u in range(n_sub):
        o_parts.append(o_intra[u] + _dot_nt(q_decayed[u], st.astype(BF16)))
        st = st * state_decay[u] + state_add[u]
    st_ref[...] = st
    return jnp.concatenate(o_parts, axis=0) if n_sub > 1 else o_parts[0]


def _gla(gq, gk, gv, gg, gr, s0, tri, mask, bd, gn, *, t, n_sub, n_batch, n_steps, first_block, name):
    rows_per_step = t * n_sub

    def rows(width):
        return pl.BlockSpec((rows_per_step, width), lambda b, c: (first_block + b * n_steps + c, 0))

    def full(a):
        return pl.BlockSpec(a.shape, lambda b, c: (0,) * a.ndim)

    state = pl.BlockSpec((None,) + s0.shape[1:], lambda b, c: (b, 0, 0))
    return pl.pallas_call(
        functools.partial(_gla_kernel, t=t, n_sub=n_sub),
        grid=(n_batch, n_steps),
        in_specs=[rows(W_GQ), rows(W_GK), rows(W_GV), rows(W_GQ), rows(W_GR), state,
                  full(tri), full(mask), full(bd), full(gn)],
        out_specs=[pl.BlockSpec((rows_per_step, W_GV), lambda b, c: (b * n_steps + c, 0)), state],
        out_shape=[jax.ShapeDtypeStruct((n_batch * n_steps * rows_per_step, W_GV), BF16),
                   jax.ShapeDtypeStruct(s0.shape, F32)],
        scratch_shapes=[pltpu.VMEM(s0.shape[1:], F32), pltpu.VMEM((rows_per_step, W_GV), F32)],
        compiler_params=_cparams(2, "mixer"),
        name=name,
    )(gq, gk, gv, gg, gr, s0, tri, mask, bd, gn)


def _front_prompt(xp, consts, head_mask, sink_b, s0, tri, mask, bd, gn):
    def full(a):
        return pl.BlockSpec(a.shape, lambda b, c: (0,) * a.ndim)

    def computed(width):
        return pl.BlockSpec((TM, width), lambda b, c: (jnp.minimum(c, N_PROMPT_TILES - 1), 0))

    def mixed(width):
        return pl.BlockSpec((TM, width), lambda b, c: (jnp.maximum(c - 1, 0), 0))

    state = pl.BlockSpec((None,) + s0.shape[1:], lambda b, c: (b, 0, 0))
    ring2 = [pltpu.VMEM((2, TM, w), dt) for w, dt in
             (PROJ_OUTPUTS[0], PROJ_OUTPUTS[3], PROJ_OUTPUTS[4], PROJ_OUTPUTS[5], PROJ_OUTPUTS[6], PROJ_OUTPUTS[7])]
    ring3 = [pltpu.VMEM((3, TM, w), dt) for w, dt in (PROJ_OUTPUTS[1], PROJ_OUTPUTS[2])]
    return pl.pallas_call(
        functools.partial(_front_prompt_kernel, t=CHUNK, n_sub=TM // CHUNK),
        grid=(1, N_PROMPT_TILES + 1),
        in_specs=[computed(D_MODEL)] + [full(a) for a in consts] + [full(head_mask), full(sink_b), state,
                                                                   full(tri), full(mask), full(bd), full(gn)],
        out_specs=[computed(W_SK), computed(W_SV), mixed(W_SQ), mixed(W_GV), state],
        out_shape=[jax.ShapeDtypeStruct((N_PROMPT, W_SK), F32), jax.ShapeDtypeStruct((N_PROMPT, W_SV), F32),
                   jax.ShapeDtypeStruct((N_PROMPT, W_SQ), BF16), jax.ShapeDtypeStruct((N_PROMPT, W_GV), BF16),
                   jax.ShapeDtypeStruct(s0.shape, F32)],
        scratch_shapes=ring2 + ring3 + [pltpu.VMEM(s0.shape[1:], F32), pltpu.VMEM((TM, W_GV), F32)],
        compiler_params=_cparams(2, "front"),
        name="front_prompt",
    )(xp, *consts, head_mask, sink_b, s0, tri, mask, bd, gn)


def _tile_row_copies(hbm_ref, tile, vmem_ref, sem, to_hbm, rows=TM):
    copies = []
    for a in range(ROW_SUB):
        h = hbm_ref.at[pl.ds(tile * rows, rows), a, :]
        v = vmem_ref.at[:, pl.ds(a * LANES, LANES)]
        copies.append(pltpu.make_async_copy(v, h, sem) if to_hbm else pltpu.make_async_copy(h, v, sem))
    return copies


def _store_tile_rows(i, n_steps, outputs, row_buf, row_sem, rows=TM):
    buf_slot = lax.rem(i, 2)

    def store(j, tile, s):
        return _tile_row_copies(outputs[j][0], tile, row_buf.at[j, s], row_sem.at[j, s], True, rows)

    for j, (_, value) in enumerate(outputs):
        @pl.when(i >= 2)
        def _():
            for c in store(j, i - 2, buf_slot):
                c.wait()

        row_buf[j, buf_slot] = value
        for c in store(j, i, buf_slot):
            c.start()

        @pl.when(i == n_steps - 1)
        def _():
            for c in store(j, i - 1, 1 - buf_slot) + store(j, i, buf_slot):
                c.wait()


def _merge_kernel(oswp_ref, osws_ref, ogp_ref, ogs_ref, xp_ref, xs_ref, wo1_ref, wo2_ref, gf_ref, wrh_ref, wrl_ref,
                  br_ref, xrow_ref, hrow_ref, lg_ref, row_buf, row_sem):
    i = pl.program_id(0)
    is_prompt = i < N_PROMPT_TILES
    x = jnp.where(is_prompt, xp_ref[...], xs_ref[...])
    o_swa = jnp.where(is_prompt, oswp_ref[...], osws_ref[...])
    og = jnp.where(is_prompt, ogp_ref[...], ogs_ref[...])
    h = x + (_dot(o_swa, wo1_ref[...]) + _dot(og, wo2_ref[...]))
    xn = _rms(h) * gf_ref[...]
    x_hi, x_lo = _split_bf16(xn)
    hi_both = _dot(x_hi, wrl_ref[...])
    logits = hi_both[:, :LANES] + _dot(x_lo, wrh_ref[...]) + hi_both[:, LANES:] + br_ref[...]
    lg_ref[...] = logits.T[:N_EXPERTS]
    _store_tile_rows(i, N_TILES, ((xrow_ref, xn), (hrow_ref, h)), row_buf, row_sem)


def _merge(o_swa_p, o_swa_s, og_p, og_s, xp, xs, wo1, wo2, gf, wrh, wrl, br):
    def prompt_rows(width):
        return pl.BlockSpec((TM, width), lambda i: (jnp.minimum(i, N_PROMPT_TILES - 1), 0))

    def sample_rows(width):
        return pl.BlockSpec((TM, width), lambda i: (0, 0))

    def full(a):
        return pl.BlockSpec(a.shape, lambda i: (0,) * a.ndim)

    consts = (wo1, wo2, gf, wrh, wrl, br)
    return pl.pallas_call(
        _merge_kernel,
        grid=(N_TILES,),
        in_specs=[prompt_rows(W_SQ), sample_rows(W_SQ), prompt_rows(W_GV), sample_rows(W_GV),
                  prompt_rows(D_MODEL), sample_rows(D_MODEL)] + [full(a) for a in consts],
        out_specs=[pl.BlockSpec(memory_space=pl.ANY)] * 2 + [pl.BlockSpec((N_EXPERTS, TM), lambda i: (0, i))],
        out_shape=[jax.ShapeDtypeStruct((N_ROWS,) + ROW_TILE, F32)] * 2 + [
                   jax.ShapeDtypeStruct((N_EXPERTS, N_ROWS), F32)],
        scratch_shapes=[pltpu.VMEM((2, 2, TM, D_MODEL), F32), pltpu.SemaphoreType.DMA((2, 2))],
        compiler_params=_cparams(1, "merge"),
        name="merge",
    )(o_swa_p, o_swa_s, og_p, og_s, xp, xs, *consts)


RT = 1280
N_ROUTER_STEPS = N_ROWS // RT


def _router_kernel(lg_ref, tri_ref, wrow_ref, ti_ref, rk_ref, cnt_ref, base_ref, row_buf, row_sem):
    i = pl.program_id(0)

    @pl.when(i == 0)
    def _():
        base_ref[...] = jnp.zeros_like(base_ref)

    logits_t = lg_ref[...]
    expert = lax.broadcasted_iota(I32, logits_t.shape, 0)
    slot = lax.broadcasted_iota(I32, (SLOT_ROWS, RT), 0)
    vals, hots = [], []
    ti = jnp.zeros((SLOT_ROWS, RT), I32)
    for kk in range(TOP_K):
        m = jnp.max(logits_t, axis=0, keepdims=True)
        idx = jnp.min(jnp.where(logits_t == m, expert, N_EXPERTS), axis=0, keepdims=True)
        hot = expert == idx
        logits_t = jnp.where(hot, NEG_BIG, logits_t)
        vals.append(m)
        hots.append(hot)
        ti = jnp.where(slot == kk, idx, ti)
    ti_ref[...] = ti
    exps = [jnp.exp(v - vals[0]) for v in vals]
    den = exps[0] + exps[1] + exps[2] + exps[3]
    tw_t = jnp.zeros((SLOT_ROWS, RT), F32)
    for kk in range(TOP_K):
        tw_t = jnp.where(slot == kk, exps[kk] / den, tw_t)
    eye = (lax.broadcasted_iota(I32, (SLOT_ROWS, LANES), 0)
           == lax.broadcasted_iota(I32, (SLOT_ROWS, LANES), 1)).astype(F32).astype(BF16)
    w_hi = tw_t.astype(BF16)
    w_mid, w_lo = _split_bf16(tw_t - w_hi.astype(F32))
    tw_col = _dot_tn(w_hi, eye) + _dot_tn(w_mid, eye) + _dot_tn(w_lo, eye)

    onehot_t = jnp.zeros(logits_t.shape, F32)
    for hot in hots:
        onehot_t = onehot_t + jnp.where(hot, 1.0, 0.0)
    before_t = _dot(onehot_t.astype(BF16), tri_ref[...]) + base_ref[:, 0:1]
    rk = jnp.zeros((SLOT_ROWS, RT), I32)
    for kk in range(TOP_K):
        r = jnp.sum(jnp.where(hots[kk], before_t, 0.0), axis=0, keepdims=True).astype(I32)
        rk = jnp.where(slot == kk, r, rk)
    rk_ref[...] = rk
    total = base_ref[...] + jnp.sum(onehot_t, axis=1, keepdims=True)
    base_ref[...] = total
    cnt_ref[...] = total.astype(I32)

    w_lanes = [jnp.broadcast_to(tw_col[:, kk:kk + 1], (RT, LANES)) for kk in range(TOP_K)]
    w_rows = jnp.concatenate(w_lanes + [jnp.zeros((RT, D_MODEL - TOP_K * LANES), F32)], axis=1)
    _store_tile_rows(i, N_ROUTER_STEPS, ((wrow_ref, w_rows),), row_buf, row_sem, RT)


def _router(logits, tri):
    return pl.pallas_call(
        _router_kernel,
        grid=(N_ROUTER_STEPS,),
        in_specs=[pl.BlockSpec((N_EXPERTS, RT), lambda i: (0, i)), pl.BlockSpec(tri.shape, lambda i: (0, 0))],
        out_specs=[pl.BlockSpec(memory_space=pl.ANY),
                   pl.BlockSpec((SLOT_ROWS, RT), lambda i: (0, i)),
                   pl.BlockSpec((SLOT_ROWS, RT), lambda i: (0, i)),
                   pl.BlockSpec((N_EXPERTS, LANES), lambda i: (0, 0))],
        out_shape=[jax.ShapeDtypeStruct((N_ROWS,) + ROW_TILE, F32),
                   jax.ShapeDtypeStruct((SLOT_ROWS, N_ROWS), I32),
                   jax.ShapeDtypeStruct((SLOT_ROWS, N_ROWS), I32),
                   jax.ShapeDtypeStruct((N_EXPERTS, LANES), I32)],
        scratch_shapes=[pltpu.VMEM((N_EXPERTS, LANES), F32), pltpu.VMEM((1, 2, RT, D_MODEL), F32),
                        pltpu.SemaphoreType.DMA((1, 2))],
        compiler_params=_cparams(1, "router"),
        name="router",
    )(logits, tri)


ISSUE_UNROLL = 4


def _row_copy(src_ref, src_row, dst_ref, dst_row, sem):
    return pltpu.make_async_copy(src_ref.at[pl.ds(src_row, 1)], dst_ref.at[pl.ds(dst_row, 1)], sem)


def _dispatch_kernel(dest_ref, end_ref, x_ref, xs_ref, zero_ref, sem, zsem):
    i = pl.program_id(0)
    base = i * TM

    @pl.when(i == 0)
    def _():
        zero_ref[...] = jnp.zeros_like(zero_ref)

        def tail_copy(e):
            last = jnp.maximum(end_ref[e] - TM, 0)
            return pltpu.make_async_copy(zero_ref, xs_ref.at[pl.ds(pl.multiple_of(last, TM), TM)], zsem)

        def fill(e, carry):
            tail_copy(e).start()
            return carry

        def fill_wait(e, carry):
            tail_copy(e).wait()
            return carry

        lax.fori_loop(0, N_EXPERTS, fill, 0)
        lax.fori_loop(0, N_EXPERTS, fill_wait, 0)

        def unused_copy(t):
            return pltpu.make_async_copy(zero_ref, xs_ref.at[pl.ds(pl.multiple_of(t * TM, TM), TM)], zsem)

        def fill_unused(t, carry):
            unused_copy(t).start()
            unused_copy(t).wait()
            return carry

        lax.fori_loop(end_ref[N_EXPERTS - 1] // TM, N_EXPERT_TILES, fill_unused, 0)

    def issue(n, carry):
        for kk in range(TOP_K):
            _row_copy(x_ref, n, xs_ref, dest_ref[kk * N_ROWS + base + n], sem).start(priority=kk % 2)
        return carry

    lax.fori_loop(0, TM, issue, 0, unroll=ISSUE_UNROLL)

    for kk in range(TOP_K):
        pltpu.make_async_copy(x_ref, xs_ref.at[pl.ds(0, TM)], sem).wait()


def _dispatch(dest, end, x_packed):
    return pl.pallas_call(
        _dispatch_kernel,
        grid_spec=pltpu.PrefetchScalarGridSpec(
            num_scalar_prefetch=2,
            grid=(N_TILES,),
            in_specs=[pl.BlockSpec((TM,) + ROW_TILE, lambda i, d, e: (i, 0, 0))],
            out_specs=pl.BlockSpec(memory_space=pl.ANY),
            scratch_shapes=[pltpu.VMEM((TM,) + ROW_TILE, F32), pltpu.SemaphoreType.DMA,
                            pltpu.SemaphoreType.DMA],
        ),
        out_shape=jax.ShapeDtypeStruct((N_SORTED_ROWS,) + ROW_TILE, F32),
        compiler_params=_cparams(1, "dispatch"),
        name="dispatch",
    )(dest, end, x_packed)


CAST_ROWS = 128


def _moe_kernel(te_ref, nu_ref, nx_ref, xs_ref, wg_ref, bg_ref, wu_ref, bu_ref, wd_ref, bd_ref, ys_ref,
                w_stage, w_bf, x_buf, y_buf, zero_buf, w_sem, in_sem, out_sem, zero_sem):
    t = pl.program_id(0)
    n_used = nu_ref[0]
    slot = lax.rem(t, 2)
    e = te_ref[t]
    e_prev = te_ref[jnp.maximum(t - 1, 0)]

    def load(tile, s):
        return _tile_row_copies(xs_ref, tile, x_buf.at[s], in_sem.at[s], to_hbm=False)

    def store(tile, s):
        return _tile_row_copies(ys_ref, tile, y_buf.at[s], out_sem.at[s], to_hbm=True)

    def weight_copies(expert):
        return [pltpu.make_async_copy(w.at[expert], w_stage.at[j], w_sem.at[j])
                for j, w in enumerate((wg_ref, wu_ref, wd_ref))]

    @pl.when(t == 0)
    def _():
        for c in weight_copies(e) + load(0, 0):
            c.start()

    @pl.when(t + 1 < n_used)
    def _():
        for c in load(t + 1, 1 - slot):
            c.start()

    @pl.when((t == 0) | (e != e_prev))
    def _():
        for c in weight_copies(e):
            c.wait()

        def cast(r, carry):
            sl = pl.ds(pl.multiple_of(r * CAST_ROWS, CAST_ROWS), CAST_ROWS)
            for j in range(3):
                w_bf[j, sl, :] = w_stage[j, sl, :].astype(BF16)
            return carry

        lax.fori_loop(0, D_MODEL // CAST_ROWS, cast, 0)
        e_next = nx_ref[e]

        @pl.when(e_next >= 0)
        def _():
            for c in weight_copies(e_next):
                c.start()

    @pl.when(t < n_used)
    def _():
        for c in load(t, slot):
            c.wait()

        @pl.when(t >= 2)
        def _():
            for c in store(t - 2, slot):
                c.wait()

        x = x_buf[slot].astype(BF16)
        gate = jnp.minimum(_dot(x, w_bf[0]) + bg_ref[...], SWIGLU_LIMIT)
        up = jnp.clip(_dot(x, w_bf[1]) + bu_ref[...], -SWIGLU_LIMIT, SWIGLU_LIMIT)
        hdn = (up + 1.0) * gate * (1.0 / (1.0 + jnp.exp(-SWIGLU_ALPHA * gate)))
        y_buf[slot] = _dot(hdn.astype(BF16), w_bf[2]) + bd_ref[...]
        for c in store(t, slot):
            c.start()

    @pl.when(t >= n_used)
    def _():
        zero_buf[...] = jnp.zeros_like(zero_buf)
        fill = pltpu.make_async_copy(zero_buf, ys_ref.at[pl.ds(t * TM, TM)], zero_sem)
        fill.start()
        fill.wait()

    @pl.when(t == N_EXPERT_TILES - 1)
    def _():
        @pl.when(n_used >= 2)
        def _():
            for c in store(n_used - 2, lax.rem(n_used, 2)):
                c.wait()

        for c in store(n_used - 1, lax.rem(n_used - 1, 2)):
            c.wait()


def _moe(tile_expert, n_used, next_expert, xs, w_gate, b_gate, w_up, b_up, w_down, b_down):
    hbm = pl.BlockSpec(memory_space=pl.ANY)
    bias = pl.BlockSpec((None, 1, D_MODEL), lambda t, te, nu, nx: (te[t], 0, 0))
    return pl.pallas_call(
        _moe_kernel,
        grid_spec=pltpu.PrefetchScalarGridSpec(
            num_scalar_prefetch=3,
            grid=(N_EXPERT_TILES,),
            in_specs=[hbm, hbm, bias, hbm, bias, hbm, bias],
            out_specs=hbm,
            scratch_shapes=[pltpu.VMEM((3, D_MODEL, D_MODEL), F32), pltpu.VMEM((3, D_MODEL, D_MODEL), BF16),
                            pltpu.VMEM((2, TM, D_MODEL), F32), pltpu.VMEM((2, TM, D_MODEL), F32),
                            pltpu.VMEM((TM,) + ROW_TILE, F32), pltpu.SemaphoreType.DMA((3,)),
                            pltpu.SemaphoreType.DMA((2,)), pltpu.SemaphoreType.DMA((2,)),
                            pltpu.SemaphoreType.DMA],
        ),
        out_shape=jax.ShapeDtypeStruct((N_SORTED_ROWS,) + ROW_TILE, F32),
        compiler_params=_cparams(1, "experts"),
        name="experts",
    )(tile_expert, n_used, next_expert, xs, w_gate, b_gate, w_up, b_up, w_down, b_down)


def _combine_kernel(dest_ref, h_ref, w_ref, ys_ref, yp_ref, ysm_ref, g_ref, out_ref, sem):
    i = pl.program_id(0)
    slot = lax.rem(i, 2)

    def issue_tile(tile, s):
        base = tile * TM

        def issue(n, carry):
            for kk in range(TOP_K):
                _row_copy(ys_ref, dest_ref[kk * N_ROWS + base + n], g_ref.at[s], kk * TM + n,
                          sem.at[s]).start(priority=kk % 2)
            return carry

        lax.fori_loop(0, TM, issue, 0, unroll=ISSUE_UNROLL)

    @pl.when(i == 0)
    def _():
        issue_tile(0, 0)

    @pl.when(i + 1 < N_TILES)
    def _():
        issue_tile(i + 1, 1 - slot)

    for kk in range(TOP_K):
        pltpu.make_async_copy(ys_ref.at[pl.ds(0, TM)], g_ref.at[slot, pl.ds(kk * TM, TM)], sem.at[slot]).wait()

    w = w_ref[...]
    acc = None
    for kk in range(TOP_K):
        part = jnp.broadcast_to(w[:, kk:kk + 1, :], (TM,) + ROW_TILE) * g_ref[slot, kk * TM:(kk + 1) * TM]
        acc = part if acc is None else acc + part
    out_ref[...] = h_ref[...] + acc

    def write(y_ref):
        for a in range(ROW_SUB):
            y_ref[:, a * LANES:(a + 1) * LANES] = out_ref[:, a, :]

    @pl.when(i < N_PROMPT_TILES)
    def _():
        write(yp_ref)

    @pl.when(i >= N_PROMPT_TILES)
    def _():
        write(ysm_ref)


def _combine(dest, h_rows, w_rows, ys):
    tile = lambda index: pl.BlockSpec((TM,) + ROW_TILE, index)
    return pl.pallas_call(
        _combine_kernel,
        grid_spec=pltpu.PrefetchScalarGridSpec(
            num_scalar_prefetch=1,
            grid=(N_TILES,),
            in_specs=[tile(lambda i, d: (i, 0, 0)), tile(lambda i, d: (i, 0, 0)),
                      pl.BlockSpec(memory_space=pl.ANY)],
            out_specs=[pl.BlockSpec((TM, D_MODEL), lambda i, d: (jnp.minimum(i, N_PROMPT_TILES - 1), 0)),
                       pl.BlockSpec((TM, D_MODEL), lambda i, d: (0, 0))],
            scratch_shapes=[pltpu.VMEM((2, TOP_K * TM) + ROW_TILE, F32), pltpu.VMEM((TM,) + ROW_TILE, F32),
                            pltpu.SemaphoreType.DMA((2,))],
        ),
        out_shape=[jax.ShapeDtypeStruct((N_PROMPT, D_MODEL), F32),
                   jax.ShapeDtypeStruct((N_SAMPLE, D_MODEL), F32)],
        compiler_params=_cparams(1, "combine"),
        name="combine",
    )(dest, h_rows, w_rows, ys)


def _block_diag_ones(n, blk):
    idx = np.arange(n) // blk
    return (idx[:, None] == idx[None, :]).astype(np.float32)


def _swa_head_mask(tq):
    row_head = np.arange(4 * tq)[:, None] // tq
    lane_head = np.arange(4 * HD)[None, :] // HD
    return jnp.asarray((row_head == lane_head).astype(np.float32), BF16)


def _gla_masks(t, n_sub):
    nb = t // SUB
    lanes = N_HEADS_GLA * DK
    tri = jnp.asarray(np.kron(np.eye(n_sub, dtype=np.float32), np.tril(np.ones((t, t), np.float32))), BF16)
    row = np.arange(N_HEADS_GLA * t)
    col = np.arange(nb * lanes)
    same_head = (row[:, None] // t) == ((col[None, :] % lanes) // DK)
    same_blk = ((row[:, None] % t) // SUB) == (col[None, :] // lanes)
    mask = jnp.asarray((same_head & same_blk).astype(np.float32), BF16)
    return tri, mask


def _sink_rows(sinks, tq):
    s = jnp.repeat(sinks.astype(F32).reshape(2, 4), tq, axis=1)
    return jnp.broadcast_to(s[:, None, :], (2, 8, 4 * tq))


def kernel(x_prompt, x_sample, state_gla, cache_swa_k, cache_swa_v, norm_mix_g, w_in, w_gla_a2, b_gla_a, q_norm_g,
           k_norm_g, swa_sinks, gla_norm_g, w_out, norm_ffn_g, w_router, b_router, w_gate, b_gate, w_up, b_up,
           w_down, b_down):
    xp = x_prompt.reshape(N_PROMPT, D_MODEL)
    xs = x_sample.reshape(N_SAMPLE, D_MODEL)

    w_in0 = w_in[0]
    w_main = w_in0[:, :W_MAIN].astype(BF16)
    w_ga = jnp.pad(w_in0[:, OFF_GA:], ((0, 0), (0, LANES - W_GA))).astype(BF16)
    w_a2 = jnp.pad(w_gla_a2[0], ((0, LANES - W_GA), (0, 0))).astype(BF16)
    b_a = b_gla_a[0].reshape(1, -1)
    qg = (jnp.tile(q_norm_g[0], N_HEADS_SWA) * (HD ** -0.5)).reshape(1, -1)
    kg = jnp.tile(k_norm_g[0], 2).reshape(1, -1)
    bdq = jnp.asarray(_block_diag_ones(W_SQ // 2, HD), BF16)
    bdk = jnp.asarray(_block_diag_ones(W_SK, HD), BF16)

    proj_consts = (norm_mix_g[0].reshape(1, -1), w_main, w_ga, w_a2, b_a, qg, kg, bdq, bdk)
    sq, sk, sv, gq, gk, gv, gg, gr = _proj_sample(xs, proj_consts)

    cache_k = cache_swa_k[0].reshape(N_STREAMS, WINDOW, 2 * HD)
    cache_v = cache_swa_v[0].reshape(N_STREAMS, WINDOW, 2 * HD)
    o_swa_s = _swa_sample(sq, sk, sv, cache_k, cache_v, _swa_head_mask(T_SAMPLE),
                          _sink_rows(swa_sinks[0], T_SAMPLE))

    bd_state = jnp.asarray(_block_diag_ones(N_HEADS_GLA, 1).repeat(DV, axis=0).repeat(DK, axis=1), F32)
    gn = gla_norm_g[0].reshape(1, -1)
    tri_p, mask_p = _gla_masks(CHUNK, TM // CHUNK)
    tri_s, mask_s = _gla_masks(T_SAMPLE, 1)
    s0_p = jnp.zeros((1, N_HEADS_GLA * DV, N_HEADS_GLA * DK), F32)
    sk_p, sv_p, o_swa_p, og_p, sfin_p = _front_prompt(xp, proj_consts, _swa_head_mask(CHUNK),
                                                      _sink_rows(swa_sinks[0], CHUNK), s0_p, tri_p, mask_p,
                                                      bd_state, gn)
    eye = jnp.eye(N_HEADS_GLA, dtype=F32)
    s0_s = jnp.einsum('bhde,hg->bhegd', state_gla[0].astype(F32), eye).reshape(
        N_STREAMS, N_HEADS_GLA * DV, N_HEADS_GLA * DK)
    og_s, sfin_s = _gla(gq, gk, gv, gg, gr, s0_s, tri_s, mask_s, bd_state, gn, t=T_SAMPLE, n_sub=1,
                        n_batch=N_STREAMS, n_steps=1, first_block=0, name="gla_sample")

    def unpack_state(sfin):
        s = sfin.reshape(-1, N_HEADS_GLA, DV, N_HEADS_GLA, DK)
        s = jnp.stack([s[:, h, :, h, :] for h in range(N_HEADS_GLA)], axis=1)
        return jnp.transpose(s, (0, 1, 3, 2))[None]

    w_out0 = w_out[0].astype(BF16)
    wr = jnp.pad(w_router[0], ((0, 0), (0, LANES - N_EXPERTS)))
    wr_hi = wr.astype(BF16)
    wr_lo = jnp.concatenate([wr_hi, (wr - wr_hi.astype(F32)).astype(BF16)], axis=1)
    br = jnp.pad(b_router[0].astype(F32), (0, LANES - N_EXPERTS)).reshape(1, -1)
    x_rows, h_rows, logits = _merge(o_swa_p, o_swa_s, og_p, og_s, xp, xs, w_out0[:W_SQ], w_out0[W_SQ:],
                                    norm_ffn_g[0].reshape(1, -1), wr_hi, wr_lo, br)
    earlier = jnp.asarray(np.triu(np.ones((RT, RT), np.float32), 1), BF16)
    w_rows, top_i, rank, counts = _router(logits, earlier)

    counts = counts[:, 0]
    padded = (counts + TM - 1) // TM * TM
    end = jnp.cumsum(padded)
    start = end - padded
    experts = jnp.arange(N_EXPERTS, dtype=I32)
    is_e = top_i[:TOP_K, :, None] == experts
    dest = (rank[:TOP_K] + jnp.sum(jnp.where(is_e, start, 0), axis=-1)).reshape(-1).astype(I32)
    n_used = (end[-1] // TM).astype(I32)
    tiles = jnp.minimum(jnp.arange(N_EXPERT_TILES, dtype=I32), n_used - 1)
    tile_expert = jnp.sum((tiles[:, None] * TM >= end[None, :]).astype(I32), axis=1)
    later_nonempty = (experts[None, :] > experts[:, None]) & (padded[None, :] > 0)
    next_expert = jnp.min(jnp.where(later_nonempty, experts[None, :], N_EXPERTS), axis=1)
    next_expert = jnp.where(next_expert < N_EXPERTS, next_expert, -1).astype(I32)

    xs_sorted = _dispatch(dest, end.astype(I32), x_rows)
    ys = _moe(tile_expert, n_used.reshape(1), next_expert, xs_sorted, w_gate[0], b_gate[0].reshape(N_EXPERTS, 1, -1),
              w_up[0], b_up[0].reshape(N_EXPERTS, 1, -1), w_down[0], b_down[0].reshape(N_EXPERTS, 1, -1))
    y_p, y_s = _combine(dest, h_rows, w_rows, ys)

    sk_s = sk.reshape(N_STREAMS, T_SAMPLE, 2 * HD)
    sv_s = sv.reshape(N_STREAMS, T_SAMPLE, 2 * HD)
    kc_s = jnp.concatenate([cache_k[:, T_SAMPLE:], sk_s], axis=1).reshape(1, N_STREAMS, WINDOW, 2, HD)
    vc_s = jnp.concatenate([cache_v[:, T_SAMPLE:], sv_s], axis=1).reshape(1, N_STREAMS, WINDOW, 2, HD)
    kc_p = sk_p[N_PROMPT - WINDOW:].reshape(1, 1, WINDOW, 2, HD)
    vc_p = sv_p[N_PROMPT - WINDOW:].reshape(1, 1, WINDOW, 2, HD)
    return (y_p.reshape(1, N_PROMPT, D_MODEL), y_s.reshape(N_STREAMS, T_SAMPLE, D_MODEL),
            unpack_state(sfin_p), kc_p, vc_p, unpack_state(sfin_s), kc_s, vc_s)
```
